```python
import jax, jax.numpy as jnp
from jax import lax
import numpy as np

D_MODEL = 2048
BATCH = 4
SEQ = 4096
DEPTH = 1

MIX_WIDTH = D_MODEL
ATTN_WIDTH = MIX_WIDTH // 2
CONV_WIDTH = MIX_WIDTH - ATTN_WIDTH
N_HEADS = 8
QK_NOPE_DIM = 128
QK_ROPE_DIM = 64
V_HEAD_DIM = ATTN_WIDTH // N_HEADS
Q_LORA_RANK = 512
KV_LORA_RANK = 512
ROPE_BASE = 10000.0
Q_BLOCK = 128
CONV_KERNEL = 31
N_EXPERTS = 64
TOP_K = 6
N_EXPERT_GROUPS = 8
TOPK_GROUPS = 4
EXPERT_FF = 512
SHARED_FF = 512
ROUTED_SCALE = 2.5
LN_EPS = 1e-5
RMS_EPS = 1e-6
IN_COLS = Q_LORA_RANK + KV_LORA_RANK + QK_ROPE_DIM + 2 * CONV_WIDTH

kernel_name = "hybrid_mla_conformer_moe_block"


def layer_norm(x, g, b):
    xf = x.astype(jnp.float32)
    mu = jnp.mean(xf, axis=-1, keepdims=True)
    var = jnp.mean(jnp.square(xf - mu), axis=-1, keepdims=True)
    return ((xf - mu) * lax.rsqrt(var + LN_EPS)).astype(x.dtype) * g + b


def rms_norm(x, g):
    xf = x.astype(jnp.float32)
    ms = jnp.mean(jnp.square(xf), axis=-1, keepdims=True)
    return (xf * lax.rsqrt(ms + RMS_EPS)).astype(x.dtype) * g


def rope_tables(positions):
    inv_freq = ROPE_BASE ** (-jnp.arange(0, QK_ROPE_DIM, 2, dtype=jnp.float32) / QK_ROPE_DIM)
    ang = positions.astype(jnp.float32)[..., None] * inv_freq
    return jnp.cos(ang), jnp.sin(ang)


def apply_rope(t, cos, sin):
    cos = cos.astype(t.dtype)
    sin = sin.astype(t.dtype)
    t1, t2 = jnp.split(t, 2, axis=-1)
    return jnp.concatenate([t1 * cos - t2 * sin, t2 * cos + t1 * sin], axis=-1)


def causal_mla_attention(q_nope, q_rope, k_nope, k_rope, v):
    B, S, H, _ = q_nope.shape
    nb = S // Q_BLOCK
    scale = (QK_NOPE_DIM + QK_ROPE_DIM) ** -0.5
    qn = q_nope.reshape(B, nb, Q_BLOCK, H, QK_NOPE_DIM).transpose(1, 0, 2, 3, 4)
    qr = q_rope.reshape(B, nb, Q_BLOCK, H, QK_ROPE_DIM).transpose(1, 0, 2, 3, 4)
    key_pos = jnp.arange(S)

    def one_block(args):
        qn_b, qr_b, blk = args
        s = (jnp.einsum('bqhd,bkhd->bhqk', qn_b, k_nope)
             + jnp.einsum('bqhd,bkd->bhqk', qr_b, k_rope)).astype(jnp.float32) * scale
        q_pos = blk * Q_BLOCK + jnp.arange(Q_BLOCK)
        mask = key_pos[None, :] <= q_pos[:, None]
        s = jnp.where(mask[None, None], s, -jnp.inf)
        p = jax.nn.softmax(s, axis=-1).astype(v.dtype)
        return jnp.einsum('bhqk,bkhd->bqhd', p, v)

    out = lax.map(one_block, (qn, qr, jnp.arange(nb)))
    return out.transpose(1, 0, 2, 3, 4).reshape(B, S, H * V_HEAD_DIM)


def conformer_conv(conv_in, conv_w, conv_b, ln_g, ln_b):
    a, gate = jnp.split(conv_in, 2, axis=-1)
    u = a * jax.nn.sigmoid(gate)
    kern = conv_w[:, None, :]
    y = lax.conv_general_dilated(u, kern, window_strides=(1,), padding=[(CONV_KERNEL - 1, 0)],
                                 dimension_numbers=('NWC', 'WIO', 'NWC'),
                                 feature_group_count=CONV_WIDTH) + conv_b
    return jax.nn.silu(layer_norm(y, ln_g, ln_b))


def token_mixing(h, cos, sin, w_in, q_norm_g, w_uq, kv_norm_g, w_ukv,
                 conv_w, conv_b, conv_ln_g, conv_ln_b, w_o):
    B, S, _ = h.shape
    proj = h @ w_in
    c_q, c_kv, k_rope, conv_in = jnp.split(
        proj, [Q_LORA_RANK, Q_LORA_RANK + KV_LORA_RANK, Q_LORA_RANK + KV_LORA_RANK + QK_ROPE_DIM], axis=-1)
    q = (rms_norm(c_q, q_norm_g) @ w_uq).reshape(B, S, N_HEADS, QK_NOPE_DIM + QK_ROPE_DIM)
    q_nope, q_rope = jnp.split(q, [QK_NOPE_DIM], axis=-1)
    q_rope = apply_rope(q_rope, cos[:, :, None, :], sin[:, :, None, :])
    k_rope = apply_rope(k_rope, cos, sin)
    kv = (rms_norm(c_kv, kv_norm_g) @ w_ukv).reshape(B, S, N_HEADS, QK_NOPE_DIM + V_HEAD_DIM)
    k_nope, v = jnp.split(kv, [QK_NOPE_DIM], axis=-1)
    attn = causal_mla_attention(q_nope, q_rope, k_nope, k_rope, v)
    conv = conformer_conv(conv_in, conv_w, conv_b, conv_ln_g, conv_ln_b)
    return jnp.concatenate([attn, conv], axis=-1) @ w_o


def moe_ffn(h, w_router, router_bias, w_gate, w_up, w_down, ws_gate, ws_up, ws_down):
    B, S, D = h.shape
    T = B * S
    xt = h.reshape(T, D)
    scores = jax.nn.sigmoid(xt.astype(jnp.float32) @ w_router.astype(jnp.float32))
    biased = scores + router_bias.astype(jnp.float32)
    grp = biased.reshape(T, N_EXPERT_GROUPS, N_EXPERTS // N_EXPERT_GROUPS)
    group_scores = lax.top_k(grp, 2)[0].sum(axis=-1)
    _, gidx = lax.top_k(group_scores, TOPK_GROUPS)
    gmask = jnp.sum(jax.nn.one_hot(gidx, N_EXPERT_GROUPS, dtype=jnp.float32), axis=1) > 0
    emask = jnp.repeat(gmask, N_EXPERTS // N_EXPERT_GROUPS, axis=1)
    _, eidx = lax.top_k(jnp.where(emask, biased, -jnp.inf), TOP_K)
    wts = jnp.take_along_axis(scores, eidx, axis=1)
    wts = wts / (jnp.sum(wts, axis=-1, keepdims=True) + 1e-20) * ROUTED_SCALE
    flat_e = eidx.reshape(-1)
    order = jnp.argsort(flat_e)
    token_of = order // TOP_K
    group_sizes = jnp.bincount(flat_e, length=N_EXPERTS).astype(jnp.int32)
    xs = xt[token_of]
    hid = jax.nn.silu(lax.ragged_dot(xs, w_gate, group_sizes)) * lax.ragged_dot(xs, w_up, group_sizes)
    out = lax.ragged_dot(hid, w_down, group_sizes)
    w_sorted = wts.reshape(-1)[order].astype(out.dtype)
    routed = jax.ops.segment_sum(out * w_sorted[:, None], token_of, num_segments=T)
    shared = (jax.nn.silu(xt @ ws_gate) * (xt @ ws_up)) @ ws_down
    return (routed + shared).reshape(B, S, D)


def setup_inputs(seed: int = 0) -> dict:
    key = jax.random.key(seed)
    ks = jax.random.split(key, 24)
    beta = (8.0 * DEPTH) ** -0.25
    f32 = jnp.float32

    def nrm(k, shape, scale):
        return jax.random.normal(k, shape, f32) * scale

    def gain(k, shape):
        return 1.0 + 0.02 * jax.random.normal(k, shape, f32)

    offsets = jax.random.randint(ks[1], (BATCH, 1), 0, 1024, dtype=jnp.int32)
    positions = jnp.arange(SEQ, dtype=jnp.int32)[None, :] + offsets
    return {
        "x": nrm(ks[0], (BATCH, SEQ, D_MODEL), 1.0),
        "positions": positions,
        "w_in": nrm(ks[2], (DEPTH, D_MODEL, IN_COLS), D_MODEL ** -0.5),
        "q_norm_g": gain(ks[3], (DEPTH, Q_LORA_RANK)),
        "w_uq": nrm(ks[4], (DEPTH, Q_LORA_RANK, N_HEADS * (QK_NOPE_DIM + QK_ROPE_DIM)), Q_LORA_RANK ** -0.5),
        "kv_norm_g": gain(ks[5], (DEPTH, KV_LORA_RANK)),
        "w_ukv": nrm(ks[6], (DEPTH, KV_LORA_RANK, N_HEADS * (QK_NOPE_DIM + V_HEAD_DIM)), KV_LORA_RANK ** -0.5),
        "conv_w": nrm(ks[7], (DEPTH, CONV_KERNEL, CONV_WIDTH), CONV_KERNEL ** -0.5),
        "conv_b": nrm(ks[8], (DEPTH, CONV_WIDTH), 0.02),
        "conv_ln_g": gain(ks[9], (DEPTH, CONV_WIDTH)),
        "conv_ln_b": nrm(ks[10], (DEPTH, CONV_WIDTH), 0.02),
        "w_o": nrm(ks[11], (DEPTH, MIX_WIDTH, D_MODEL), MIX_WIDTH ** -0.5 * beta),
        "ln1_g": gain(ks[12], (DEPTH, D_MODEL)),
        "ln1_b": nrm(ks[13], (DEPTH, D_MODEL), 0.02),
        "w_router": nrm(ks[14], (DEPTH, D_MODEL, N_EXPERTS), D_MODEL ** -0.5),
        "router_bias": nrm(ks[15], (DEPTH, N_EXPERTS), 0.01),
        "w_gate": nrm(ks[16], (DEPTH, N_EXPERTS, D_MODEL, EXPERT_FF), D_MODEL ** -0.5),
        "w_up": nrm(ks[17], (DEPTH, N_EXPERTS, D_MODEL, EXPERT_FF), D_MODEL ** -0.5),
        "w_down": nrm(ks[18], (DEPTH, N_EXPERTS, EXPERT_FF, D_MODEL), EXPERT_FF ** -0.5 * beta),
        "ws_gate": nrm(ks[19], (DEPTH, D_MODEL, SHARED_FF), D_MODEL ** -0.5),
        "ws_up": nrm(ks[20], (DEPTH, D_MODEL, SHARED_FF), D_MODEL ** -0.5),
        "ws_down": nrm(ks[21], (DEPTH, SHARED_FF, D_MODEL), SHARED_FF ** -0.5 * beta),
        "ln2_g": gain(ks[22], (DEPTH, D_MODEL)),
        "ln2_b": nrm(ks[23], (DEPTH, D_MODEL), 0.02),
    }


def reference(x, positions, w_in, q_norm_g, w_uq, kv_norm_g, w_ukv, conv_w, conv_b,
              conv_ln_g, conv_ln_b, w_o, ln1_g, ln1_b, w_router, router_bias,
              w_gate, w_up, w_down, ws_gate, ws_up, ws_down, ln2_g, ln2_b):
    alpha = (2.0 * DEPTH) ** 0.25
    cos, sin = rope_tables(positions)
    h = x
    for l in range(DEPTH):
        mix = token_mixing(h, cos, sin, w_in[l], q_norm_g[l], w_uq[l], kv_norm_g[l], w_ukv[l],
                           conv_w[l], conv_b[l], conv_ln_g[l], conv_ln_b[l], w_o[l])
        h = layer_norm(alpha * h + mix, ln1_g[l], ln1_b[l])
        ffn = moe_ffn(h, w_router[l], router_bias[l], w_gate[l], w_up[l], w_down[l],
                      ws_gate[l], ws_up[l], ws_down[l])
        h = layer_norm(alpha * h + ffn, ln2_g[l], ln2_b[l])
    return h
```

```python
import functools

import jax
import jax.numpy as jnp
from jax import lax
from jax.experimental import pallas as pl
from jax.experimental.pallas import tpu as pltpu

F32 = jnp.float32
BF16 = jnp.bfloat16
I32 = jnp.int32
U32 = jnp.uint32

N_HEADS = 8
QK_NOPE_DIM = 128
QK_ROPE_DIM = 64
V_HEAD_DIM = 128
HEAD_QK_PAD = 256
CONV_KERNEL = 31
N_EXPERTS = 64
TOP_K = 6
TOP_K_PAD = 8
N_EXPERT_GROUPS = 8
GROUP_SIZE = N_EXPERTS // N_EXPERT_GROUPS
TOPK_GROUPS = 4
ROUTED_SCALE = 2.5
ROPE_BASE = 10000.0
LN_EPS = 1e-5
RMS_EPS = 1e-6

LANES = 128
CONV_HALO = 32

TM_PROJ = 256
TM_CONV = 512
TQ_ATTN = 512
TM_ROUTE = 512
TE_ROWS = 256
TM_COMB = 256
TC_DISP = 1024
CONV_ROWS = 64
CONV_COLS = 256
VMEM_LIMIT = 56 * 1024 * 1024


def _cparams(sem):
    return pltpu.CompilerParams(dimension_semantics=sem, vmem_limit_bytes=VMEM_LIMIT)


def _sigmoid(v):
    return 1.0 / (1.0 + jnp.exp(-v))


def _layer_norm(v, g, b):
    mu = jnp.mean(v, axis=-1, keepdims=True)
    d = v - mu
    var = jnp.mean(d * d, axis=-1, keepdims=True)
    return d * lax.rsqrt(var + LN_EPS) * g + b


def _rms_norm(v, g):
    ms = jnp.mean(v * v, axis=-1, keepdims=True)
    return v * lax.rsqrt(ms + RMS_EPS) * g


def _qkv_kernel(x_ref, pos_ref, invf_ref, wa_ref, qg_ref, kvg_ref, wuq_ref, wuqr_ref, wukv_ref,
                q_ref, k_ref, v_ref):
    ql = qg_ref.shape[1]
    kvl = kvg_ref.shape[1]
    xb = x_ref[...].astype(BF16)
    lat = jnp.dot(xb, wa_ref[...], preferred_element_type=F32)
    ang = pos_ref[...].astype(F32) * invf_ref[...]
    cos = jnp.cos(ang)
    sin = jnp.sin(ang)
    cq = _rms_norm(lat[:, :ql], qg_ref[...]).astype(BF16)
    ckv = _rms_norm(lat[:, ql:ql + kvl], kvg_ref[...]).astype(BF16)
    kr = lat[:, ql + kvl:ql + kvl + LANES] * cos + lat[:, ql + kvl + LANES:ql + kvl + 2 * LANES] * sin
    kr = kr.astype(BF16)
    q = jnp.dot(cq, wuq_ref[...], preferred_element_type=F32)
    qrot = jnp.dot(cq, wuqr_ref[...], preferred_element_type=F32)
    kv = jnp.dot(ckv, wukv_ref[...], preferred_element_type=F32)
    for h in range(N_HEADS):
        c0 = h * HEAD_QK_PAD
        q_ref[:, c0:c0 + LANES] = q[:, c0:c0 + LANES].astype(BF16)
        q_ref[:, c0 + LANES:c0 + 2 * LANES] = (
            q[:, c0 + LANES:c0 + 2 * LANES] * cos + qrot[:, h * LANES:(h + 1) * LANES] * sin).astype(BF16)
        k_ref[:, c0:c0 + LANES] = kv[:, h * LANES:(h + 1) * LANES].astype(BF16)
        k_ref[:, c0 + LANES:c0 + 2 * LANES] = kr
    v_ref[...] = kv[:, N_HEADS * QK_NOPE_DIM:].astype(BF16)


def _qkv(x2, pos2, invf4, wa, qg, kvg, wuq, wuqr, wukv):
    t, d = x2.shape
    tm = TM_PROJ
    full = lambda a: pl.BlockSpec(a.shape, lambda i: (0,) * a.ndim)
    return pl.pallas_call(
        _qkv_kernel,
        grid=(t // tm,),
        in_specs=[pl.BlockSpec((tm, d), lambda i: (i, 0)),
                  pl.BlockSpec((tm, 1), lambda i: (i, 0)),
                  full(invf4), full(wa), full(qg), full(kvg), full(wuq), full(wuqr), full(wukv)],
        out_specs=[pl.BlockSpec((tm, N_HEADS * HEAD_QK_PAD), lambda i: (i, 0)),
                   pl.BlockSpec((tm, N_HEADS * HEAD_QK_PAD), lambda i: (i, 0)),
                   pl.BlockSpec((tm, N_HEADS * V_HEAD_DIM), lambda i: (i, 0))],
        out_shape=[jax.ShapeDtypeStruct((t, N_HEADS * HEAD_QK_PAD), BF16),
                   jax.ShapeDtypeStruct((t, N_HEADS * HEAD_QK_PAD), BF16),
                   jax.ShapeDtypeStruct((t, N_HEADS * V_HEAD_DIM), BF16)],
        compiler_params=_cparams(("arbitrary",)),
        name="qkv",
    )(x2, pos2, invf4, wa, qg, kvg, wuq, wuqr, wukv)


def _conv_kernel(x_ref, wc_ref, cw_ref, cb_ref, g_ref, b_ref, o_ref, ubuf, ybuf, *, tiles_per_seq):
    tm, cw = o_ref.shape
    i = pl.program_id(0)
    xb = x_ref[...].astype(BF16)
    ag = jnp.dot(xb, wc_ref[...], preferred_element_type=F32)
    u = ag[:, :cw] * _sigmoid(ag[:, cw:])

    @pl.when(i % tiles_per_seq == 0)
    def _():
        ubuf[0:CONV_HALO, :] = jnp.zeros((CONV_HALO, cw), F32)

    ubuf[CONV_HALO:CONV_HALO + tm, :] = u
    shift0 = CONV_HALO - (CONV_KERNEL - 1)
    n_col = cw // CONV_COLS

    def chunk(c, carry):
        r0 = pl.multiple_of((c // n_col) * CONV_ROWS, CONV_ROWS)
        c0 = pl.multiple_of((c % n_col) * CONV_COLS, CONV_COLS)
        win = ubuf[pl.ds(r0, CONV_ROWS + CONV_HALO), pl.ds(c0, CONV_COLS)]
        acc = jnp.zeros((CONV_ROWS, CONV_COLS), F32)
        for k in range(CONV_KERNEL):
            wk = cw_ref[k:k + 1, pl.ds(c0, CONV_COLS)]
            acc = acc + wk * win[shift0 + k:shift0 + k + CONV_ROWS, :]
        ybuf[pl.ds(r0, CONV_ROWS), pl.ds(c0, CONV_COLS)] = acc
        return carry

    lax.fori_loop(0, (tm // CONV_ROWS) * n_col, chunk, 0)
    ubuf[0:CONV_HALO, :] = ubuf[tm:tm + CONV_HALO, :]
    y = _layer_norm(ybuf[...] + cb_ref[...], g_ref[...], b_ref[...])
    o_ref[...] = (y * _sigmoid(y)).astype(BF16)


def _conv(x2, wc, cw, cb, g, b, seq):
    t, d = x2.shape
    tm = TM_CONV
    c = cw.shape[1]
    full = lambda a: pl.BlockSpec(a.shape, lambda i: (0,) * a.ndim)
    return pl.pallas_call(
        functools.partial(_conv_kernel, tiles_per_seq=seq // tm),
        grid=(t // tm,),
        in_specs=[pl.BlockSpec((tm, d), lambda i: (i, 0)), full(wc), full(cw), full(cb), full(g), full(b)],
        out_specs=pl.BlockSpec((tm, c), lambda i: (i, 0)),
        out_shape=jax.ShapeDtypeStruct((t, c), BF16),
        scratch_shapes=[pltpu.VMEM((tm + CONV_HALO, c), F32), pltpu.VMEM((tm, c), F32)],
        compiler_params=_cparams(("arbitrary",)),
        name="conv",
    )(x2, wc, cw, cb, g, b)


def _attn_kernel(q_ref, k_ref, v_ref, o_ref, *, scale):
    tq = q_ref.shape[0]
    i = pl.program_id(2)
    q = q_ref[...]

    def block(j, carry, masked):
        m, l, acc = carry
        r0 = pl.multiple_of(j * tq, tq)
        kb = k_ref[pl.ds(r0, tq), :]
        vb = v_ref[pl.ds(r0, tq), :]
        s = lax.dot_general(q, kb, (((1,), (1,)), ((), ())), preferred_element_type=F32) * scale
        if masked:
            row = lax.broadcasted_iota(I32, (tq, tq), 0)
            col = lax.broadcasted_iota(I32, (tq, tq), 1)
            s = jnp.where(col <= row, s, -jnp.inf)
        m_new = jnp.maximum(m, jnp.max(s, axis=1, keepdims=True))
        alpha = jnp.exp(m - m_new)
        p = jnp.exp(s - m_new)
        l = alpha * l + jnp.sum(p, axis=1, keepdims=True)
        acc = alpha * acc + jnp.dot(p.astype(BF16), vb, preferred_element_type=F32)
        return m_new, l, acc

    init = (jnp.full((tq, 1), -jnp.inf, F32), jnp.zeros((tq, 1), F32), jnp.zeros((tq, V_HEAD_DIM), F32))
    carry = lax.fori_loop(0, i, lambda j, c: block(j, c, False), init)
    _, l, acc = block(i, carry, True)
    o_ref[...] = (acc / l).astype(BF16)


def _attn(q, k, v, batch, seq):
    tq = TQ_ATTN
    nq = seq // tq
    scale = (QK_NOPE_DIM + QK_ROPE_DIM) ** -0.5
    return pl.pallas_call(
        functools.partial(_attn_kernel, scale=scale),
        grid=(batch, N_HEADS, nq),
        in_specs=[pl.BlockSpec((tq, HEAD_QK_PAD), lambda b, h, i: (b * nq + i, h)),
                  pl.BlockSpec((seq, HEAD_QK_PAD), lambda b, h, i: (b, h)),
                  pl.BlockSpec((seq, V_HEAD_DIM), lambda b, h, i: (b, h))],
        out_specs=pl.BlockSpec((tq, V_HEAD_DIM), lambda b, h, i: (b * nq + i, h)),
        out_shape=jax.ShapeDtypeStruct((batch * seq, N_HEADS * V_HEAD_DIM), BF16),
        compiler_params=_cparams(("arbitrary", "arbitrary", "arbitrary")),
        name="attn",
    )(q, k, v)


def _pack_bf16_pairs(v):
    c = v.shape[1] // 2
    lo = pltpu.bitcast(v[:, :c].astype(BF16).astype(F32), U32)
    hi = pltpu.bitcast(v[:, c:].astype(BF16).astype(F32), U32)
    return (hi & jnp.uint32(0xFFFF0000)) | (lo >> 16)


def _unpack_bf16_pairs(p):
    lo = pltpu.bitcast(p << 16, F32).astype(BF16)
    hi = pltpu.bitcast(p & jnp.uint32(0xFFFF0000), F32).astype(BF16)
    return jnp.concatenate([lo, hi], axis=1)


def _oproj_kernel(a_ref, c_ref, x_ref, woa_ref, woc_ref, g_ref, b_ref, wr_ref,
                  h_ref, hp_ref, lg_ref, *, alpha):
    mix = jnp.dot(a_ref[...], woa_ref[...], preferred_element_type=F32)
    mix = mix + jnp.dot(c_ref[...], woc_ref[...], preferred_element_type=F32)
    h = _layer_norm(alpha * x_ref[...] + mix, g_ref[...], b_ref[...])
    h_ref[...] = h
    hp_ref[...] = _pack_bf16_pairs(h)
    logits = jnp.dot(h, wr_ref[...], preferred_element_type=F32, precision=lax.Precision.HIGHEST)
    lg_ref[...] = logits.T[:N_EXPERTS, :]


def _oproj(attn, conv, x2, woa, woc, g, b, wr, alpha):
    t, d = x2.shape
    tm = TM_PROJ
    full = lambda a: pl.BlockSpec(a.shape, lambda i: (0,) * a.ndim)
    return pl.pallas_call(
        functools.partial(_oproj_kernel, alpha=alpha),
        grid=(t // tm,),
        in_specs=[pl.BlockSpec((tm, attn.shape[1]), lambda i: (i, 0)),
                  pl.BlockSpec((tm, conv.shape[1]), lambda i: (i, 0)),
                  pl.BlockSpec((tm, d), lambda i: (i, 0)),
                  full(woa), full(woc), full(g), full(b), full(wr)],
        out_specs=[pl.BlockSpec((tm, d), lambda i: (i, 0)),
                   pl.BlockSpec((tm, d // 2), lambda i: (i, 0)),
                   pl.BlockSpec((N_EXPERTS, tm), lambda i: (0, i))],
        out_shape=[jax.ShapeDtypeStruct((t, d), F32),
                   jax.ShapeDtypeStruct((t, d // 2), U32),
                   jax.ShapeDtypeStruct((N_EXPERTS, t), F32)],
        compiler_params=_cparams(("arbitrary",)),
        name="oproj",
    )(attn, conv, x2, woa, woc, g, b, wr)


def _route_kernel(lg_ref, bias_ref, eidx_ref, rank_ref, wts_ref, cnt_ref, carry_ref):
    ne, tm = lg_ref.shape
    i = pl.program_id(0)

    @pl.when(i == 0)
    def _():
        carry_ref[...] = jnp.zeros(carry_ref.shape, F32)

    scores = _sigmoid(lg_ref[...])
    biased = scores + bias_ref[...]
    neg = -jnp.inf
    sub8 = lax.broadcasted_iota(I32, (GROUP_SIZE, tm), 0)
    gscore = []
    for g in range(N_EXPERT_GROUPS):
        blk = biased[g * GROUP_SIZE:(g + 1) * GROUP_SIZE, :]
        m1 = jnp.max(blk, axis=0, keepdims=True)
        i1 = jnp.min(jnp.where(blk == m1, sub8, GROUP_SIZE), axis=0, keepdims=True)
        m2 = jnp.max(jnp.where(sub8 == i1, neg, blk), axis=0, keepdims=True)
        gscore.append(m1 + m2)
    kept = []
    for g in range(N_EXPERT_GROUPS):
        beat = jnp.zeros((1, tm), I32)
        for o in range(N_EXPERT_GROUPS):
            if o < g:
                beat = beat + (gscore[o] >= gscore[g]).astype(I32)
            elif o > g:
                beat = beat + (gscore[o] > gscore[g]).astype(I32)
        kept.append(jnp.where(beat < TOPK_GROUPS, biased[g * GROUP_SIZE:(g + 1) * GROUP_SIZE, :], neg))
    cur = jnp.concatenate(kept, axis=0)
    sub = lax.broadcasted_iota(I32, (ne, tm), 0)
    sel_any = jnp.zeros((ne, tm), F32)
    picks = []
    wsum = jnp.zeros((1, tm), F32)
    for k in range(TOP_K):
        m = jnp.max(cur, axis=0, keepdims=True)
        ei = jnp.min(jnp.where(cur == m, sub, ne), axis=0, keepdims=True)
        sel = sub == ei
        w = jnp.sum(jnp.where(sel, scores, 0.0), axis=0, keepdims=True)
        cur = jnp.where(sel, neg, cur)
        sel_any = sel_any + sel.astype(F32)
        wsum = wsum + w
        picks.append((ei, sel, w))
    r = lax.broadcasted_iota(I32, (tm, tm), 0)
    c = lax.broadcasted_iota(I32, (tm, tm), 1)
    upper = (r <= c).astype(BF16)
    cum = jnp.dot(sel_any.astype(BF16), upper, preferred_element_type=F32)
    carry = carry_ref[:, 0:1]
    excl = cum - sel_any + carry
    total = carry + jnp.sum(sel_any, axis=1, keepdims=True)
    carry_ref[...] = jnp.broadcast_to(total, carry_ref.shape)
    cnt_ref[...] = jnp.broadcast_to(total, cnt_ref.shape)
    denom = wsum + 1e-20
    eidx_ref[...] = jnp.zeros(eidx_ref.shape, I32)
    rank_ref[...] = jnp.zeros(rank_ref.shape, I32)
    wts_ref[...] = jnp.zeros(wts_ref.shape, F32)
    for k, (ei, sel, w) in enumerate(picks):
        eidx_ref[k:k + 1, :] = ei
        rank_ref[k:k + 1, :] = jnp.sum(jnp.where(sel, excl, 0.0), axis=0, keepdims=True).astype(I32)
        wts_ref[k:k + 1, :] = w / denom * ROUTED_SCALE


def _route(logits_t, bias):
    ne, t = logits_t.shape
    tm = TM_ROUTE
    return pl.pallas_call(
        _route_kernel,
        grid=(t // tm,),
        in_specs=[pl.BlockSpec((ne, tm), lambda i: (0, i)), pl.BlockSpec((ne, 1), lambda i: (0, 0))],
        out_specs=[pl.BlockSpec((TOP_K_PAD, tm), lambda i: (0, i)),
                   pl.BlockSpec((TOP_K_PAD, tm), lambda i: (0, i)),
                   pl.BlockSpec((TOP_K_PAD, tm), lambda i: (0, i)),
                   pl.BlockSpec((ne, LANES), lambda i: (0, 0))],
        out_shape=[jax.ShapeDtypeStruct((TOP_K_PAD, t), I32),
                   jax.ShapeDtypeStruct((TOP_K_PAD, t), I32),
                   jax.ShapeDtypeStruct((TOP_K_PAD, t), F32),
                   jax.ShapeDtypeStruct((ne, LANES), F32)],
        scratch_shapes=[pltpu.VMEM((ne, LANES), F32)],
        compiler_params=_cparams(("arbitrary",)),
        name="route",
    )(logits_t, bias)


def _dispatch_kernel(zstart_ref, slots_hbm, hp_hbm, xs_hbm, slot_smem, zero_vmem, sem, zsem, *, tc):
    i = pl.program_id(0)
    te = zero_vmem.shape[0]

    def zero_copy(e):
        start = pl.multiple_of(jnp.maximum(zstart_ref[e], 0), te)
        return pltpu.make_async_copy(zero_vmem, xs_hbm.at[pl.ds(start, te)], zsem)

    @pl.when(i == 0)
    def _():
        zero_vmem[...] = jnp.zeros(zero_vmem.shape, U32)

        def zstart(e, c):
            @pl.when(zstart_ref[e] >= 0)
            def _():
                zero_copy(e).start()
            return c

        def zwait(e, c):
            @pl.when(zstart_ref[e] >= 0)
            def _():
                zero_copy(e).wait()
            return c

        lax.fori_loop(0, N_EXPERTS, zstart, 0)
        lax.fori_loop(0, N_EXPERTS, zwait, 0)

    n = tc * TOP_K_PAD
    cp = pltpu.make_async_copy(slots_hbm.at[pl.ds(pl.multiple_of(i * n, n), n)], slot_smem, zsem)
    cp.start()
    cp.wait()

    def row_copy(t, k):
        slot = slot_smem[t * TOP_K_PAD + k]
        return pltpu.make_async_copy(hp_hbm.at[pl.ds(i * tc + t, 1)], xs_hbm.at[pl.ds(slot, 1)], sem)

    def issue(t, c):
        for k in range(TOP_K):
            row_copy(t, k).start()
        return c

    def drain(t, c):
        for k in range(TOP_K):
            row_copy(t, k).wait()
        return c

    lax.fori_loop(0, tc, issue, 0)
    lax.fori_loop(0, tc, drain, 0)


def _dispatch(zstart, slots_flat, hp, n_rows):
    t, c = hp.shape
    tc = TC_DISP
    return pl.pallas_call(
        functools.partial(_dispatch_kernel, tc=tc),
        grid_spec=pltpu.PrefetchScalarGridSpec(
            num_scalar_prefetch=1,
            grid=(t // tc,),
            in_specs=[pl.BlockSpec(memory_space=pl.ANY), pl.BlockSpec(memory_space=pl.ANY)],
            out_specs=pl.BlockSpec(memory_space=pl.ANY),
            scratch_shapes=[pltpu.SMEM((tc * TOP_K_PAD,), I32), pltpu.VMEM((TE_ROWS, c), U32),
                            pltpu.SemaphoreType.DMA, pltpu.SemaphoreType.DMA],
        ),
        out_shape=jax.ShapeDtypeStruct((n_rows, c), U32),
        compiler_params=_cparams(("arbitrary",)),
        name="dispatch",
    )(zstart, slots_flat, hp)


def _experts_kernel(te_ref, nv_ref, xs_ref, wg_ref, wu_ref, wd_ref, y_ref, wgb, wub, wdb):
    i = pl.program_id(0)
    prev = te_ref[jnp.maximum(i - 1, 0)]
    valid = i < nv_ref[0]

    @pl.when(valid & ((i == 0) | (te_ref[i] != prev)))
    def _():
        wgb[...] = wg_ref[...].astype(BF16)
        wub[...] = wu_ref[...].astype(BF16)
        wdb[...] = wd_ref[...].astype(BF16)

    @pl.when(valid)
    def _():
        xs = _unpack_bf16_pairs(xs_ref[...])
        g = jnp.dot(xs, wgb[...], preferred_element_type=F32)
        u = jnp.dot(xs, wub[...], preferred_element_type=F32)
        hid = (g * _sigmoid(g) * u).astype(BF16)
        y_ref[...] = jnp.dot(hid, wdb[...], preferred_element_type=F32)


def _experts(tile_expert, n_valid, xs, w_gate, w_up, w_down):
    n_rows, c = xs.shape
    ne, d, ff = w_gate.shape
    te = TE_ROWS
    row_map = lambda i, te_ref, nv_ref: (jnp.minimum(i, nv_ref[0] - 1), 0)
    w_map = lambda i, te_ref, nv_ref: (te_ref[i], 0, 0)
    return pl.pallas_call(
        _experts_kernel,
        grid_spec=pltpu.PrefetchScalarGridSpec(
            num_scalar_prefetch=2,
            grid=(n_rows // te,),
            in_specs=[pl.BlockSpec((te, c), row_map),
                      pl.BlockSpec((None, d, ff), w_map),
                      pl.BlockSpec((None, d, ff), w_map),
                      pl.BlockSpec((None, ff, d), w_map)],
            out_specs=pl.BlockSpec((te, d), row_map),
            scratch_shapes=[pltpu.VMEM((d, ff), BF16), pltpu.VMEM((d, ff), BF16), pltpu.VMEM((ff, d), BF16)],
        ),
        out_shape=jax.ShapeDtypeStruct((n_rows, d), F32),
        compiler_params=_cparams(("arbitrary",)),
        name="experts",
    )(tile_expert, n_valid, xs, w_gate, w_up, w_down)


def _combine_kernel(slots_hbm, y_hbm, h_ref, wts_ref, wsg_ref, wsu_ref, wsd_ref, g_ref, b_ref,
                    o_ref, slot_smem, gbuf, sem, ssem, *, alpha):
    tm, d = o_ref.shape
    i = pl.program_id(0)
    n = tm * TOP_K_PAD
    cp = pltpu.make_async_copy(slots_hbm.at[pl.ds(pl.multiple_of(i * n, n), n)], slot_smem, ssem)
    cp.start()
    cp.wait()

    def row_copy(t, k):
        slot = slot_smem[t * TOP_K_PAD + k]
        return pltpu.make_async_copy(y_hbm.at[pl.ds(slot, 1)], gbuf.at[k, pl.ds(t, 1)], sem)

    def issue(t, c):
        for k in range(TOP_K):
            row_copy(t, k).start()
        return c

    def drain(t, c):
        for k in range(TOP_K):
            row_copy(t, k).wait()
        return c

    lax.fori_loop(0, tm, issue, 0)
    h = h_ref[...]
    hb = h.astype(BF16)
    sg = jnp.dot(hb, wsg_ref[...], preferred_element_type=F32)
    su = jnp.dot(hb, wsu_ref[...], preferred_element_type=F32)
    shared = jnp.dot((sg * _sigmoid(sg) * su).astype(BF16), wsd_ref[...], preferred_element_type=F32)
    lax.fori_loop(0, tm, drain, 0)
    w = wts_ref[...]
    routed = w[:, 0:1] * gbuf[0]
    for k in range(1, TOP_K):
        routed = routed + w[:, k:k + 1] * gbuf[k]
    o_ref[...] = _layer_norm(alpha * h + (routed + shared), g_ref[...], b_ref[...])


def _combine(slots_flat, y, h, wts_t, wsg, wsu, wsd, g, b, alpha):
    t, d = h.shape
    tm = TM_COMB
    full = lambda a: pl.BlockSpec(a.shape, lambda i: (0,) * a.ndim)
    return pl.pallas_call(
        functools.partial(_combine_kernel, alpha=alpha),
        grid=(t // tm,),
        in_specs=[pl.BlockSpec(memory_space=pl.ANY), pl.BlockSpec(memory_space=pl.ANY),
                  pl.BlockSpec((tm, d), lambda i: (i, 0)),
                  pl.BlockSpec((tm, TOP_K_PAD), lambda i: (i, 0)),
                  full(wsg), full(wsu), full(wsd), full(g), full(b)],
        out_specs=pl.BlockSpec((tm, d), lambda i: (i, 0)),
        out_shape=jax.ShapeDtypeStruct((t, d), F32),
        scratch_shapes=[pltpu.SMEM((tm * TOP_K_PAD,), I32), pltpu.VMEM((TOP_K, tm, d), F32),
                        pltpu.SemaphoreType.DMA, pltpu.SemaphoreType.DMA],
        compiler_params=_cparams(("arbitrary",)),
        name="combine",
    )(slots_flat, y, h, wts_t, wsg, wsu, wsd, g, b)


def _prep_attention_weights(w_in, w_uq, w_ukv):
    d = w_in.shape[0]
    ql = w_uq.shape[0]
    kvl = w_ukv.shape[0]
    half = QK_ROPE_DIM // 2
    pad = LANES - QK_ROPE_DIM
    w_kr = w_in[:, ql + kvl:ql + kvl + QK_ROPE_DIM]
    w_kr_rot = jnp.concatenate([-w_kr[:, half:], w_kr[:, :half]], axis=1)
    zpad = jnp.zeros((d, pad), w_in.dtype)
    wa = jnp.concatenate([w_in[:, :ql + kvl], w_kr, zpad, w_kr_rot, zpad], axis=1).astype(BF16)
    wc = w_in[:, ql + kvl + QK_ROPE_DIM:].astype(BF16)
    uq = w_uq.reshape(ql, N_HEADS, QK_NOPE_DIM + QK_ROPE_DIM)
    uq_nope, uq_rope = uq[..., :QK_NOPE_DIM], uq[..., QK_NOPE_DIM:]
    zq = jnp.zeros((ql, N_HEADS, pad), w_uq.dtype)
    wuq = jnp.concatenate([uq_nope, uq_rope, zq], axis=-1).reshape(ql, N_HEADS * HEAD_QK_PAD).astype(BF16)
    uq_rot = jnp.concatenate([-uq_rope[..., half:], uq_rope[..., :half], zq], axis=-1)
    wuqr = uq_rot.reshape(ql, N_HEADS * LANES).astype(BF16)
    ukv = w_ukv.reshape(kvl, N_HEADS, QK_NOPE_DIM + V_HEAD_DIM)
    wukv = jnp.concatenate([ukv[..., :QK_NOPE_DIM].reshape(kvl, -1), ukv[..., QK_NOPE_DIM:].reshape(kvl, -1)],
                           axis=1).astype(BF16)
    return wa, wc, wuq, wuqr, wukv


def _layer(h, pos2, invf4, batch, seq, alpha, w_in, q_norm_g, w_uq, kv_norm_g, w_ukv, conv_w, conv_b,
           conv_ln_g, conv_ln_b, w_o, ln1_g, ln1_b, w_router, router_bias, w_gate, w_up, w_down,
           ws_gate, ws_up, ws_down, ln2_g, ln2_b):
    t, d = h.shape
    row = lambda a: a.reshape(1, -1)
    wa, wc, wuq, wuqr, wukv = _prep_attention_weights(w_in, w_uq, w_ukv)
    q, k, v = _qkv(h, pos2, invf4, wa, row(q_norm_g), row(kv_norm_g), wuq, wuqr, wukv)
    conv = _conv(h, wc, conv_w, row(conv_b), row(conv_ln_g), row(conv_ln_b), seq)
    attn = _attn(q, k, v, batch, seq)
    aw = attn.shape[1]
    wr = jnp.pad(w_router.astype(F32), ((0, 0), (0, LANES - N_EXPERTS)))
    h1, h1p, logits_t = _oproj(attn, conv, h, w_o[:aw].astype(BF16), w_o[aw:].astype(BF16),
                               row(ln1_g), row(ln1_b), wr, alpha)
    eidx, rank, wts, cnt = _route(logits_t, router_bias.astype(F32).reshape(N_EXPERTS, 1))
    counts = cnt[:, 0].astype(I32)
    padded = (counts + TE_ROWS - 1) // TE_ROWS * TE_ROWS
    pend = jnp.cumsum(padded)
    poff = pend - padded
    n_rows = t * TOP_K + N_EXPERTS * TE_ROWS
    n_tiles = n_rows // TE_ROWS
    n_valid = (pend[-1] // TE_ROWS).astype(I32).reshape(1)
    tile_row = jnp.minimum(jnp.arange(n_tiles, dtype=I32), n_valid[0] - 1) * TE_ROWS
    tile_expert = jnp.minimum(jnp.searchsorted(pend, tile_row, side="right"), N_EXPERTS - 1).astype(I32)
    zstart = jnp.where(padded > 0, pend - TE_ROWS, -1).astype(I32)
    slots_flat = (poff[eidx] + rank).T.reshape(-1)
    xs = _dispatch(zstart, slots_flat, h1p, n_rows)
    y = _experts(tile_expert, n_valid, xs, w_gate, w_up, w_down)
    return _combine(slots_flat, y, h1, wts.T, ws_gate.astype(BF16), ws_up.astype(BF16),
                    ws_down.astype(BF16), row(ln2_g), row(ln2_b), alpha)


def kernel(x, positions, w_in, q_norm_g, w_uq, kv_norm_g, w_ukv, conv_w, conv_b, conv_ln_g, conv_ln_b, w_o, ln1_g, ln1_b, w_router, router_bias, w_gate, w_up, w_down, ws_gate, ws_up, ws_down, ln2_g, ln2_b):
    batch, seq, d = x.shape
    depth = w_in.shape[0]
    alpha = (2.0 * depth) ** 0.25
    inv_freq = ROPE_BASE ** (-jnp.arange(0, QK_ROPE_DIM, 2, dtype=F32) / QK_ROPE_DIM)
    invf4 = jnp.tile(inv_freq, LANES // inv_freq.shape[0]).reshape(1, LANES)
    pos2 = positions.reshape(batch * seq, 1)
    h = x.reshape(batch * seq, d)
    for l in range(depth):
        h = _layer(h, pos2, invf4, batch, seq, alpha, w_in[l], q_norm_g[l], w_uq[l], kv_norm_g[l], w_ukv[l],
                   conv_w[l], conv_b[l], conv_ln_g[l], conv_ln_b[l], w_o[l], ln1_g[l], ln1_b[l],
                   w_router[l], router_bias[l], w_gate[l], w_up[l], w_down[l],
                   ws_gate[l], ws_up[l], ws_down[l], ln2_g[l], ln2_b[l])
    return h.reshape(batch, seq, d)
```

```python
import functools

import jax
import jax.numpy as jnp
from jax import lax
from jax.experimental import pallas as pl
from jax.experimental.pallas import tpu as pltpu

F32 = jnp.float32
BF16 = jnp.bfloat16
I32 = jnp.int32
U32 = jnp.uint32

N_HEADS = 8
QK_NOPE_DIM = 128
QK_ROPE_DIM = 64
V_HEAD_DIM = 128
HEAD_QK_PAD = 256
CONV_KERNEL = 31
N_EXPERTS = 64
TOP_K = 6
TOP_K_PAD = 8
N_EXPERT_GROUPS = 8
GROUP_SIZE = N_EXPERTS // N_EXPERT_GROUPS
TOPK_GROUPS = 4
ROUTED_SCALE = 2.5
ROPE_BASE = 10000.0
LN_EPS = 1e-5
RMS_EPS = 1e-6

LANES = 128
CONV_HALO = 32

TM_PROJ = 256
TM_CONV = 512
TQ_ATTN = 512
TM_ROUTE = 512
TE_ROWS = 256
TM_COMB = 256
TC_DISP = 1024
CONV_ROWS = 64
CONV_COLS = 256
VMEM_LIMIT = 56 * 1024 * 1024


def _cparams(sem):
    return pltpu.CompilerParams(dimension_semantics=sem, vmem_limit_bytes=VMEM_LIMIT)


def _sigmoid(v):
    return 1.0 / (1.0 + jnp.exp(-v))


def _layer_norm(v, g, b):
    mu = jnp.mean(v, axis=-1, keepdims=True)
    d = v - mu
    var = jnp.mean(d * d, axis=-1, keepdims=True)
    return d * lax.rsqrt(var + LN_EPS) * g + b


def _rms_norm(v, g):
    ms = jnp.mean(v * v, axis=-1, keepdims=True)
    return v * lax.rsqrt(ms + RMS_EPS) * g


def _qkv_kernel(x_ref, pos_ref, invf_ref, wa_ref, qg_ref, kvg_ref, wuq_ref, wuqr_ref, wukv_ref,
                q_ref, k_ref, v_ref):
    ql = qg_ref.shape[1]
    kvl = kvg_ref.shape[1]
    xb = x_ref[...].astype(BF16)
    lat = jnp.dot(xb, wa_ref[...], preferred_element_type=F32)
    ang = pos_ref[...].astype(F32) * invf_ref[...]
    cos = jnp.cos(ang)
    sin = jnp.sin(ang)
    cq = _rms_norm(lat[:, :ql], qg_ref[...]).astype(BF16)
    ckv = _rms_norm(lat[:, ql:ql + kvl], kvg_ref[...]).astype(BF16)
    kr = lat[:, ql + kvl:ql + kvl + LANES] * cos + lat[:, ql + kvl + LANES:ql + kvl + 2 * LANES] * sin
    kr = kr.astype(BF16)
    q = jnp.dot(cq, wuq_ref[...], preferred_element_type=F32)
    qrot = jnp.dot(cq, wuqr_ref[...], preferred_element_type=F32)
    kv = jnp.dot(ckv, wukv_ref[...], preferred_element_type=F32)
    for h in range(N_HEADS):
        c0 = h * HEAD_QK_PAD
        q_ref[:, c0:c0 + LANES] = q[:, c0:c0 + LANES].astype(BF16)
        q_ref[:, c0 + LANES:c0 + 2 * LANES] = (
            q[:, c0 + LANES:c0 + 2 * LANES] * cos + qrot[:, h * LANES:(h + 1) * LANES] * sin).astype(BF16)
        k_ref[:, c0:c0 + LANES] = kv[:, h * LANES:(h + 1) * LANES].astype(BF16)
        k_ref[:, c0 + LANES:c0 + 2 * LANES] = kr
    v_ref[...] = kv[:, N_HEADS * QK_NOPE_DIM:].astype(BF16)


def _qkv(x2, pos2, invf4, wa, qg, kvg, wuq, wuqr, wukv):
    t, d = x2.shape
    tm = TM_PROJ
    full = lambda a: pl.BlockSpec(a.shape, lambda i: (0,) * a.ndim)
    return pl.pallas_call(
        _qkv_kernel,
        grid=(t // tm,),
        in_specs=[pl.BlockSpec((tm, d), lambda i: (i, 0)),
                  pl.BlockSpec((tm, 1), lambda i: (i, 0)),
                  full(invf4), full(wa), full(qg), full(kvg), full(wuq), full(wuqr), full(wukv)],
        out_specs=[pl.BlockSpec((tm, N_HEADS * HEAD_QK_PAD), lambda i: (i, 0)),
                   pl.BlockSpec((tm, N_HEADS * HEAD_QK_PAD), lambda i: (i, 0)),
                   pl.BlockSpec((tm, N_HEADS * V_HEAD_DIM), lambda i: (i, 0))],
        out_shape=[jax.ShapeDtypeStruct((t, N_HEADS * HEAD_QK_PAD), BF16),
                   jax.ShapeDtypeStruct((t, N_HEADS * HEAD_QK_PAD), BF16),
                   jax.ShapeDtypeStruct((t, N_HEADS * V_HEAD_DIM), BF16)],
        compiler_params=_cparams(("arbitrary",)),
        name="qkv",
    )(x2, pos2, invf4, wa, qg, kvg, wuq, wuqr, wukv)


def _conv_kernel(x_ref, wc_ref, cw_ref, cb_ref, g_ref, b_ref, o_ref, ubuf, ybuf, *, tiles_per_seq):
    tm, cw = o_ref.shape
    i = pl.program_id(0)
    xb = x_ref[...].astype(BF16)
    ag = jnp.dot(xb, wc_ref[...], preferred_element_type=F32)
    u = ag[:, :cw] * _sigmoid(ag[:, cw:])

    @pl.when(i % tiles_per_seq == 0)
    def _():
        ubuf[0:CONV_HALO, :] = jnp.zeros((CONV_HALO, cw), F32)

    ubuf[CONV_HALO:CONV_HALO + tm, :] = u
    shift0 = CONV_HALO - (CONV_KERNEL - 1)
    n_col = cw // CONV_COLS

    def chunk(c, carry):
        r0 = pl.multiple_of((c // n_col) * CONV_ROWS, CONV_ROWS)
        c0 = pl.multiple_of((c % n_col) * CONV_COLS, CONV_COLS)
        win = ubuf[pl.ds(r0, CONV_ROWS + CONV_HALO), pl.ds(c0, CONV_COLS)]
        acc = jnp.zeros((CONV_ROWS, CONV_COLS), F32)
        for k in range(CONV_KERNEL):
            wk = cw_ref[k:k + 1, pl.ds(c0, CONV_COLS)]
            acc = acc + wk * win[shift0 + k:shift0 + k + CONV_ROWS, :]
        ybuf[pl.ds(r0, CONV_ROWS), pl.ds(c0, CONV_COLS)] = acc
        return carry

    lax.fori_loop(0, (tm // CONV_ROWS) * n_col, chunk, 0)
    ubuf[0:CONV_HALO, :] = ubuf[tm:tm + CONV_HALO, :]
    y = _layer_norm(ybuf[...] + cb_ref[...], g_ref[...], b_ref[...])
    o_ref[...] = (y * _sigmoid(y)).astype(BF16)


def _conv(x2, wc, cw, cb, g, b, seq):
    t, d = x2.shape
    tm = TM_CONV
    c = cw.shape[1]
    full = lambda a: pl.BlockSpec(a.shape, lambda i: (0,) * a.ndim)
    return pl.pallas_call(
        functools.partial(_conv_kernel, tiles_per_seq=seq // tm),
        grid=(t // tm,),
        in_specs=[pl.BlockSpec((tm, d), lambda i: (i, 0)), full(wc), full(cw), full(cb), full(g), full(b)],
        out_specs=pl.BlockSpec((tm, c), lambda i: (i, 0)),
        out_shape=jax.ShapeDtypeStruct((t, c), BF16),
        scratch_shapes=[pltpu.VMEM((tm + CONV_HALO, c), F32), pltpu.VMEM((tm, c), F32)],
        compiler_params=_cparams(("arbitrary",)),
        name="conv",
    )(x2, wc, cw, cb, g, b)


def _attn_kernel(q_ref, k_ref, v_ref, o_ref, *, scale):
    tq = q_ref.shape[0]
    i = pl.program_id(2)
    q = q_ref[...]

    def block(j, carry, masked):
        m, l, acc = carry
        r0 = pl.multiple_of(j * tq, tq)
        kb = k_ref[pl.ds(r0, tq), :]
        vb = v_ref[pl.ds(r0, tq), :]
        s = lax.dot_general(q, kb, (((1,), (1,)), ((), ())), preferred_element_type=F32) * scale
        if masked:
            row = lax.broadcasted_iota(I32, (tq, tq), 0)
            col = lax.broadcasted_iota(I32, (tq, tq), 1)
            s = jnp.where(col <= row, s, -jnp.inf)
        m_new = jnp.maximum(m, jnp.max(s, axis=1, keepdims=True))
        alpha = jnp.exp(m - m_new)
        p = jnp.exp(s - m_new)
        l = alpha * l + jnp.sum(p, axis=1, keepdims=True)
        acc = alpha * acc + jnp.dot(p.astype(BF16), vb, preferred_element_type=F32)
        return m_new, l, acc

    init = (jnp.full((tq, 1), -jnp.inf, F32), jnp.zeros((tq, 1), F32), jnp.zeros((tq, V_HEAD_DIM), F32))
    carry = lax.fori_loop(0, i, lambda j, c: block(j, c, False), init)
    _, l, acc = block(i, carry, True)
    o_ref[...] = (acc / l).astype(BF16)


def _attn(q, k, v, batch, seq):
    tq = TQ_ATTN
    nq = seq // tq
    scale = (QK_NOPE_DIM + QK_ROPE_DIM) ** -0.5
    return pl.pallas_call(
        functools.partial(_attn_kernel, scale=scale),
        grid=(batch, N_HEADS, nq),
        in_specs=[pl.BlockSpec((tq, HEAD_QK_PAD), lambda b, h, i: (b * nq + i, h)),
                  pl.BlockSpec((seq, HEAD_QK_PAD), lambda b, h, i: (b, h)),
                  pl.BlockSpec((seq, V_HEAD_DIM), lambda b, h, i: (b, h))],
        out_specs=pl.BlockSpec((tq, V_HEAD_DIM), lambda b, h, i: (b * nq + i, h)),
        out_shape=jax.ShapeDtypeStruct((batch * seq, N_HEADS * V_HEAD_DIM), BF16),
        compiler_params=_cparams(("arbitrary", "arbitrary", "arbitrary")),
        name="attn",
    )(q, k, v)


def _pack_bf16_pairs(v):
    c = v.shape[1] // 2
    lo = pltpu.bitcast(v[:, :c].astype(BF16).astype(F32), U32)
    hi = pltpu.bitcast(v[:, c:].astype(BF16).astype(F32), U32)
    return (hi & jnp.uint32(0xFFFF0000)) | (lo >> 16)


def _unpack_bf16_pairs(p):
    lo = pltpu.bitcast(p << 16, F32).astype(BF16)
    hi = pltpu.bitcast(p & jnp.uint32(0xFFFF0000), F32).astype(BF16)
    return jnp.concatenate([lo, hi], axis=1)


def _oproj_kernel(a_ref, c_ref, x_ref, woa_ref, woc_ref, g_ref, b_ref, wr_ref,
                  h_ref, hp_ref, lg_ref, *, alpha):
    mix = jnp.dot(a_ref[...], woa_ref[...], preferred_element_type=F32)
    mix = mix + jnp.dot(c_ref[...], woc_ref[...], preferred_element_type=F32)
    h = _layer_norm(alpha * x_ref[...] + mix, g_ref[...], b_ref[...])
    h_ref[...] = h
    hp_ref[...] = _pack_bf16_pairs(h)
    logits = jnp.dot(h, wr_ref[...], preferred_element_type=F32, precision=lax.Precision.HIGHEST)
    lg_ref[...] = logits.T[:N_EXPERTS, :]


def _oproj(attn, conv, x2, woa, woc, g, b, wr, alpha):
    t, d = x2.shape
    tm = TM_PROJ
    full = lambda a: pl.BlockSpec(a.shape, lambda i: (0,) * a.ndim)
    return pl.pallas_call(
        functools.partial(_oproj_kernel, alpha=alpha),
        grid=(t // tm,),
        in_specs=[pl.BlockSpec((tm, attn.shape[1]), lambda i: (i, 0)),
                  pl.BlockSpec((tm, conv.shape[1]), lambda i: (i, 0)),
                  pl.BlockSpec((tm, d), lambda i: (i, 0)),
                  full(woa), full(woc), full(g), full(b), full(wr)],
        out_specs=[pl.BlockSpec((tm, d), lambda i: (i, 0)),
                   pl.BlockSpec((tm, d // 2), lambda i: (i, 0)),
                   pl.BlockSpec((N_EXPERTS, tm), lambda i: (0, i))],
        out_shape=[jax.ShapeDtypeStruct((t, d), F32),
                   jax.ShapeDtypeStruct((t, d // 2), U32),
                   jax.ShapeDtypeStruct((N_EXPERTS, t), F32)],
        compiler_params=_cparams(("arbitrary",)),
        name="oproj",
    )(attn, conv, x2, woa, woc, g, b, wr)


def _route_kernel(lg_ref, bias_ref, eidx_ref, rank_ref, wts_ref, cnt_ref, carry_ref):
    ne, tm = lg_ref.shape
    i = pl.program_id(0)

    @pl.when(i == 0)
    def _():
        carry_ref[...] = jnp.zeros(carry_ref.shape, F32)

    scores = _sigmoid(lg_ref[...])
    biased = scores + bias_ref[...]
    neg = -jnp.inf
    sub8 = lax.broadcasted_iota(I32, (GROUP_SIZE, tm), 0)
    gscore = []
    for g in range(N_EXPERT_GROUPS):
        blk = biased[g * GROUP_SIZE:(g + 1) * GROUP_SIZE, :]
        m1 = jnp.max(blk, axis=0, keepdims=True)
        i1 = jnp.min(jnp.where(blk == m1, sub8, GROUP_SIZE), axis=0, keepdims=True)
        m2 = jnp.max(jnp.where(sub8 == i1, neg, blk), axis=0, keepdims=True)
        gscore.append(m1 + m2)
    kept = []
    for g in range(N_EXPERT_GROUPS):
        beat = jnp.zeros((1, tm), I32)
        for o in range(N_EXPERT_GROUPS):
            if o < g:
                beat = beat + (gscore[o] >= gscore[g]).astype(I32)
            elif o > g:
                beat = beat + (gscore[o] > gscore[g]).astype(I32)
        kept.append(jnp.where(beat < TOPK_GROUPS, biased[g * GROUP_SIZE:(g + 1) * GROUP_SIZE, :], neg))
    cur = jnp.concatenate(kept, axis=0)
    sub = lax.broadcasted_iota(I32, (ne, tm), 0)
    sel_any = jnp.zeros((ne, tm), F32)
    picks = []
    wsum = jnp.zeros((1, tm), F32)
    for k in range(TOP_K):
        m = jnp.max(cur, axis=0, keepdims=True)
        ei = jnp.min(jnp.where(cur == m, sub, ne), axis=0, keepdims=True)
        sel = sub == ei
        w = jnp.sum(jnp.where(sel, scores, 0.0), axis=0, keepdims=True)
        cur = jnp.where(sel, neg, cur)
        sel_any = sel_any + sel.astype(F32)
        wsum = wsum + w
        picks.append((ei, sel, w))
    r = lax.broadcasted_iota(I32, (tm, tm), 0)
    c = lax.broadcasted_iota(I32, (tm, tm), 1)
    upper = (r <= c).astype(BF16)
    cum = jnp.dot(sel_any.astype(BF16), upper, preferred_element_type=F32)
    carry = carry_ref[:, 0:1]
    excl = cum - sel_any + carry
    total = carry + jnp.sum(sel_any, axis=1, keepdims=True)
    carry_ref[...] = jnp.broadcast_to(total, carry_ref.shape)
    cnt_ref[...] = jnp.broadcast_to(total, cnt_ref.shape)
    denom = wsum + 1e-20
    eidx_ref[...] = jnp.zeros(eidx_ref.shape, I32)
    rank_ref[...] = jnp.zeros(rank_ref.shape, I32)
    wts_ref[...] = jnp.zeros(wts_ref.shape, F32)
    for k, (ei, sel, w) in enumerate(picks):
        eidx_ref[k:k + 1, :] = ei
        rank_ref[k:k + 1, :] = jnp.sum(jnp.where(sel, excl, 0.0), axis=0, keepdims=True).astype(I32)
        wts_ref[k:k + 1, :] = w / denom * ROUTED_SCALE


def _route(logits_t, bias):
    ne, t = logits_t.shape
    tm = TM_ROUTE
    return pl.pallas_call(
        _route_kernel,
        grid=(t // tm,),
        in_specs=[pl.BlockSpec((ne, tm), lambda i: (0, i)), pl.BlockSpec((ne, 1), lambda i: (0, 0))],
        out_specs=[pl.BlockSpec((TOP_K_PAD, tm), lambda i: (0, i)),
                   pl.BlockSpec((TOP_K_PAD, tm), lambda i: (0, i)),
                   pl.BlockSpec((TOP_K_PAD, tm), lambda i: (0, i)),
                   pl.BlockSpec((ne, LANES), lambda i: (0, 0))],
        out_shape=[jax.ShapeDtypeStruct((TOP_K_PAD, t), I32),
                   jax.ShapeDtypeStruct((TOP_K_PAD, t), I32),
                   jax.ShapeDtypeStruct((TOP_K_PAD, t), F32),
                   jax.ShapeDtypeStruct((ne, LANES), F32)],
        scratch_shapes=[pltpu.VMEM((ne, LANES), F32)],
        compiler_params=_cparams(("arbitrary",)),
        name="route",
    )(logits_t, bias)


def _dispatch_kernel(zstart_ref, slots_hbm, hp_ref, xs_hbm, slot_smem, zero_vmem, sem, zsem):
    i = pl.program_id(0)
    te = zero_vmem.shape[0]
    tc = hp_ref.shape[0]

    def zero_copy(e):
        start = pl.multiple_of(jnp.maximum(zstart_ref[e], 0), te)
        return pltpu.make_async_copy(zero_vmem, xs_hbm.at[pl.ds(start, te)], zsem)

    @pl.when(i == 0)
    def _():
        zero_vmem[...] = jnp.zeros(zero_vmem.shape, U32)

        def zstart(e, c):
            @pl.when(zstart_ref[e] >= 0)
            def _():
                zero_copy(e).start()
            return c

        def zwait(e, c):
            @pl.when(zstart_ref[e] >= 0)
            def _():
                zero_copy(e).wait()
            return c

        lax.fori_loop(0, N_EXPERTS, zstart, 0)
        lax.fori_loop(0, N_EXPERTS, zwait, 0)

    n = tc * TOP_K_PAD
    cp = pltpu.make_async_copy(slots_hbm.at[pl.ds(pl.multiple_of(i * n, n), n)], slot_smem, zsem)
    cp.start()
    cp.wait()

    def row_copy(t, k):
        slot = slot_smem[t * TOP_K_PAD + k]
        return pltpu.make_async_copy(hp_ref.at[pl.ds(t, 1)], xs_hbm.at[pl.ds(slot, 1)], sem)

    def issue(t, c):
        for k in range(TOP_K):
            row_copy(t, k).start()
        return c

    def drain(t, c):
        for k in range(TOP_K):
            row_copy(t, k).wait()
        return c

    lax.fori_loop(0, tc, issue, 0)
    lax.fori_loop(0, tc, drain, 0)


def _dispatch(zstart, slots_flat, hp, n_rows):
    t, c = hp.shape
    tc = TC_DISP
    return pl.pallas_call(
        _dispatch_kernel,
        grid_spec=pltpu.PrefetchScalarGridSpec(
            num_scalar_prefetch=1,
            grid=(t // tc,),
            in_specs=[pl.BlockSpec(memory_space=pl.ANY), pl.BlockSpec((tc, c), lambda i, z: (i, 0))],
            out_specs=pl.BlockSpec(memory_space=pl.ANY),
            scratch_shapes=[pltpu.SMEM((tc * TOP_K_PAD,), I32), pltpu.VMEM((TE_ROWS, c), U32),
                            pltpu.SemaphoreType.DMA, pltpu.SemaphoreType.DMA],
        ),
        out_shape=jax.ShapeDtypeStruct((n_rows, c), U32),
        compiler_params=_cparams(("arbitrary",)),
        name="dispatch",
    )(zstart, slots_flat, hp)


def _experts_kernel(te_ref, nv_ref, xs_ref, wg_ref, wu_ref, wd_ref, y_ref, wgb, wub, wdb):
    i = pl.program_id(0)
    prev = te_ref[jnp.maximum(i - 1, 0)]
    valid = i < nv_ref[0]

    @pl.when(valid & ((i == 0) | (te_ref[i] != prev)))
    def _():
        wgb[...] = wg_ref[...].astype(BF16)
        wub[...] = wu_ref[...].astype(BF16)
        wdb[...] = wd_ref[...].astype(BF16)

    @pl.when(valid)
    def _():
        xs = _unpack_bf16_pairs(xs_ref[...])
        g = jnp.dot(xs, wgb[...], preferred_element_type=F32)
        u = jnp.dot(xs, wub[...], preferred_element_type=F32)
        hid = (g * _sigmoid(g) * u).astype(BF16)
        y_ref[...] = jnp.dot(hid, wdb[...], preferred_element_type=F32)


def _experts(tile_expert, n_valid, xs, w_gate, w_up, w_down):
    n_rows, c = xs.shape
    ne, d, ff = w_gate.shape
    te = TE_ROWS
    row_map = lambda i, te_ref, nv_ref: (jnp.minimum(i, nv_ref[0] - 1), 0)
    w_map = lambda i, te_ref, nv_ref: (te_ref[i], 0, 0)
    return pl.pallas_call(
        _experts_kernel,
        grid_spec=pltpu.PrefetchScalarGridSpec(
            num_scalar_prefetch=2,
            grid=(n_rows // te,),
            in_specs=[pl.BlockSpec((te, c), row_map),
                      pl.BlockSpec((None, d, ff), w_map),
                      pl.BlockSpec((None, d, ff), w_map),
                      pl.BlockSpec((None, ff, d), w_map)],
            out_specs=pl.BlockSpec((te, d), row_map),
            scratch_shapes=[pltpu.VMEM((d, ff), BF16), pltpu.VMEM((d, ff), BF16), pltpu.VMEM((ff, d), BF16)],
        ),
        out_shape=jax.ShapeDtypeStruct((n_rows, d), F32),
        compiler_params=_cparams(("arbitrary",)),
        name="experts",
    )(tile_expert, n_valid, xs, w_gate, w_up, w_down)


def _combine_kernel(slots_hbm, y_hbm, h_ref, wts_ref, wsg_ref, wsu_ref, wsd_ref, g_ref, b_ref,
                    o_ref, slot_smem, gbuf, sem, ssem, *, alpha):
    tm, d = o_ref.shape
    i = pl.program_id(0)
    n = tm * TOP_K_PAD
    cp = pltpu.make_async_copy(slots_hbm.at[pl.ds(pl.multiple_of(i * n, n), n)], slot_smem, ssem)
    cp.start()
    cp.wait()

    def row_copy(t, k):
        slot = slot_smem[t * TOP_K_PAD + k]
        return pltpu.make_async_copy(y_hbm.at[pl.ds(slot, 1)], gbuf.at[k, pl.ds(t, 1)], sem)

    def issue(t, c):
        for k in range(TOP_K):
            row_copy(t, k).start()
        return c

    def drain(t, c):
        for k in range(TOP_K):
            row_copy(t, k).wait()
        return c

    lax.fori_loop(0, tm, issue, 0)
    h = h_ref[...]
    hb = h.astype(BF16)
    sg = jnp.dot(hb, wsg_ref[...], preferred_element_type=F32)
    su = jnp.dot(hb, wsu_ref[...], preferred_element_type=F32)
    shared = jnp.dot((sg * _sigmoid(sg) * su).astype(BF16), wsd_ref[...], preferred_element_type=F32)
    lax.fori_loop(0, tm, drain, 0)
    w = wts_ref[...]
    routed = w[:, 0:1] * gbuf[0]
    for k in range(1, TOP_K):
        routed = routed + w[:, k:k + 1] * gbuf[k]
    o_ref[...] = _layer_norm(alpha * h + (routed + shared), g_ref[...], b_ref[...])


def _combine(slots_flat, y, h, wts_t, wsg, wsu, wsd, g, b, alpha):
    t, d = h.shape
    tm = TM_COMB
    full = lambda a: pl.BlockSpec(a.shape, lambda i: (0,) * a.ndim)
    return pl.pallas_call(
        functools.partial(_combine_kernel, alpha=alpha),
        grid=(t // tm,),
        in_specs=[pl.BlockSpec(memory_space=pl.ANY), pl.BlockSpec(memory_space=pl.ANY),
                  pl.BlockSpec((tm, d), lambda i: (i, 0)),
                  pl.BlockSpec((tm, TOP_K_PAD), lambda i: (i, 0)),
                  full(wsg), full(wsu), full(wsd), full(g), full(b)],
        out_specs=pl.BlockSpec((tm, d), lambda i: (i, 0)),
        out_shape=jax.ShapeDtypeStruct((t, d), F32),
        scratch_shapes=[pltpu.SMEM((tm * TOP_K_PAD,), I32), pltpu.VMEM((TOP_K, tm, d), F32),
                        pltpu.SemaphoreType.DMA, pltpu.SemaphoreType.DMA],
        compiler_params=_cparams(("arbitrary",)),
        name="combine",
    )(slots_flat, y, h, wts_t, wsg, wsu, wsd, g, b)


def _prep_attention_weights(w_in, w_uq, w_ukv):
    d = w_in.shape[0]
    ql = w_uq.shape[0]
    kvl = w_ukv.shape[0]
    half = QK_ROPE_DIM // 2
    pad = LANES - QK_ROPE_DIM
    w_kr = w_in[:, ql + kvl:ql + kvl + QK_ROPE_DIM]
    w_kr_rot = jnp.concatenate([-w_kr[:, half:], w_kr[:, :half]], axis=1)
    zpad = jnp.zeros((d, pad), w_in.dtype)
    wa = jnp.concatenate([w_in[:, :ql + kvl], w_kr, zpad, w_kr_rot, zpad], axis=1).astype(BF16)
    wc = w_in[:, ql + kvl + QK_ROPE_DIM:].astype(BF16)
    uq = w_uq.reshape(ql, N_HEADS, QK_NOPE_DIM + QK_ROPE_DIM)
    uq_nope, uq_rope = uq[..., :QK_NOPE_DIM], uq[..., QK_NOPE_DIM:]
    zq = jnp.zeros((ql, N_HEADS, pad), w_uq.dtype)
    wuq = jnp.concatenate([uq_nope, uq_rope, zq], axis=-1).reshape(ql, N_HEADS * HEAD_QK_PAD).astype(BF16)
    uq_rot = jnp.concatenate([-uq_rope[..., half:], uq_rope[..., :half], zq], axis=-1)
    wuqr = uq_rot.reshape(ql, N_HEADS * LANES).astype(BF16)
    ukv = w_ukv.reshape(kvl, N_HEADS, QK_NOPE_DIM + V_HEAD_DIM)
    wukv = jnp.concatenate([ukv[..., :QK_NOPE_DIM].reshape(kvl, -1), ukv[..., QK_NOPE_DIM:].reshape(kvl, -1)],
                           axis=1).astype(BF16)
    return wa, wc, wuq, wuqr, wukv


def _layer(h, pos2, invf4, batch, seq, alpha, w_in, q_norm_g, w_uq, kv_norm_g, w_ukv, conv_w, conv_b,
           conv_ln_g, conv_ln_b, w_o, ln1_g, ln1_b, w_router, router_bias, w_gate, w_up, w_down,
           ws_gate, ws_up, ws_down, ln2_g, ln2_b):
    t, d = h.shape
    row = lambda a: a.reshape(1, -1)
    wa, wc, wuq, wuqr, wukv = _prep_attention_weights(w_in, w_uq, w_ukv)
    q, k, v = _qkv(h, pos2, invf4, wa, row(q_norm_g), row(kv_norm_g), wuq, wuqr, wukv)
    conv = _conv(h, wc, conv_w, row(conv_b), row(conv_ln_g), row(conv_ln_b), seq)
    attn = _attn(q, k, v, batch, seq)
    aw = attn.shape[1]
    wr = jnp.pad(w_router.astype(F32), ((0, 0), (0, LANES - N_EXPERTS)))
    h1, h1p, logits_t = _oproj(attn, conv, h, w_o[:aw].astype(BF16), w_o[aw:].astype(BF16),
                               row(ln1_g), row(ln1_b), wr, alpha)
    eidx, rank, wts, cnt = _route(logits_t, router_bias.astype(F32).reshape(N_EXPERTS, 1))
    counts = cnt[:, 0].astype(I32)
    padded = (counts + TE_ROWS - 1) // TE_ROWS * TE_ROWS
    pend = jnp.cumsum(padded)
    poff = pend - padded
    n_rows = t * TOP_K + N_EXPERTS * TE_ROWS
    n_tiles = n_rows // TE_ROWS
    n_valid = (pend[-1] // TE_ROWS).astype(I32).reshape(1)
    tile_row = jnp.minimum(jnp.arange(n_tiles, dtype=I32), n_valid[0] - 1) * TE_ROWS
    tile_expert = jnp.sum((pend[None, :] <= tile_row[:, None]).astype(I32), axis=1)
    tile_expert = jnp.minimum(tile_expert, N_EXPERTS - 1)
    zstart = jnp.where(padded > 0, pend - TE_ROWS, -1).astype(I32)
    expert_ids = jnp.arange(N_EXPERTS, dtype=I32)[:, None, None]
    slot_base = jnp.sum(jnp.where(eidx[None] == expert_ids, poff[:, None, None], 0), axis=0)
    slots_flat = (slot_base + rank).T.reshape(-1)
    xs = _dispatch(zstart, slots_flat, h1p, n_rows)
    y = _experts(tile_expert, n_valid, xs, w_gate, w_up, w_down)
    return _combine(slots_flat, y, h1, wts.T, ws_gate.astype(BF16), ws_up.astype(BF16),
                    ws_down.astype(BF16), row(ln2_g), row(ln2_b), alpha)


def kernel(x, positions, w_in, q_norm_g, w_uq, kv_norm_g, w_ukv, conv_w, conv_b, conv_ln_g, conv_ln_b, w_o, ln1_g, ln1_b, w_router, router_bias, w_gate, w_up, w_down, ws_gate, ws_up, ws_down, ln2_g, ln2_b):
    batch, seq, d = x.shape
    depth = w_in.shape[0]
    alpha = (2.0 * depth) ** 0.25
    inv_freq = ROPE_BASE ** (-jnp.arange(0, QK_ROPE_DIM, 2, dtype=F32) / QK_ROPE_DIM)
    invf4 = jnp.tile(inv_freq, LANES // inv_freq.shape[0]).reshape(1, LANES)
    pos2 = positions.reshape(batch * seq, 1)
    h = x.reshape(batch * seq, d)
    for l in range(depth):
        h = _layer(h, pos2, invf4, batch, seq, alpha, w_in[l], q_norm_g[l], w_uq[l], kv_norm_g[l], w_ukv[l],
                   conv_w[l], conv_b[l], conv_ln_g[l], conv_ln_b[l], w_o[l], ln1_g[l], ln1_b[l],
                   w_router[l], router_bias[l], w_gate[l], w_up[l], w_down[l],
                   ws_gate[l], ws_up[l], ws_down[l], ln2_g[l], ln2_b[l])
    return h.reshape(batch, seq, d)
```

```python
import functools

import jax
import jax.numpy as jnp
from jax import lax
from jax.experimental import pallas as pl
from jax.experimental.pallas import tpu as pltpu

F32 = jnp.float32
BF16 = jnp.bfloat16
I32 = jnp.int32
U32 = jnp.uint32

N_HEADS = 8
QK_NOPE_DIM = 128
QK_ROPE_DIM = 64
V_HEAD_DIM = 128
HEAD_QK_PAD = 256
CONV_KERNEL = 31
N_EXPERTS = 64
TOP_K = 6
TOP_K_PAD = 8
N_EXPERT_GROUPS = 8
GROUP_SIZE = N_EXPERTS // N_EXPERT_GROUPS
TOPK_GROUPS = 4
ROUTED_SCALE = 2.5
ROPE_BASE = 10000.0
LN_EPS = 1e-5
RMS_EPS = 1e-6

LANES = 128
SUBLANES = 8
CONV_HALO = 32

TM_PROJ = 256
TM_CONV = 512
TQ_ATTN = 512
HEADS_PER_ATTN_STEP = 2
ATTN_STRIP = 512
TM_ROUTE = 512
TE_ROWS = 256
TM_COMB = 256
TC_DISP = 1024
WAIT_ROWS = 128
CONV_ROWS = 64
CONV_COLS = 256
VMEM_LIMIT = 56 * 1024 * 1024


def _cparams(sem):
    return pltpu.CompilerParams(dimension_semantics=sem, vmem_limit_bytes=VMEM_LIMIT)


def _sigmoid(v):
    return 1.0 / (1.0 + jnp.exp(-v))


def _layer_norm(v, g, b):
    mu = jnp.mean(v, axis=-1, keepdims=True)
    d = v - mu
    var = jnp.mean(d * d, axis=-1, keepdims=True)
    return d * lax.rsqrt(var + LN_EPS) * g + b


def _rms_norm(v, g):
    ms = jnp.mean(v * v, axis=-1, keepdims=True)
    return v * lax.rsqrt(ms + RMS_EPS) * g


def _qkv_kernel(x_ref, pos_ref, invf_ref, wa_ref, qg_ref, kvg_ref, wuq_ref, wuqr_ref, wukv_ref,
                q_ref, k_ref, v_ref):
    ql = qg_ref.shape[1]
    kvl = kvg_ref.shape[1]
    xb = x_ref[...].astype(BF16)
    lat = jnp.dot(xb, wa_ref[...], preferred_element_type=F32)
    ang = pos_ref[...].astype(F32) * invf_ref[...]
    cos = jnp.cos(ang)
    sin = jnp.sin(ang)
    cq = _rms_norm(lat[:, :ql], qg_ref[...]).astype(BF16)
    ckv = _rms_norm(lat[:, ql:ql + kvl], kvg_ref[...]).astype(BF16)
    kr = lat[:, ql + kvl:ql + kvl + LANES] * cos + lat[:, ql + kvl + LANES:ql + kvl + 2 * LANES] * sin
    kr = kr.astype(BF16)
    q = jnp.dot(cq, wuq_ref[...], preferred_element_type=F32)
    qrot = jnp.dot(cq, wuqr_ref[...], preferred_element_type=F32)
    kv = jnp.dot(ckv, wukv_ref[...], preferred_element_type=F32)
    for h in range(N_HEADS):
        c0 = h * HEAD_QK_PAD
        q_ref[:, c0:c0 + LANES] = q[:, c0:c0 + LANES].astype(BF16)
        q_ref[:, c0 + LANES:c0 + 2 * LANES] = (
            q[:, c0 + LANES:c0 + 2 * LANES] * cos + qrot[:, h * LANES:(h + 1) * LANES] * sin).astype(BF16)
        k_ref[:, c0:c0 + LANES] = kv[:, h * LANES:(h + 1) * LANES].astype(BF16)
        k_ref[:, c0 + LANES:c0 + 2 * LANES] = kr
    v_ref[...] = kv[:, N_HEADS * QK_NOPE_DIM:].astype(BF16)


def _qkv(x2, pos2, invf4, wa, qg, kvg, wuq, wuqr, wukv):
    t, d = x2.shape
    tm = TM_PROJ
    full = lambda a: pl.BlockSpec(a.shape, lambda i: (0,) * a.ndim)
    return pl.pallas_call(
        _qkv_kernel,
        grid=(t // tm,),
        in_specs=[pl.BlockSpec((tm, d), lambda i: (i, 0)),
                  pl.BlockSpec((tm, 1), lambda i: (i, 0)),
                  full(invf4), full(wa), full(qg), full(kvg), full(wuq), full(wuqr), full(wukv)],
        out_specs=[pl.BlockSpec((tm, N_HEADS * HEAD_QK_PAD), lambda i: (i, 0)),
                   pl.BlockSpec((tm, N_HEADS * HEAD_QK_PAD), lambda i: (i, 0)),
                   pl.BlockSpec((tm, N_HEADS * V_HEAD_DIM), lambda i: (i, 0))],
        out_shape=[jax.ShapeDtypeStruct((t, N_HEADS * HEAD_QK_PAD), BF16),
                   jax.ShapeDtypeStruct((t, N_HEADS * HEAD_QK_PAD), BF16),
                   jax.ShapeDtypeStruct((t, N_HEADS * V_HEAD_DIM), BF16)],
        compiler_params=_cparams(("arbitrary",)),
        name="qkv",
    )(x2, pos2, invf4, wa, qg, kvg, wuq, wuqr, wukv)


def _conv_kernel(x_ref, wc_ref, cw_ref, cb_ref, g_ref, b_ref, o_ref, ubuf, ybuf, shbuf, *, tiles_per_seq):
    tm, cw = o_ref.shape
    i = pl.program_id(0)
    xb = x_ref[...].astype(BF16)
    ag = jnp.dot(xb, wc_ref[...], preferred_element_type=F32)
    u = ag[:, :cw] * _sigmoid(ag[:, cw:])

    @pl.when(i % tiles_per_seq == 0)
    def _():
        ubuf[0:CONV_HALO, :] = jnp.zeros((CONV_HALO, cw), F32)

    ubuf[CONV_HALO:CONV_HALO + tm, :] = u
    shift0 = CONV_HALO - (CONV_KERNEL - 1)
    n_col = cw // CONV_COLS

    def chunk(c, carry):
        r0 = pl.multiple_of((c // n_col) * CONV_ROWS, CONV_ROWS)
        c0 = pl.multiple_of((c % n_col) * CONV_COLS, CONV_COLS)
        win = ubuf[pl.ds(r0, CONV_ROWS + CONV_HALO), pl.ds(c0, CONV_COLS)]
        for r in range(1, SUBLANES):
            shbuf[r] = win[r:r + CONV_ROWS + CONV_HALO - SUBLANES, :]
        acc = jnp.zeros((CONV_ROWS, CONV_COLS), F32)
        for k in range(CONV_KERNEL):
            wk = cw_ref[k:k + 1, pl.ds(c0, CONV_COLS)]
            r, j = (shift0 + k) % SUBLANES, (shift0 + k) // SUBLANES
            if r == 0:
                tap = win[j * SUBLANES:j * SUBLANES + CONV_ROWS, :]
            else:
                tap = shbuf[r, j * SUBLANES:j * SUBLANES + CONV_ROWS, :]
            acc = acc + wk * tap
        ybuf[pl.ds(r0, CONV_ROWS), pl.ds(c0, CONV_COLS)] = acc
        return carry

    lax.fori_loop(0, (tm // CONV_ROWS) * n_col, chunk, 0)
    ubuf[0:CONV_HALO, :] = ubuf[tm:tm + CONV_HALO, :]
    y = _layer_norm(ybuf[...] + cb_ref[...], g_ref[...], b_ref[...])
    o_ref[...] = (y * _sigmoid(y)).astype(BF16)


def _conv(x2, wc, cw, cb, g, b, seq):
    t, d = x2.shape
    tm = TM_CONV
    c = cw.shape[1]
    full = lambda a: pl.BlockSpec(a.shape, lambda i: (0,) * a.ndim)
    return pl.pallas_call(
        functools.partial(_conv_kernel, tiles_per_seq=seq // tm),
        grid=(t // tm,),
        in_specs=[pl.BlockSpec((tm, d), lambda i: (i, 0)), full(wc), full(cw), full(cb), full(g), full(b)],
        out_specs=pl.BlockSpec((tm, c), lambda i: (i, 0)),
        out_shape=jax.ShapeDtypeStruct((t, c), BF16),
        scratch_shapes=[pltpu.VMEM((tm + CONV_HALO, c), F32), pltpu.VMEM((tm, c), F32),
                        pltpu.VMEM((SUBLANES, CONV_ROWS + CONV_HALO - SUBLANES, CONV_COLS), F32)],
        compiler_params=_cparams(("arbitrary",)),
        name="conv",
    )(x2, wc, cw, cb, g, b)


def _attn_kernel(q_ref, k_ref, v_ref, o_ref, *, scale):
    tq = q_ref.shape[0]
    heads = q_ref.shape[1] // HEAD_QK_PAD
    i = pl.program_id(2)
    c = scale * 1.4426950408889634

    ts = ATTN_STRIP
    chains = [(h, r) for h in range(heads) for r in range(tq // ts)]

    def block(j, carry, masked):
        r0 = pl.multiple_of(j * tq, tq)
        out = []
        for (h, r), (m, l, acc) in zip(chains, carry):
            q = q_ref[r * ts:(r + 1) * ts, h * HEAD_QK_PAD:(h + 1) * HEAD_QK_PAD]
            kb = k_ref[pl.ds(r0, tq), h * HEAD_QK_PAD:(h + 1) * HEAD_QK_PAD]
            vb = v_ref[pl.ds(r0, tq), h * V_HEAD_DIM:(h + 1) * V_HEAD_DIM]
            s = lax.dot_general(q, kb, (((1,), (1,)), ((), ())), preferred_element_type=F32)
            if masked:
                row = lax.broadcasted_iota(I32, (ts, tq), 0) + r * ts
                col = lax.broadcasted_iota(I32, (ts, tq), 1)
                s = jnp.where(col <= row, s, -jnp.inf)
            m_new = jnp.maximum(m, jnp.max(s, axis=1, keepdims=True))
            alpha = jnp.exp2((m - m_new) * c)
            p = jnp.exp2((s - m_new) * c)
            l = alpha * l + jnp.sum(p, axis=1, keepdims=True)
            acc = alpha * acc + jnp.dot(p.astype(BF16), vb, preferred_element_type=F32)
            out.append((m_new, l, acc))
        return tuple(out)

    init = tuple((jnp.full((ts, 1), -jnp.inf, F32), jnp.zeros((ts, 1), F32),
                  jnp.zeros((ts, V_HEAD_DIM), F32)) for _ in chains)
    carry = lax.fori_loop(0, i, lambda j, cr: block(j, cr, False), init)
    final = block(i, carry, True)
    for (h, r), (_, l, acc) in zip(chains, final):
        o_ref[r * ts:(r + 1) * ts, h * V_HEAD_DIM:(h + 1) * V_HEAD_DIM] = (acc / l).astype(BF16)


def _attn(q, k, v, batch, seq):
    tq = TQ_ATTN
    nq = seq // tq
    hb = HEADS_PER_ATTN_STEP
    scale = (QK_NOPE_DIM + QK_ROPE_DIM) ** -0.5
    return pl.pallas_call(
        functools.partial(_attn_kernel, scale=scale),
        grid=(batch, N_HEADS // hb, nq),
        in_specs=[pl.BlockSpec((tq, hb * HEAD_QK_PAD), lambda b, h, i: (b * nq + i, h)),
                  pl.BlockSpec((seq, hb * HEAD_QK_PAD), lambda b, h, i: (b, h)),
                  pl.BlockSpec((seq, hb * V_HEAD_DIM), lambda b, h, i: (b, h))],
        out_specs=pl.BlockSpec((tq, hb * V_HEAD_DIM), lambda b, h, i: (b * nq + i, h)),
        out_shape=jax.ShapeDtypeStruct((batch * seq, N_HEADS * V_HEAD_DIM), BF16),
        compiler_params=_cparams(("arbitrary", "arbitrary", "arbitrary")),
        name="attn",
    )(q, k, v)


def _pack_bf16_pairs(v):
    c = v.shape[1] // 2
    lo = pltpu.bitcast(v[:, :c].astype(BF16).astype(F32), U32)
    hi = pltpu.bitcast(v[:, c:].astype(BF16).astype(F32), U32)
    return (hi & jnp.uint32(0xFFFF0000)) | (lo >> 16)


def _unpack_bf16_pairs(p):
    lo = pltpu.bitcast(p << 16, F32).astype(BF16)
    hi = pltpu.bitcast(p & jnp.uint32(0xFFFF0000), F32).astype(BF16)
    return jnp.concatenate([lo, hi], axis=1)


def _oproj_kernel(a_ref, c_ref, x_ref, woa_ref, woc_ref, g_ref, b_ref, wr_ref,
                  h_ref, hp_ref, lg_ref, *, alpha):
    mix = jnp.dot(a_ref[...], woa_ref[...], preferred_element_type=F32)
    mix = mix + jnp.dot(c_ref[...], woc_ref[...], preferred_element_type=F32)
    h = _layer_norm(alpha * x_ref[...] + mix, g_ref[...], b_ref[...])
    h_ref[...] = h
    hp_ref[...] = _pack_bf16_pairs(h)
    h_hi = h.astype(BF16)
    h_lo = (h - h_hi.astype(F32)).astype(BF16)
    logits = jnp.dot(jnp.concatenate([h_hi, h_lo, h_hi], axis=1), wr_ref[...], preferred_element_type=F32)
    lg_ref[...] = logits.T[:N_EXPERTS, :]


def _oproj(attn, conv, x2, woa, woc, g, b, wr, alpha):
    t, d = x2.shape
    tm = TM_PROJ
    full = lambda a: pl.BlockSpec(a.shape, lambda i: (0,) * a.ndim)
    return pl.pallas_call(
        functools.partial(_oproj_kernel, alpha=alpha),
        grid=(t // tm,),
        in_specs=[pl.BlockSpec((tm, attn.shape[1]), lambda i: (i, 0)),
                  pl.BlockSpec((tm, conv.shape[1]), lambda i: (i, 0)),
                  pl.BlockSpec((tm, d), lambda i: (i, 0)),
                  full(woa), full(woc), full(g), full(b), full(wr)],
        out_specs=[pl.BlockSpec((tm, d), lambda i: (i, 0)),
                   pl.BlockSpec((tm, d // 2), lambda i: (i, 0)),
                   pl.BlockSpec((N_EXPERTS, tm), lambda i: (0, i))],
        out_shape=[jax.ShapeDtypeStruct((t, d), F32),
                   jax.ShapeDtypeStruct((t, d // 2), U32),
                   jax.ShapeDtypeStruct((N_EXPERTS, t), F32)],
        compiler_params=_cparams(("arbitrary",)),
        name="oproj",
    )(attn, conv, x2, woa, woc, g, b, wr)


def _route_kernel(lg_ref, bias_ref, eidx_ref, rank_ref, wts_ref, cnt_ref, carry_ref):
    ne, tm = lg_ref.shape
    i = pl.program_id(0)

    @pl.when(i == 0)
    def _():
        carry_ref[...] = jnp.zeros(carry_ref.shape, F32)

    scores = _sigmoid(lg_ref[...])
    biased = scores + bias_ref[...]
    neg = -jnp.inf
    sub8 = lax.broadcasted_iota(I32, (GROUP_SIZE, tm), 0)
    gscore = []
    for g in range(N_EXPERT_GROUPS):
        blk = biased[g * GROUP_SIZE:(g + 1) * GROUP_SIZE, :]
        m1 = jnp.max(blk, axis=0, keepdims=True)
        i1 = jnp.min(jnp.where(blk == m1, sub8, GROUP_SIZE), axis=0, keepdims=True)
        m2 = jnp.max(jnp.where(sub8 == i1, neg, blk), axis=0, keepdims=True)
        gscore.append(m1 + m2)
    kept = []
    for g in range(N_EXPERT_GROUPS):
        beat = jnp.zeros((1, tm), I32)
        for o in range(N_EXPERT_GROUPS):
            if o < g:
                beat = beat + (gscore[o] >= gscore[g]).astype(I32)
            elif o > g:
                beat = beat + (gscore[o] > gscore[g]).astype(I32)
        kept.append(jnp.where(beat < TOPK_GROUPS, biased[g * GROUP_SIZE:(g + 1) * GROUP_SIZE, :], neg))
    cur = jnp.concatenate(kept, axis=0)
    sub = lax.broadcasted_iota(I32, (ne, tm), 0)
    sel_any = jnp.zeros((ne, tm), F32)
    picks = []
    wsum = jnp.zeros((1, tm), F32)
    for k in range(TOP_K):
        m = jnp.max(cur, axis=0, keepdims=True)
        ei = jnp.min(jnp.where(cur == m, sub, ne), axis=0, keepdims=True)
        sel = sub == ei
        w = jnp.sum(jnp.where(sel, scores, 0.0), axis=0, keepdims=True)
        cur = jnp.where(sel, neg, cur)
        sel_any = sel_any + sel.astype(F32)
        wsum = wsum + w
        picks.append((ei, sel, w))
    r = lax.broadcasted_iota(I32, (tm, tm), 0)
    c = lax.broadcasted_iota(I32, (tm, tm), 1)
    upper = (r <= c).astype(BF16)
    cum = jnp.dot(sel_any.astype(BF16), upper, preferred_element_type=F32)
    carry = carry_ref[:, 0:1]
    excl = cum - sel_any + carry
    total = carry + jnp.sum(sel_any, axis=1, keepdims=True)
    carry_ref[...] = jnp.broadcast_to(total, carry_ref.shape)
    cnt_ref[...] = jnp.broadcast_to(total, cnt_ref.shape)
    denom = wsum + 1e-20
    eidx_ref[...] = jnp.zeros(eidx_ref.shape, I32)
    rank_ref[...] = jnp.zeros(rank_ref.shape, I32)
    wts_ref[...] = jnp.zeros(wts_ref.shape, F32)
    for k, (ei, sel, w) in enumerate(picks):
        eidx_ref[k:k + 1, :] = ei
        rank_ref[k:k + 1, :] = jnp.sum(jnp.where(sel, excl, 0.0), axis=0, keepdims=True).astype(I32)
        wts_ref[k:k + 1, :] = w / denom * ROUTED_SCALE


def _route(logits_t, bias):
    ne, t = logits_t.shape
    tm = TM_ROUTE
    return pl.pallas_call(
        _route_kernel,
        grid=(t // tm,),
        in_specs=[pl.BlockSpec((ne, tm), lambda i: (0, i)), pl.BlockSpec((ne, 1), lambda i: (0, 0))],
        out_specs=[pl.BlockSpec((TOP_K_PAD, tm), lambda i: (0, i)),
                   pl.BlockSpec((TOP_K_PAD, tm), lambda i: (0, i)),
                   pl.BlockSpec((TOP_K_PAD, tm), lambda i: (0, i)),
                   pl.BlockSpec((ne, LANES), lambda i: (0, 0))],
        out_shape=[jax.ShapeDtypeStruct((TOP_K_PAD, t), I32),
                   jax.ShapeDtypeStruct((TOP_K_PAD, t), I32),
                   jax.ShapeDtypeStruct((TOP_K_PAD, t), F32),
                   jax.ShapeDtypeStruct((ne, LANES), F32)],
        scratch_shapes=[pltpu.VMEM((ne, LANES), F32)],
        compiler_params=_cparams(("arbitrary",)),
        name="route",
    )(logits_t, bias)


def _wait_rows(src_rows, dst_rows, sem, n_rows):
    def body(_, c):
        pltpu.make_async_copy(src_rows, dst_rows, sem).wait()
        return c
    lax.fori_loop(0, n_rows // WAIT_ROWS, body, 0)


def _dispatch_kernel(zfill_ref, slots_ref, hp_ref, xs_hbm, zero_vmem, sem, zsem):
    i = pl.program_id(0)
    te = zero_vmem.shape[0]
    tc = hp_ref.shape[0]
    n_tiles = xs_hbm.shape[0] // te

    def zero_copy(tile):
        return pltpu.make_async_copy(zero_vmem, xs_hbm.at[pl.ds(pl.multiple_of(tile * te, te), te)], zsem)

    @pl.when(i == 0)
    def _():
        zero_vmem[...] = jnp.zeros(zero_vmem.shape, U32)

        def zstart(tile, c):
            @pl.when(zfill_ref[tile] != 0)
            def _():
                zero_copy(tile).start()
            return c

        def zwait(tile, c):
            @pl.when(zfill_ref[tile] != 0)
            def _():
                zero_copy(tile).wait()
            return c

        lax.fori_loop(0, n_tiles, zstart, 0)
        lax.fori_loop(0, n_tiles, zwait, 0)

    base = i * (tc * TOP_K_PAD)

    def issue(t, c):
        for k in range(TOP_K):
            slot = slots_ref[base + t * TOP_K_PAD + k]
            pltpu.make_async_copy(hp_ref.at[pl.ds(t, 1)], xs_hbm.at[pl.ds(slot, 1)], sem).start()
        return c

    lax.fori_loop(0, tc, issue, 0)
    _wait_rows(hp_ref.at[pl.ds(0, WAIT_ROWS)], xs_hbm.at[pl.ds(0, WAIT_ROWS)], sem, tc * TOP_K)


def _dispatch(zfill, slots_flat, hp, n_rows):
    t, c = hp.shape
    tc = TC_DISP
    return pl.pallas_call(
        _dispatch_kernel,
        grid_spec=pltpu.PrefetchScalarGridSpec(
            num_scalar_prefetch=2,
            grid=(t // tc,),
            in_specs=[pl.BlockSpec((tc, c), lambda i, z, s: (i, 0))],
            out_specs=pl.BlockSpec(memory_space=pl.ANY),
            scratch_shapes=[pltpu.VMEM((TE_ROWS, c), U32), pltpu.SemaphoreType.DMA, pltpu.SemaphoreType.DMA],
        ),
        out_shape=jax.ShapeDtypeStruct((n_rows, c), U32),
        compiler_params=_cparams(("arbitrary",)),
        name="dispatch",
    )(zfill, slots_flat, hp)


def _experts_kernel(te_ref, nv_ref, nx_ref, xs_ref, wg_hbm, wu_hbm, wd_hbm, y_ref,
                    sg, su, sd, wgb, wub, wdb, sem):
    i = pl.program_id(0)
    e = te_ref[i]
    prev = te_ref[jnp.maximum(i - 1, 0)]
    valid = i < nv_ref[0]

    def fetch(ex):
        return (pltpu.make_async_copy(wg_hbm.at[ex], sg, sem.at[0]),
                pltpu.make_async_copy(wu_hbm.at[ex], su, sem.at[1]),
                pltpu.make_async_copy(wd_hbm.at[ex], sd, sem.at[2]))

    @pl.when(i == 0)
    def _():
        for cp in fetch(e):
            cp.start()

    @pl.when(valid & ((i == 0) | (e != prev)))
    def _():
        for cp in fetch(e):
            cp.wait()
        wgb[...] = sg[...].astype(BF16)
        wub[...] = su[...].astype(BF16)
        wdb[...] = sd[...].astype(BF16)

        @pl.when(nx_ref[i] >= 0)
        def _():
            for cp in fetch(nx_ref[i]):
                cp.start()

    @pl.when(valid)
    def _():
        xs = _unpack_bf16_pairs(xs_ref[...])
        g = jnp.dot(xs, wgb[...], preferred_element_type=F32)
        u = jnp.dot(xs, wub[...], preferred_element_type=F32)
        hid = (g * _sigmoid(g) * u).astype(BF16)
        y_ref[...] = jnp.dot(hid, wdb[...], preferred_element_type=F32)

    @pl.when(jnp.logical_not(valid))
    def _():
        y_ref[...] = jnp.zeros(y_ref.shape, F32)


def _experts(tile_expert, n_valid, next_expert, xs, w_gate, w_up, w_down):
    n_rows, c = xs.shape
    ne, d, ff = w_gate.shape
    te = TE_ROWS
    in_map = lambda i, te_ref, nv_ref, nx_ref: (jnp.minimum(i, nv_ref[0] - 1), 0)
    out_map = lambda i, te_ref, nv_ref, nx_ref: (i, 0)
    hbm = pl.BlockSpec(memory_space=pl.ANY)
    return pl.pallas_call(
        _experts_kernel,
        grid_spec=pltpu.PrefetchScalarGridSpec(
            num_scalar_prefetch=3,
            grid=(n_rows // te,),
            in_specs=[pl.BlockSpec((te, c), in_map), hbm, hbm, hbm],
            out_specs=pl.BlockSpec((te, d), out_map),
            scratch_shapes=[pltpu.VMEM((d, ff), F32), pltpu.VMEM((d, ff), F32), pltpu.VMEM((ff, d), F32),
                            pltpu.VMEM((d, ff), BF16), pltpu.VMEM((d, ff), BF16), pltpu.VMEM((ff, d), BF16),
                            pltpu.SemaphoreType.DMA((3,))],
        ),
        out_shape=jax.ShapeDtypeStruct((n_rows, d), F32),
        compiler_params=_cparams(("arbitrary",)),
        name="experts",
    )(tile_expert, n_valid, next_expert, xs, w_gate, w_up, w_down)


def _combine_kernel(slots_ref, y_hbm, h_ref, wts_ref, wsg_ref, wsu_ref, wsd_ref, g_ref, b_ref,
                    o_ref, gbuf, sem, *, alpha):
    tm, d = o_ref.shape
    i = pl.program_id(0)

    def issue_step(step):
        buf = step % 2
        base = step * (tm * TOP_K_PAD)

        def issue(t, c):
            for k in range(TOP_K):
                slot = slots_ref[base + t * TOP_K_PAD + k]
                pltpu.make_async_copy(y_hbm.at[pl.ds(slot, 1)], gbuf.at[buf, k, pl.ds(t, 1)], sem.at[buf]).start()
            return c

        lax.fori_loop(0, tm, issue, 0)

    @pl.when(i == 0)
    def _():
        issue_step(0)

    @pl.when(i + 1 < pl.num_programs(0))
    def _():
        issue_step(i + 1)

    h = h_ref[...]
    hb = h.astype(BF16)
    sg = jnp.dot(hb, wsg_ref[...], preferred_element_type=F32)
    su = jnp.dot(hb, wsu_ref[...], preferred_element_type=F32)
    shared = jnp.dot((sg * _sigmoid(sg) * su).astype(BF16), wsd_ref[...], preferred_element_type=F32)
    cur = i % 2
    _wait_rows(y_hbm.at[pl.ds(0, WAIT_ROWS)], gbuf.at[cur, 0, pl.ds(0, WAIT_ROWS)], sem.at[cur], tm * TOP_K)
    w = wts_ref[...]
    routed = w[:, 0:1] * gbuf[cur, 0]
    for k in range(1, TOP_K):
        routed = routed + w[:, k:k + 1] * gbuf[cur, k]
    o_ref[...] = _layer_norm(alpha * h + (routed + shared), g_ref[...], b_ref[...])


def _combine(slots_flat, y, h, wts_t, wsg, wsu, wsd, g, b, alpha):
    t, d = h.shape
    tm = TM_COMB
    full = lambda a: pl.BlockSpec(a.shape, lambda i, s: (0,) * a.ndim)
    once = lambda a: pl.BlockSpec(a.shape, lambda i, s: (0,) * a.ndim, pipeline_mode=pl.Buffered(1))
    return pl.pallas_call(
        functools.partial(_combine_kernel, alpha=alpha),
        grid_spec=pltpu.PrefetchScalarGridSpec(
            num_scalar_prefetch=1,
            grid=(t // tm,),
            in_specs=[pl.BlockSpec(memory_space=pl.ANY),
                      pl.BlockSpec((tm, d), lambda i, s: (i, 0)),
                      pl.BlockSpec((tm, TOP_K_PAD), lambda i, s: (i, 0)),
                      once(wsg), once(wsu), once(wsd), full(g), full(b)],
            out_specs=pl.BlockSpec((tm, d), lambda i, s: (i, 0)),
            scratch_shapes=[pltpu.VMEM((2, TOP_K, tm, d), F32), pltpu.SemaphoreType.DMA((2,))],
        ),
        out_shape=jax.ShapeDtypeStruct((t, d), F32),
        compiler_params=_cparams(("arbitrary",)),
        name="combine",
    )(slots_flat, y, h, wts_t, wsg, wsu, wsd, g, b)


def _prep_attention_weights(w_in, w_uq, w_ukv):
    d = w_in.shape[0]
    ql = w_uq.shape[0]
    kvl = w_ukv.shape[0]
    half = QK_ROPE_DIM // 2
    pad = LANES - QK_ROPE_DIM
    w_kr = w_in[:, ql + kvl:ql + kvl + QK_ROPE_DIM]
    w_kr_rot = jnp.concatenate([-w_kr[:, half:], w_kr[:, :half]], axis=1)
    zpad = jnp.zeros((d, pad), w_in.dtype)
    wa = jnp.concatenate([w_in[:, :ql + kvl], w_kr, zpad, w_kr_rot, zpad], axis=1).astype(BF16)
    wc = w_in[:, ql + kvl + QK_ROPE_DIM:].astype(BF16)
    uq = w_uq.reshape(ql, N_HEADS, QK_NOPE_DIM + QK_ROPE_DIM)
    uq_nope, uq_rope = uq[..., :QK_NOPE_DIM], uq[..., QK_NOPE_DIM:]
    zq = jnp.zeros((ql, N_HEADS, pad), w_uq.dtype)
    wuq = jnp.concatenate([uq_nope, uq_rope, zq], axis=-1).reshape(ql, N_HEADS * HEAD_QK_PAD).astype(BF16)
    uq_rot = jnp.concatenate([-uq_rope[..., half:], uq_rope[..., :half], zq], axis=-1)
    wuqr = uq_rot.reshape(ql, N_HEADS * LANES).astype(BF16)
    ukv = w_ukv.reshape(kvl, N_HEADS, QK_NOPE_DIM + V_HEAD_DIM)
    wukv = jnp.concatenate([ukv[..., :QK_NOPE_DIM].reshape(kvl, -1), ukv[..., QK_NOPE_DIM:].reshape(kvl, -1)],
                           axis=1).astype(BF16)
    return wa, wc, wuq, wuqr, wukv


def _layer(h, pos2, invf4, batch, seq, alpha, w_in, q_norm_g, w_uq, kv_norm_g, w_ukv, conv_w, conv_b,
           conv_ln_g, conv_ln_b, w_o, ln1_g, ln1_b, w_router, router_bias, w_gate, w_up, w_down,
           ws_gate, ws_up, ws_down, ln2_g, ln2_b):
    t, d = h.shape
    row = lambda a: a.reshape(1, -1)
    wa, wc, wuq, wuqr, wukv = _prep_attention_weights(w_in, w_uq, w_ukv)
    q, k, v = _qkv(h, pos2, invf4, wa, row(q_norm_g), row(kv_norm_g), wuq, wuqr, wukv)
    conv = _conv(h, wc, conv_w, row(conv_b), row(conv_ln_g), row(conv_ln_b), seq)
    attn = _attn(q, k, v, batch, seq)
    aw = attn.shape[1]
    wr32 = jnp.pad(w_router.astype(F32), ((0, 0), (0, LANES - N_EXPERTS)))
    wr_hi = wr32.astype(BF16)
    wr_lo = (wr32 - wr_hi.astype(F32)).astype(BF16)
    wr = jnp.concatenate([wr_hi, wr_hi, wr_lo], axis=0)
    h1, h1p, logits_t = _oproj(attn, conv, h, w_o[:aw].astype(BF16), w_o[aw:].astype(BF16),
                               row(ln1_g), row(ln1_b), wr, alpha)
    return _moe_ffn(h1, h1p, logits_t, alpha, router_bias, w_gate, w_up, w_down,
                    ws_gate, ws_up, ws_down, ln2_g, ln2_b)


def _moe_ffn(h1, h1p, logits_t, alpha, router_bias, w_gate, w_up, w_down, ws_gate, ws_up, ws_down, ln2_g, ln2_b):
    t, d = h1.shape
    row = lambda a: a.reshape(1, -1)
    eidx, rank, wts, cnt = _route(logits_t, router_bias.astype(F32).reshape(N_EXPERTS, 1))
    counts = cnt[:, 0].astype(I32)
    padded = (counts + TE_ROWS - 1) // TE_ROWS * TE_ROWS
    pend = jnp.cumsum(padded)
    poff = pend - padded
    n_rows = t * TOP_K + N_EXPERTS * TE_ROWS
    n_tiles = n_rows // TE_ROWS
    n_valid = (pend[-1] // TE_ROWS).astype(I32).reshape(1)
    tile_row = jnp.minimum(jnp.arange(n_tiles, dtype=I32), n_valid[0] - 1) * TE_ROWS
    tile_expert = jnp.sum((pend[None, :] <= tile_row[:, None]).astype(I32), axis=1)
    tile_expert = jnp.minimum(tile_expert, N_EXPERTS - 1)
    tiles = jnp.arange(n_tiles, dtype=I32)
    last_tile = jnp.where(padded > 0, pend // TE_ROWS - 1, -1)
    zfill = ((tiles >= n_valid[0]) | jnp.any(tiles[:, None] == last_tile[None, :], axis=1)).astype(I32)
    ids = jnp.arange(N_EXPERTS, dtype=I32)
    later = (ids[None, :] > ids[:, None]) & (padded[None, :] > 0)
    next_of = jnp.min(jnp.where(later, ids[None, :], N_EXPERTS), axis=1)
    next_of = jnp.where(next_of == N_EXPERTS, -1, next_of)
    next_expert = jnp.sum(jnp.where(tile_expert[:, None] == ids[None, :], next_of[None, :], 0), axis=1).astype(I32)
    slot_base = jnp.sum(jnp.where(eidx[None] == ids[:, None, None], poff[:, None, None], 0), axis=0)
    slots_flat = (slot_base + rank).T.reshape(-1)
    xs = _dispatch(zfill, slots_flat, h1p, n_rows)
    y = _experts(tile_expert, n_valid, next_expert, xs, w_gate, w_up, w_down)
    return _combine(slots_flat, y, h1, wts.T, ws_gate.astype(BF16), ws_up.astype(BF16),
                    ws_down.astype(BF16), row(ln2_g), row(ln2_b), alpha)


def kernel(x, positions, w_in, q_norm_g, w_uq, kv_norm_g, w_ukv, conv_w, conv_b, conv_ln_g, conv_ln_b, w_o, ln1_g, ln1_b, w_router, router_bias, w_gate, w_up, w_down, ws_gate, ws_up, ws_down, ln2_g, ln2_b):
    batch, seq, d = x.shape
    depth = w_in.shape[0]
    alpha = (2.0 * depth) ** 0.25
    inv_freq = ROPE_BASE ** (-jnp.arange(0, QK_ROPE_DIM, 2, dtype=F32) / QK_ROPE_DIM)
    invf4 = jnp.tile(inv_freq, LANES // inv_freq.shape[0]).reshape(1, LANES)
    pos2 = positions.reshape(batch * seq, 1)
    h = x.reshape(batch * seq, d)
    for l in range(depth):
        h = _layer(h, pos2, invf4, batch, seq, alpha, w_in[l], q_norm_g[l], w_uq[l], kv_norm_g[l], w_ukv[l],
                   conv_w[l], conv_b[l], conv_ln_g[l], conv_ln_b[l], w_o[l], ln1_g[l], ln1_b[l],
                   w_router[l], router_bias[l], w_gate[l], w_up[l], w_down[l],
                   ws_gate[l], ws_up[l], ws_down[l], ln2_g[l], ln2_b[l])
    return h.reshape(batch, seq, d)
```

```python
import functools

import jax
import jax.numpy as jnp
from jax import lax
from jax.experimental import pallas as pl
from jax.experimental.pallas import tpu as pltpu

F32 = jnp.float32
BF16 = jnp.bfloat16
I32 = jnp.int32
U32 = jnp.uint32

N_HEADS = 8
QK_NOPE_DIM = 128
QK_ROPE_DIM = 64
V_HEAD_DIM = 128
HEAD_QK_PAD = 256
CONV_KERNEL = 31
N_EXPERTS = 64
TOP_K = 6
TOP_K_PAD = 8
N_EXPERT_GROUPS = 8
GROUP_SIZE = N_EXPERTS // N_EXPERT_GROUPS
TOPK_GROUPS = 4
ROUTED_SCALE = 2.5
ROPE_BASE = 10000.0
LN_EPS = 1e-5
RMS_EPS = 1e-6

LANES = 128
SUBLANES = 8
CONV_HALO = 32

TM_PROJ = 256
TM_CONV = 512
TQ_ATTN = 512
HEADS_PER_ATTN_STEP = 2
ATTN_STRIP = 512
TM_ROUTE = 512
TE_ROWS = 256
TM_COMB = 256
TC_DISP = 1024
WAIT_ROWS = 128
CONV_ROWS = 64
CONV_COLS = 256
VMEM_LIMIT = 56 * 1024 * 1024


def _cparams(sem):
    return pltpu.CompilerParams(dimension_semantics=sem, vmem_limit_bytes=VMEM_LIMIT)


def _sigmoid(v):
    return 1.0 / (1.0 + jnp.exp(-v))


def _layer_norm(v, g, b):
    mu = jnp.mean(v, axis=-1, keepdims=True)
    d = v - mu
    var = jnp.mean(d * d, axis=-1, keepdims=True)
    return d * lax.rsqrt(var + LN_EPS) * g + b


def _rms_norm(v, g):
    ms = jnp.mean(v * v, axis=-1, keepdims=True)
    return v * lax.rsqrt(ms + RMS_EPS) * g


def _qkv_kernel(x_ref, pos_ref, invf_ref, wa_ref, qg_ref, kvg_ref, wuq_ref, wuqr_ref, wukv_ref,
                q_ref, k_ref, v_ref):
    ql = qg_ref.shape[1]
    kvl = kvg_ref.shape[1]
    xb = x_ref[...].astype(BF16)
    lat = jnp.dot(xb, wa_ref[...], preferred_element_type=F32)
    ang = pos_ref[...].astype(F32) * invf_ref[...]
    cos = jnp.cos(ang)
    sin = jnp.sin(ang)
    cq = _rms_norm(lat[:, :ql], qg_ref[...]).astype(BF16)
    ckv = _rms_norm(lat[:, ql:ql + kvl], kvg_ref[...]).astype(BF16)
    kr = lat[:, ql + kvl:ql + kvl + LANES] * cos + lat[:, ql + kvl + LANES:ql + kvl + 2 * LANES] * sin
    kr = kr.astype(BF16)
    q = jnp.dot(cq, wuq_ref[...], preferred_element_type=F32)
    qrot = jnp.dot(cq, wuqr_ref[...], preferred_element_type=F32)
    kv = jnp.dot(ckv, wukv_ref[...], preferred_element_type=F32)
    for h in range(N_HEADS):
        c0 = h * HEAD_QK_PAD
        q_ref[:, c0:c0 + LANES] = q[:, c0:c0 + LANES].astype(BF16)
        q_ref[:, c0 + LANES:c0 + 2 * LANES] = (
            q[:, c0 + LANES:c0 + 2 * LANES] * cos + qrot[:, h * LANES:(h + 1) * LANES] * sin).astype(BF16)
        k_ref[:, c0:c0 + LANES] = kv[:, h * LANES:(h + 1) * LANES].astype(BF16)
        k_ref[:, c0 + LANES:c0 + 2 * LANES] = kr
    v_ref[...] = kv[:, N_HEADS * QK_NOPE_DIM:].astype(BF16)


def _qkv(x2, pos2, invf4, wa, qg, kvg, wuq, wuqr, wukv):
    t, d = x2.shape
    tm = TM_PROJ
    full = lambda a: pl.BlockSpec(a.shape, lambda i: (0,) * a.ndim)
    return pl.pallas_call(
        _qkv_kernel,
        grid=(t // tm,),
        in_specs=[pl.BlockSpec((tm, d), lambda i: (i, 0)),
                  pl.BlockSpec((tm, 1), lambda i: (i, 0)),
                  full(invf4), full(wa), full(qg), full(kvg), full(wuq), full(wuqr), full(wukv)],
        out_specs=[pl.BlockSpec((tm, N_HEADS * HEAD_QK_PAD), lambda i: (i, 0)),
                   pl.BlockSpec((tm, N_HEADS * HEAD_QK_PAD), lambda i: (i, 0)),
                   pl.BlockSpec((tm, N_HEADS * V_HEAD_DIM), lambda i: (i, 0))],
        out_shape=[jax.ShapeDtypeStruct((t, N_HEADS * HEAD_QK_PAD), BF16),
                   jax.ShapeDtypeStruct((t, N_HEADS * HEAD_QK_PAD), BF16),
                   jax.ShapeDtypeStruct((t, N_HEADS * V_HEAD_DIM), BF16)],
        compiler_params=_cparams(("arbitrary",)),
        name="qkv",
    )(x2, pos2, invf4, wa, qg, kvg, wuq, wuqr, wukv)


def _conv_kernel(x_ref, wc_ref, cw_ref, cb_ref, g_ref, b_ref, o_ref, ubuf, ybuf, shbuf, *, tiles_per_seq):
    tm, cw = o_ref.shape
    i = pl.program_id(0)
    xb = x_ref[...].astype(BF16)
    ag = jnp.dot(xb, wc_ref[...], preferred_element_type=F32)
    u = ag[:, :cw] * _sigmoid(ag[:, cw:])

    @pl.when(i % tiles_per_seq == 0)
    def _():
        ubuf[0:CONV_HALO, :] = jnp.zeros((CONV_HALO, cw), F32)

    ubuf[CONV_HALO:CONV_HALO + tm, :] = u
    shift0 = CONV_HALO - (CONV_KERNEL - 1)
    n_col = cw // CONV_COLS

    def chunk(c, carry):
        r0 = pl.multiple_of((c // n_col) * CONV_ROWS, CONV_ROWS)
        c0 = pl.multiple_of((c % n_col) * CONV_COLS, CONV_COLS)
        win = ubuf[pl.ds(r0, CONV_ROWS + CONV_HALO), pl.ds(c0, CONV_COLS)]
        for r in range(1, SUBLANES):
            shbuf[r] = win[r:r + CONV_ROWS + CONV_HALO - SUBLANES, :]
        acc = jnp.zeros((CONV_ROWS, CONV_COLS), F32)
        for k in range(CONV_KERNEL):
            wk = cw_ref[k:k + 1, pl.ds(c0, CONV_COLS)]
            r, j = (shift0 + k) % SUBLANES, (shift0 + k) // SUBLANES
            if r == 0:
                tap = win[j * SUBLANES:j * SUBLANES + CONV_ROWS, :]
            else:
                tap = shbuf[r, j * SUBLANES:j * SUBLANES + CONV_ROWS, :]
            acc = acc + wk * tap
        ybuf[pl.ds(r0, CONV_ROWS), pl.ds(c0, CONV_COLS)] = acc
        return carry

    lax.fori_loop(0, (tm // CONV_ROWS) * n_col, chunk, 0)
    ubuf[0:CONV_HALO, :] = ubuf[tm:tm + CONV_HALO, :]
    y = _layer_norm(ybuf[...] + cb_ref[...], g_ref[...], b_ref[...])
    o_ref[...] = (y * _sigmoid(y)).astype(BF16)


def _conv(x2, wc, cw, cb, g, b, seq):
    t, d = x2.shape
    tm = TM_CONV
    c = cw.shape[1]
    full = lambda a: pl.BlockSpec(a.shape, lambda i: (0,) * a.ndim)
    return pl.pallas_call(
        functools.partial(_conv_kernel, tiles_per_seq=seq // tm),
        grid=(t // tm,),
        in_specs=[pl.BlockSpec((tm, d), lambda i: (i, 0)), full(wc), full(cw), full(cb), full(g), full(b)],
        out_specs=pl.BlockSpec((tm, c), lambda i: (i, 0)),
        out_shape=jax.ShapeDtypeStruct((t, c), BF16),
        scratch_shapes=[pltpu.VMEM((tm + CONV_HALO, c), F32), pltpu.VMEM((tm, c), F32),
                        pltpu.VMEM((SUBLANES, CONV_ROWS + CONV_HALO - SUBLANES, CONV_COLS), F32)],
        compiler_params=_cparams(("arbitrary",)),
        name="conv",
    )(x2, wc, cw, cb, g, b)


def _attn_kernel(q_ref, k_ref, v_ref, o_ref, *, scale):
    tq = q_ref.shape[0]
    heads = q_ref.shape[1] // HEAD_QK_PAD
    i = pl.program_id(2)
    c = scale * 1.4426950408889634

    ts = ATTN_STRIP
    chains = [(h, r) for h in range(heads) for r in range(tq // ts)]

    def block(j, carry, masked):
        r0 = pl.multiple_of(j * tq, tq)
        out = []
        for (h, r), (m, l, acc) in zip(chains, carry):
            q = q_ref[r * ts:(r + 1) * ts, h * HEAD_QK_PAD:(h + 1) * HEAD_QK_PAD]
            kb = k_ref[pl.ds(r0, tq), h * HEAD_QK_PAD:(h + 1) * HEAD_QK_PAD]
            vb = v_ref[pl.ds(r0, tq), h * V_HEAD_DIM:(h + 1) * V_HEAD_DIM]
            s = lax.dot_general(q, kb, (((1,), (1,)), ((), ())), preferred_element_type=F32)
            if masked:
                row = lax.broadcasted_iota(I32, (ts, tq), 0) + r * ts
                col = lax.broadcasted_iota(I32, (ts, tq), 1)
                s = jnp.where(col <= row, s, -jnp.inf)
            m_new = jnp.maximum(m, jnp.max(s, axis=1, keepdims=True))
            alpha = jnp.exp2((m - m_new) * c)
            p = jnp.exp2((s - m_new) * c)
            l = alpha * l + jnp.sum(p, axis=1, keepdims=True)
            acc = alpha * acc + jnp.dot(p.astype(BF16), vb, preferred_element_type=F32)
            out.append((m_new, l, acc))
        return tuple(out)

    init = tuple((jnp.full((ts, 1), -jnp.inf, F32), jnp.zeros((ts, 1), F32),
                  jnp.zeros((ts, V_HEAD_DIM), F32)) for _ in chains)
    carry = lax.fori_loop(0, i, lambda j, cr: block(j, cr, False), init)
    final = block(i, carry, True)
    for (h, r), (_, l, acc) in zip(chains, final):
        o_ref[r * ts:(r + 1) * ts, h * V_HEAD_DIM:(h + 1) * V_HEAD_DIM] = (acc / l).astype(BF16)


def _attn(q, k, v, batch, seq):
    tq = TQ_ATTN
    nq = seq // tq
    hb = HEADS_PER_ATTN_STEP
    scale = (QK_NOPE_DIM + QK_ROPE_DIM) ** -0.5
    return pl.pallas_call(
        functools.partial(_attn_kernel, scale=scale),
        grid=(batch, N_HEADS // hb, nq),
        in_specs=[pl.BlockSpec((tq, hb * HEAD_QK_PAD), lambda b, h, i: (b * nq + i, h)),
                  pl.BlockSpec((seq, hb * HEAD_QK_PAD), lambda b, h, i: (b, h)),
                  pl.BlockSpec((seq, hb * V_HEAD_DIM), lambda b, h, i: (b, h))],
        out_specs=pl.BlockSpec((tq, hb * V_HEAD_DIM), lambda b, h, i: (b * nq + i, h)),
        out_shape=jax.ShapeDtypeStruct((batch * seq, N_HEADS * V_HEAD_DIM), BF16),
        compiler_params=_cparams(("arbitrary", "arbitrary", "arbitrary")),
        name="attn",
    )(q, k, v)


def _pack_bf16_pairs(v):
    c = v.shape[1] // 2
    lo = pltpu.bitcast(v[:, :c].astype(BF16).astype(F32), U32)
    hi = pltpu.bitcast(v[:, c:].astype(BF16).astype(F32), U32)
    return (hi & jnp.uint32(0xFFFF0000)) | (lo >> 16)


def _unpack_bf16_pairs(p):
    lo = pltpu.bitcast(p << 16, F32).astype(BF16)
    hi = pltpu.bitcast(p & jnp.uint32(0xFFFF0000), F32).astype(BF16)
    return jnp.concatenate([lo, hi], axis=1)


def _store_row_planes(ref, v):
    n = v.shape[0]
    p = ref.shape[0] // n
    for j in range(p):
        ref[pl.ds(j, n, stride=p), :] = v[:, j * LANES:(j + 1) * LANES]


def _load_row_planes(ref, p):
    n = ref.shape[0] // p
    return jnp.concatenate([ref[pl.ds(j, n, stride=p), :] for j in range(p)], axis=1)


def _oproj_kernel(a_ref, c_ref, x_ref, woa_ref, woc_ref, g_ref, b_ref, wr_ref,
                  h_ref, hp_ref, lg_ref, *, alpha):
    mix = jnp.dot(a_ref[...], woa_ref[...], preferred_element_type=F32)
    mix = mix + jnp.dot(c_ref[...], woc_ref[...], preferred_element_type=F32)
    h = _layer_norm(alpha * x_ref[...] + mix, g_ref[...], b_ref[...])
    h_ref[...] = h
    _store_row_planes(hp_ref, _pack_bf16_pairs(h))
    h_hi = h.astype(BF16)
    h_lo = (h - h_hi.astype(F32)).astype(BF16)
    logits = jnp.dot(jnp.concatenate([h_hi, h_lo, h_hi], axis=1), wr_ref[...], preferred_element_type=F32)
    lg_ref[...] = logits.T[:N_EXPERTS, :]


def _oproj(attn, conv, x2, woa, woc, g, b, wr, alpha):
    t, d = x2.shape
    tm = TM_PROJ
    full = lambda a: pl.BlockSpec(a.shape, lambda i: (0,) * a.ndim)
    return pl.pallas_call(
        functools.partial(_oproj_kernel, alpha=alpha),
        grid=(t // tm,),
        in_specs=[pl.BlockSpec((tm, attn.shape[1]), lambda i: (i, 0)),
                  pl.BlockSpec((tm, conv.shape[1]), lambda i: (i, 0)),
                  pl.BlockSpec((tm, d), lambda i: (i, 0)),
                  full(woa), full(woc), full(g), full(b), full(wr)],
        out_specs=[pl.BlockSpec((tm, d), lambda i: (i, 0)),
                   pl.BlockSpec((tm * (d // 2 // LANES), LANES), lambda i: (i, 0)),
                   pl.BlockSpec((N_EXPERTS, tm), lambda i: (0, i))],
        out_shape=[jax.ShapeDtypeStruct((t, d), F32),
                   jax.ShapeDtypeStruct((t * (d // 2 // LANES), LANES), U32),
                   jax.ShapeDtypeStruct((N_EXPERTS, t), F32)],
        compiler_params=_cparams(("arbitrary",)),
        name="oproj",
    )(attn, conv, x2, woa, woc, g, b, wr)


def _route_kernel(lg_ref, bias_ref, eidx_ref, rank_ref, wts_ref, cnt_ref, carry_ref):
    ne, tm = lg_ref.shape
    i = pl.program_id(0)

    @pl.when(i == 0)
    def _():
        carry_ref[...] = jnp.zeros(carry_ref.shape, F32)

    scores = _sigmoid(lg_ref[...])
    biased = scores + bias_ref[...]
    neg = -jnp.inf
    sub8 = lax.broadcasted_iota(I32, (GROUP_SIZE, tm), 0)
    gscore = []
    for g in range(N_EXPERT_GROUPS):
        blk = biased[g * GROUP_SIZE:(g + 1) * GROUP_SIZE, :]
        m1 = jnp.max(blk, axis=0, keepdims=True)
        i1 = jnp.min(jnp.where(blk == m1, sub8, GROUP_SIZE), axis=0, keepdims=True)
        m2 = jnp.max(jnp.where(sub8 == i1, neg, blk), axis=0, keepdims=True)
        gscore.append(m1 + m2)
    kept = []
    for g in range(N_EXPERT_GROUPS):
        beat = jnp.zeros((1, tm), I32)
        for o in range(N_EXPERT_GROUPS):
            if o < g:
                beat = beat + (gscore[o] >= gscore[g]).astype(I32)
            elif o > g:
                beat = beat + (gscore[o] > gscore[g]).astype(I32)
        kept.append(jnp.where(beat < TOPK_GROUPS, biased[g * GROUP_SIZE:(g + 1) * GROUP_SIZE, :], neg))
    cur = jnp.concatenate(kept, axis=0)
    sub = lax.broadcasted_iota(I32, (ne, tm), 0)
    sel_any = jnp.zeros((ne, tm), F32)
    picks = []
    wsum = jnp.zeros((1, tm), F32)
    for k in range(TOP_K):
        m = jnp.max(cur, axis=0, keepdims=True)
        ei = jnp.min(jnp.where(cur == m, sub, ne), axis=0, keepdims=True)
        sel = sub == ei
        w = jnp.sum(jnp.where(sel, scores, 0.0), axis=0, keepdims=True)
        cur = jnp.where(sel, neg, cur)
        sel_any = sel_any + sel.astype(F32)
        wsum = wsum + w
        picks.append((ei, sel, w))
    r = lax.broadcasted_iota(I32, (tm, tm), 0)
    c = lax.broadcasted_iota(I32, (tm, tm), 1)
    upper = (r <= c).astype(BF16)
    cum = jnp.dot(sel_any.astype(BF16), upper, preferred_element_type=F32)
    carry = carry_ref[:, 0:1]
    excl = cum - sel_any + carry
    total = carry + jnp.sum(sel_any, axis=1, keepdims=True)
    carry_ref[...] = jnp.broadcast_to(total, carry_ref.shape)
    cnt_ref[...] = jnp.broadcast_to(total, cnt_ref.shape)
    denom = wsum + 1e-20
    eidx_ref[...] = jnp.zeros(eidx_ref.shape, I32)
    rank_ref[...] = jnp.zeros(rank_ref.shape, I32)
    wts_ref[...] = jnp.zeros(wts_ref.shape, F32)
    for k, (ei, sel, w) in enumerate(picks):
        eidx_ref[k:k + 1, :] = ei
        rank_ref[k:k + 1, :] = jnp.sum(jnp.where(sel, excl, 0.0), axis=0, keepdims=True).astype(I32)
        wts_ref[k:k + 1, :] = w / denom * ROUTED_SCALE


def _route(logits_t, bias):
    ne, t = logits_t.shape
    tm = TM_ROUTE
    return pl.pallas_call(
        _route_kernel,
        grid=(t // tm,),
        in_specs=[pl.BlockSpec((ne, tm), lambda i: (0, i)), pl.BlockSpec((ne, 1), lambda i: (0, 0))],
        out_specs=[pl.BlockSpec((TOP_K_PAD, tm), lambda i: (0, i)),
                   pl.BlockSpec((TOP_K_PAD, tm), lambda i: (0, i)),
                   pl.BlockSpec((TOP_K_PAD, tm), lambda i: (0, i)),
                   pl.BlockSpec((ne, LANES), lambda i: (0, 0))],
        out_shape=[jax.ShapeDtypeStruct((TOP_K_PAD, t), I32),
                   jax.ShapeDtypeStruct((TOP_K_PAD, t), I32),
                   jax.ShapeDtypeStruct((TOP_K_PAD, t), F32),
                   jax.ShapeDtypeStruct((ne, LANES), F32)],
        scratch_shapes=[pltpu.VMEM((ne, LANES), F32)],
        compiler_params=_cparams(("arbitrary",)),
        name="route",
    )(logits_t, bias)


def _wait_rows(src_rows, dst_rows, sem, n_rows):
    def body(_, c):
        pltpu.make_async_copy(src_rows, dst_rows, sem).wait()
        return c
    lax.fori_loop(0, n_rows // WAIT_ROWS, body, 0)


def _dispatch_kernel(zfill_ref, slots_ref, hp_ref, xs_hbm, zero_vmem, sem, zsem):
    i = pl.program_id(0)
    te = zero_vmem.shape[0]
    tc = hp_ref.shape[0]
    n_tiles = xs_hbm.shape[0] // te

    def zero_copy(tile):
        return pltpu.make_async_copy(zero_vmem, xs_hbm.at[pl.ds(pl.multiple_of(tile * te, te), te)], zsem)

    @pl.when(i == 0)
    def _():
        zero_vmem[...] = jnp.zeros(zero_vmem.shape, U32)

        def zstart(tile, c):
            @pl.when(zfill_ref[tile] != 0)
            def _():
                zero_copy(tile).start()
            return c

        def zwait(tile, c):
            @pl.when(zfill_ref[tile] != 0)
            def _():
                zero_copy(tile).wait()
            return c

        lax.fori_loop(0, n_tiles, zstart, 0)
        lax.fori_loop(0, n_tiles, zwait, 0)

    base = i * (tc * TOP_K_PAD)

    def issue(t, c):
        for k in range(TOP_K):
            slot = slots_ref[base + t * TOP_K_PAD + k]
            pltpu.make_async_copy(hp_ref.at[t], xs_hbm.at[slot], sem).start(priority=k % 2)
        return c

    lax.fori_loop(0, tc, issue, 0)
    _wait_rows(hp_ref.at[pl.ds(0, WAIT_ROWS)], xs_hbm.at[pl.ds(0, WAIT_ROWS)], sem, tc * TOP_K)


def _dispatch(zfill, slots_flat, hp, n_rows):
    t, planes, lanes = hp.shape
    tc = TC_DISP
    return pl.pallas_call(
        _dispatch_kernel,
        grid_spec=pltpu.PrefetchScalarGridSpec(
            num_scalar_prefetch=2,
            grid=(t // tc,),
            in_specs=[pl.BlockSpec((tc, planes, lanes), lambda i, z, s: (i, 0, 0))],
            out_specs=pl.BlockSpec(memory_space=pl.ANY),
            scratch_shapes=[pltpu.VMEM((TE_ROWS, planes, lanes), U32),
                            pltpu.SemaphoreType.DMA, pltpu.SemaphoreType.DMA],
        ),
        out_shape=jax.ShapeDtypeStruct((n_rows, planes, lanes), U32),
        compiler_params=_cparams(("arbitrary",)),
        name="dispatch",
    )(zfill, slots_flat, hp)


def _experts_kernel(te_ref, nv_ref, nx_ref, xs_ref, wg_hbm, wu_hbm, wd_hbm, y_ref,
                    sg, su, sd, wgb, wub, wdb, sem):
    i = pl.program_id(0)
    e = te_ref[i]
    prev = te_ref[jnp.maximum(i - 1, 0)]
    valid = i < nv_ref[0]

    def fetch(ex):
        return (pltpu.make_async_copy(wg_hbm.at[ex], sg, sem.at[0]),
                pltpu.make_async_copy(wu_hbm.at[ex], su, sem.at[1]),
                pltpu.make_async_copy(wd_hbm.at[ex], sd, sem.at[2]))

    @pl.when(i == 0)
    def _():
        for cp in fetch(e):
            cp.start()

    @pl.when(valid & ((i == 0) | (e != prev)))
    def _():
        for cp in fetch(e):
            cp.wait()
        wgb[...] = sg[...].astype(BF16)
        wub[...] = su[...].astype(BF16)
        wdb[...] = sd[...].astype(BF16)

        @pl.when(nx_ref[i] >= 0)
        def _():
            for cp in fetch(nx_ref[i]):
                cp.start()

    @pl.when(valid)
    def _():
        xs = _unpack_bf16_pairs(_load_row_planes(xs_ref, xs_ref.shape[0] // TE_ROWS))
        g = jnp.dot(xs, wgb[...], preferred_element_type=F32)
        u = jnp.dot(xs, wub[...], preferred_element_type=F32)
        hid = (g * _sigmoid(g) * u).astype(BF16)
        y = jnp.dot(hid, wdb[...], preferred_element_type=F32)
        _store_row_planes(y_ref, _pack_bf16_pairs(y))

    @pl.when(jnp.logical_not(valid))
    def _():
        y_ref[...] = jnp.zeros(y_ref.shape, U32)


def _experts(tile_expert, n_valid, next_expert, xs, w_gate, w_up, w_down):
    n_rows, xp, _ = xs.shape
    ne, d, ff = w_gate.shape
    te = TE_ROWS
    in_map = lambda i, te_ref, nv_ref, nx_ref: (jnp.minimum(i, nv_ref[0] - 1), 0)
    out_map = lambda i, te_ref, nv_ref, nx_ref: (i, 0)
    hbm = pl.BlockSpec(memory_space=pl.ANY)
    y = pl.pallas_call(
        _experts_kernel,
        grid_spec=pltpu.PrefetchScalarGridSpec(
            num_scalar_prefetch=3,
            grid=(n_rows // te,),
            in_specs=[pl.BlockSpec((te * xp, LANES), in_map), hbm, hbm, hbm],
            out_specs=pl.BlockSpec((te * xp, LANES), out_map),
            scratch_shapes=[pltpu.VMEM((d, ff), F32), pltpu.VMEM((d, ff), F32), pltpu.VMEM((ff, d), F32),
                            pltpu.VMEM((d, ff), BF16), pltpu.VMEM((d, ff), BF16), pltpu.VMEM((ff, d), BF16),
                            pltpu.SemaphoreType.DMA((3,))],
        ),
        out_shape=jax.ShapeDtypeStruct((n_rows * xp, LANES), U32),
        compiler_params=_cparams(("arbitrary",)),
        name="experts",
    )(tile_expert, n_valid, next_expert, xs.reshape(n_rows * xp, LANES), w_gate, w_up, w_down)
    return y.reshape(n_rows, xp, LANES)


def _combine_kernel(slots_ref, y_hbm, h_ref, wts_ref, wsg_ref, wsu_ref, wsd_ref, g_ref, b_ref,
                    o_ref, gbuf, rbuf, sem, *, alpha):
    tm, d = o_ref.shape
    i = pl.program_id(0)

    def issue_step(step):
        buf = step % 2
        base = step * (tm * TOP_K_PAD)

        def issue(t, c):
            for k in range(TOP_K):
                slot = slots_ref[base + t * TOP_K_PAD + k]
                pltpu.make_async_copy(y_hbm.at[slot], gbuf.at[buf, k, t], sem.at[buf]).start(priority=k % 2)
            return c

        lax.fori_loop(0, tm, issue, 0)

    @pl.when(i == 0)
    def _():
        issue_step(0)

    @pl.when(i + 1 < pl.num_programs(0))
    def _():
        issue_step(i + 1)

    h = h_ref[...]
    hb = h.astype(BF16)
    sg = jnp.dot(hb, wsg_ref[...], preferred_element_type=F32)
    su = jnp.dot(hb, wsu_ref[...], preferred_element_type=F32)
    shared = jnp.dot((sg * _sigmoid(sg) * su).astype(BF16), wsd_ref[...], preferred_element_type=F32)
    cur = i % 2
    _wait_rows(y_hbm.at[pl.ds(0, WAIT_ROWS)], gbuf.at[cur, 0, pl.ds(0, WAIT_ROWS)], sem.at[cur], tm * TOP_K)
    r_lo = r_hi = None
    for k in range(TOP_K):
        packed = gbuf[cur, k]
        w = wts_ref[:, k:k + 1, :]
        lo = w * pltpu.bitcast(packed << 16, F32)
        hi = w * pltpu.bitcast(packed & jnp.uint32(0xFFFF0000), F32)
        r_lo = lo if r_lo is None else r_lo + lo
        r_hi = hi if r_hi is None else r_hi + hi
    planes = r_lo.shape[1]
    rbuf[0] = r_lo.reshape(tm * planes, LANES)
    rbuf[1] = r_hi.reshape(tm * planes, LANES)
    routed = jnp.concatenate([_load_row_planes(rbuf.at[0], planes), _load_row_planes(rbuf.at[1], planes)], axis=1)
    o_ref[...] = _layer_norm(alpha * h + (routed + shared), g_ref[...], b_ref[...])


def _combine(slots_flat, y, h, wts_planes, wsg, wsu, wsd, g, b, alpha):
    t, d = h.shape
    tm = TM_COMB
    planes = y.shape[1]
    full = lambda a: pl.BlockSpec(a.shape, lambda i, s: (0,) * a.ndim)
    once = lambda a: pl.BlockSpec(a.shape, lambda i, s: (0,) * a.ndim, pipeline_mode=pl.Buffered(1))
    return pl.pallas_call(
        functools.partial(_combine_kernel, alpha=alpha),
        grid_spec=pltpu.PrefetchScalarGridSpec(
            num_scalar_prefetch=1,
            grid=(t // tm,),
            in_specs=[pl.BlockSpec(memory_space=pl.ANY),
                      pl.BlockSpec((tm, d), lambda i, s: (i, 0)),
                      pl.BlockSpec((tm, TOP_K_PAD, LANES), lambda i, s: (i, 0, 0)),
                      once(wsg), once(wsu), once(wsd), full(g), full(b)],
            out_specs=pl.BlockSpec((tm, d), lambda i, s: (i, 0)),
            scratch_shapes=[pltpu.VMEM((2, TOP_K, tm, planes, LANES), U32),
                            pltpu.VMEM((2, tm * planes, LANES), F32), pltpu.SemaphoreType.DMA((2,))],
        ),
        out_shape=jax.ShapeDtypeStruct((t, d), F32),
        compiler_params=_cparams(("arbitrary",)),
        name="combine",
    )(slots_flat, y, h, wts_planes, wsg, wsu, wsd, g, b)


def _prep_attention_weights(w_in, w_uq, w_ukv):
    d = w_in.shape[0]
    ql = w_uq.shape[0]
    kvl = w_ukv.shape[0]
    half = QK_ROPE_DIM // 2
    pad = LANES - QK_ROPE_DIM
    w_kr = w_in[:, ql + kvl:ql + kvl + QK_ROPE_DIM]
    w_kr_rot = jnp.concatenate([-w_kr[:, half:], w_kr[:, :half]], axis=1)
    zpad = jnp.zeros((d, pad), w_in.dtype)
    wa = jnp.concatenate([w_in[:, :ql + kvl], w_kr, zpad, w_kr_rot, zpad], axis=1).astype(BF16)
    wc = w_in[:, ql + kvl + QK_ROPE_DIM:].astype(BF16)
    uq = w_uq.reshape(ql, N_HEADS, QK_NOPE_DIM + QK_ROPE_DIM)
    uq_nope, uq_rope = uq[..., :QK_NOPE_DIM], uq[..., QK_NOPE_DIM:]
    zq = jnp.zeros((ql, N_HEADS, pad), w_uq.dtype)
    wuq = jnp.concatenate([uq_nope, uq_rope, zq], axis=-1).reshape(ql, N_HEADS * HEAD_QK_PAD).astype(BF16)
    uq_rot = jnp.concatenate([-uq_rope[..., half:], uq_rope[..., :half], zq], axis=-1)
    wuqr = uq_rot.reshape(ql, N_HEADS * LANES).astype(BF16)
    ukv = w_ukv.reshape(kvl, N_HEADS, QK_NOPE_DIM + V_HEAD_DIM)
    wukv = jnp.concatenate([ukv[..., :QK_NOPE_DIM].reshape(kvl, -1), ukv[..., QK_NOPE_DIM:].reshape(kvl, -1)],
                           axis=1).astype(BF16)
    return wa, wc, wuq, wuqr, wukv


def _layer(h, pos2, invf4, batch, seq, alpha, w_in, q_norm_g, w_uq, kv_norm_g, w_ukv, conv_w, conv_b,
           conv_ln_g, conv_ln_b, w_o, ln1_g, ln1_b, w_router, router_bias, w_gate, w_up, w_down,
           ws_gate, ws_up, ws_down, ln2_g, ln2_b):
    t, d = h.shape
    row = lambda a: a.reshape(1, -1)
    wa, wc, wuq, wuqr, wukv = _prep_attention_weights(w_in, w_uq, w_ukv)
    q, k, v = _qkv(h, pos2, invf4, wa, row(q_norm_g), row(kv_norm_g), wuq, wuqr, wukv)
    conv = _conv(h, wc, conv_w, row(conv_b), row(conv_ln_g), row(conv_ln_b), seq)
    attn = _attn(q, k, v, batch, seq)
    aw = attn.shape[1]
    wr32 = jnp.pad(w_router.astype(F32), ((0, 0), (0, LANES - N_EXPERTS)))
    wr_hi = wr32.astype(BF16)
    wr_lo = (wr32 - wr_hi.astype(F32)).astype(BF16)
    wr = jnp.concatenate([wr_hi, wr_hi, wr_lo], axis=0)
    h1, h1p, logits_t = _oproj(attn, conv, h, w_o[:aw].astype(BF16), w_o[aw:].astype(BF16),
                               row(ln1_g), row(ln1_b), wr, alpha)
    return _moe_ffn(h1, h1p, logits_t, alpha, router_bias, w_gate, w_up, w_down,
                    ws_gate, ws_up, ws_down, ln2_g, ln2_b)


def _moe_ffn(h1, h1p, logits_t, alpha, router_bias, w_gate, w_up, w_down, ws_gate, ws_up, ws_down, ln2_g, ln2_b):
    t, d = h1.shape
    row = lambda a: a.reshape(1, -1)
    eidx, rank, wts, cnt = _route(logits_t, router_bias.astype(F32).reshape(N_EXPERTS, 1))
    counts = cnt[:, 0].astype(I32)
    padded = (counts + TE_ROWS - 1) // TE_ROWS * TE_ROWS
    pend = jnp.cumsum(padded)
    poff = pend - padded
    n_rows = t * TOP_K + N_EXPERTS * TE_ROWS
    n_tiles = n_rows // TE_ROWS
    n_valid = (pend[-1] // TE_ROWS).astype(I32).reshape(1)
    tile_row = jnp.minimum(jnp.arange(n_tiles, dtype=I32), n_valid[0] - 1) * TE_ROWS
    tile_expert = jnp.sum((pend[None, :] <= tile_row[:, None]).astype(I32), axis=1)
    tile_expert = jnp.minimum(tile_expert, N_EXPERTS - 1)
    tiles = jnp.arange(n_tiles, dtype=I32)
    last_tile = jnp.where(padded > 0, pend // TE_ROWS - 1, -1)
    zfill = ((tiles >= n_valid[0]) | jnp.any(tiles[:, None] == last_tile[None, :], axis=1)).astype(I32)
    ids = jnp.arange(N_EXPERTS, dtype=I32)
    later = (ids[None, :] > ids[:, None]) & (padded[None, :] > 0)
    next_of = jnp.min(jnp.where(later, ids[None, :], N_EXPERTS), axis=1)
    next_of = jnp.where(next_of == N_EXPERTS, -1, next_of)
    next_expert = jnp.sum(jnp.where(tile_expert[:, None] == ids[None, :], next_of[None, :], 0), axis=1).astype(I32)
    slot_base = jnp.sum(jnp.where(eidx[None] == ids[:, None, None], poff[:, None, None], 0), axis=0)
    slots_flat = (slot_base + rank).T.reshape(-1)
    xs = _dispatch(zfill, slots_flat, h1p.reshape(t, -1, LANES), n_rows)
    y = _experts(tile_expert, n_valid, next_expert, xs, w_gate, w_up, w_down)
    wts_planes = jnp.broadcast_to(wts.T[:, :, None], (t, TOP_K_PAD, LANES))
    return _combine(slots_flat, y, h1, wts_planes, ws_gate.astype(BF16), ws_up.astype(BF16),
                    ws_down.astype(BF16), row(ln2_g), row(ln2_b), alpha)


def kernel(x, positions, w_in, q_norm_g, w_uq, kv_norm_g, w_ukv, conv_w, conv_b, conv_ln_g, conv_ln_b, w_o, ln1_g, ln1_b, w_router, router_bias, w_gate, w_up, w_down, ws_gate, ws_up, ws_down, ln2_g, ln2_b):
    batch, seq, d = x.shape
    depth = w_in.shape[0]
    alpha = (2.0 * depth) ** 0.25
    inv_freq = ROPE_BASE ** (-jnp.arange(0, QK_ROPE_DIM, 2, dtype=F32) / QK_ROPE_DIM)
    invf4 = jnp.tile(inv_freq, LANES // inv_freq.shape[0]).reshape(1, LANES)
    pos2 = positions.reshape(batch * seq, 1)
    h = x.reshape(batch * seq, d)
    for l in range(depth):
        h = _layer(h, pos2, invf4, batch, seq, alpha, w_in[l], q_norm_g[l], w_uq[l], kv_norm_g[l], w_ukv[l],
                   conv_w[l], conv_b[l], conv_ln_g[l], conv_ln_b[l], w_o[l], ln1_g[l], ln1_b[l],
                   w_router[l], router_bias[l], w_gate[l], w_up[l], w_down[l],
                   ws_gate[l], ws_up[l], ws_down[l], ln2_g[l], ln2_b[l])
    return h.reshape(batch, seq, d)
```

```python
import functools

import jax
import jax.numpy as jnp
from jax import lax
from jax.experimental import pallas as pl
from jax.experimental.pallas import tpu as pltpu

F32 = jnp.float32
BF16 = jnp.bfloat16
I32 = jnp.int32
U32 = jnp.uint32

N_HEADS = 8
QK_NOPE_DIM = 128
QK_ROPE_DIM = 64
V_HEAD_DIM = 128
HEAD_QK_PAD = 256
CONV_KERNEL = 31
N_EXPERTS = 64
TOP_K = 6
TOP_K_PAD = 8
N_EXPERT_GROUPS = 8
GROUP_SIZE = N_EXPERTS // N_EXPERT_GROUPS
TOPK_GROUPS = 4
ROUTED_SCALE = 2.5
ROPE_BASE = 10000.0
LN_EPS = 1e-5
RMS_EPS = 1e-6

LANES = 128
SUBLANES = 8
CONV_HALO = 32

TM_PROJ = 256
TM_CONV = 512
TQ_ATTN = 512
HEADS_PER_ATTN_STEP = 2
ATTN_STRIP = 512
TM_ROUTE = 512
TE_ROWS = 256
TM_COMB = 256
TC_DISP = 1024
WAIT_ROWS = 128
ISSUE_UNROLL = 4
CONV_ROWS = 64
CONV_COLS = 256
CONV_SHIFT_BUFS = 4
VMEM_LIMIT = 56 * 1024 * 1024


def _cparams(sem):
    return pltpu.CompilerParams(dimension_semantics=sem, vmem_limit_bytes=VMEM_LIMIT)


def _sigmoid(v):
    return 1.0 / (1.0 + jnp.exp(-v))


def _layer_norm(v, g, b):
    mu = jnp.mean(v, axis=-1, keepdims=True)
    d = v - mu
    var = jnp.mean(d * d, axis=-1, keepdims=True)
    return d * lax.rsqrt(var + LN_EPS) * g + b


def _rms_norm(v, g):
    ms = jnp.mean(v * v, axis=-1, keepdims=True)
    return v * lax.rsqrt(ms + RMS_EPS) * g


def _qkv_kernel(x_ref, pos_ref, invf_ref, wa_ref, qg_ref, kvg_ref, wuq_ref, wuqr_ref, wukv_ref,
                q_ref, k_ref, v_ref):
    ql = qg_ref.shape[1]
    kvl = kvg_ref.shape[1]
    xb = x_ref[...].astype(BF16)
    lat = jnp.dot(xb, wa_ref[...], preferred_element_type=F32)
    ang = pos_ref[...].astype(F32) * invf_ref[...]
    cos = jnp.cos(ang)
    sin = jnp.sin(ang)
    cq = _rms_norm(lat[:, :ql], qg_ref[...]).astype(BF16)
    ckv = _rms_norm(lat[:, ql:ql + kvl], kvg_ref[...]).astype(BF16)
    kr = lat[:, ql + kvl:ql + kvl + LANES] * cos + lat[:, ql + kvl + LANES:ql + kvl + 2 * LANES] * sin
    kr = kr.astype(BF16)
    q = jnp.dot(cq, wuq_ref[...], preferred_element_type=F32)
    qrot = jnp.dot(cq, wuqr_ref[...], preferred_element_type=F32)
    kv = jnp.dot(ckv, wukv_ref[...], preferred_element_type=F32)
    for h in range(N_HEADS):
        c0 = h * HEAD_QK_PAD
        q_ref[:, c0:c0 + LANES] = q[:, c0:c0 + LANES].astype(BF16)
        q_ref[:, c0 + LANES:c0 + 2 * LANES] = (
            q[:, c0 + LANES:c0 + 2 * LANES] * cos + qrot[:, h * LANES:(h + 1) * LANES] * sin).astype(BF16)
        k_ref[:, c0:c0 + LANES] = kv[:, h * LANES:(h + 1) * LANES].astype(BF16)
        k_ref[:, c0 + LANES:c0 + 2 * LANES] = kr
    v_ref[...] = kv[:, N_HEADS * QK_NOPE_DIM:].astype(BF16)


def _qkv(x2, pos2, invf4, wa, qg, kvg, wuq, wuqr, wukv):
    t, d = x2.shape
    tm = TM_PROJ
    full = lambda a: pl.BlockSpec(a.shape, lambda i: (0,) * a.ndim)
    return pl.pallas_call(
        _qkv_kernel,
        grid=(t // tm,),
        in_specs=[pl.BlockSpec((tm, d), lambda i: (i, 0)),
                  pl.BlockSpec((tm, 1), lambda i: (i, 0)),
                  full(invf4), full(wa), full(qg), full(kvg), full(wuq), full(wuqr), full(wukv)],
        out_specs=[pl.BlockSpec((tm, N_HEADS * HEAD_QK_PAD), lambda i: (i, 0)),
                   pl.BlockSpec((tm, N_HEADS * HEAD_QK_PAD), lambda i: (i, 0)),
                   pl.BlockSpec((tm, N_HEADS * V_HEAD_DIM), lambda i: (i, 0))],
        out_shape=[jax.ShapeDtypeStruct((t, N_HEADS * HEAD_QK_PAD), BF16),
                   jax.ShapeDtypeStruct((t, N_HEADS * HEAD_QK_PAD), BF16),
                   jax.ShapeDtypeStruct((t, N_HEADS * V_HEAD_DIM), BF16)],
        compiler_params=_cparams(("arbitrary",)),
        name="qkv",
    )(x2, pos2, invf4, wa, qg, kvg, wuq, wuqr, wukv)


def _conv_kernel(x_ref, wc_ref, cw_ref, cb_ref, g_ref, b_ref, o_ref, ubuf, ybuf, shbuf, *, tiles_per_seq):
    tm, cw = o_ref.shape
    i = pl.program_id(0)
    xb = x_ref[...].astype(BF16)

    @pl.when(i % tiles_per_seq == 0)
    def _():
        ubuf[0:CONV_HALO, :] = jnp.zeros((CONV_HALO, cw), F32)

    shift0 = CONV_HALO - (CONV_KERNEL - 1)
    n_sh = shbuf.shape[0]
    chunk_no = 0
    for c0 in range(0, cw, CONV_COLS):
        a = jnp.dot(xb, wc_ref[:, c0:c0 + CONV_COLS], preferred_element_type=F32)
        gate = jnp.dot(xb, wc_ref[:, cw + c0:cw + c0 + CONV_COLS], preferred_element_type=F32)
        ubuf[CONV_HALO:CONV_HALO + tm, c0:c0 + CONV_COLS] = a * _sigmoid(gate)
        for r0 in range(0, tm, CONV_ROWS):
            sh = shbuf.at[chunk_no % n_sh]
            chunk_no += 1
            win = ubuf[r0:r0 + CONV_ROWS + CONV_HALO, c0:c0 + CONV_COLS]
            for r in range(1, SUBLANES):
                sh[r] = win[r:r + CONV_ROWS + CONV_HALO - SUBLANES, :]
            acc = jnp.zeros((CONV_ROWS, CONV_COLS), F32)
            for k in range(CONV_KERNEL):
                wk = cw_ref[k:k + 1, c0:c0 + CONV_COLS]
                r, j = (shift0 + k) % SUBLANES, (shift0 + k) // SUBLANES
                if r == 0:
                    tap = win[j * SUBLANES:j * SUBLANES + CONV_ROWS, :]
                else:
                    tap = sh[r, j * SUBLANES:j * SUBLANES + CONV_ROWS, :]
                acc = acc + wk * tap
            ybuf[r0:r0 + CONV_ROWS, c0:c0 + CONV_COLS] = acc
    ubuf[0:CONV_HALO, :] = ubuf[tm:tm + CONV_HALO, :]
    y = _layer_norm(ybuf[...] + cb_ref[...], g_ref[...], b_ref[...])
    o_ref[...] = (y * _sigmoid(y)).astype(BF16)


def _conv(x2, wc, cw, cb, g, b, seq):
    t, d = x2.shape
    tm = TM_CONV
    c = cw.shape[1]
    full = lambda a: pl.BlockSpec(a.shape, lambda i: (0,) * a.ndim)
    return pl.pallas_call(
        functools.partial(_conv_kernel, tiles_per_seq=seq // tm),
        grid=(t // tm,),
        in_specs=[pl.BlockSpec((tm, d), lambda i: (i, 0)), full(wc), full(cw), full(cb), full(g), full(b)],
        out_specs=pl.BlockSpec((tm, c), lambda i: (i, 0)),
        out_shape=jax.ShapeDtypeStruct((t, c), BF16),
        scratch_shapes=[pltpu.VMEM((tm + CONV_HALO, c), F32), pltpu.VMEM((tm, c), F32),
                        pltpu.VMEM((CONV_SHIFT_BUFS, SUBLANES, CONV_ROWS + CONV_HALO - SUBLANES, CONV_COLS), F32)],
        compiler_params=_cparams(("arbitrary",)),
        name="conv",
    )(x2, wc, cw, cb, g, b)


def _attn_kernel(q_ref, k_ref, v_ref, o_ref, *, scale):
    tq = q_ref.shape[0]
    heads = q_ref.shape[1] // HEAD_QK_PAD
    i = pl.program_id(2)
    c = scale * 1.4426950408889634

    ts = ATTN_STRIP
    chains = [(h, r) for h in range(heads) for r in range(tq // ts)]

    def block(j, carry, masked):
        r0 = pl.multiple_of(j * tq, tq)
        out = []
        for (h, r), (m, l, acc) in zip(chains, carry):
            q = q_ref[r * ts:(r + 1) * ts, h * HEAD_QK_PAD:(h + 1) * HEAD_QK_PAD]
            kb = k_ref[pl.ds(r0, tq), h * HEAD_QK_PAD:(h + 1) * HEAD_QK_PAD]
            vb = v_ref[pl.ds(r0, tq), h * V_HEAD_DIM:(h + 1) * V_HEAD_DIM]
            s = lax.dot_general(q, kb, (((1,), (1,)), ((), ())), preferred_element_type=F32)
            if masked:
                row = lax.broadcasted_iota(I32, (ts, tq), 0) + r * ts
                col = lax.broadcasted_iota(I32, (ts, tq), 1)
                s = jnp.where(col <= row, s, -jnp.inf)
            m_new = jnp.maximum(m, jnp.max(s, axis=1, keepdims=True))
            alpha = jnp.exp2((m - m_new) * c)
            p = jnp.exp2((s - m_new) * c)
            l = alpha * l + jnp.sum(p, axis=1, keepdims=True)
            acc = alpha * acc + jnp.dot(p.astype(BF16), vb, preferred_element_type=F32)
            out.append((m_new, l, acc))
        return tuple(out)

    init = tuple((jnp.full((ts, 1), -jnp.inf, F32), jnp.zeros((ts, 1), F32),
                  jnp.zeros((ts, V_HEAD_DIM), F32)) for _ in chains)
    carry = lax.fori_loop(0, i, lambda j, cr: block(j, cr, False), init)
    final = block(i, carry, True)
    for (h, r), (_, l, acc) in zip(chains, final):
        o_ref[r * ts:(r + 1) * ts, h * V_HEAD_DIM:(h + 1) * V_HEAD_DIM] = (acc / l).astype(BF16)


def _attn(q, k, v, batch, seq):
    tq = TQ_ATTN
    nq = seq // tq
    hb = HEADS_PER_ATTN_STEP
    scale = (QK_NOPE_DIM + QK_ROPE_DIM) ** -0.5
    return pl.pallas_call(
        functools.partial(_attn_kernel, scale=scale),
        grid=(batch, N_HEADS // hb, nq),
        in_specs=[pl.BlockSpec((tq, hb * HEAD_QK_PAD), lambda b, h, i: (b * nq + i, h)),
                  pl.BlockSpec((seq, hb * HEAD_QK_PAD), lambda b, h, i: (b, h)),
                  pl.BlockSpec((seq, hb * V_HEAD_DIM), lambda b, h, i: (b, h))],
        out_specs=pl.BlockSpec((tq, hb * V_HEAD_DIM), lambda b, h, i: (b * nq + i, h)),
        out_shape=jax.ShapeDtypeStruct((batch * seq, N_HEADS * V_HEAD_DIM), BF16),
        compiler_params=_cparams(("arbitrary", "arbitrary", "arbitrary")),
        name="attn",
    )(q, k, v)


def _pack_bf16_pairs(v):
    c = v.shape[1] // 2
    lo = pltpu.bitcast(v[:, :c].astype(BF16).astype(F32), U32)
    hi = pltpu.bitcast(v[:, c:].astype(BF16).astype(F32), U32)
    return (hi & jnp.uint32(0xFFFF0000)) | (lo >> 16)


def _unpack_bf16_pairs(p):
    lo = pltpu.bitcast(p << 16, F32).astype(BF16)
    hi = pltpu.bitcast(p & jnp.uint32(0xFFFF0000), F32).astype(BF16)
    return jnp.concatenate([lo, hi], axis=1)


def _store_row_planes(ref, v):
    n = v.shape[0]
    p = ref.shape[0] // n
    for j in range(p):
        ref[pl.ds(j, n, stride=p), :] = v[:, j * LANES:(j + 1) * LANES]


def _load_row_planes(ref, p):
    n = ref.shape[0] // p
    return jnp.concatenate([ref[pl.ds(j, n, stride=p), :] for j in range(p)], axis=1)


def _oproj_kernel(a_ref, c_ref, x_ref, woa_ref, woc_ref, g_ref, b_ref, wr_ref,
                  h_ref, hp_ref, lg_ref, *, alpha):
    mix = jnp.dot(a_ref[...], woa_ref[...], preferred_element_type=F32)
    mix = mix + jnp.dot(c_ref[...], woc_ref[...], preferred_element_type=F32)
    h = _layer_norm(alpha * x_ref[...] + mix, g_ref[...], b_ref[...])
    h_ref[...] = h
    _store_row_planes(hp_ref, _pack_bf16_pairs(h))
    h_hi = h.astype(BF16)
    h_lo = (h - h_hi.astype(F32)).astype(BF16)
    logits = jnp.dot(jnp.concatenate([h_hi, h_lo, h_hi], axis=1), wr_ref[...], preferred_element_type=F32)
    lg_ref[...] = logits.T[:N_EXPERTS, :]


def _oproj(attn, conv, x2, woa, woc, g, b, wr, alpha):
    t, d = x2.shape
    tm = TM_PROJ
    full = lambda a: pl.BlockSpec(a.shape, lambda i: (0,) * a.ndim)
    return pl.pallas_call(
        functools.partial(_oproj_kernel, alpha=alpha),
        grid=(t // tm,),
        in_specs=[pl.BlockSpec((tm, attn.shape[1]), lambda i: (i, 0)),
                  pl.BlockSpec((tm, conv.shape[1]), lambda i: (i, 0)),
                  pl.BlockSpec((tm, d), lambda i: (i, 0)),
                  full(woa), full(woc), full(g), full(b), full(wr)],
        out_specs=[pl.BlockSpec((tm, d), lambda i: (i, 0)),
                   pl.BlockSpec((tm * (d // 2 // LANES), LANES), lambda i: (i, 0)),
                   pl.BlockSpec((N_EXPERTS, tm), lambda i: (0, i))],
        out_shape=[jax.ShapeDtypeStruct((t, d), F32),
                   jax.ShapeDtypeStruct((t * (d // 2 // LANES), LANES), U32),
                   jax.ShapeDtypeStruct((N_EXPERTS, t), F32)],
        compiler_params=_cparams(("arbitrary",)),
        name="oproj",
    )(attn, conv, x2, woa, woc, g, b, wr)


def _route_kernel(lg_ref, bias_ref, eidx_ref, rank_ref, wts_ref, cnt_ref, carry_ref):
    ne, tm = lg_ref.shape
    i = pl.program_id(0)

    @pl.when(i == 0)
    def _():
        carry_ref[...] = jnp.zeros(carry_ref.shape, F32)

    scores = _sigmoid(lg_ref[...])
    biased = scores + bias_ref[...]
    neg = -jnp.inf
    sub8 = lax.broadcasted_iota(I32, (GROUP_SIZE, tm), 0)
    gscore = []
    for g in range(N_EXPERT_GROUPS):
        blk = biased[g * GROUP_SIZE:(g + 1) * GROUP_SIZE, :]
        m1 = jnp.max(blk, axis=0, keepdims=True)
        i1 = jnp.min(jnp.where(blk == m1, sub8, GROUP_SIZE), axis=0, keepdims=True)
        m2 = jnp.max(jnp.where(sub8 == i1, neg, blk), axis=0, keepdims=True)
        gscore.append(m1 + m2)
    kept = []
    for g in range(N_EXPERT_GROUPS):
        beat = jnp.zeros((1, tm), I32)
        for o in range(N_EXPERT_GROUPS):
            if o < g:
                beat = beat + (gscore[o] >= gscore[g]).astype(I32)
            elif o > g:
                beat = beat + (gscore[o] > gscore[g]).astype(I32)
        kept.append(jnp.where(beat < TOPK_GROUPS, biased[g * GROUP_SIZE:(g + 1) * GROUP_SIZE, :], neg))
    cur = jnp.concatenate(kept, axis=0)
    sub = lax.broadcasted_iota(I32, (ne, tm), 0)
    sel_any = jnp.zeros((ne, tm), F32)
    picks = []
    wsum = jnp.zeros((1, tm), F32)
    for k in range(TOP_K):
        m = jnp.max(cur, axis=0, keepdims=True)
        ei = jnp.min(jnp.where(cur == m, sub, ne), axis=0, keepdims=True)
        sel = sub == ei
        w = jnp.sum(jnp.where(sel, scores, 0.0), axis=0, keepdims=True)
        cur = jnp.where(sel, neg, cur)
        sel_any = sel_any + sel.astype(F32)
        wsum = wsum + w
        picks.append((ei, sel, w))
    r = lax.broadcasted_iota(I32, (tm, tm), 0)
    c = lax.broadcasted_iota(I32, (tm, tm), 1)
    upper = (r <= c).astype(BF16)
    cum = jnp.dot(sel_any.astype(BF16), upper, preferred_element_type=F32)
    carry = carry_ref[:, 0:1]
    excl = cum - sel_any + carry
    total = carry + jnp.sum(sel_any, axis=1, keepdims=True)
    carry_ref[...] = jnp.broadcast_to(total, carry_ref.shape)
    cnt_ref[...] = jnp.broadcast_to(total, cnt_ref.shape)
    denom = wsum + 1e-20
    eidx_ref[...] = jnp.zeros(eidx_ref.shape, I32)
    rank_ref[...] = jnp.zeros(rank_ref.shape, I32)
    wts_ref[...] = jnp.zeros(wts_ref.shape, F32)
    for k, (ei, sel, w) in enumerate(picks):
        eidx_ref[k:k + 1, :] = ei
        rank_ref[k:k + 1, :] = jnp.sum(jnp.where(sel, excl, 0.0), axis=0, keepdims=True).astype(I32)
        wts_ref[k:k + 1, :] = w / denom * ROUTED_SCALE


def _route(logits_t, bias):
    ne, t = logits_t.shape
    tm = TM_ROUTE
    return pl.pallas_call(
        _route_kernel,
        grid=(t // tm,),
        in_specs=[pl.BlockSpec((ne, tm), lambda i: (0, i)), pl.BlockSpec((ne, 1), lambda i: (0, 0))],
        out_specs=[pl.BlockSpec((TOP_K_PAD, tm), lambda i: (0, i)),
                   pl.BlockSpec((TOP_K_PAD, tm), lambda i: (0, i)),
                   pl.BlockSpec((TOP_K_PAD, tm), lambda i: (0, i)),
                   pl.BlockSpec((ne, LANES), lambda i: (0, 0))],
        out_shape=[jax.ShapeDtypeStruct((TOP_K_PAD, t), I32),
                   jax.ShapeDtypeStruct((TOP_K_PAD, t), I32),
                   jax.ShapeDtypeStruct((TOP_K_PAD, t), F32),
                   jax.ShapeDtypeStruct((ne, LANES), F32)],
        scratch_shapes=[pltpu.VMEM((ne, LANES), F32)],
        compiler_params=_cparams(("arbitrary",)),
        name="route",
    )(logits_t, bias)


def _wait_rows(src_rows, dst_rows, sem, n_rows):
    def body(_, c):
        pltpu.make_async_copy(src_rows, dst_rows, sem).wait()
        return c
    lax.fori_loop(0, n_rows // WAIT_ROWS, body, 0)


def _dispatch_kernel(zfill_ref, slots_ref, hp_ref, xs_hbm, zero_vmem, sem, zsem):
    i = pl.program_id(0)
    te = zero_vmem.shape[0]
    tc = hp_ref.shape[0]
    n_tiles = xs_hbm.shape[0] // te

    def zero_copy(tile):
        return pltpu.make_async_copy(zero_vmem, xs_hbm.at[pl.ds(pl.multiple_of(tile * te, te), te)], zsem)

    @pl.when(i == 0)
    def _():
        zero_vmem[...] = jnp.zeros(zero_vmem.shape, U32)

        def zstart(tile, c):
            @pl.when(zfill_ref[tile] != 0)
            def _():
                zero_copy(tile).start()
            return c

        def zwait(tile, c):
            @pl.when(zfill_ref[tile] != 0)
            def _():
                zero_copy(tile).wait()
            return c

        lax.fori_loop(0, n_tiles, zstart, 0)
        lax.fori_loop(0, n_tiles, zwait, 0)

    base = i * (tc * TOP_K_PAD)

    def issue(t, c):
        for k in range(TOP_K):
            slot = slots_ref[base + t * TOP_K_PAD + k]
            pltpu.make_async_copy(hp_ref.at[t], xs_hbm.at[slot], sem).start(priority=k % 2)
        return c

    lax.fori_loop(0, tc, issue, 0, unroll=ISSUE_UNROLL)
    _wait_rows(hp_ref.at[pl.ds(0, WAIT_ROWS)], xs_hbm.at[pl.ds(0, WAIT_ROWS)], sem, tc * TOP_K)


def _dispatch(zfill, slots_flat, hp, n_rows):
    t, planes, lanes = hp.shape
    tc = TC_DISP
    return pl.pallas_call(
        _dispatch_kernel,
        grid_spec=pltpu.PrefetchScalarGridSpec(
            num_scalar_prefetch=2,
            grid=(t // tc,),
            in_specs=[pl.BlockSpec((tc, planes, lanes), lambda i, z, s: (i, 0, 0))],
            out_specs=pl.BlockSpec(memory_space=pl.ANY),
            scratch_shapes=[pltpu.VMEM((TE_ROWS, planes, lanes), U32),
                            pltpu.SemaphoreType.DMA, pltpu.SemaphoreType.DMA],
        ),
        out_shape=jax.ShapeDtypeStruct((n_rows, planes, lanes), U32),
        compiler_params=_cparams(("arbitrary",)),
        name="dispatch",
    )(zfill, slots_flat, hp)


def _experts_kernel(te_ref, nv_ref, nx_ref, xs_ref, wg_hbm, wu_hbm, wd_hbm, y_ref,
                    sg, su, sd, wgb, wub, wdb, sem):
    i = pl.program_id(0)
    e = te_ref[i]
    prev = te_ref[jnp.maximum(i - 1, 0)]
    valid = i < nv_ref[0]

    def fetch(ex):
        return (pltpu.make_async_copy(wg_hbm.at[ex], sg, sem.at[0]),
                pltpu.make_async_copy(wu_hbm.at[ex], su, sem.at[1]),
                pltpu.make_async_copy(wd_hbm.at[ex], sd, sem.at[2]))

    @pl.when(i == 0)
    def _():
        for cp in fetch(e):
            cp.start()

    @pl.when(valid & ((i == 0) | (e != prev)))
    def _():
        for cp in fetch(e):
            cp.wait()
        wgb[...] = sg[...].astype(BF16)
        wub[...] = su[...].astype(BF16)
        wdb[...] = sd[...].astype(BF16)

        @pl.when(nx_ref[i] >= 0)
        def _():
            for cp in fetch(nx_ref[i]):
                cp.start()

    @pl.when(valid)
    def _():
        xs = _unpack_bf16_pairs(_load_row_planes(xs_ref, xs_ref.shape[0] // TE_ROWS))
        g = jnp.dot(xs, wgb[...], preferred_element_type=F32)
        u = jnp.dot(xs, wub[...], preferred_element_type=F32)
        hid = (g * _sigmoid(g) * u).astype(BF16)
        y = jnp.dot(hid, wdb[...], preferred_element_type=F32)
        _store_row_planes(y_ref, _pack_bf16_pairs(y))

    @pl.when(jnp.logical_not(valid))
    def _():
        y_ref[...] = jnp.zeros(y_ref.shape, U32)


def _experts(tile_expert, n_valid, next_expert, xs, w_gate, w_up, w_down):
    n_rows, xp, _ = xs.shape
    ne, d, ff = w_gate.shape
    te = TE_ROWS
    in_map = lambda i, te_ref, nv_ref, nx_ref: (jnp.minimum(i, nv_ref[0] - 1), 0)
    out_map = lambda i, te_ref, nv_ref, nx_ref: (i, 0)
    hbm = pl.BlockSpec(memory_space=pl.ANY)
    y = pl.pallas_call(
        _experts_kernel,
        grid_spec=pltpu.PrefetchScalarGridSpec(
            num_scalar_prefetch=3,
            grid=(n_rows // te,),
            in_specs=[pl.BlockSpec((te * xp, LANES), in_map), hbm, hbm, hbm],
            out_specs=pl.BlockSpec((te * xp, LANES), out_map),
            scratch_shapes=[pltpu.VMEM((d, ff), F32), pltpu.VMEM((d, ff), F32), pltpu.VMEM((ff, d), F32),
                            pltpu.VMEM((d, ff), BF16), pltpu.VMEM((d, ff), BF16), pltpu.VMEM((ff, d), BF16),
                            pltpu.SemaphoreType.DMA((3,))],
        ),
        out_shape=jax.ShapeDtypeStruct((n_rows * xp, LANES), U32),
        compiler_params=_cparams(("arbitrary",)),
        name="experts",
    )(tile_expert, n_valid, next_expert, xs.reshape(n_rows * xp, LANES), w_gate, w_up, w_down)
    return y.reshape(n_rows, xp, LANES)


def _combine_kernel(slots_ref, y_hbm, h_ref, wts_ref, wsg_ref, wsu_ref, wsd_ref, g_ref, b_ref,
                    o_ref, gbuf, rbuf, sem, *, alpha):
    tm, d = o_ref.shape
    i = pl.program_id(0)

    def issue_step(step):
        buf = step % 2
        base = step * (tm * TOP_K_PAD)

        def issue(t, c):
            for k in range(TOP_K):
                slot = slots_ref[base + t * TOP_K_PAD + k]
                pltpu.make_async_copy(y_hbm.at[slot], gbuf.at[buf, k, t], sem.at[buf]).start(priority=k % 2)
            return c

        lax.fori_loop(0, tm, issue, 0, unroll=ISSUE_UNROLL)

    @pl.when(i == 0)
    def _():
        issue_step(0)

    @pl.when(i + 1 < pl.num_programs(0))
    def _():
        issue_step(i + 1)

    h = h_ref[...]
    hb = h.astype(BF16)
    sg = jnp.dot(hb, wsg_ref[...], preferred_element_type=F32)
    su = jnp.dot(hb, wsu_ref[...], preferred_element_type=F32)
    shared = jnp.dot((sg * _sigmoid(sg) * su).astype(BF16), wsd_ref[...], preferred_element_type=F32)
    cur = i % 2
    _wait_rows(y_hbm.at[pl.ds(0, WAIT_ROWS)], gbuf.at[cur, 0, pl.ds(0, WAIT_ROWS)], sem.at[cur], tm * TOP_K)
    r_lo = r_hi = None
    for k in range(TOP_K):
        packed = gbuf[cur, k]
        w = wts_ref[:, k:k + 1, :]
        lo = w * pltpu.bitcast(packed << 16, F32)
        hi = w * pltpu.bitcast(packed & jnp.uint32(0xFFFF0000), F32)
        r_lo = lo if r_lo is None else r_lo + lo
        r_hi = hi if r_hi is None else r_hi + hi
    planes = r_lo.shape[1]
    rbuf[0] = r_lo.reshape(tm * planes, LANES)
    rbuf[1] = r_hi.reshape(tm * planes, LANES)
    routed = jnp.concatenate([_load_row_planes(rbuf.at[0], planes), _load_row_planes(rbuf.at[1], planes)], axis=1)
    o_ref[...] = _layer_norm(alpha * h + (routed + shared), g_ref[...], b_ref[...])


def _combine(slots_flat, y, h, wts_planes, wsg, wsu, wsd, g, b, alpha):
    t, d = h.shape
    tm = TM_COMB
    planes = y.shape[1]
    full = lambda a: pl.BlockSpec(a.shape, lambda i, s: (0,) * a.ndim)
    once = lambda a: pl.BlockSpec(a.shape, lambda i, s: (0,) * a.ndim, pipeline_mode=pl.Buffered(1))
    return pl.pallas_call(
        functools.partial(_combine_kernel, alpha=alpha),
        grid_spec=pltpu.PrefetchScalarGridSpec(
            num_scalar_prefetch=1,
            grid=(t // tm,),
            in_specs=[pl.BlockSpec(memory_space=pl.ANY),
                      pl.BlockSpec((tm, d), lambda i, s: (i, 0)),
                      pl.BlockSpec((tm, TOP_K_PAD, LANES), lambda i, s: (i, 0, 0)),
                      once(wsg), once(wsu), once(wsd), full(g), full(b)],
            out_specs=pl.BlockSpec((tm, d), lambda i, s: (i, 0)),
            scratch_shapes=[pltpu.VMEM((2, TOP_K, tm, planes, LANES), U32),
                            pltpu.VMEM((2, tm * planes, LANES), F32), pltpu.SemaphoreType.DMA((2,))],
        ),
        out_shape=jax.ShapeDtypeStruct((t, d), F32),
        compiler_params=_cparams(("arbitrary",)),
        name="combine",
    )(slots_flat, y, h, wts_planes, wsg, wsu, wsd, g, b)


def _prep_attention_weights(w_in, w_uq, w_ukv):
    d = w_in.shape[0]
    ql = w_uq.shape[0]
    kvl = w_ukv.shape[0]
    half = QK_ROPE_DIM // 2
    pad = LANES - QK_ROPE_DIM
    w_kr = w_in[:, ql + kvl:ql + kvl + QK_ROPE_DIM]
    w_kr_rot = jnp.concatenate([-w_kr[:, half:], w_kr[:, :half]], axis=1)
    zpad = jnp.zeros((d, pad), w_in.dtype)
    wa = jnp.concatenate([w_in[:, :ql + kvl], w_kr, zpad, w_kr_rot, zpad], axis=1).astype(BF16)
    wc = w_in[:, ql + kvl + QK_ROPE_DIM:].astype(BF16)
    uq = w_uq.reshape(ql, N_HEADS, QK_NOPE_DIM + QK_ROPE_DIM)
    uq_nope, uq_rope = uq[..., :QK_NOPE_DIM], uq[..., QK_NOPE_DIM:]
    zq = jnp.zeros((ql, N_HEADS, pad), w_uq.dtype)
    wuq = jnp.concatenate([uq_nope, uq_rope, zq], axis=-1).reshape(ql, N_HEADS * HEAD_QK_PAD).astype(BF16)
    uq_rot = jnp.concatenate([-uq_rope[..., half:], uq_rope[..., :half], zq], axis=-1)
    wuqr = uq_rot.reshape(ql, N_HEADS * LANES).astype(BF16)
    ukv = w_ukv.reshape(kvl, N_HEADS, QK_NOPE_DIM + V_HEAD_DIM)
    wukv = jnp.concatenate([ukv[..., :QK_NOPE_DIM].reshape(kvl, -1), ukv[..., QK_NOPE_DIM:].reshape(kvl, -1)],
                           axis=1).astype(BF16)
    return wa, wc, wuq, wuqr, wukv


def _layer(h, pos2, invf4, batch, seq, alpha, w_in, q_norm_g, w_uq, kv_norm_g, w_ukv, conv_w, conv_b,
           conv_ln_g, conv_ln_b, w_o, ln1_g, ln1_b, w_router, router_bias, w_gate, w_up, w_down,
           ws_gate, ws_up, ws_down, ln2_g, ln2_b):
    t, d = h.shape
    row = lambda a: a.reshape(1, -1)
    wa, wc, wuq, wuqr, wukv = _prep_attention_weights(w_in, w_uq, w_ukv)
    q, k, v = _qkv(h, pos2, invf4, wa, row(q_norm_g), row(kv_norm_g), wuq, wuqr, wukv)
    conv = _conv(h, wc, conv_w, row(conv_b), row(conv_ln_g), row(conv_ln_b), seq)
    attn = _attn(q, k, v, batch, seq)
    aw = attn.shape[1]
    wr32 = jnp.pad(w_router.astype(F32), ((0, 0), (0, LANES - N_EXPERTS)))
    wr_hi = wr32.astype(BF16)
    wr_lo = (wr32 - wr_hi.astype(F32)).astype(BF16)
    wr = jnp.concatenate([wr_hi, wr_hi, wr_lo], axis=0)
    h1, h1p, logits_t = _oproj(attn, conv, h, w_o[:aw].astype(BF16), w_o[aw:].astype(BF16),
                               row(ln1_g), row(ln1_b), wr, alpha)
    return _moe_ffn(h1, h1p, logits_t, alpha, router_bias, w_gate, w_up, w_down,
                    ws_gate, ws_up, ws_down, ln2_g, ln2_b)


def _moe_ffn(h1, h1p, logits_t, alpha, router_bias, w_gate, w_up, w_down, ws_gate, ws_up, ws_down, ln2_g, ln2_b):
    t, d = h1.shape
    row = lambda a: a.reshape(1, -1)
    eidx, rank, wts, cnt = _route(logits_t, router_bias.astype(F32).reshape(N_EXPERTS, 1))
    counts = cnt[:, 0].astype(I32)
    padded = (counts + TE_ROWS - 1) // TE_ROWS * TE_ROWS
    pend = jnp.cumsum(padded)
    poff = pend - padded
    n_rows = t * TOP_K + N_EXPERTS * TE_ROWS
    n_tiles = n_rows // TE_ROWS
    n_valid = (pend[-1] // TE_ROWS).astype(I32).reshape(1)
    tile_row = jnp.minimum(jnp.arange(n_tiles, dtype=I32), n_valid[0] - 1) * TE_ROWS
    tile_expert = jnp.sum((pend[None, :] <= tile_row[:, None]).astype(I32), axis=1)
    tile_expert = jnp.minimum(tile_expert, N_EXPERTS - 1)
    tiles = jnp.arange(n_tiles, dtype=I32)
    last_tile = jnp.where(padded > 0, pend // TE_ROWS - 1, -1)
    zfill = ((tiles >= n_valid[0]) | jnp.any(tiles[:, None] == last_tile[None, :], axis=1)).astype(I32)
    ids = jnp.arange(N_EXPERTS, dtype=I32)
    later = (ids[None, :] > ids[:, None]) & (padded[None, :] > 0)
    next_of = jnp.min(jnp.where(later, ids[None, :], N_EXPERTS), axis=1)
    next_of = jnp.where(next_of == N_EXPERTS, -1, next_of)
    next_expert = jnp.sum(jnp.where(tile_expert[:, None] == ids[None, :], next_of[None, :], 0), axis=1).astype(I32)
    slot_base = jnp.sum(jnp.where(eidx[None] == ids[:, None, None], poff[:, None, None], 0), axis=0)
    slots_flat = (slot_base + rank).T.reshape(-1)
    xs = _dispatch(zfill, slots_flat, h1p.reshape(t, -1, LANES), n_rows)
    y = _experts(tile_expert, n_valid, next_expert, xs, w_gate, w_up, w_down)
    wts_planes = jnp.broadcast_to(wts.T[:, :, None], (t, TOP_K_PAD, LANES))
    return _combine(slots_flat, y, h1, wts_planes, ws_gate.astype(BF16), ws_up.astype(BF16),
                    ws_down.astype(BF16), row(ln2_g), row(ln2_b), alpha)


def kernel(x, positions, w_in, q_norm_g, w_uq, kv_norm_g, w_ukv, conv_w, conv_b, conv_ln_g, conv_ln_b, w_o, ln1_g, ln1_b, w_router, router_bias, w_gate, w_up, w_down, ws_gate, ws_up, ws_down, ln2_g, ln2_b):
    batch, seq, d = x.shape
    depth = w_in.shape[0]
    alpha = (2.0 * depth) ** 0.25
    inv_freq = ROPE_BASE ** (-jnp.arange(0, QK_ROPE_DIM, 2, dtype=F32) / QK_ROPE_DIM)
    invf4 = jnp.tile(inv_freq, LANES // inv_freq.shape[0]).reshape(1, LANES)
    pos2 = positions.reshape(batch * seq, 1)
    h = x.reshape(batch * seq, d)
    for l in range(depth):
        h = _layer(h, pos2, invf4, batch, seq, alpha, w_in[l], q_norm_g[l], w_uq[l], kv_norm_g[l], w_ukv[l],
                   conv_w[l], conv_b[l], conv_ln_g[l], conv_ln_b[l], w_o[l], ln1_g[l], ln1_b[l],
                   w_router[l], router_bias[l], w_gate[l], w_up[l], w_down[l],
                   ws_gate[l], ws_up[l], ws_down[l], ln2_g[l], ln2_b[l])
    return h.reshape(batch, seq, d)
```

```python
import functools

import jax
import jax.numpy as jnp
from jax import lax
from jax.experimental import pallas as pl
from jax.experimental.pallas import tpu as pltpu

F32 = jnp.float32
BF16 = jnp.bfloat16
I32 = jnp.int32
U32 = jnp.uint32

N_HEADS = 8
QK_NOPE_DIM = 128
QK_ROPE_DIM = 64
V_HEAD_DIM = 128
HEAD_QK_PAD = 256
CONV_KERNEL = 31
N_EXPERTS = 64
TOP_K = 6
TOP_K_PAD = 8
N_EXPERT_GROUPS = 8
GROUP_SIZE = N_EXPERTS // N_EXPERT_GROUPS
TOPK_GROUPS = 4
ROUTED_SCALE = 2.5
ROPE_BASE = 10000.0
LN_EPS = 1e-5
RMS_EPS = 1e-6

LANES = 128
SUBLANES = 8
CONV_HALO = 32

TM_PROJ = 256
TM_CONV = 512
TQ_ATTN = 512
TK_ATTN = 512
HEADS_PER_ATTN_STEP = 4
TM_ROUTE = 512
TE_ROWS = 256
TM_COMB = 256
TC_DISP = 1024
WAIT_ROWS = 128
ISSUE_UNROLL = 4
CONV_ROWS = 64
CONV_COLS = 256
CONV_SHIFT_BUFS = 4
VMEM_LIMIT = 56 * 1024 * 1024


def _cparams(sem):
    return pltpu.CompilerParams(dimension_semantics=sem, vmem_limit_bytes=VMEM_LIMIT)


def _sigmoid(v):
    return 1.0 / (1.0 + jnp.exp(-v))


def _layer_norm(v, g, b):
    mu = jnp.mean(v, axis=-1, keepdims=True)
    d = v - mu
    var = jnp.mean(d * d, axis=-1, keepdims=True)
    return d * lax.rsqrt(var + LN_EPS) * g + b


def _rms_norm(v, g):
    ms = jnp.mean(v * v, axis=-1, keepdims=True)
    return v * lax.rsqrt(ms + RMS_EPS) * g


def _qkv_kernel(x_ref, pos_ref, invf_ref, wa_ref, qg_ref, kvg_ref, wuq_ref, wuqr_ref, wukv_ref,
                q_ref, k_ref, vt_ref):
    ql = qg_ref.shape[1]
    kvl = kvg_ref.shape[1]
    xb = x_ref[...].astype(BF16)
    lat = jnp.dot(xb, wa_ref[...], preferred_element_type=F32)
    ang = pos_ref[...].astype(F32) * invf_ref[...]
    cos = jnp.cos(ang)
    sin = jnp.sin(ang)
    cq = _rms_norm(lat[:, :ql], qg_ref[...]).astype(BF16)
    ckv = _rms_norm(lat[:, ql:ql + kvl], kvg_ref[...]).astype(BF16)
    kr = lat[:, ql + kvl:ql + kvl + LANES] * cos + lat[:, ql + kvl + LANES:ql + kvl + 2 * LANES] * sin
    kr = kr.astype(BF16)
    q = jnp.dot(cq, wuq_ref[...], preferred_element_type=F32)
    qrot = jnp.dot(cq, wuqr_ref[...], preferred_element_type=F32)
    kv = jnp.dot(ckv, wukv_ref[...], preferred_element_type=F32)
    for h in range(N_HEADS):
        c0 = h * HEAD_QK_PAD
        q_ref[:, c0:c0 + LANES] = q[:, c0:c0 + LANES].astype(BF16)
        q_ref[:, c0 + LANES:c0 + 2 * LANES] = (
            q[:, c0 + LANES:c0 + 2 * LANES] * cos + qrot[:, h * LANES:(h + 1) * LANES] * sin).astype(BF16)
        k_ref[:, c0:c0 + LANES] = kv[:, h * LANES:(h + 1) * LANES].astype(BF16)
        k_ref[:, c0 + LANES:c0 + 2 * LANES] = kr
    vt_ref[...] = kv[:, N_HEADS * QK_NOPE_DIM:].T.astype(BF16)


def _qkv(x2, pos2, invf4, wa, qg, kvg, wuq, wuqr, wukv):
    t, d = x2.shape
    tm = TM_PROJ
    full = lambda a: pl.BlockSpec(a.shape, lambda i: (0,) * a.ndim)
    return pl.pallas_call(
        _qkv_kernel,
        grid=(t // tm,),
        in_specs=[pl.BlockSpec((tm, d), lambda i: (i, 0)),
                  pl.BlockSpec((tm, 1), lambda i: (i, 0)),
                  full(invf4), full(wa), full(qg), full(kvg), full(wuq), full(wuqr), full(wukv)],
        out_specs=[pl.BlockSpec((tm, N_HEADS * HEAD_QK_PAD), lambda i: (i, 0)),
                   pl.BlockSpec((tm, N_HEADS * HEAD_QK_PAD), lambda i: (i, 0)),
                   pl.BlockSpec((N_HEADS * V_HEAD_DIM, tm), lambda i: (0, i))],
        out_shape=[jax.ShapeDtypeStruct((t, N_HEADS * HEAD_QK_PAD), BF16),
                   jax.ShapeDtypeStruct((t, N_HEADS * HEAD_QK_PAD), BF16),
                   jax.ShapeDtypeStruct((N_HEADS * V_HEAD_DIM, t), BF16)],
        compiler_params=_cparams(("arbitrary",)),
        name="qkv",
    )(x2, pos2, invf4, wa, qg, kvg, wuq, wuqr, wukv)


def _conv_kernel(x_ref, wc_ref, cw_ref, cb_ref, g_ref, b_ref, o_ref, ubuf, ybuf, shbuf, *, tiles_per_seq):
    tm, cw = o_ref.shape
    i = pl.program_id(0)
    xb = x_ref[...].astype(BF16)

    @pl.when(i % tiles_per_seq == 0)
    def _():
        ubuf[0:CONV_HALO, :] = jnp.zeros((CONV_HALO, cw), F32)

    shift0 = CONV_HALO - (CONV_KERNEL - 1)
    n_sh = shbuf.shape[0]
    chunk_no = 0
    for c0 in range(0, cw, CONV_COLS):
        a = jnp.dot(xb, wc_ref[:, c0:c0 + CONV_COLS], preferred_element_type=F32)
        gate = jnp.dot(xb, wc_ref[:, cw + c0:cw + c0 + CONV_COLS], preferred_element_type=F32)
        ubuf[CONV_HALO:CONV_HALO + tm, c0:c0 + CONV_COLS] = a * _sigmoid(gate)
        for r0 in range(0, tm, CONV_ROWS):
            sh = shbuf.at[chunk_no % n_sh]
            chunk_no += 1
            win = ubuf[r0:r0 + CONV_ROWS + CONV_HALO, c0:c0 + CONV_COLS]
            for r in range(1, SUBLANES):
                sh[r] = win[r:r + CONV_ROWS + CONV_HALO - SUBLANES, :]
            acc = jnp.zeros((CONV_ROWS, CONV_COLS), F32)
            for k in range(CONV_KERNEL):
                wk = cw_ref[k:k + 1, c0:c0 + CONV_COLS]
                r, j = (shift0 + k) % SUBLANES, (shift0 + k) // SUBLANES
                if r == 0:
                    tap = win[j * SUBLANES:j * SUBLANES + CONV_ROWS, :]
                else:
                    tap = sh[r, j * SUBLANES:j * SUBLANES + CONV_ROWS, :]
                acc = acc + wk * tap
            ybuf[r0:r0 + CONV_ROWS, c0:c0 + CONV_COLS] = acc
    ubuf[0:CONV_HALO, :] = ubuf[tm:tm + CONV_HALO, :]
    y = _layer_norm(ybuf[...] + cb_ref[...], g_ref[...], b_ref[...])
    o_ref[...] = (y * _sigmoid(y)).astype(BF16)


def _conv(x2, wc, cw, cb, g, b, seq):
    t, d = x2.shape
    tm = TM_CONV
    c = cw.shape[1]
    full = lambda a: pl.BlockSpec(a.shape, lambda i: (0,) * a.ndim)
    return pl.pallas_call(
        functools.partial(_conv_kernel, tiles_per_seq=seq // tm),
        grid=(t // tm,),
        in_specs=[pl.BlockSpec((tm, d), lambda i: (i, 0)), full(wc), full(cw), full(cb), full(g), full(b)],
        out_specs=pl.BlockSpec((tm, c), lambda i: (i, 0)),
        out_shape=jax.ShapeDtypeStruct((t, c), BF16),
        scratch_shapes=[pltpu.VMEM((tm + CONV_HALO, c), F32), pltpu.VMEM((tm, c), F32),
                        pltpu.VMEM((CONV_SHIFT_BUFS, SUBLANES, CONV_ROWS + CONV_HALO - SUBLANES, CONV_COLS), F32)],
        compiler_params=_cparams(("arbitrary",)),
        name="conv",
    )(x2, wc, cw, cb, g, b)


def _attn_kernel(q_ref, k_ref, vt_ref, o_ref, *, scale, tk):
    tq = q_ref.shape[0]
    heads = q_ref.shape[1] // HEAD_QK_PAD
    i = pl.program_id(2)
    c = scale * 1.4426950408889634
    n_full = (i * tq) // tk
    q_off = i * tq - n_full * tk

    def block(j, carry, masked):
        r0 = pl.multiple_of(j * tk, tk)
        out = []
        scores = []
        for h in range(heads):
            q = q_ref[:, h * HEAD_QK_PAD:(h + 1) * HEAD_QK_PAD]
            kb = k_ref[pl.ds(r0, tk), h * HEAD_QK_PAD:(h + 1) * HEAD_QK_PAD]
            scores.append(lax.dot_general(kb, q, (((1,), (1,)), ((), ())),
                                          preferred_element_type=F32))
        for h, (m, l, acc) in enumerate(carry):
            vtb = vt_ref[h * V_HEAD_DIM:(h + 1) * V_HEAD_DIM, pl.ds(r0, tk)]
            st = scores[h]
            if masked:
                key = lax.broadcasted_iota(I32, (tk, tq), 0)
                qry = lax.broadcasted_iota(I32, (tk, tq), 1) + q_off
                st = jnp.where(key <= qry, st, -jnp.inf)
            m_blk = jnp.max(jnp.max(st.reshape(SUBLANES, tk // SUBLANES, tq), axis=0), axis=0, keepdims=True)
            m_new = jnp.maximum(m, m_blk)
            alpha = jnp.exp2((m - m_new) * c)
            pt = jnp.exp2((st - m_new) * c)
            l_blk = jnp.sum(jnp.sum(pt.reshape(SUBLANES, tk // SUBLANES, tq), axis=0), axis=0, keepdims=True)
            l = alpha * l + l_blk
            acc = alpha * acc + jnp.dot(vtb, pt.astype(BF16), preferred_element_type=F32)
            out.append((m_new, l, acc))
        return tuple(out)

    init = tuple((jnp.full((1, tq), -jnp.inf, F32), jnp.zeros((1, tq), F32),
                  jnp.zeros((V_HEAD_DIM, tq), F32)) for _ in range(heads))
    carry = lax.fori_loop(0, n_full, lambda j, cr: block(j, cr, False), init)
    final = block(n_full, carry, True)
    for h, (_, l, acc) in enumerate(final):
        o_ref[:, h * V_HEAD_DIM:(h + 1) * V_HEAD_DIM] = (acc / l).T.astype(BF16)


def _attn(q, k, v, batch, seq):
    tq = TQ_ATTN
    nq = seq // tq
    hb = HEADS_PER_ATTN_STEP
    scale = (QK_NOPE_DIM + QK_ROPE_DIM) ** -0.5
    return pl.pallas_call(
        functools.partial(_attn_kernel, scale=scale, tk=TK_ATTN),
        grid=(batch, N_HEADS // hb, nq),
        in_specs=[pl.BlockSpec((tq, hb * HEAD_QK_PAD), lambda b, h, i: (b * nq + i, h)),
                  pl.BlockSpec((seq, hb * HEAD_QK_PAD), lambda b, h, i: (b, h)),
                  pl.BlockSpec((hb * V_HEAD_DIM, seq), lambda b, h, i: (h, b))],
        out_specs=pl.BlockSpec((tq, hb * V_HEAD_DIM), lambda b, h, i: (b * nq + i, h)),
        out_shape=jax.ShapeDtypeStruct((batch * seq, N_HEADS * V_HEAD_DIM), BF16),
        compiler_params=_cparams(("arbitrary", "arbitrary", "arbitrary")),
        name="attn",
    )(q, k, v)


def _pack_bf16_pairs(v):
    c = v.shape[1] // 2
    lo = pltpu.bitcast(v[:, :c].astype(BF16).astype(F32), U32)
    hi = pltpu.bitcast(v[:, c:].astype(BF16).astype(F32), U32)
    return (hi & jnp.uint32(0xFFFF0000)) | (lo >> 16)


def _unpack_bf16_pairs(p):
    lo = pltpu.bitcast(p << 16, F32).astype(BF16)
    hi = pltpu.bitcast(p & jnp.uint32(0xFFFF0000), F32).astype(BF16)
    return jnp.concatenate([lo, hi], axis=1)


def _store_row_planes(ref, v):
    n = v.shape[0]
    p = ref.shape[0] // n
    for j in range(p):
        ref[pl.ds(j, n, stride=p), :] = v[:, j * LANES:(j + 1) * LANES]


def _load_row_planes(ref, p):
    n = ref.shape[0] // p
    return jnp.concatenate([ref[pl.ds(j, n, stride=p), :] for j in range(p)], axis=1)


def _oproj_kernel(a_ref, c_ref, x_ref, woa_ref, woc_ref, g_ref, b_ref, wr_ref,
                  h_ref, hp_ref, lg_ref, *, alpha):
    mix = jnp.dot(a_ref[...], woa_ref[...], preferred_element_type=F32)
    mix = mix + jnp.dot(c_ref[...], woc_ref[...], preferred_element_type=F32)
    h = _layer_norm(alpha * x_ref[...] + mix, g_ref[...], b_ref[...])
    h_ref[...] = h
    _store_row_planes(hp_ref, _pack_bf16_pairs(h))
    h_hi = h.astype(BF16)
    h_lo = (h - h_hi.astype(F32)).astype(BF16)
    logits = jnp.dot(jnp.concatenate([h_hi, h_lo, h_hi], axis=1), wr_ref[...], preferred_element_type=F32)
    lg_ref[...] = logits.T[:N_EXPERTS, :]


def _oproj(attn, conv, x2, woa, woc, g, b, wr, alpha):
    t, d = x2.shape
    tm = TM_PROJ
    full = lambda a: pl.BlockSpec(a.shape, lambda i: (0,) * a.ndim)
    return pl.pallas_call(
        functools.partial(_oproj_kernel, alpha=alpha),
        grid=(t // tm,),
        in_specs=[pl.BlockSpec((tm, attn.shape[1]), lambda i: (i, 0)),
                  pl.BlockSpec((tm, conv.shape[1]), lambda i: (i, 0)),
                  pl.BlockSpec((tm, d), lambda i: (i, 0)),
                  full(woa), full(woc), full(g), full(b), full(wr)],
        out_specs=[pl.BlockSpec((tm, d), lambda i: (i, 0)),
                   pl.BlockSpec((tm * (d // 2 // LANES), LANES), lambda i: (i, 0)),
                   pl.BlockSpec((N_EXPERTS, tm), lambda i: (0, i))],
        out_shape=[jax.ShapeDtypeStruct((t, d), F32),
                   jax.ShapeDtypeStruct((t * (d // 2 // LANES), LANES), U32),
                   jax.ShapeDtypeStruct((N_EXPERTS, t), F32)],
        compiler_params=_cparams(("arbitrary",)),
        name="oproj",
    )(attn, conv, x2, woa, woc, g, b, wr)


def _route_kernel(lg_ref, bias_ref, eidx_ref, rank_ref, wts_ref, cnt_ref, carry_ref):
    ne, tm = lg_ref.shape
    i = pl.program_id(0)

    @pl.when(i == 0)
    def _():
        carry_ref[...] = jnp.zeros(carry_ref.shape, F32)

    scores = _sigmoid(lg_ref[...])
    biased = scores + bias_ref[...]
    neg = -jnp.inf
    sub8 = lax.broadcasted_iota(I32, (GROUP_SIZE, tm), 0)
    gscore = []
    for g in range(N_EXPERT_GROUPS):
        blk = biased[g * GROUP_SIZE:(g + 1) * GROUP_SIZE, :]
        m1 = jnp.max(blk, axis=0, keepdims=True)
        i1 = jnp.min(jnp.where(blk == m1, sub8, GROUP_SIZE), axis=0, keepdims=True)
        m2 = jnp.max(jnp.where(sub8 == i1, neg, blk), axis=0, keepdims=True)
        gscore.append(m1 + m2)
    kept = []
    for g in range(N_EXPERT_GROUPS):
        beat = jnp.zeros((1, tm), I32)
        for o in range(N_EXPERT_GROUPS):
            if o < g:
                beat = beat + (gscore[o] >= gscore[g]).astype(I32)
            elif o > g:
                beat = beat + (gscore[o] > gscore[g]).astype(I32)
        kept.append(jnp.where(beat < TOPK_GROUPS, biased[g * GROUP_SIZE:(g + 1) * GROUP_SIZE, :], neg))
    cur = jnp.concatenate(kept, axis=0)
    sub = lax.broadcasted_iota(I32, (ne, tm), 0)
    sel_any = jnp.zeros((ne, tm), F32)
    picks = []
    wsum = jnp.zeros((1, tm), F32)
    for k in range(TOP_K):
        m = jnp.max(cur, axis=0, keepdims=True)
        ei = jnp.min(jnp.where(cur == m, sub, ne), axis=0, keepdims=True)
        sel = sub == ei
        w = jnp.sum(jnp.where(sel, scores, 0.0), axis=0, keepdims=True)
        cur = jnp.where(sel, neg, cur)
        sel_any = sel_any + sel.astype(F32)
        wsum = wsum + w
        picks.append((ei, sel, w))
    r = lax.broadcasted_iota(I32, (tm, tm), 0)
    c = lax.broadcasted_iota(I32, (tm, tm), 1)
    upper = (r <= c).astype(BF16)
    cum = jnp.dot(sel_any.astype(BF16), upper, preferred_element_type=F32)
    carry = carry_ref[:, 0:1]
    excl = cum - sel_any + carry
    total = carry + jnp.sum(sel_any, axis=1, keepdims=True)
    carry_ref[...] = jnp.broadcast_to(total, carry_ref.shape)
    cnt_ref[...] = jnp.broadcast_to(total, cnt_ref.shape)
    denom = wsum + 1e-20
    eidx_ref[...] = jnp.zeros(eidx_ref.shape, I32)
    rank_ref[...] = jnp.zeros(rank_ref.shape, I32)
    wts_ref[...] = jnp.zeros(wts_ref.shape, F32)
    for k, (ei, sel, w) in enumerate(picks):
        eidx_ref[k:k + 1, :] = ei
        rank_ref[k:k + 1, :] = jnp.sum(jnp.where(sel, excl, 0.0), axis=0, keepdims=True).astype(I32)
        wts_ref[k:k + 1, :] = w / denom * ROUTED_SCALE


def _route(logits_t, bias):
    ne, t = logits_t.shape
    tm = TM_ROUTE
    return pl.pallas_call(
        _route_kernel,
        grid=(t // tm,),
        in_specs=[pl.BlockSpec((ne, tm), lambda i: (0, i)), pl.BlockSpec((ne, 1), lambda i: (0, 0))],
        out_specs=[pl.BlockSpec((TOP_K_PAD, tm), lambda i: (0, i)),
                   pl.BlockSpec((TOP_K_PAD, tm), lambda i: (0, i)),
                   pl.BlockSpec((TOP_K_PAD, tm), lambda i: (0, i)),
                   pl.BlockSpec((ne, LANES), lambda i: (0, 0))],
        out_shape=[jax.ShapeDtypeStruct((TOP_K_PAD, t), I32),
                   jax.ShapeDtypeStruct((TOP_K_PAD, t), I32),
                   jax.ShapeDtypeStruct((TOP_K_PAD, t), F32),
                   jax.ShapeDtypeStruct((ne, LANES), F32)],
        scratch_shapes=[pltpu.VMEM((ne, LANES), F32)],
        compiler_params=_cparams(("arbitrary",)),
        name="route",
    )(logits_t, bias)


def _wait_rows(src_rows, dst_rows, sem, n_rows):
    def body(_, c):
        pltpu.make_async_copy(src_rows, dst_rows, sem).wait()
        return c
    lax.fori_loop(0, n_rows // WAIT_ROWS, body, 0)


def _dispatch_kernel(zfill_ref, slots_ref, hp_ref, xs_hbm, zero_vmem, sem, zsem):
    i = pl.program_id(0)
    te = zero_vmem.shape[0]
    tc = hp_ref.shape[0]
    n_tiles = xs_hbm.shape[0] // te

    def zero_copy(tile):
        return pltpu.make_async_copy(zero_vmem, xs_hbm.at[pl.ds(pl.multiple_of(tile * te, te), te)], zsem)

    @pl.when(i == 0)
    def _():
        zero_vmem[...] = jnp.zeros(zero_vmem.shape, U32)

        def zstart(tile, c):
            @pl.when(zfill_ref[tile] != 0)
            def _():
                zero_copy(tile).start()
            return c

        def zwait(tile, c):
            @pl.when(zfill_ref[tile] != 0)
            def _():
                zero_copy(tile).wait()
            return c

        lax.fori_loop(0, n_tiles, zstart, 0)
        lax.fori_loop(0, n_tiles, zwait, 0)

    base = i * (tc * TOP_K_PAD)

    def issue(t, c):
        for k in range(TOP_K):
            slot = slots_ref[base + t * TOP_K_PAD + k]
            pltpu.make_async_copy(hp_ref.at[t], xs_hbm.at[slot], sem).start(priority=k % 2)
        return c

    lax.fori_loop(0, tc, issue, 0, unroll=ISSUE_UNROLL)
    _wait_rows(hp_ref.at[pl.ds(0, WAIT_ROWS)], xs_hbm.at[pl.ds(0, WAIT_ROWS)], sem, tc * TOP_K)


def _dispatch(zfill, slots_flat, hp, n_rows):
    t, planes, lanes = hp.shape
    tc = TC_DISP
    return pl.pallas_call(
        _dispatch_kernel,
        grid_spec=pltpu.PrefetchScalarGridSpec(
            num_scalar_prefetch=2,
            grid=(t // tc,),
            in_specs=[pl.BlockSpec((tc, planes, lanes), lambda i, z, s: (i, 0, 0))],
            out_specs=pl.BlockSpec(memory_space=pl.ANY),
            scratch_shapes=[pltpu.VMEM((TE_ROWS, planes, lanes), U32),
                            pltpu.SemaphoreType.DMA, pltpu.SemaphoreType.DMA],
        ),
        out_shape=jax.ShapeDtypeStruct((n_rows, planes, lanes), U32),
        compiler_params=_cparams(("arbitrary",)),
        name="dispatch",
    )(zfill, slots_flat, hp)


def _experts_kernel(te_ref, nv_ref, nx_ref, xs_ref, wg_hbm, wu_hbm, wd_hbm, y_ref,
                    sg, su, sd, wgb, wub, wdb, sem):
    i = pl.program_id(0)
    e = te_ref[i]
    prev = te_ref[jnp.maximum(i - 1, 0)]
    valid = i < nv_ref[0]

    def fetch(ex):
        return (pltpu.make_async_copy(wg_hbm.at[ex], sg, sem.at[0]),
                pltpu.make_async_copy(wu_hbm.at[ex], su, sem.at[1]),
                pltpu.make_async_copy(wd_hbm.at[ex], sd, sem.at[2]))

    @pl.when(i == 0)
    def _():
        for cp in fetch(e):
            cp.start()

    @pl.when(valid & ((i == 0) | (e != prev)))
    def _():
        for cp in fetch(e):
            cp.wait()
        wgb[...] = sg[...].astype(BF16)
        wub[...] = su[...].astype(BF16)
        wdb[...] = sd[...].astype(BF16)

        @pl.when(nx_ref[i] >= 0)
        def _():
            for cp in fetch(nx_ref[i]):
                cp.start()

    @pl.when(valid)
    def _():
        xs = _unpack_bf16_pairs(_load_row_planes(xs_ref, xs_ref.shape[0] // TE_ROWS))
        g = jnp.dot(xs, wgb[...], preferred_element_type=F32)
        u = jnp.dot(xs, wub[...], preferred_element_type=F32)
        hid = (g * _sigmoid(g) * u).astype(BF16)
        y = jnp.dot(hid, wdb[...], preferred_element_type=F32)
        _store_row_planes(y_ref, _pack_bf16_pairs(y))

    @pl.when(jnp.logical_not(valid))
    def _():
        y_ref[...] = jnp.zeros(y_ref.shape, U32)


def _experts(tile_expert, n_valid, next_expert, xs, w_gate, w_up, w_down):
    n_rows, xp, _ = xs.shape
    ne, d, ff = w_gate.shape
    te = TE_ROWS
    in_map = lambda i, te_ref, nv_ref, nx_ref: (jnp.minimum(i, nv_ref[0] - 1), 0)
    out_map = lambda i, te_ref, nv_ref, nx_ref: (i, 0)
    hbm = pl.BlockSpec(memory_space=pl.ANY)
    y = pl.pallas_call(
        _experts_kernel,
        grid_spec=pltpu.PrefetchScalarGridSpec(
            num_scalar_prefetch=3,
            grid=(n_rows // te,),
            in_specs=[pl.BlockSpec((te * xp, LANES), in_map), hbm, hbm, hbm],
            out_specs=pl.BlockSpec((te * xp, LANES), out_map),
            scratch_shapes=[pltpu.VMEM((d, ff), F32), pltpu.VMEM((d, ff), F32), pltpu.VMEM((ff, d), F32),
                            pltpu.VMEM((d, ff), BF16), pltpu.VMEM((d, ff), BF16), pltpu.VMEM((ff, d), BF16),
                            pltpu.SemaphoreType.DMA((3,))],
        ),
        out_shape=jax.ShapeDtypeStruct((n_rows * xp, LANES), U32),
        compiler_params=_cparams(("arbitrary",)),
        name="experts",
    )(tile_expert, n_valid, next_expert, xs.reshape(n_rows * xp, LANES), w_gate, w_up, w_down)
    return y.reshape(n_rows, xp, LANES)


def _combine_kernel(slots_ref, y_hbm, h_ref, wts_ref, wsg_ref, wsu_ref, wsd_ref, g_ref, b_ref,
                    o_ref, gbuf, rbuf, sem, *, alpha):
    tm, d = o_ref.shape
    i = pl.program_id(0)

    def issue_step(step):
        buf = step % 2
        base = step * (tm * TOP_K_PAD)

        def issue(t, c):
            for k in range(TOP_K):
                slot = slots_ref[base + t * TOP_K_PAD + k]
                pltpu.make_async_copy(y_hbm.at[slot], gbuf.at[buf, k, t], sem.at[buf]).start(priority=k % 2)
            return c

        lax.fori_loop(0, tm, issue, 0, unroll=ISSUE_UNROLL)

    @pl.when(i == 0)
    def _():
        issue_step(0)

    @pl.when(i + 1 < pl.num_programs(0))
    def _():
        issue_step(i + 1)

    h = h_ref[...]
    hb = h.astype(BF16)
    sg = jnp.dot(hb, wsg_ref[...], preferred_element_type=F32)
    su = jnp.dot(hb, wsu_ref[...], preferred_element_type=F32)
    shared = jnp.dot((sg * _sigmoid(sg) * su).astype(BF16), wsd_ref[...], preferred_element_type=F32)
    cur = i % 2
    _wait_rows(y_hbm.at[pl.ds(0, WAIT_ROWS)], gbuf.at[cur, 0, pl.ds(0, WAIT_ROWS)], sem.at[cur], tm * TOP_K)
    r_lo = r_hi = None
    for k in range(TOP_K):
        packed = gbuf[cur, k]
        w = wts_ref[:, k:k + 1, :]
        lo = w * pltpu.bitcast(packed << 16, F32)
        hi = w * pltpu.bitcast(packed & jnp.uint32(0xFFFF0000), F32)
        r_lo = lo if r_lo is None else r_lo + lo
        r_hi = hi if r_hi is None else r_hi + hi
    planes = r_lo.shape[1]
    rbuf[0] = r_lo.reshape(tm * planes, LANES)
    rbuf[1] = r_hi.reshape(tm * planes, LANES)
    routed = jnp.concatenate([_load_row_planes(rbuf.at[0], planes), _load_row_planes(rbuf.at[1], planes)], axis=1)
    o_ref[...] = _layer_norm(alpha * h + (routed + shared), g_ref[...], b_ref[...])


def _combine(slots_flat, y, h, wts_planes, wsg, wsu, wsd, g, b, alpha):
    t, d = h.shape
    tm = TM_COMB
    planes = y.shape[1]
    full = lambda a: pl.BlockSpec(a.shape, lambda i, s: (0,) * a.ndim)
    once = lambda a: pl.BlockSpec(a.shape, lambda i, s: (0,) * a.ndim, pipeline_mode=pl.Buffered(1))
    return pl.pallas_call(
        functools.partial(_combine_kernel, alpha=alpha),
        grid_spec=pltpu.PrefetchScalarGridSpec(
            num_scalar_prefetch=1,
            grid=(t // tm,),
            in_specs=[pl.BlockSpec(memory_space=pl.ANY),
                      pl.BlockSpec((tm, d), lambda i, s: (i, 0)),
                      pl.BlockSpec((tm, TOP_K_PAD, LANES), lambda i, s: (i, 0, 0)),
                      once(wsg), once(wsu), once(wsd), full(g), full(b)],
            out_specs=pl.BlockSpec((tm, d), lambda i, s: (i, 0)),
            scratch_shapes=[pltpu.VMEM((2, TOP_K, tm, planes, LANES), U32),
                            pltpu.VMEM((2, tm * planes, LANES), F32), pltpu.SemaphoreType.DMA((2,))],
        ),
        out_shape=jax.ShapeDtypeStruct((t, d), F32),
        compiler_params=_cparams(("arbitrary",)),
        name="combine",
    )(slots_flat, y, h, wts_planes, wsg, wsu, wsd, g, b)


def _prep_attention_weights(w_in, w_uq, w_ukv):
    d = w_in.shape[0]
    ql = w_uq.shape[0]
    kvl = w_ukv.shape[0]
    half = QK_ROPE_DIM // 2
    pad = LANES - QK_ROPE_DIM
    w_kr = w_in[:, ql + kvl:ql + kvl + QK_ROPE_DIM]
    w_kr_rot = jnp.concatenate([-w_kr[:, half:], w_kr[:, :half]], axis=1)
    zpad = jnp.zeros((d, pad), w_in.dtype)
    wa = jnp.concatenate([w_in[:, :ql + kvl], w_kr, zpad, w_kr_rot, zpad], axis=1).astype(BF16)
    wc = w_in[:, ql + kvl + QK_ROPE_DIM:].astype(BF16)
    uq = w_uq.reshape(ql, N_HEADS, QK_NOPE_DIM + QK_ROPE_DIM)
    uq_nope, uq_rope = uq[..., :QK_NOPE_DIM], uq[..., QK_NOPE_DIM:]
    zq = jnp.zeros((ql, N_HEADS, pad), w_uq.dtype)
    wuq = jnp.concatenate([uq_nope, uq_rope, zq], axis=-1).reshape(ql, N_HEADS * HEAD_QK_PAD).astype(BF16)
    uq_rot = jnp.concatenate([-uq_rope[..., half:], uq_rope[..., :half], zq], axis=-1)
    wuqr = uq_rot.reshape(ql, N_HEADS * LANES).astype(BF16)
    ukv = w_ukv.reshape(kvl, N_HEADS, QK_NOPE_DIM + V_HEAD_DIM)
    wukv = jnp.concatenate([ukv[..., :QK_NOPE_DIM].reshape(kvl, -1), ukv[..., QK_NOPE_DIM:].reshape(kvl, -1)],
                           axis=1).astype(BF16)
    return wa, wc, wuq, wuqr, wukv


def _layer(h, pos2, invf4, batch, seq, alpha, w_in, q_norm_g, w_uq, kv_norm_g, w_ukv, conv_w, conv_b,
           conv_ln_g, conv_ln_b, w_o, ln1_g, ln1_b, w_router, router_bias, w_gate, w_up, w_down,
           ws_gate, ws_up, ws_down, ln2_g, ln2_b):
    t, d = h.shape
    row = lambda a: a.reshape(1, -1)
    wa, wc, wuq, wuqr, wukv = _prep_attention_weights(w_in, w_uq, w_ukv)
    q, k, v = _qkv(h, pos2, invf4, wa, row(q_norm_g), row(kv_norm_g), wuq, wuqr, wukv)
    conv = _conv(h, wc, conv_w, row(conv_b), row(conv_ln_g), row(conv_ln_b), seq)
    attn = _attn(q, k, v, batch, seq)
    aw = attn.shape[1]
    wr32 = jnp.pad(w_router.astype(F32), ((0, 0), (0, LANES - N_EXPERTS)))
    wr_hi = wr32.astype(BF16)
    wr_lo = (wr32 - wr_hi.astype(F32)).astype(BF16)
    wr = jnp.concatenate([wr_hi, wr_hi, wr_lo], axis=0)
    h1, h1p, logits_t = _oproj(attn, conv, h, w_o[:aw].astype(BF16), w_o[aw:].astype(BF16),
                               row(ln1_g), row(ln1_b), wr, alpha)
    return _moe_ffn(h1, h1p, logits_t, alpha, router_bias, w_gate, w_up, w_down,
                    ws_gate, ws_up, ws_down, ln2_g, ln2_b)


def _moe_ffn(h1, h1p, logits_t, alpha, router_bias, w_gate, w_up, w_down, ws_gate, ws_up, ws_down, ln2_g, ln2_b):
    t, d = h1.shape
    row = lambda a: a.reshape(1, -1)
    eidx, rank, wts, cnt = _route(logits_t, router_bias.astype(F32).reshape(N_EXPERTS, 1))
    counts = cnt[:, 0].astype(I32)
    padded = (counts + TE_ROWS - 1) // TE_ROWS * TE_ROWS
    pend = jnp.cumsum(padded)
    poff = pend - padded
    n_rows = t * TOP_K + N_EXPERTS * TE_ROWS
    n_tiles = n_rows // TE_ROWS
    n_valid = (pend[-1] // TE_ROWS).astype(I32).reshape(1)
    tile_row = jnp.minimum(jnp.arange(n_tiles, dtype=I32), n_valid[0] - 1) * TE_ROWS
    tile_expert = jnp.sum((pend[None, :] <= tile_row[:, None]).astype(I32), axis=1)
    tile_expert = jnp.minimum(tile_expert, N_EXPERTS - 1)
    tiles = jnp.arange(n_tiles, dtype=I32)
    last_tile = jnp.where(padded > 0, pend // TE_ROWS - 1, -1)
    zfill = ((tiles >= n_valid[0]) | jnp.any(tiles[:, None] == last_tile[None, :], axis=1)).astype(I32)
    ids = jnp.arange(N_EXPERTS, dtype=I32)
    later = (ids[None, :] > ids[:, None]) & (padded[None, :] > 0)
    next_of = jnp.min(jnp.where(later, ids[None, :], N_EXPERTS), axis=1)
    next_of = jnp.where(next_of == N_EXPERTS, -1, next_of)
    next_expert = jnp.sum(jnp.where(tile_expert[:, None] == ids[None, :], next_of[None, :], 0), axis=1).astype(I32)
    slot_base = jnp.sum(jnp.where(eidx[None] == ids[:, None, None], poff[:, None, None], 0), axis=0)
    slots_flat = (slot_base + rank).T.reshape(-1)
    xs = _dispatch(zfill, slots_flat, h1p.reshape(t, -1, LANES), n_rows)
    y = _experts(tile_expert, n_valid, next_expert, xs, w_gate, w_up, w_down)
    wts_planes = jnp.broadcast_to(wts.T[:, :, None], (t, TOP_K_PAD, LANES))
    return _combine(slots_flat, y, h1, wts_planes, ws_gate.astype(BF16), ws_up.astype(BF16),
                    ws_down.astype(BF16), row(ln2_g), row(ln2_b), alpha)


def kernel(x, positions, w_in, q_norm_g, w_uq, kv_norm_g, w_ukv, conv_w, conv_b, conv_ln_g, conv_ln_b, w_o, ln1_g, ln1_b, w_router, router_bias, w_gate, w_up, w_down, ws_gate, ws_up, ws_down, ln2_g, ln2_b):
    batch, seq, d = x.shape
    depth = w_in.shape[0]
    alpha = (2.0 * depth) ** 0.25
    inv_freq = ROPE_BASE ** (-jnp.arange(0, QK_ROPE_DIM, 2, dtype=F32) / QK_ROPE_DIM)
    invf4 = jnp.tile(inv_freq, LANES // inv_freq.shape[0]).reshape(1, LANES)
    pos2 = positions.reshape(batch * seq, 1)
    h = x.reshape(batch * seq, d)
    for l in range(depth):
        h = _layer(h, pos2, invf4, batch, seq, alpha, w_in[l], q_norm_g[l], w_uq[l], kv_norm_g[l], w_ukv[l],
                   conv_w[l], conv_b[l], conv_ln_g[l], conv_ln_b[l], w_o[l], ln1_g[l], ln1_b[l],
                   w_router[l], router_bias[l], w_gate[l], w_up[l], w_down[l],
                   ws_gate[l], ws_up[l], ws_down[l], ln2_g[l], ln2_b[l])
    return h.reshape(batch, seq, d)
```

```python
import functools

import jax
import jax.numpy as jnp
from jax import lax
from jax.experimental import pallas as pl
from jax.experimental.pallas import tpu as pltpu

F32 = jnp.float32
BF16 = jnp.bfloat16
I32 = jnp.int32
U32 = jnp.uint32

N_HEADS = 8
QK_NOPE_DIM = 128
QK_ROPE_DIM = 64
V_HEAD_DIM = 128
HEAD_QK_PAD = 256
CONV_KERNEL = 31
N_EXPERTS = 64
TOP_K = 6
TOP_K_PAD = 8
N_EXPERT_GROUPS = 8
GROUP_SIZE = N_EXPERTS // N_EXPERT_GROUPS
TOPK_GROUPS = 4
ROUTED_SCALE = 2.5
ROPE_BASE = 10000.0
LN_EPS = 1e-5
RMS_EPS = 1e-6

LANES = 128
SUBLANES = 8
CONV_HALO = 32

TM_PROJ = 256
TM_CONV = 512
TQ_ATTN = 512
TK_ATTN = 512
HEADS_PER_ATTN_STEP = 4
TM_ROUTE = 512
TE_ROWS = 256
TM_COMB = 256
TC_DISP = 1024
WAIT_ROWS = 128
ISSUE_UNROLL = 4
SHARED_ROWS = 256
CONV_ROWS = 64
CONV_COLS = 256
CONV_SHIFT_BUFS = 4
VMEM_LIMIT = 56 * 1024 * 1024


def _cparams(sem):
    return pltpu.CompilerParams(dimension_semantics=sem, vmem_limit_bytes=VMEM_LIMIT)


def _sigmoid(v):
    return 1.0 / (1.0 + jnp.exp(-v))


def _layer_norm(v, g, b):
    mu = jnp.mean(v, axis=-1, keepdims=True)
    d = v - mu
    var = jnp.mean(d * d, axis=-1, keepdims=True)
    return d * lax.rsqrt(var + LN_EPS) * g + b


def _rms_norm(v, g):
    ms = jnp.mean(v * v, axis=-1, keepdims=True)
    return v * lax.rsqrt(ms + RMS_EPS) * g


def _qkv_kernel(x_ref, pos_ref, invf_ref, wa_ref, qg_ref, kvg_ref, wuq_ref, wuqr_ref, wukv_ref,
                q_ref, k_ref, vt_ref):
    ql = qg_ref.shape[1]
    kvl = kvg_ref.shape[1]
    xb = x_ref[...].astype(BF16)
    lat = jnp.dot(xb, wa_ref[...], preferred_element_type=F32)
    ang = pos_ref[...].astype(F32) * invf_ref[...]
    cos = jnp.cos(ang)
    sin = jnp.sin(ang)
    cq = _rms_norm(lat[:, :ql], qg_ref[...]).astype(BF16)
    ckv = _rms_norm(lat[:, ql:ql + kvl], kvg_ref[...]).astype(BF16)
    kr = lat[:, ql + kvl:ql + kvl + LANES] * cos + lat[:, ql + kvl + LANES:ql + kvl + 2 * LANES] * sin
    kr = kr.astype(BF16)
    q = jnp.dot(cq, wuq_ref[...], preferred_element_type=F32)
    qrot = jnp.dot(cq, wuqr_ref[...], preferred_element_type=F32)
    kv = jnp.dot(ckv, wukv_ref[...], preferred_element_type=F32)
    for h in range(N_HEADS):
        c0 = h * HEAD_QK_PAD
        q_ref[:, c0:c0 + LANES] = q[:, c0:c0 + LANES].astype(BF16)
        q_ref[:, c0 + LANES:c0 + 2 * LANES] = (
            q[:, c0 + LANES:c0 + 2 * LANES] * cos + qrot[:, h * LANES:(h + 1) * LANES] * sin).astype(BF16)
        k_ref[:, c0:c0 + LANES] = kv[:, h * LANES:(h + 1) * LANES].astype(BF16)
        k_ref[:, c0 + LANES:c0 + 2 * LANES] = kr
    vt_ref[...] = kv[:, N_HEADS * QK_NOPE_DIM:].T.astype(BF16)


def _qkv(x2, pos2, invf4, wa, qg, kvg, wuq, wuqr, wukv):
    t, d = x2.shape
    tm = TM_PROJ
    full = lambda a: pl.BlockSpec(a.shape, lambda i: (0,) * a.ndim)
    return pl.pallas_call(
        _qkv_kernel,
        grid=(t // tm,),
        in_specs=[pl.BlockSpec((tm, d), lambda i: (i, 0)),
                  pl.BlockSpec((tm, 1), lambda i: (i, 0)),
                  full(invf4), full(wa), full(qg), full(kvg), full(wuq), full(wuqr), full(wukv)],
        out_specs=[pl.BlockSpec((tm, N_HEADS * HEAD_QK_PAD), lambda i: (i, 0)),
                   pl.BlockSpec((tm, N_HEADS * HEAD_QK_PAD), lambda i: (i, 0)),
                   pl.BlockSpec((N_HEADS * V_HEAD_DIM, tm), lambda i: (0, i))],
        out_shape=[jax.ShapeDtypeStruct((t, N_HEADS * HEAD_QK_PAD), BF16),
                   jax.ShapeDtypeStruct((t, N_HEADS * HEAD_QK_PAD), BF16),
                   jax.ShapeDtypeStruct((N_HEADS * V_HEAD_DIM, t), BF16)],
        compiler_params=_cparams(("arbitrary",)),
        name="qkv",
    )(x2, pos2, invf4, wa, qg, kvg, wuq, wuqr, wukv)


def _conv_kernel(x_ref, wc_ref, cw_ref, cb_ref, g_ref, b_ref, o_ref, ubuf, ybuf, shbuf, *, tiles_per_seq):
    tm, cw = o_ref.shape
    i = pl.program_id(0)
    xb = x_ref[...].astype(BF16)

    @pl.when(i % tiles_per_seq == 0)
    def _():
        ubuf[0:CONV_HALO, :] = jnp.zeros((CONV_HALO, cw), F32)

    shift0 = CONV_HALO - (CONV_KERNEL - 1)
    n_sh = shbuf.shape[0]
    chunk_no = 0
    for c0 in range(0, cw, CONV_COLS):
        a = jnp.dot(xb, wc_ref[:, c0:c0 + CONV_COLS], preferred_element_type=F32)
        gate = jnp.dot(xb, wc_ref[:, cw + c0:cw + c0 + CONV_COLS], preferred_element_type=F32)
        ubuf[CONV_HALO:CONV_HALO + tm, c0:c0 + CONV_COLS] = a * _sigmoid(gate)
        for r0 in range(0, tm, CONV_ROWS):
            sh = shbuf.at[chunk_no % n_sh]
            chunk_no += 1
            win = ubuf[r0:r0 + CONV_ROWS + CONV_HALO, c0:c0 + CONV_COLS]
            for r in range(1, SUBLANES):
                sh[r] = win[r:r + CONV_ROWS + CONV_HALO - SUBLANES, :]
            acc = jnp.zeros((CONV_ROWS, CONV_COLS), F32)
            for k in range(CONV_KERNEL):
                wk = cw_ref[k:k + 1, c0:c0 + CONV_COLS]
                r, j = (shift0 + k) % SUBLANES, (shift0 + k) // SUBLANES
                if r == 0:
                    tap = win[j * SUBLANES:j * SUBLANES + CONV_ROWS, :]
                else:
                    tap = sh[r, j * SUBLANES:j * SUBLANES + CONV_ROWS, :]
                acc = acc + wk * tap
            ybuf[r0:r0 + CONV_ROWS, c0:c0 + CONV_COLS] = acc
    ubuf[0:CONV_HALO, :] = ubuf[tm:tm + CONV_HALO, :]
    y = _layer_norm(ybuf[...] + cb_ref[...], g_ref[...], b_ref[...])
    o_ref[...] = (y * _sigmoid(y)).astype(BF16)


def _conv(x2, wc, cw, cb, g, b, seq):
    t, d = x2.shape
    tm = TM_CONV
    c = cw.shape[1]
    full = lambda a: pl.BlockSpec(a.shape, lambda i: (0,) * a.ndim)
    return pl.pallas_call(
        functools.partial(_conv_kernel, tiles_per_seq=seq // tm),
        grid=(t // tm,),
        in_specs=[pl.BlockSpec((tm, d), lambda i: (i, 0)), full(wc), full(cw), full(cb), full(g), full(b)],
        out_specs=pl.BlockSpec((tm, c), lambda i: (i, 0)),
        out_shape=jax.ShapeDtypeStruct((t, c), BF16),
        scratch_shapes=[pltpu.VMEM((tm + CONV_HALO, c), F32), pltpu.VMEM((tm, c), F32),
                        pltpu.VMEM((CONV_SHIFT_BUFS, SUBLANES, CONV_ROWS + CONV_HALO - SUBLANES, CONV_COLS), F32)],
        compiler_params=_cparams(("arbitrary",)),
        name="conv",
    )(x2, wc, cw, cb, g, b)


def _attn_kernel(q_ref, k_ref, vt_ref, o_ref, *, scale, tk):
    tq = q_ref.shape[0]
    heads = q_ref.shape[1] // HEAD_QK_PAD
    i = pl.program_id(2)
    c = scale * 1.4426950408889634
    n_full = (i * tq) // tk
    q_off = i * tq - n_full * tk

    def block(j, carry, masked):
        r0 = pl.multiple_of(j * tk, tk)
        out = []
        scores = []
        for h in range(heads):
            q = q_ref[:, h * HEAD_QK_PAD:(h + 1) * HEAD_QK_PAD]
            kb = k_ref[pl.ds(r0, tk), h * HEAD_QK_PAD:(h + 1) * HEAD_QK_PAD]
            scores.append(lax.dot_general(kb, q, (((1,), (1,)), ((), ())),
                                          preferred_element_type=F32))
        for h, (m, l, acc) in enumerate(carry):
            vtb = vt_ref[h * V_HEAD_DIM:(h + 1) * V_HEAD_DIM, pl.ds(r0, tk)]
            st = scores[h]
            if masked:
                key = lax.broadcasted_iota(I32, (tk, tq), 0)
                qry = lax.broadcasted_iota(I32, (tk, tq), 1) + q_off
                st = jnp.where(key <= qry, st, -jnp.inf)
            m_blk = jnp.max(jnp.max(st.reshape(SUBLANES, tk // SUBLANES, tq), axis=0), axis=0, keepdims=True)
            m_new = jnp.maximum(m, m_blk)
            alpha = jnp.exp2((m - m_new) * c)
            pt = jnp.exp2((st - m_new) * c)
            l_blk = jnp.sum(jnp.sum(pt.reshape(SUBLANES, tk // SUBLANES, tq), axis=0), axis=0, keepdims=True)
            l = alpha * l + l_blk
            acc = alpha * acc + jnp.dot(vtb, pt.astype(BF16), preferred_element_type=F32)
            out.append((m_new, l, acc))
        return tuple(out)

    init = tuple((jnp.full((1, tq), -jnp.inf, F32), jnp.zeros((1, tq), F32),
                  jnp.zeros((V_HEAD_DIM, tq), F32)) for _ in range(heads))
    carry = lax.fori_loop(0, n_full, lambda j, cr: block(j, cr, False), init)
    final = block(n_full, carry, True)
    for h, (_, l, acc) in enumerate(final):
        o_ref[:, h * V_HEAD_DIM:(h + 1) * V_HEAD_DIM] = (acc / l).T.astype(BF16)


def _attn(q, k, v, batch, seq):
    tq = TQ_ATTN
    nq = seq // tq
    hb = HEADS_PER_ATTN_STEP
    scale = (QK_NOPE_DIM + QK_ROPE_DIM) ** -0.5
    return pl.pallas_call(
        functools.partial(_attn_kernel, scale=scale, tk=TK_ATTN),
        grid=(batch, N_HEADS // hb, nq),
        in_specs=[pl.BlockSpec((tq, hb * HEAD_QK_PAD), lambda b, h, i: (b * nq + i, h)),
                  pl.BlockSpec((seq, hb * HEAD_QK_PAD), lambda b, h, i: (b, h)),
                  pl.BlockSpec((hb * V_HEAD_DIM, seq), lambda b, h, i: (h, b))],
        out_specs=pl.BlockSpec((tq, hb * V_HEAD_DIM), lambda b, h, i: (b * nq + i, h)),
        out_shape=jax.ShapeDtypeStruct((batch * seq, N_HEADS * V_HEAD_DIM), BF16),
        compiler_params=_cparams(("arbitrary", "arbitrary", "arbitrary")),
        name="attn",
    )(q, k, v)


def _pack_bf16_pairs(v):
    c = v.shape[1] // 2
    lo = pltpu.bitcast(v[:, :c].astype(BF16).astype(F32), U32)
    hi = pltpu.bitcast(v[:, c:].astype(BF16).astype(F32), U32)
    return (hi & jnp.uint32(0xFFFF0000)) | (lo >> 16)


def _unpack_bf16_pairs(p):
    lo = pltpu.bitcast(p << 16, F32).astype(BF16)
    hi = pltpu.bitcast(p & jnp.uint32(0xFFFF0000), F32).astype(BF16)
    return jnp.concatenate([lo, hi], axis=1)


def _store_row_planes(ref, v):
    n = v.shape[0]
    p = ref.shape[0] // n
    for j in range(p):
        ref[pl.ds(j, n, stride=p), :] = v[:, j * LANES:(j + 1) * LANES]


def _load_row_planes(ref, p):
    n = ref.shape[0] // p
    return jnp.concatenate([ref[pl.ds(j, n, stride=p), :] for j in range(p)], axis=1)


def _oproj_kernel(a_ref, c_ref, x_ref, woa_ref, woc_ref, g_ref, b_ref, wr_ref,
                  h_ref, hp_ref, lg_ref, *, alpha):
    mix = jnp.dot(a_ref[...], woa_ref[...], preferred_element_type=F32)
    mix = mix + jnp.dot(c_ref[...], woc_ref[...], preferred_element_type=F32)
    h = _layer_norm(alpha * x_ref[...] + mix, g_ref[...], b_ref[...])
    h_ref[...] = h
    _store_row_planes(hp_ref, _pack_bf16_pairs(h))
    h_hi = h.astype(BF16)
    h_lo = (h - h_hi.astype(F32)).astype(BF16)
    logits = jnp.dot(jnp.concatenate([h_hi, h_lo, h_hi], axis=1), wr_ref[...], preferred_element_type=F32)
    lg_ref[...] = logits.T[:N_EXPERTS, :]


def _oproj(attn, conv, x2, woa, woc, g, b, wr, alpha):
    t, d = x2.shape
    tm = TM_PROJ
    full = lambda a: pl.BlockSpec(a.shape, lambda i: (0,) * a.ndim)
    return pl.pallas_call(
        functools.partial(_oproj_kernel, alpha=alpha),
        grid=(t // tm,),
        in_specs=[pl.BlockSpec((tm, attn.shape[1]), lambda i: (i, 0)),
                  pl.BlockSpec((tm, conv.shape[1]), lambda i: (i, 0)),
                  pl.BlockSpec((tm, d), lambda i: (i, 0)),
                  full(woa), full(woc), full(g), full(b), full(wr)],
        out_specs=[pl.BlockSpec((tm, d), lambda i: (i, 0)),
                   pl.BlockSpec((tm * (d // 2 // LANES), LANES), lambda i: (i, 0)),
                   pl.BlockSpec((N_EXPERTS, tm), lambda i: (0, i))],
        out_shape=[jax.ShapeDtypeStruct((t, d), F32),
                   jax.ShapeDtypeStruct((t * (d // 2 // LANES), LANES), U32),
                   jax.ShapeDtypeStruct((N_EXPERTS, t), F32)],
        compiler_params=_cparams(("arbitrary",)),
        name="oproj",
    )(attn, conv, x2, woa, woc, g, b, wr)


def _route_kernel(lg_ref, bias_ref, eidx_ref, rank_ref, wts_ref, cnt_ref, carry_ref):
    ne, tm = lg_ref.shape
    i = pl.program_id(0)

    @pl.when(i == 0)
    def _():
        carry_ref[...] = jnp.zeros(carry_ref.shape, F32)

    scores = _sigmoid(lg_ref[...])
    biased = scores + bias_ref[...]
    neg = -jnp.inf
    sub8 = lax.broadcasted_iota(I32, (GROUP_SIZE, tm), 0)
    gscore = []
    for g in range(N_EXPERT_GROUPS):
        blk = biased[g * GROUP_SIZE:(g + 1) * GROUP_SIZE, :]
        m1 = jnp.max(blk, axis=0, keepdims=True)
        i1 = jnp.min(jnp.where(blk == m1, sub8, GROUP_SIZE), axis=0, keepdims=True)
        m2 = jnp.max(jnp.where(sub8 == i1, neg, blk), axis=0, keepdims=True)
        gscore.append(m1 + m2)
    kept = []
    for g in range(N_EXPERT_GROUPS):
        beat = jnp.zeros((1, tm), I32)
        for o in range(N_EXPERT_GROUPS):
            if o < g:
                beat = beat + (gscore[o] >= gscore[g]).astype(I32)
            elif o > g:
                beat = beat + (gscore[o] > gscore[g]).astype(I32)
        kept.append(jnp.where(beat < TOPK_GROUPS, biased[g * GROUP_SIZE:(g + 1) * GROUP_SIZE, :], neg))
    cur = jnp.concatenate(kept, axis=0)
    sub = lax.broadcasted_iota(I32, (ne, tm), 0)
    sel_any = jnp.zeros((ne, tm), F32)
    picks = []
    wsum = jnp.zeros((1, tm), F32)
    for k in range(TOP_K):
        m = jnp.max(cur, axis=0, keepdims=True)
        ei = jnp.min(jnp.where(cur == m, sub, ne), axis=0, keepdims=True)
        sel = sub == ei
        w = jnp.sum(jnp.where(sel, scores, 0.0), axis=0, keepdims=True)
        cur = jnp.where(sel, neg, cur)
        sel_any = sel_any + sel.astype(F32)
        wsum = wsum + w
        picks.append((ei, sel, w))
    r = lax.broadcasted_iota(I32, (tm, tm), 0)
    c = lax.broadcasted_iota(I32, (tm, tm), 1)
    upper = (r <= c).astype(BF16)
    cum = jnp.dot(sel_any.astype(BF16), upper, preferred_element_type=F32)
    carry = carry_ref[:, 0:1]
    excl = cum - sel_any + carry
    total = carry + jnp.sum(sel_any, axis=1, keepdims=True)
    carry_ref[...] = jnp.broadcast_to(total, carry_ref.shape)
    cnt_ref[...] = jnp.broadcast_to(total, cnt_ref.shape)
    denom = wsum + 1e-20
    eidx_ref[...] = jnp.zeros(eidx_ref.shape, I32)
    rank_ref[...] = jnp.zeros(rank_ref.shape, I32)
    wts_ref[...] = jnp.zeros(wts_ref.shape, F32)
    for k, (ei, sel, w) in enumerate(picks):
        eidx_ref[k:k + 1, :] = ei
        rank_ref[k:k + 1, :] = jnp.sum(jnp.where(sel, excl, 0.0), axis=0, keepdims=True).astype(I32)
        wts_ref[k:k + 1, :] = w / denom * ROUTED_SCALE


def _route(logits_t, bias):
    ne, t = logits_t.shape
    tm = TM_ROUTE
    return pl.pallas_call(
        _route_kernel,
        grid=(t // tm,),
        in_specs=[pl.BlockSpec((ne, tm), lambda i: (0, i)), pl.BlockSpec((ne, 1), lambda i: (0, 0))],
        out_specs=[pl.BlockSpec((TOP_K_PAD, tm), lambda i: (0, i)),
                   pl.BlockSpec((TOP_K_PAD, tm), lambda i: (0, i)),
                   pl.BlockSpec((TOP_K_PAD, tm), lambda i: (0, i)),
                   pl.BlockSpec((ne, LANES), lambda i: (0, 0))],
        out_shape=[jax.ShapeDtypeStruct((TOP_K_PAD, t), I32),
                   jax.ShapeDtypeStruct((TOP_K_PAD, t), I32),
                   jax.ShapeDtypeStruct((TOP_K_PAD, t), F32),
                   jax.ShapeDtypeStruct((ne, LANES), F32)],
        scratch_shapes=[pltpu.VMEM((ne, LANES), F32)],
        compiler_params=_cparams(("arbitrary",)),
        name="route",
    )(logits_t, bias)


def _wait_rows(src_rows, dst_rows, sem, n_rows):
    def body(_, c):
        pltpu.make_async_copy(src_rows, dst_rows, sem).wait()
        return c
    lax.fori_loop(0, n_rows // WAIT_ROWS, body, 0)


def _dispatch_kernel(zfill_ref, slots_ref, hp_ref, h_ref, wsg_ref, wsu_ref, wsd_ref,
                     xs_hbm, sh_ref, zero_vmem, sem, zsem):
    i = pl.program_id(0)
    te = zero_vmem.shape[0]
    tc = hp_ref.shape[0]
    n_tiles = xs_hbm.shape[0] // te

    def zero_copy(tile):
        return pltpu.make_async_copy(zero_vmem, xs_hbm.at[pl.ds(pl.multiple_of(tile * te, te), te)], zsem)

    @pl.when(i == 0)
    def _():
        zero_vmem[...] = jnp.zeros(zero_vmem.shape, U32)

        def zstart(tile, c):
            @pl.when(zfill_ref[tile] != 0)
            def _():
                zero_copy(tile).start()
            return c

        def zwait(tile, c):
            @pl.when(zfill_ref[tile] != 0)
            def _():
                zero_copy(tile).wait()
            return c

        lax.fori_loop(0, n_tiles, zstart, 0)
        lax.fori_loop(0, n_tiles, zwait, 0)

    base = i * (tc * TOP_K_PAD)

    def issue(t, c):
        for k in range(TOP_K):
            slot = slots_ref[base + t * TOP_K_PAD + k]
            pltpu.make_async_copy(hp_ref.at[t], xs_hbm.at[slot], sem).start(priority=k % 2)
        return c

    lax.fori_loop(0, tc, issue, 0, unroll=ISSUE_UNROLL)
    for r0 in range(0, tc, SHARED_ROWS):
        hb = h_ref[r0:r0 + SHARED_ROWS, :].astype(BF16)
        sg = jnp.dot(hb, wsg_ref[...], preferred_element_type=F32)
        su = jnp.dot(hb, wsu_ref[...], preferred_element_type=F32)
        sh_ref[r0:r0 + SHARED_ROWS, :] = jnp.dot((sg * _sigmoid(sg) * su).astype(BF16), wsd_ref[...],
                                                 preferred_element_type=F32)
    _wait_rows(hp_ref.at[pl.ds(0, WAIT_ROWS)], xs_hbm.at[pl.ds(0, WAIT_ROWS)], sem, tc * TOP_K)


def _dispatch(zfill, slots_flat, hp, h, wsg, wsu, wsd, n_rows):
    t, planes, lanes = hp.shape
    d = h.shape[1]
    tc = TC_DISP
    full = lambda a: pl.BlockSpec(a.shape, lambda i, z, s: (0,) * a.ndim)
    return pl.pallas_call(
        _dispatch_kernel,
        grid_spec=pltpu.PrefetchScalarGridSpec(
            num_scalar_prefetch=2,
            grid=(t // tc,),
            in_specs=[pl.BlockSpec((tc, planes, lanes), lambda i, z, s: (i, 0, 0)),
                      pl.BlockSpec((tc, d), lambda i, z, s: (i, 0)),
                      full(wsg), full(wsu), full(wsd)],
            out_specs=[pl.BlockSpec(memory_space=pl.ANY), pl.BlockSpec((tc, d), lambda i, z, s: (i, 0))],
            scratch_shapes=[pltpu.VMEM((TE_ROWS, planes, lanes), U32),
                            pltpu.SemaphoreType.DMA, pltpu.SemaphoreType.DMA],
        ),
        out_shape=[jax.ShapeDtypeStruct((n_rows, planes, lanes), U32), jax.ShapeDtypeStruct((t, d), F32)],
        compiler_params=_cparams(("arbitrary",)),
        name="dispatch",
    )(zfill, slots_flat, hp, h, wsg, wsu, wsd)


def _experts_kernel(te_ref, nv_ref, nx_ref, xs_ref, wg_hbm, wu_hbm, wd_hbm, y_ref,
                    sg, su, sd, wgb, wub, wdb, sem):
    i = pl.program_id(0)
    e = te_ref[i]
    prev = te_ref[jnp.maximum(i - 1, 0)]
    valid = i < nv_ref[0]

    def fetch(ex):
        return (pltpu.make_async_copy(wg_hbm.at[ex], sg, sem.at[0]),
                pltpu.make_async_copy(wu_hbm.at[ex], su, sem.at[1]),
                pltpu.make_async_copy(wd_hbm.at[ex], sd, sem.at[2]))

    @pl.when(i == 0)
    def _():
        for cp in fetch(e):
            cp.start()

    @pl.when(valid & ((i == 0) | (e != prev)))
    def _():
        for cp in fetch(e):
            cp.wait()
        wgb[...] = sg[...].astype(BF16)
        wub[...] = su[...].astype(BF16)
        wdb[...] = sd[...].astype(BF16)

        @pl.when(nx_ref[i] >= 0)
        def _():
            for cp in fetch(nx_ref[i]):
                cp.start()

    @pl.when(valid)
    def _():
        xs = _unpack_bf16_pairs(_load_row_planes(xs_ref, xs_ref.shape[0] // TE_ROWS))
        g = jnp.dot(xs, wgb[...], preferred_element_type=F32)
        u = jnp.dot(xs, wub[...], preferred_element_type=F32)
        hid = (g * _sigmoid(g) * u).astype(BF16)
        y = jnp.dot(hid, wdb[...], preferred_element_type=F32)
        _store_row_planes(y_ref, _pack_bf16_pairs(y))

    @pl.when(jnp.logical_not(valid))
    def _():
        y_ref[...] = jnp.zeros(y_ref.shape, U32)


def _experts(tile_expert, n_valid, next_expert, xs, w_gate, w_up, w_down):
    n_rows, xp, _ = xs.shape
    ne, d, ff = w_gate.shape
    te = TE_ROWS
    in_map = lambda i, te_ref, nv_ref, nx_ref: (jnp.minimum(i, nv_ref[0] - 1), 0)
    out_map = lambda i, te_ref, nv_ref, nx_ref: (i, 0)
    hbm = pl.BlockSpec(memory_space=pl.ANY)
    y = pl.pallas_call(
        _experts_kernel,
        grid_spec=pltpu.PrefetchScalarGridSpec(
            num_scalar_prefetch=3,
            grid=(n_rows // te,),
            in_specs=[pl.BlockSpec((te * xp, LANES), in_map), hbm, hbm, hbm],
            out_specs=pl.BlockSpec((te * xp, LANES), out_map),
            scratch_shapes=[pltpu.VMEM((d, ff), F32), pltpu.VMEM((d, ff), F32), pltpu.VMEM((ff, d), F32),
                            pltpu.VMEM((d, ff), BF16), pltpu.VMEM((d, ff), BF16), pltpu.VMEM((ff, d), BF16),
                            pltpu.SemaphoreType.DMA((3,))],
        ),
        out_shape=jax.ShapeDtypeStruct((n_rows * xp, LANES), U32),
        compiler_params=_cparams(("arbitrary",)),
        name="experts",
    )(tile_expert, n_valid, next_expert, xs.reshape(n_rows * xp, LANES), w_gate, w_up, w_down)
    return y.reshape(n_rows, xp, LANES)


def _combine_kernel(slots_ref, y_hbm, h_ref, sh_ref, wts_ref, g_ref, b_ref,
                    o_ref, gbuf, rbuf, sem, *, alpha):
    tm, d = o_ref.shape
    i = pl.program_id(0)

    def issue_step(step):
        buf = step % 2
        base = step * (tm * TOP_K_PAD)

        def issue(t, c):
            for k in range(TOP_K):
                slot = slots_ref[base + t * TOP_K_PAD + k]
                pltpu.make_async_copy(y_hbm.at[slot], gbuf.at[buf, k, t], sem.at[buf]).start(priority=k % 2)
            return c

        lax.fori_loop(0, tm, issue, 0, unroll=ISSUE_UNROLL)

    @pl.when(i == 0)
    def _():
        issue_step(0)

    @pl.when(i + 1 < pl.num_programs(0))
    def _():
        issue_step(i + 1)

    cur = i % 2
    _wait_rows(y_hbm.at[pl.ds(0, WAIT_ROWS)], gbuf.at[cur, 0, pl.ds(0, WAIT_ROWS)], sem.at[cur], tm * TOP_K)
    r_lo = r_hi = None
    for k in range(TOP_K):
        packed = gbuf[cur, k]
        w = wts_ref[:, k:k + 1, :]
        lo = w * pltpu.bitcast(packed << 16, F32)
        hi = w * pltpu.bitcast(packed & jnp.uint32(0xFFFF0000), F32)
        r_lo = lo if r_lo is None else r_lo + lo
        r_hi = hi if r_hi is None else r_hi + hi
    planes = r_lo.shape[1]
    rbuf[0] = r_lo.reshape(tm * planes, LANES)
    rbuf[1] = r_hi.reshape(tm * planes, LANES)
    routed = jnp.concatenate([_load_row_planes(rbuf.at[0], planes), _load_row_planes(rbuf.at[1], planes)], axis=1)
    o_ref[...] = _layer_norm(alpha * h_ref[...] + (routed + sh_ref[...]), g_ref[...], b_ref[...])


def _combine(slots_flat, y, h, shared, wts_planes, g, b, alpha):
    t, d = h.shape
    tm = TM_COMB
    planes = y.shape[1]
    full = lambda a: pl.BlockSpec(a.shape, lambda i, s: (0,) * a.ndim)
    return pl.pallas_call(
        functools.partial(_combine_kernel, alpha=alpha),
        grid_spec=pltpu.PrefetchScalarGridSpec(
            num_scalar_prefetch=1,
            grid=(t // tm,),
            in_specs=[pl.BlockSpec(memory_space=pl.ANY),
                      pl.BlockSpec((tm, d), lambda i, s: (i, 0)),
                      pl.BlockSpec((tm, d), lambda i, s: (i, 0)),
                      pl.BlockSpec((tm, TOP_K_PAD, LANES), lambda i, s: (i, 0, 0)),
                      full(g), full(b)],
            out_specs=pl.BlockSpec((tm, d), lambda i, s: (i, 0)),
            scratch_shapes=[pltpu.VMEM((2, TOP_K, tm, planes, LANES), U32),
                            pltpu.VMEM((2, tm * planes, LANES), F32), pltpu.SemaphoreType.DMA((2,))],
        ),
        out_shape=jax.ShapeDtypeStruct((t, d), F32),
        compiler_params=_cparams(("arbitrary",)),
        name="combine",
    )(slots_flat, y, h, shared, wts_planes, g, b)


def _prep_attention_weights(w_in, w_uq, w_ukv):
    d = w_in.shape[0]
    ql = w_uq.shape[0]
    kvl = w_ukv.shape[0]
    half = QK_ROPE_DIM // 2
    pad = LANES - QK_ROPE_DIM
    w_kr = w_in[:, ql + kvl:ql + kvl + QK_ROPE_DIM]
    w_kr_rot = jnp.concatenate([-w_kr[:, half:], w_kr[:, :half]], axis=1)
    zpad = jnp.zeros((d, pad), w_in.dtype)
    wa = jnp.concatenate([w_in[:, :ql + kvl], w_kr, zpad, w_kr_rot, zpad], axis=1).astype(BF16)
    wc = w_in[:, ql + kvl + QK_ROPE_DIM:].astype(BF16)
    uq = w_uq.reshape(ql, N_HEADS, QK_NOPE_DIM + QK_ROPE_DIM)
    uq_nope, uq_rope = uq[..., :QK_NOPE_DIM], uq[..., QK_NOPE_DIM:]
    zq = jnp.zeros((ql, N_HEADS, pad), w_uq.dtype)
    wuq = jnp.concatenate([uq_nope, uq_rope, zq], axis=-1).reshape(ql, N_HEADS * HEAD_QK_PAD).astype(BF16)
    uq_rot = jnp.concatenate([-uq_rope[..., half:], uq_rope[..., :half], zq], axis=-1)
    wuqr = uq_rot.reshape(ql, N_HEADS * LANES).astype(BF16)
    ukv = w_ukv.reshape(kvl, N_HEADS, QK_NOPE_DIM + V_HEAD_DIM)
    wukv = jnp.concatenate([ukv[..., :QK_NOPE_DIM].reshape(kvl, -1), ukv[..., QK_NOPE_DIM:].reshape(kvl, -1)],
                           axis=1).astype(BF16)
    return wa, wc, wuq, wuqr, wukv


def _layer(h, pos2, invf4, batch, seq, alpha, w_in, q_norm_g, w_uq, kv_norm_g, w_ukv, conv_w, conv_b,
           conv_ln_g, conv_ln_b, w_o, ln1_g, ln1_b, w_router, router_bias, w_gate, w_up, w_down,
           ws_gate, ws_up, ws_down, ln2_g, ln2_b):
    t, d = h.shape
    row = lambda a: a.reshape(1, -1)
    wa, wc, wuq, wuqr, wukv = _prep_attention_weights(w_in, w_uq, w_ukv)
    q, k, v = _qkv(h, pos2, invf4, wa, row(q_norm_g), row(kv_norm_g), wuq, wuqr, wukv)
    conv = _conv(h, wc, conv_w, row(conv_b), row(conv_ln_g), row(conv_ln_b), seq)
    attn = _attn(q, k, v, batch, seq)
    aw = attn.shape[1]
    wr32 = jnp.pad(w_router.astype(F32), ((0, 0), (0, LANES - N_EXPERTS)))
    wr_hi = wr32.astype(BF16)
    wr_lo = (wr32 - wr_hi.astype(F32)).astype(BF16)
    wr = jnp.concatenate([wr_hi, wr_hi, wr_lo], axis=0)
    h1, h1p, logits_t = _oproj(attn, conv, h, w_o[:aw].astype(BF16), w_o[aw:].astype(BF16),
                               row(ln1_g), row(ln1_b), wr, alpha)
    return _moe_ffn(h1, h1p, logits_t, alpha, router_bias, w_gate, w_up, w_down,
                    ws_gate, ws_up, ws_down, ln2_g, ln2_b)


def _moe_ffn(h1, h1p, logits_t, alpha, router_bias, w_gate, w_up, w_down, ws_gate, ws_up, ws_down, ln2_g, ln2_b):
    t, d = h1.shape
    row = lambda a: a.reshape(1, -1)
    eidx, rank, wts, cnt = _route(logits_t, router_bias.astype(F32).reshape(N_EXPERTS, 1))
    counts = cnt[:, 0].astype(I32)
    padded = (counts + TE_ROWS - 1) // TE_ROWS * TE_ROWS
    pend = jnp.cumsum(padded)
    poff = pend - padded
    n_rows = t * TOP_K + N_EXPERTS * TE_ROWS
    n_tiles = n_rows // TE_ROWS
    n_valid = (pend[-1] // TE_ROWS).astype(I32).reshape(1)
    tile_row = jnp.minimum(jnp.arange(n_tiles, dtype=I32), n_valid[0] - 1) * TE_ROWS
    tile_expert = jnp.sum((pend[None, :] <= tile_row[:, None]).astype(I32), axis=1)
    tile_expert = jnp.minimum(tile_expert, N_EXPERTS - 1)
    tiles = jnp.arange(n_tiles, dtype=I32)
    last_tile = jnp.where(padded > 0, pend // TE_ROWS - 1, -1)
    zfill = ((tiles >= n_valid[0]) | jnp.any(tiles[:, None] == last_tile[None, :], axis=1)).astype(I32)
    ids = jnp.arange(N_EXPERTS, dtype=I32)
    later = (ids[None, :] > ids[:, None]) & (padded[None, :] > 0)
    next_of = jnp.min(jnp.where(later, ids[None, :], N_EXPERTS), axis=1)
    next_of = jnp.where(next_of == N_EXPERTS, -1, next_of)
    next_expert = jnp.sum(jnp.where(tile_expert[:, None] == ids[None, :], next_of[None, :], 0), axis=1).astype(I32)
    slot_base = jnp.sum(jnp.where(eidx[None] == ids[:, None, None], poff[:, None, None], 0), axis=0)
    slots_flat = (slot_base + rank).T.reshape(-1)
    xs, shared = _dispatch(zfill, slots_flat, h1p.reshape(t, -1, LANES), h1, ws_gate.astype(BF16),
                           ws_up.astype(BF16), ws_down.astype(BF16), n_rows)
    y = _experts(tile_expert, n_valid, next_expert, xs, w_gate, w_up, w_down)
    wts_planes = jnp.broadcast_to(wts.T[:, :, None], (t, TOP_K_PAD, LANES))
    return _combine(slots_flat, y, h1, shared, wts_planes, row(ln2_g), row(ln2_b), alpha)


def kernel(x, positions, w_in, q_norm_g, w_uq, kv_norm_g, w_ukv, conv_w, conv_b, conv_ln_g, conv_ln_b, w_o, ln1_g, ln1_b, w_router, router_bias, w_gate, w_up, w_down, ws_gate, ws_up, ws_down, ln2_g, ln2_b):
    batch, seq, d = x.shape
    depth = w_in.shape[0]
    alpha = (2.0 * depth) ** 0.25
    inv_freq = ROPE_BASE ** (-jnp.arange(0, QK_ROPE_DIM, 2, dtype=F32) / QK_ROPE_DIM)
    invf4 = jnp.tile(inv_freq, LANES // inv_freq.shape[0]).reshape(1, LANES)
    pos2 = positions.reshape(batch * seq, 1)
    h = x.reshape(batch * seq, d)
    for l in range(depth):
        h = _layer(h, pos2, invf4, batch, seq, alpha, w_in[l], q_norm_g[l], w_uq[l], kv_norm_g[l], w_ukv[l],
                   conv_w[l], conv_b[l], conv_ln_g[l], conv_ln_b[l], w_o[l], ln1_g[l], ln1_b[l],
                   w_router[l], router_bias[l], w_gate[l], w_up[l], w_down[l],
                   ws_gate[l], ws_up[l], ws_down[l], ln2_g[l], ln2_b[l])
    return h.reshape(batch, seq, d)
```

```python
import functools

import jax
import jax.numpy as jnp
from jax import lax
from jax.experimental import pallas as pl
from jax.experimental.pallas import tpu as pltpu

F32 = jnp.float32
BF16 = jnp.bfloat16
I32 = jnp.int32
U32 = jnp.uint32

N_HEADS = 8
QK_NOPE_DIM = 128
QK_ROPE_DIM = 64
V_HEAD_DIM = 128
HEAD_QK_PAD = 256
CONV_KERNEL = 31
N_EXPERTS = 64
TOP_K = 6
TOP_K_PAD = 8
N_EXPERT_GROUPS = 8
GROUP_SIZE = N_EXPERTS // N_EXPERT_GROUPS
TOPK_GROUPS = 4
ROUTED_SCALE = 2.5
ROPE_BASE = 10000.0
LN_EPS = 1e-5
RMS_EPS = 1e-6

LANES = 128
SUBLANES = 8
CONV_HALO = 32

TM_PROJ = 256
TM_CONV = 512
TQ_ATTN = 512
TK_ATTN = 512
HEADS_PER_ATTN_STEP = 4
TM_ROUTE = 512
TE_ROWS = 256
TM_COMB = 256
TC_DISP = 1024
WAIT_ROWS = 128
ISSUE_UNROLL = 4
SHARED_ROWS = 256
CONV_ROWS = 64
CONV_COLS = 256
CONV_SHIFT_BUFS = 4
VMEM_LIMIT = 56 * 1024 * 1024


def _cparams(sem):
    return pltpu.CompilerParams(dimension_semantics=sem, vmem_limit_bytes=VMEM_LIMIT)


def _sigmoid(v):
    return 1.0 / (1.0 + jnp.exp(-v))


def _layer_norm(v, g, b):
    mu = jnp.mean(v, axis=-1, keepdims=True)
    d = v - mu
    var = jnp.mean(d * d, axis=-1, keepdims=True)
    return d * lax.rsqrt(var + LN_EPS) * g + b


def _rms_norm(v, g):
    ms = jnp.mean(v * v, axis=-1, keepdims=True)
    return v * lax.rsqrt(ms + RMS_EPS) * g


def _qkv_kernel(x_ref, pos_ref, invf_ref, wa_ref, qg_ref, kvg_ref, wuq_ref, wuqr_ref, wukv_ref,
                q_ref, k_ref, vt_ref):
    ql = qg_ref.shape[1]
    kvl = kvg_ref.shape[1]
    xb = x_ref[...].astype(BF16)
    lat = jnp.dot(xb, wa_ref[...], preferred_element_type=F32)
    ang = pos_ref[...].astype(F32) * invf_ref[...]
    cos = jnp.cos(ang)
    sin = jnp.sin(ang)
    cq = _rms_norm(lat[:, :ql], qg_ref[...]).astype(BF16)
    ckv = _rms_norm(lat[:, ql:ql + kvl], kvg_ref[...]).astype(BF16)
    kr = lat[:, ql + kvl:ql + kvl + LANES] * cos + lat[:, ql + kvl + LANES:ql + kvl + 2 * LANES] * sin
    kr = kr.astype(BF16)
    q = jnp.dot(cq, wuq_ref[...], preferred_element_type=F32)
    qrot = jnp.dot(cq, wuqr_ref[...], preferred_element_type=F32)
    kv = jnp.dot(ckv, wukv_ref[...], preferred_element_type=F32)
    for h in range(N_HEADS):
        c0 = h * HEAD_QK_PAD
        q_ref[:, c0:c0 + LANES] = q[:, c0:c0 + LANES].astype(BF16)
        q_ref[:, c0 + LANES:c0 + 2 * LANES] = (
            q[:, c0 + LANES:c0 + 2 * LANES] * cos + qrot[:, h * LANES:(h + 1) * LANES] * sin).astype(BF16)
        k_ref[:, c0:c0 + LANES] = kv[:, h * LANES:(h + 1) * LANES].astype(BF16)
        k_ref[:, c0 + LANES:c0 + 2 * LANES] = kr
    vt_ref[...] = kv[:, N_HEADS * QK_NOPE_DIM:].T.astype(BF16)


def _qkv(x2, pos2, invf4, wa, qg, kvg, wuq, wuqr, wukv):
    t, d = x2.shape
    tm = TM_PROJ
    full = lambda a: pl.BlockSpec(a.shape, lambda i: (0,) * a.ndim)
    return pl.pallas_call(
        _qkv_kernel,
        grid=(t // tm,),
        in_specs=[pl.BlockSpec((tm, d), lambda i: (i, 0)),
                  pl.BlockSpec((tm, 1), lambda i: (i, 0)),
                  full(invf4), full(wa), full(qg), full(kvg), full(wuq), full(wuqr), full(wukv)],
        out_specs=[pl.BlockSpec((tm, N_HEADS * HEAD_QK_PAD), lambda i: (i, 0)),
                   pl.BlockSpec((tm, N_HEADS * HEAD_QK_PAD), lambda i: (i, 0)),
                   pl.BlockSpec((N_HEADS * V_HEAD_DIM, tm), lambda i: (0, i))],
        out_shape=[jax.ShapeDtypeStruct((t, N_HEADS * HEAD_QK_PAD), BF16),
                   jax.ShapeDtypeStruct((t, N_HEADS * HEAD_QK_PAD), BF16),
                   jax.ShapeDtypeStruct((N_HEADS * V_HEAD_DIM, t), BF16)],
        compiler_params=_cparams(("arbitrary",)),
        name="qkv",
    )(x2, pos2, invf4, wa, qg, kvg, wuq, wuqr, wukv)


def _conv_kernel(x_ref, wc_ref, cw_ref, cb_ref, g_ref, b_ref, o_ref, ubuf, ybuf, shbuf, *, tiles_per_seq):
    tm, cw = o_ref.shape
    i = pl.program_id(0)
    xb = x_ref[...].astype(BF16)

    @pl.when(i % tiles_per_seq == 0)
    def _():
        ubuf[0:CONV_HALO, :] = jnp.zeros((CONV_HALO, cw), F32)

    shift0 = CONV_HALO - (CONV_KERNEL - 1)
    n_sh = shbuf.shape[0]
    chunk_no = 0
    for c0 in range(0, cw, CONV_COLS):
        a = jnp.dot(xb, wc_ref[:, c0:c0 + CONV_COLS], preferred_element_type=F32)
        gate = jnp.dot(xb, wc_ref[:, cw + c0:cw + c0 + CONV_COLS], preferred_element_type=F32)
        ubuf[CONV_HALO:CONV_HALO + tm, c0:c0 + CONV_COLS] = a * _sigmoid(gate)
        for r0 in range(0, tm, CONV_ROWS):
            sh = shbuf.at[chunk_no % n_sh]
            chunk_no += 1
            win = ubuf[r0:r0 + CONV_ROWS + CONV_HALO, c0:c0 + CONV_COLS]
            for r in range(1, SUBLANES):
                sh[r] = win[r:r + CONV_ROWS + CONV_HALO - SUBLANES, :]
            acc = jnp.zeros((CONV_ROWS, CONV_COLS), F32)
            for k in range(CONV_KERNEL):
                wk = cw_ref[k:k + 1, c0:c0 + CONV_COLS]
                r, j = (shift0 + k) % SUBLANES, (shift0 + k) // SUBLANES
                if r == 0:
                    tap = win[j * SUBLANES:j * SUBLANES + CONV_ROWS, :]
                else:
                    tap = sh[r, j * SUBLANES:j * SUBLANES + CONV_ROWS, :]
                acc = acc + wk * tap
            ybuf[r0:r0 + CONV_ROWS, c0:c0 + CONV_COLS] = acc
    ubuf[0:CONV_HALO, :] = ubuf[tm:tm + CONV_HALO, :]
    y = _layer_norm(ybuf[...] + cb_ref[...], g_ref[...], b_ref[...])
    o_ref[...] = (y * _sigmoid(y)).astype(BF16)


def _conv(x2, wc, cw, cb, g, b, seq):
    t, d = x2.shape
    tm = TM_CONV
    c = cw.shape[1]
    full = lambda a: pl.BlockSpec(a.shape, lambda i: (0,) * a.ndim)
    return pl.pallas_call(
        functools.partial(_conv_kernel, tiles_per_seq=seq // tm),
        grid=(t // tm,),
        in_specs=[pl.BlockSpec((tm, d), lambda i: (i, 0)), full(wc), full(cw), full(cb), full(g), full(b)],
        out_specs=pl.BlockSpec((tm, c), lambda i: (i, 0)),
        out_shape=jax.ShapeDtypeStruct((t, c), BF16),
        scratch_shapes=[pltpu.VMEM((tm + CONV_HALO, c), F32), pltpu.VMEM((tm, c), F32),
                        pltpu.VMEM((CONV_SHIFT_BUFS, SUBLANES, CONV_ROWS + CONV_HALO - SUBLANES, CONV_COLS), F32)],
        compiler_params=_cparams(("arbitrary",)),
        name="conv",
    )(x2, wc, cw, cb, g, b)


def _attn_kernel(q_ref, k_ref, vt_ref, o_ref, *, scale, tk):
    tq = q_ref.shape[0]
    heads = q_ref.shape[1] // HEAD_QK_PAD
    i = pl.program_id(2)
    c = scale * 1.4426950408889634
    n_full = (i * tq) // tk
    q_off = i * tq - n_full * tk

    def block(j, carry, masked):
        r0 = pl.multiple_of(j * tk, tk)
        out = []
        scores = []
        for h in range(heads):
            q = q_ref[:, h * HEAD_QK_PAD:(h + 1) * HEAD_QK_PAD]
            kb = k_ref[pl.ds(r0, tk), h * HEAD_QK_PAD:(h + 1) * HEAD_QK_PAD]
            scores.append(lax.dot_general(kb, q, (((1,), (1,)), ((), ())),
                                          preferred_element_type=F32))
        for h, (m, l, acc) in enumerate(carry):
            vtb = vt_ref[h * V_HEAD_DIM:(h + 1) * V_HEAD_DIM, pl.ds(r0, tk)]
            st = scores[h]
            if masked:
                key = lax.broadcasted_iota(I32, (tk, tq), 0)
                qry = lax.broadcasted_iota(I32, (tk, tq), 1) + q_off
                st = jnp.where(key <= qry, st, -jnp.inf)
            m_blk = jnp.max(jnp.max(st.reshape(SUBLANES, tk // SUBLANES, tq), axis=0), axis=0, keepdims=True)
            m_new = jnp.maximum(m, m_blk)
            alpha = jnp.exp2((m - m_new) * c)
            pt = jnp.exp2((st - m_new) * c)
            l_blk = jnp.sum(jnp.sum(pt.reshape(SUBLANES, tk // SUBLANES, tq), axis=0), axis=0, keepdims=True)
            l = alpha * l + l_blk
            acc = alpha * acc + jnp.dot(vtb, pt.astype(BF16), preferred_element_type=F32)
            out.append((m_new, l, acc))
        return tuple(out)

    init = tuple((jnp.full((1, tq), -jnp.inf, F32), jnp.zeros((1, tq), F32),
                  jnp.zeros((V_HEAD_DIM, tq), F32)) for _ in range(heads))
    carry = lax.fori_loop(0, n_full, lambda j, cr: block(j, cr, False), init)
    final = block(n_full, carry, True)
    for h, (_, l, acc) in enumerate(final):
        o_ref[:, h * V_HEAD_DIM:(h + 1) * V_HEAD_DIM] = (acc / l).T.astype(BF16)


def _attn(q, k, v, batch, seq):
    tq = TQ_ATTN
    nq = seq // tq
    hb = HEADS_PER_ATTN_STEP
    scale = (QK_NOPE_DIM + QK_ROPE_DIM) ** -0.5
    return pl.pallas_call(
        functools.partial(_attn_kernel, scale=scale, tk=TK_ATTN),
        grid=(batch, N_HEADS // hb, nq),
        in_specs=[pl.BlockSpec((tq, hb * HEAD_QK_PAD), lambda b, h, i: (b * nq + i, h)),
                  pl.BlockSpec((seq, hb * HEAD_QK_PAD), lambda b, h, i: (b, h)),
                  pl.BlockSpec((hb * V_HEAD_DIM, seq), lambda b, h, i: (h, b))],
        out_specs=pl.BlockSpec((tq, hb * V_HEAD_DIM), lambda b, h, i: (b * nq + i, h)),
        out_shape=jax.ShapeDtypeStruct((batch * seq, N_HEADS * V_HEAD_DIM), BF16),
        compiler_params=_cparams(("arbitrary", "arbitrary", "arbitrary")),
        name="attn",
    )(q, k, v)


def _pack_bf16_pairs(v):
    c = v.shape[1] // 2
    lo = pltpu.bitcast(v[:, :c].astype(BF16).astype(F32), U32)
    hi = pltpu.bitcast(v[:, c:].astype(BF16).astype(F32), U32)
    return (hi & jnp.uint32(0xFFFF0000)) | (lo >> 16)


def _unpack_bf16_pairs(p):
    lo = pltpu.bitcast(p << 16, F32).astype(BF16)
    hi = pltpu.bitcast(p & jnp.uint32(0xFFFF0000), F32).astype(BF16)
    return jnp.concatenate([lo, hi], axis=1)


def _store_row_planes(ref, v):
    n = v.shape[0]
    p = ref.shape[0] // n
    for j in range(p):
        ref[pl.ds(j, n, stride=p), :] = v[:, j * LANES:(j + 1) * LANES]


def _load_row_planes(ref, p):
    n = ref.shape[0] // p
    return jnp.concatenate([ref[pl.ds(j, n, stride=p), :] for j in range(p)], axis=1)


def _oproj_kernel(a_ref, c_ref, x_ref, woa_ref, woc_ref, g_ref, b_ref, wr_ref,
                  h_ref, hp_ref, lg_ref, *, alpha):
    mix = jnp.dot(a_ref[...], woa_ref[...], preferred_element_type=F32)
    mix = mix + jnp.dot(c_ref[...], woc_ref[...], preferred_element_type=F32)
    h = _layer_norm(alpha * x_ref[...] + mix, g_ref[...], b_ref[...])
    h_ref[...] = h
    _store_row_planes(hp_ref, _pack_bf16_pairs(h))
    h_hi = h.astype(BF16)
    h_lo = (h - h_hi.astype(F32)).astype(BF16)
    logits = jnp.dot(jnp.concatenate([h_hi, h_lo, h_hi], axis=1), wr_ref[...], preferred_element_type=F32)
    lg_ref[...] = logits.T[:N_EXPERTS, :]


def _oproj(attn, conv, x2, woa, woc, g, b, wr, alpha):
    t, d = x2.shape
    tm = TM_PROJ
    full = lambda a: pl.BlockSpec(a.shape, lambda i: (0,) * a.ndim)
    return pl.pallas_call(
        functools.partial(_oproj_kernel, alpha=alpha),
        grid=(t // tm,),
        in_specs=[pl.BlockSpec((tm, attn.shape[1]), lambda i: (i, 0)),
                  pl.BlockSpec((tm, conv.shape[1]), lambda i: (i, 0)),
                  pl.BlockSpec((tm, d), lambda i: (i, 0)),
                  full(woa), full(woc), full(g), full(b), full(wr)],
        out_specs=[pl.BlockSpec((tm, d), lambda i: (i, 0)),
                   pl.BlockSpec((tm * (d // 2 // LANES), LANES), lambda i: (i, 0)),
                   pl.BlockSpec((N_EXPERTS, tm), lambda i: (0, i))],
        out_shape=[jax.ShapeDtypeStruct((t, d), F32),
                   jax.ShapeDtypeStruct((t * (d // 2 // LANES), LANES), U32),
                   jax.ShapeDtypeStruct((N_EXPERTS, t), F32)],
        compiler_params=_cparams(("arbitrary",)),
        name="oproj",
    )(attn, conv, x2, woa, woc, g, b, wr)


def _route_kernel(lg_ref, bias_ref, eidx_ref, rank_ref, wts_ref, cnt_ref, carry_ref):
    ne, tm = lg_ref.shape
    i = pl.program_id(0)

    @pl.when(i == 0)
    def _():
        carry_ref[...] = jnp.zeros(carry_ref.shape, F32)

    scores = _sigmoid(lg_ref[...])
    biased = scores + bias_ref[...]
    neg = -jnp.inf
    sub8 = lax.broadcasted_iota(I32, (GROUP_SIZE, tm), 0)
    gscore = []
    for g in range(N_EXPERT_GROUPS):
        blk = biased[g * GROUP_SIZE:(g + 1) * GROUP_SIZE, :]
        m1 = jnp.max(blk, axis=0, keepdims=True)
        i1 = jnp.min(jnp.where(blk == m1, sub8, GROUP_SIZE), axis=0, keepdims=True)
        m2 = jnp.max(jnp.where(sub8 == i1, neg, blk), axis=0, keepdims=True)
        gscore.append(m1 + m2)
    kept = []
    for g in range(N_EXPERT_GROUPS):
        beat = jnp.zeros((1, tm), I32)
        for o in range(N_EXPERT_GROUPS):
            if o < g:
                beat = beat + (gscore[o] >= gscore[g]).astype(I32)
            elif o > g:
                beat = beat + (gscore[o] > gscore[g]).astype(I32)
        kept.append(jnp.where(beat < TOPK_GROUPS, biased[g * GROUP_SIZE:(g + 1) * GROUP_SIZE, :], neg))
    cur = jnp.concatenate(kept, axis=0)
    sub = lax.broadcasted_iota(I32, (ne, tm), 0)
    sel_any = jnp.zeros((ne, tm), F32)
    picks = []
    wsum = jnp.zeros((1, tm), F32)
    for k in range(TOP_K):
        m = jnp.max(cur, axis=0, keepdims=True)
        ei = jnp.min(jnp.where(cur == m, sub, ne), axis=0, keepdims=True)
        sel = sub == ei
        w = jnp.sum(jnp.where(sel, scores, 0.0), axis=0, keepdims=True)
        cur = jnp.where(sel, neg, cur)
        sel_any = sel_any + sel.astype(F32)
        wsum = wsum + w
        picks.append((ei, sel, w))
    r = lax.broadcasted_iota(I32, (tm, tm), 0)
    c = lax.broadcasted_iota(I32, (tm, tm), 1)
    upper = (r <= c).astype(BF16)
    cum = jnp.dot(sel_any.astype(BF16), upper, preferred_element_type=F32)
    carry = carry_ref[:, 0:1]
    excl = cum - sel_any + carry
    total = carry + jnp.sum(sel_any, axis=1, keepdims=True)
    carry_ref[...] = jnp.broadcast_to(total, carry_ref.shape)
    cnt_ref[...] = jnp.broadcast_to(total, cnt_ref.shape)
    denom = wsum + 1e-20
    eidx_ref[...] = jnp.zeros(eidx_ref.shape, I32)
    rank_ref[...] = jnp.zeros(rank_ref.shape, I32)
    wts_ref[...] = jnp.zeros(wts_ref.shape, F32)
    for k, (ei, sel, w) in enumerate(picks):
        eidx_ref[k:k + 1, :] = ei
        rank_ref[k:k + 1, :] = jnp.sum(jnp.where(sel, excl, 0.0), axis=0, keepdims=True).astype(I32)
        wts_ref[k:k + 1, :] = w / denom * ROUTED_SCALE


def _route(logits_t, bias):
    ne, t = logits_t.shape
    tm = TM_ROUTE
    return pl.pallas_call(
        _route_kernel,
        grid=(t // tm,),
        in_specs=[pl.BlockSpec((ne, tm), lambda i: (0, i)), pl.BlockSpec((ne, 1), lambda i: (0, 0))],
        out_specs=[pl.BlockSpec((TOP_K_PAD, tm), lambda i: (0, i)),
                   pl.BlockSpec((TOP_K_PAD, tm), lambda i: (0, i)),
                   pl.BlockSpec((TOP_K_PAD, tm), lambda i: (0, i)),
                   pl.BlockSpec((ne, LANES), lambda i: (0, 0))],
        out_shape=[jax.ShapeDtypeStruct((TOP_K_PAD, t), I32),
                   jax.ShapeDtypeStruct((TOP_K_PAD, t), I32),
                   jax.ShapeDtypeStruct((TOP_K_PAD, t), F32),
                   jax.ShapeDtypeStruct((ne, LANES), F32)],
        scratch_shapes=[pltpu.VMEM((ne, LANES), F32)],
        compiler_params=_cparams(("arbitrary",)),
        name="route",
    )(logits_t, bias)


def _wait_rows(src_rows, dst_rows, sem, n_rows):
    def body(_, c):
        pltpu.make_async_copy(src_rows, dst_rows, sem).wait()
        return c
    lax.fori_loop(0, n_rows // WAIT_ROWS, body, 0)


def _dispatch_kernel(zfill_ref, slots_ref, hp_ref, h_ref, wsg_ref, wsu_ref, wsd_ref,
                     xs_hbm, sh_ref, zero_vmem, sem, zsem):
    i = pl.program_id(0)
    te = zero_vmem.shape[0]
    tc = hp_ref.shape[0]
    n_tiles = xs_hbm.shape[0] // te

    def zero_copy(tile):
        return pltpu.make_async_copy(zero_vmem, xs_hbm.at[pl.ds(pl.multiple_of(tile * te, te), te)], zsem)

    @pl.when(i == 0)
    def _():
        zero_vmem[...] = jnp.zeros(zero_vmem.shape, U32)

        def zstart(tile, c):
            @pl.when(zfill_ref[tile] != 0)
            def _():
                zero_copy(tile).start()
            return c

        def zwait(tile, c):
            @pl.when(zfill_ref[tile] != 0)
            def _():
                zero_copy(tile).wait()
            return c

        lax.fori_loop(0, n_tiles, zstart, 0)
        lax.fori_loop(0, n_tiles, zwait, 0)

    base = i * (tc * TOP_K_PAD)

    def issue(t, c):
        for k in range(TOP_K):
            slot = slots_ref[base + t * TOP_K_PAD + k]
            pltpu.make_async_copy(hp_ref.at[t], xs_hbm.at[slot], sem).start(priority=k % 2)
        return c

    for r0 in range(0, tc, SHARED_ROWS):
        lax.fori_loop(r0, r0 + SHARED_ROWS, issue, 0, unroll=ISSUE_UNROLL)
        hb = h_ref[r0:r0 + SHARED_ROWS, :].astype(BF16)
        sg = jnp.dot(hb, wsg_ref[...], preferred_element_type=F32)
        su = jnp.dot(hb, wsu_ref[...], preferred_element_type=F32)
        sh_ref[r0:r0 + SHARED_ROWS, :] = jnp.dot((sg * _sigmoid(sg) * su).astype(BF16), wsd_ref[...],
                                                 preferred_element_type=F32)
    _wait_rows(hp_ref.at[pl.ds(0, WAIT_ROWS)], xs_hbm.at[pl.ds(0, WAIT_ROWS)], sem, tc * TOP_K)


def _dispatch(zfill, slots_flat, hp, h, wsg, wsu, wsd, n_rows):
    t, planes, lanes = hp.shape
    d = h.shape[1]
    tc = TC_DISP
    full = lambda a: pl.BlockSpec(a.shape, lambda i, z, s: (0,) * a.ndim)
    return pl.pallas_call(
        _dispatch_kernel,
        grid_spec=pltpu.PrefetchScalarGridSpec(
            num_scalar_prefetch=2,
            grid=(t // tc,),
            in_specs=[pl.BlockSpec((tc, planes, lanes), lambda i, z, s: (i, 0, 0)),
                      pl.BlockSpec((tc, d), lambda i, z, s: (i, 0)),
                      full(wsg), full(wsu), full(wsd)],
            out_specs=[pl.BlockSpec(memory_space=pl.ANY), pl.BlockSpec((tc, d), lambda i, z, s: (i, 0))],
            scratch_shapes=[pltpu.VMEM((TE_ROWS, planes, lanes), U32),
                            pltpu.SemaphoreType.DMA, pltpu.SemaphoreType.DMA],
        ),
        out_shape=[jax.ShapeDtypeStruct((n_rows, planes, lanes), U32), jax.ShapeDtypeStruct((t, d), F32)],
        compiler_params=_cparams(("arbitrary",)),
        name="dispatch",
    )(zfill, slots_flat, hp, h, wsg, wsu, wsd)


def _experts_kernel(te_ref, nv_ref, nx_ref, xs_ref, wg_hbm, wu_hbm, wd_hbm, y_ref,
                    sg, su, sd, wgb, wub, wdb, sem):
    i = pl.program_id(0)
    e = te_ref[i]
    prev = te_ref[jnp.maximum(i - 1, 0)]
    valid = i < nv_ref[0]

    def fetch(ex):
        return (pltpu.make_async_copy(wg_hbm.at[ex], sg, sem.at[0]),
                pltpu.make_async_copy(wu_hbm.at[ex], su, sem.at[1]),
                pltpu.make_async_copy(wd_hbm.at[ex], sd, sem.at[2]))

    @pl.when(i == 0)
    def _():
        for cp in fetch(e):
            cp.start()

    @pl.when(valid & ((i == 0) | (e != prev)))
    def _():
        for cp in fetch(e):
            cp.wait()
        wgb[...] = sg[...].astype(BF16)
        wub[...] = su[...].astype(BF16)
        wdb[...] = sd[...].astype(BF16)

        @pl.when(nx_ref[i] >= 0)
        def _():
            for cp in fetch(nx_ref[i]):
                cp.start()

    @pl.when(valid)
    def _():
        xs = _unpack_bf16_pairs(_load_row_planes(xs_ref, xs_ref.shape[0] // TE_ROWS))
        g = jnp.dot(xs, wgb[...], preferred_element_type=F32)
        u = jnp.dot(xs, wub[...], preferred_element_type=F32)
        hid = (g * _sigmoid(g) * u).astype(BF16)
        y = jnp.dot(hid, wdb[...], preferred_element_type=F32)
        _store_row_planes(y_ref, _pack_bf16_pairs(y))

    @pl.when(jnp.logical_not(valid))
    def _():
        y_ref[...] = jnp.zeros(y_ref.shape, U32)


def _experts(tile_expert, n_valid, next_expert, xs, w_gate, w_up, w_down):
    n_rows, xp, _ = xs.shape
    ne, d, ff = w_gate.shape
    te = TE_ROWS
    in_map = lambda i, te_ref, nv_ref, nx_ref: (jnp.minimum(i, nv_ref[0] - 1), 0)
    out_map = lambda i, te_ref, nv_ref, nx_ref: (i, 0)
    hbm = pl.BlockSpec(memory_space=pl.ANY)
    y = pl.pallas_call(
        _experts_kernel,
        grid_spec=pltpu.PrefetchScalarGridSpec(
            num_scalar_prefetch=3,
            grid=(n_rows // te,),
            in_specs=[pl.BlockSpec((te * xp, LANES), in_map), hbm, hbm, hbm],
            out_specs=pl.BlockSpec((te * xp, LANES), out_map),
            scratch_shapes=[pltpu.VMEM((d, ff), F32), pltpu.VMEM((d, ff), F32), pltpu.VMEM((ff, d), F32),
                            pltpu.VMEM((d, ff), BF16), pltpu.VMEM((d, ff), BF16), pltpu.VMEM((ff, d), BF16),
                            pltpu.SemaphoreType.DMA((3,))],
        ),
        out_shape=jax.ShapeDtypeStruct((n_rows * xp, LANES), U32),
        compiler_params=_cparams(("arbitrary",)),
        name="experts",
    )(tile_expert, n_valid, next_expert, xs.reshape(n_rows * xp, LANES), w_gate, w_up, w_down)
    return y.reshape(n_rows, xp, LANES)


def _combine_kernel(slots_ref, y_hbm, h_ref, sh_ref, wts_ref, g_ref, b_ref,
                    o_ref, gbuf, rbuf, sem, *, alpha):
    tm, d = o_ref.shape
    i = pl.program_id(0)

    def issue_step(step):
        buf = step % 2
        base = step * (tm * TOP_K_PAD)

        def issue(t, c):
            for k in range(TOP_K):
                slot = slots_ref[base + t * TOP_K_PAD + k]
                pltpu.make_async_copy(y_hbm.at[slot], gbuf.at[buf, k, t], sem.at[buf]).start(priority=k % 2)
            return c

        lax.fori_loop(0, tm, issue, 0, unroll=ISSUE_UNROLL)

    @pl.when(i == 0)
    def _():
        issue_step(0)

    @pl.when(i + 1 < pl.num_programs(0))
    def _():
        issue_step(i + 1)

    cur = i % 2
    _wait_rows(y_hbm.at[pl.ds(0, WAIT_ROWS)], gbuf.at[cur, 0, pl.ds(0, WAIT_ROWS)], sem.at[cur], tm * TOP_K)
    r_lo = r_hi = None
    for k in range(TOP_K):
        packed = gbuf[cur, k]
        w = wts_ref[:, k:k + 1, :]
        lo = w * pltpu.bitcast(packed << 16, F32)
        hi = w * pltpu.bitcast(packed & jnp.uint32(0xFFFF0000), F32)
        r_lo = lo if r_lo is None else r_lo + lo
        r_hi = hi if r_hi is None else r_hi + hi
    planes = r_lo.shape[1]
    rbuf[0] = r_lo.reshape(tm * planes, LANES)
    rbuf[1] = r_hi.reshape(tm * planes, LANES)
    routed = jnp.concatenate([_load_row_planes(rbuf.at[0], planes), _load_row_planes(rbuf.at[1], planes)], axis=1)
    o_ref[...] = _layer_norm(alpha * h_ref[...] + (routed + sh_ref[...]), g_ref[...], b_ref[...])


def _combine(slots_flat, y, h, shared, wts_planes, g, b, alpha):
    t, d = h.shape
    tm = TM_COMB
    planes = y.shape[1]
    full = lambda a: pl.BlockSpec(a.shape, lambda i, s: (0,) * a.ndim)
    return pl.pallas_call(
        functools.partial(_combine_kernel, alpha=alpha),
        grid_spec=pltpu.PrefetchScalarGridSpec(
            num_scalar_prefetch=1,
            grid=(t // tm,),
            in_specs=[pl.BlockSpec(memory_space=pl.ANY),
                      pl.BlockSpec((tm, d), lambda i, s: (i, 0)),
                      pl.BlockSpec((tm, d), lambda i, s: (i, 0)),
                      pl.BlockSpec((tm, TOP_K_PAD, LANES), lambda i, s: (i, 0, 0)),
                      full(g), full(b)],
            out_specs=pl.BlockSpec((tm, d), lambda i, s: (i, 0)),
            scratch_shapes=[pltpu.VMEM((2, TOP_K, tm, planes, LANES), U32),
                            pltpu.VMEM((2, tm * planes, LANES), F32), pltpu.SemaphoreType.DMA((2,))],
        ),
        out_shape=jax.ShapeDtypeStruct((t, d), F32),
        compiler_params=_cparams(("arbitrary",)),
        name="combine",
    )(slots_flat, y, h, shared, wts_planes, g, b)


def _prep_attention_weights(w_in, w_uq, w_ukv):
    d = w_in.shape[0]
    ql = w_uq.shape[0]
    kvl = w_ukv.shape[0]
    half = QK_ROPE_DIM // 2
    pad = LANES - QK_ROPE_DIM
    w_kr = w_in[:, ql + kvl:ql + kvl + QK_ROPE_DIM]
    w_kr_rot = jnp.concatenate([-w_kr[:, half:], w_kr[:, :half]], axis=1)
    zpad = jnp.zeros((d, pad), w_in.dtype)
    wa = jnp.concatenate([w_in[:, :ql + kvl], w_kr, zpad, w_kr_rot, zpad], axis=1).astype(BF16)
    wc = w_in[:, ql + kvl + QK_ROPE_DIM:].astype(BF16)
    uq = w_uq.reshape(ql, N_HEADS, QK_NOPE_DIM + QK_ROPE_DIM)
    uq_nope, uq_rope = uq[..., :QK_NOPE_DIM], uq[..., QK_NOPE_DIM:]
    zq = jnp.zeros((ql, N_HEADS, pad), w_uq.dtype)
    wuq = jnp.concatenate([uq_nope, uq_rope, zq], axis=-1).reshape(ql, N_HEADS * HEAD_QK_PAD).astype(BF16)
    uq_rot = jnp.concatenate([-uq_rope[..., half:], uq_rope[..., :half], zq], axis=-1)
    wuqr = uq_rot.reshape(ql, N_HEADS * LANES).astype(BF16)
    ukv = w_ukv.reshape(kvl, N_HEADS, QK_NOPE_DIM + V_HEAD_DIM)
    wukv = jnp.concatenate([ukv[..., :QK_NOPE_DIM].reshape(kvl, -1), ukv[..., QK_NOPE_DIM:].reshape(kvl, -1)],
                           axis=1).astype(BF16)
    return wa, wc, wuq, wuqr, wukv


def _layer(h, pos2, invf4, batch, seq, alpha, w_in, q_norm_g, w_uq, kv_norm_g, w_ukv, conv_w, conv_b,
           conv_ln_g, conv_ln_b, w_o, ln1_g, ln1_b, w_router, router_bias, w_gate, w_up, w_down,
           ws_gate, ws_up, ws_down, ln2_g, ln2_b):
    t, d = h.shape
    row = lambda a: a.reshape(1, -1)
    wa, wc, wuq, wuqr, wukv = _prep_attention_weights(w_in, w_uq, w_ukv)
    q, k, v = _qkv(h, pos2, invf4, wa, row(q_norm_g), row(kv_norm_g), wuq, wuqr, wukv)
    conv = _conv(h, wc, conv_w, row(conv_b), row(conv_ln_g), row(conv_ln_b), seq)
    attn = _attn(q, k, v, batch, seq)
    aw = attn.shape[1]
    wr32 = jnp.pad(w_router.astype(F32), ((0, 0), (0, LANES - N_EXPERTS)))
    wr_hi = wr32.astype(BF16)
    wr_lo = (wr32 - wr_hi.astype(F32)).astype(BF16)
    wr = jnp.concatenate([wr_hi, wr_hi, wr_lo], axis=0)
    h1, h1p, logits_t = _oproj(attn, conv, h, w_o[:aw].astype(BF16), w_o[aw:].astype(BF16),
                               row(ln1_g), row(ln1_b), wr, alpha)
    return _moe_ffn(h1, h1p, logits_t, alpha, router_bias, w_gate, w_up, w_down,
                    ws_gate, ws_up, ws_down, ln2_g, ln2_b)


def _moe_ffn(h1, h1p, logits_t, alpha, router_bias, w_gate, w_up, w_down, ws_gate, ws_up, ws_down, ln2_g, ln2_b):
    t, d = h1.shape
    row = lambda a: a.reshape(1, -1)
    eidx, rank, wts, cnt = _route(logits_t, router_bias.astype(F32).reshape(N_EXPERTS, 1))
    counts = cnt[:, 0].astype(I32)
    padded = (counts + TE_ROWS - 1) // TE_ROWS * TE_ROWS
    pend = jnp.cumsum(padded)
    poff = pend - padded
    n_rows = t * TOP_K + N_EXPERTS * TE_ROWS
    n_tiles = n_rows // TE_ROWS
    n_valid = (pend[-1] // TE_ROWS).astype(I32).reshape(1)
    tile_row = jnp.minimum(jnp.arange(n_tiles, dtype=I32), n_valid[0] - 1) * TE_ROWS
    tile_expert = jnp.sum((pend[None, :] <= tile_row[:, None]).astype(I32), axis=1)
    tile_expert = jnp.minimum(tile_expert, N_EXPERTS - 1)
    tiles = jnp.arange(n_tiles, dtype=I32)
    last_tile = jnp.where(padded > 0, pend // TE_ROWS - 1, -1)
    zfill = ((tiles >= n_valid[0]) | jnp.any(tiles[:, None] == last_tile[None, :], axis=1)).astype(I32)
    ids = jnp.arange(N_EXPERTS, dtype=I32)
    later = (ids[None, :] > ids[:, None]) & (padded[None, :] > 0)
    next_of = jnp.min(jnp.where(later, ids[None, :], N_EXPERTS), axis=1)
    next_of = jnp.where(next_of == N_EXPERTS, -1, next_of)
    next_expert = jnp.sum(jnp.where(tile_expert[:, None] == ids[None, :], next_of[None, :], 0), axis=1).astype(I32)
    slot_base = jnp.sum(jnp.where(eidx[None] == ids[:, None, None], poff[:, None, None], 0), axis=0)
    slots_flat = (slot_base + rank).T.reshape(-1)
    xs, shared = _dispatch(zfill, slots_flat, h1p.reshape(t, -1, LANES), h1, ws_gate.astype(BF16),
                           ws_up.astype(BF16), ws_down.astype(BF16), n_rows)
    y = _experts(tile_expert, n_valid, next_expert, xs, w_gate, w_up, w_down)
    wts_planes = jnp.broadcast_to(wts.T[:, :, None], (t, TOP_K_PAD, LANES))
    return _combine(slots_flat, y, h1, shared, wts_planes, row(ln2_g), row(ln2_b), alpha)


def kernel(x, positions, w_in, q_norm_g, w_uq, kv_norm_g, w_ukv, conv_w, conv_b, conv_ln_g, conv_ln_b, w_o, ln1_g, ln1_b, w_router, router_bias, w_gate, w_up, w_down, ws_gate, ws_up, ws_down, ln2_g, ln2_b):
    batch, seq, d = x.shape
    depth = w_in.shape[0]
    alpha = (2.0 * depth) ** 0.25
    inv_freq = ROPE_BASE ** (-jnp.arange(0, QK_ROPE_DIM, 2, dtype=F32) / QK_ROPE_DIM)
    invf4 = jnp.tile(inv_freq, LANES // inv_freq.shape[0]).reshape(1, LANES)
    pos2 = positions.reshape(batch * seq, 1)
    h = x.reshape(batch * seq, d)
    for l in range(depth):
        h = _layer(h, pos2, invf4, batch, seq, alpha, w_in[l], q_norm_g[l], w_uq[l], kv_norm_g[l], w_ukv[l],
                   conv_w[l], conv_b[l], conv_ln_g[l], conv_ln_b[l], w_o[l], ln1_g[l], ln1_b[l],
                   w_router[l], router_bias[l], w_gate[l], w_up[l], w_down[l],
                   ws_gate[l], ws_up[l], ws_down[l], ln2_g[l], ln2_b[l])
    return h.reshape(batch, seq, d)
```

```python
import functools

import jax
import jax.numpy as jnp
from jax import lax
from jax.experimental import pallas as pl
from jax.experimental.pallas import tpu as pltpu

F32 = jnp.float32
BF16 = jnp.bfloat16
I32 = jnp.int32
U32 = jnp.uint32

N_HEADS = 8
QK_NOPE_DIM = 128
QK_ROPE_DIM = 64
V_HEAD_DIM = 128
HEAD_QK_PAD = 256
CONV_KERNEL = 31
N_EXPERTS = 64
TOP_K = 6
TOP_K_PAD = 8
N_EXPERT_GROUPS = 8
GROUP_SIZE = N_EXPERTS // N_EXPERT_GROUPS
TOPK_GROUPS = 4
ROUTED_SCALE = 2.5
ROPE_BASE = 10000.0
LN_EPS = 1e-5
RMS_EPS = 1e-6

LANES = 128
SUBLANES = 8
CONV_HALO = 32

TM_PROJ = 256
TM_CONV = 512
TQ_ATTN = 512
TK_ATTN = 512
HEADS_PER_ATTN_STEP = 4
TM_ROUTE = 512
TE_ROWS = 256
TM_COMB = 256
TC_DISP = 1024
WAIT_ROWS = 128
ISSUE_UNROLL = 4
GROUP_UNROLL = 8
CONV_ROWS = 64
CONV_COLS = 256
CONV_SHIFT_BUFS = 4
VMEM_LIMIT = 56 * 1024 * 1024


def _cparams(sem):
    return pltpu.CompilerParams(dimension_semantics=sem, vmem_limit_bytes=VMEM_LIMIT)


def _sigmoid(v):
    return 1.0 / (1.0 + jnp.exp(-v))


def _layer_norm(v, g, b):
    mu = jnp.mean(v, axis=-1, keepdims=True)
    d = v - mu
    var = jnp.mean(d * d, axis=-1, keepdims=True)
    return d * lax.rsqrt(var + LN_EPS) * g + b


def _rms_norm(v, g):
    ms = jnp.mean(v * v, axis=-1, keepdims=True)
    return v * lax.rsqrt(ms + RMS_EPS) * g


def _qkv_kernel(x_ref, pos_ref, invf_ref, wa_ref, qg_ref, kvg_ref, wuq_ref, wuqr_ref, wukv_ref,
                q_ref, k_ref, vt_ref):
    ql = qg_ref.shape[1]
    kvl = kvg_ref.shape[1]
    xb = x_ref[...].astype(BF16)
    lat = jnp.dot(xb, wa_ref[...], preferred_element_type=F32)
    ang = pos_ref[...].astype(F32) * invf_ref[...]
    cos = jnp.cos(ang)
    sin = jnp.sin(ang)
    cq = _rms_norm(lat[:, :ql], qg_ref[...]).astype(BF16)
    ckv = _rms_norm(lat[:, ql:ql + kvl], kvg_ref[...]).astype(BF16)
    kr = lat[:, ql + kvl:ql + kvl + LANES] * cos + lat[:, ql + kvl + LANES:ql + kvl + 2 * LANES] * sin
    kr = kr.astype(BF16)
    q = jnp.dot(cq, wuq_ref[...], preferred_element_type=F32)
    qrot = jnp.dot(cq, wuqr_ref[...], preferred_element_type=F32)
    kv = jnp.dot(ckv, wukv_ref[...], preferred_element_type=F32)
    for h in range(N_HEADS):
        c0 = h * HEAD_QK_PAD
        q_ref[:, c0:c0 + LANES] = q[:, c0:c0 + LANES].astype(BF16)
        q_ref[:, c0 + LANES:c0 + 2 * LANES] = (
            q[:, c0 + LANES:c0 + 2 * LANES] * cos + qrot[:, h * LANES:(h + 1) * LANES] * sin).astype(BF16)
        k_ref[:, c0:c0 + LANES] = kv[:, h * LANES:(h + 1) * LANES].astype(BF16)
        k_ref[:, c0 + LANES:c0 + 2 * LANES] = kr
    vt_ref[...] = kv[:, N_HEADS * QK_NOPE_DIM:].T.astype(BF16)


def _qkv(x2, pos2, invf4, wa, qg, kvg, wuq, wuqr, wukv):
    t, d = x2.shape
    tm = TM_PROJ
    full = lambda a: pl.BlockSpec(a.shape, lambda i: (0,) * a.ndim)
    return pl.pallas_call(
        _qkv_kernel,
        grid=(t // tm,),
        in_specs=[pl.BlockSpec((tm, d), lambda i: (i, 0)),
                  pl.BlockSpec((tm, 1), lambda i: (i, 0)),
                  full(invf4), full(wa), full(qg), full(kvg), full(wuq), full(wuqr), full(wukv)],
        out_specs=[pl.BlockSpec((tm, N_HEADS * HEAD_QK_PAD), lambda i: (i, 0)),
                   pl.BlockSpec((tm, N_HEADS * HEAD_QK_PAD), lambda i: (i, 0)),
                   pl.BlockSpec((N_HEADS * V_HEAD_DIM, tm), lambda i: (0, i))],
        out_shape=[jax.ShapeDtypeStruct((t, N_HEADS * HEAD_QK_PAD), BF16),
                   jax.ShapeDtypeStruct((t, N_HEADS * HEAD_QK_PAD), BF16),
                   jax.ShapeDtypeStruct((N_HEADS * V_HEAD_DIM, t), BF16)],
        compiler_params=_cparams(("arbitrary",)),
        name="qkv",
    )(x2, pos2, invf4, wa, qg, kvg, wuq, wuqr, wukv)


def _conv_kernel(x_ref, wc_ref, cw_ref, cb_ref, g_ref, b_ref, o_ref, ubuf, ybuf, shbuf, *, tiles_per_seq):
    tm, cw = o_ref.shape
    i = pl.program_id(0)
    xb = x_ref[...].astype(BF16)

    @pl.when(i % tiles_per_seq == 0)
    def _():
        ubuf[0:CONV_HALO, :] = jnp.zeros((CONV_HALO, cw), F32)

    shift0 = CONV_HALO - (CONV_KERNEL - 1)
    n_sh = shbuf.shape[0]
    chunk_no = 0
    for c0 in range(0, cw, CONV_COLS):
        a = jnp.dot(xb, wc_ref[:, c0:c0 + CONV_COLS], preferred_element_type=F32)
        gate = jnp.dot(xb, wc_ref[:, cw + c0:cw + c0 + CONV_COLS], preferred_element_type=F32)
        ubuf[CONV_HALO:CONV_HALO + tm, c0:c0 + CONV_COLS] = a * _sigmoid(gate)
        for r0 in range(0, tm, CONV_ROWS):
            sh = shbuf.at[chunk_no % n_sh]
            chunk_no += 1
            win = ubuf[r0:r0 + CONV_ROWS + CONV_HALO, c0:c0 + CONV_COLS]
            for r in range(1, SUBLANES):
                sh[r] = win[r:r + CONV_ROWS + CONV_HALO - SUBLANES, :]
            acc = jnp.zeros((CONV_ROWS, CONV_COLS), F32)
            for k in range(CONV_KERNEL):
                wk = cw_ref[k:k + 1, c0:c0 + CONV_COLS]
                r, j = (shift0 + k) % SUBLANES, (shift0 + k) // SUBLANES
                if r == 0:
                    tap = win[j * SUBLANES:j * SUBLANES + CONV_ROWS, :]
                else:
                    tap = sh[r, j * SUBLANES:j * SUBLANES + CONV_ROWS, :]
                acc = acc + wk * tap
            ybuf[r0:r0 + CONV_ROWS, c0:c0 + CONV_COLS] = acc
    ubuf[0:CONV_HALO, :] = ubuf[tm:tm + CONV_HALO, :]
    y = _layer_norm(ybuf[...] + cb_ref[...], g_ref[...], b_ref[...])
    o_ref[...] = (y * _sigmoid(y)).astype(BF16)


def _conv(x2, wc, cw, cb, g, b, seq):
    t, d = x2.shape
    tm = TM_CONV
    c = cw.shape[1]
    full = lambda a: pl.BlockSpec(a.shape, lambda i: (0,) * a.ndim)
    return pl.pallas_call(
        functools.partial(_conv_kernel, tiles_per_seq=seq // tm),
        grid=(t // tm,),
        in_specs=[pl.BlockSpec((tm, d), lambda i: (i, 0)), full(wc), full(cw), full(cb), full(g), full(b)],
        out_specs=pl.BlockSpec((tm, c), lambda i: (i, 0)),
        out_shape=jax.ShapeDtypeStruct((t, c), BF16),
        scratch_shapes=[pltpu.VMEM((tm + CONV_HALO, c), F32), pltpu.VMEM((tm, c), F32),
                        pltpu.VMEM((CONV_SHIFT_BUFS, SUBLANES, CONV_ROWS + CONV_HALO - SUBLANES, CONV_COLS), F32)],
        compiler_params=_cparams(("arbitrary",)),
        name="conv",
    )(x2, wc, cw, cb, g, b)


def _attn_kernel(q_ref, k_ref, vt_ref, o_ref, *, scale, tk):
    tq = q_ref.shape[0]
    heads = q_ref.shape[1] // HEAD_QK_PAD
    i = pl.program_id(2)
    c = scale * 1.4426950408889634
    n_full = (i * tq) // tk
    q_off = i * tq - n_full * tk

    def block(j, carry, masked):
        r0 = pl.multiple_of(j * tk, tk)
        out = []
        scores = []
        for h in range(heads):
            q = q_ref[:, h * HEAD_QK_PAD:(h + 1) * HEAD_QK_PAD]
            kb = k_ref[pl.ds(r0, tk), h * HEAD_QK_PAD:(h + 1) * HEAD_QK_PAD]
            scores.append(lax.dot_general(kb, q, (((1,), (1,)), ((), ())),
                                          preferred_element_type=F32))
        for h, (m, l, acc) in enumerate(carry):
            vtb = vt_ref[h * V_HEAD_DIM:(h + 1) * V_HEAD_DIM, pl.ds(r0, tk)]
            st = scores[h]
            if masked:
                key = lax.broadcasted_iota(I32, (tk, tq), 0)
                qry = lax.broadcasted_iota(I32, (tk, tq), 1) + q_off
                st = jnp.where(key <= qry, st, -jnp.inf)
            m_blk = jnp.max(jnp.max(st.reshape(SUBLANES, tk // SUBLANES, tq), axis=0), axis=0, keepdims=True)
            m_new = jnp.maximum(m, m_blk)
            alpha = jnp.exp2((m - m_new) * c)
            pt = jnp.exp2((st - m_new) * c)
            l_blk = jnp.sum(jnp.sum(pt.reshape(SUBLANES, tk // SUBLANES, tq), axis=0), axis=0, keepdims=True)
            l = alpha * l + l_blk
            acc = alpha * acc + jnp.dot(vtb, pt.astype(BF16), preferred_element_type=F32)
            out.append((m_new, l, acc))
        return tuple(out)

    init = tuple((jnp.full((1, tq), -jnp.inf, F32), jnp.zeros((1, tq), F32),
                  jnp.zeros((V_HEAD_DIM, tq), F32)) for _ in range(heads))
    carry = lax.fori_loop(0, n_full, lambda j, cr: block(j, cr, False), init)
    final = block(n_full, carry, True)
    for h, (_, l, acc) in enumerate(final):
        o_ref[:, h * V_HEAD_DIM:(h + 1) * V_HEAD_DIM] = (acc / l).T.astype(BF16)


def _attn(q, k, v, batch, seq):
    tq = TQ_ATTN
    nq = seq // tq
    hb = HEADS_PER_ATTN_STEP
    scale = (QK_NOPE_DIM + QK_ROPE_DIM) ** -0.5
    return pl.pallas_call(
        functools.partial(_attn_kernel, scale=scale, tk=TK_ATTN),
        grid=(batch, N_HEADS // hb, nq),
        in_specs=[pl.BlockSpec((tq, hb * HEAD_QK_PAD), lambda b, h, i: (b * nq + i, h)),
                  pl.BlockSpec((seq, hb * HEAD_QK_PAD), lambda b, h, i: (b, h)),
                  pl.BlockSpec((hb * V_HEAD_DIM, seq), lambda b, h, i: (h, b))],
        out_specs=pl.BlockSpec((tq, hb * V_HEAD_DIM), lambda b, h, i: (b * nq + i, h)),
        out_shape=jax.ShapeDtypeStruct((batch * seq, N_HEADS * V_HEAD_DIM), BF16),
        compiler_params=_cparams(("arbitrary", "arbitrary", "arbitrary")),
        name="attn",
    )(q, k, v)


def _pack_bf16_pairs(v):
    c = v.shape[1] // 2
    lo = pltpu.bitcast(v[:, :c].astype(BF16).astype(F32), U32)
    hi = pltpu.bitcast(v[:, c:].astype(BF16).astype(F32), U32)
    return (hi & jnp.uint32(0xFFFF0000)) | (lo >> 16)


def _unpack_bf16_pairs(p):
    lo = pltpu.bitcast(p << 16, F32).astype(BF16)
    hi = pltpu.bitcast(p & jnp.uint32(0xFFFF0000), F32).astype(BF16)
    return jnp.concatenate([lo, hi], axis=1)


def _store_row_planes(ref, v):
    n = v.shape[0]
    p = ref.shape[0] // n
    for j in range(p):
        ref[pl.ds(j, n, stride=p), :] = v[:, j * LANES:(j + 1) * LANES]


def _load_row_planes(ref, p):
    n = ref.shape[0] // p
    return jnp.concatenate([ref[pl.ds(j, n, stride=p), :] for j in range(p)], axis=1)


def _oproj_kernel(a_ref, c_ref, x_ref, woa_ref, woc_ref, g_ref, b_ref, wr_ref,
                  h_ref, hp_ref, lg_ref, *, alpha):
    mix = jnp.dot(a_ref[...], woa_ref[...], preferred_element_type=F32)
    mix = mix + jnp.dot(c_ref[...], woc_ref[...], preferred_element_type=F32)
    h = _layer_norm(alpha * x_ref[...] + mix, g_ref[...], b_ref[...])
    h_ref[...] = h
    _store_row_planes(hp_ref, _pack_bf16_pairs(h))
    h_hi = h.astype(BF16)
    h_lo = (h - h_hi.astype(F32)).astype(BF16)
    logits = jnp.dot(jnp.concatenate([h_hi, h_lo, h_hi], axis=1), wr_ref[...], preferred_element_type=F32)
    lg_ref[...] = logits.T[:N_EXPERTS, :]


def _oproj(attn, conv, x2, woa, woc, g, b, wr, alpha):
    t, d = x2.shape
    tm = TM_PROJ
    full = lambda a: pl.BlockSpec(a.shape, lambda i: (0,) * a.ndim)
    return pl.pallas_call(
        functools.partial(_oproj_kernel, alpha=alpha),
        grid=(t // tm,),
        in_specs=[pl.BlockSpec((tm, attn.shape[1]), lambda i: (i, 0)),
                  pl.BlockSpec((tm, conv.shape[1]), lambda i: (i, 0)),
                  pl.BlockSpec((tm, d), lambda i: (i, 0)),
                  full(woa), full(woc), full(g), full(b), full(wr)],
        out_specs=[pl.BlockSpec((tm, d), lambda i: (i, 0)),
                   pl.BlockSpec((tm * (d // 2 // LANES), LANES), lambda i: (i, 0)),
                   pl.BlockSpec((N_EXPERTS, tm), lambda i: (0, i))],
        out_shape=[jax.ShapeDtypeStruct((t, d), F32),
                   jax.ShapeDtypeStruct((t * (d // 2 // LANES), LANES), U32),
                   jax.ShapeDtypeStruct((N_EXPERTS, t), F32)],
        compiler_params=_cparams(("arbitrary",)),
        name="oproj",
    )(attn, conv, x2, woa, woc, g, b, wr)


def _route_kernel(lg_ref, bias_ref, eidx_ref, rank_ref, wts_ref, cnt_ref, carry_ref):
    ne, tm = lg_ref.shape
    i = pl.program_id(0)

    @pl.when(i == 0)
    def _():
        carry_ref[...] = jnp.zeros(carry_ref.shape, F32)

    scores = _sigmoid(lg_ref[...])
    biased = scores + bias_ref[...]
    neg = -jnp.inf
    sub8 = lax.broadcasted_iota(I32, (GROUP_SIZE, tm), 0)
    gscore = []
    for g in range(N_EXPERT_GROUPS):
        blk = biased[g * GROUP_SIZE:(g + 1) * GROUP_SIZE, :]
        m1 = jnp.max(blk, axis=0, keepdims=True)
        i1 = jnp.min(jnp.where(blk == m1, sub8, GROUP_SIZE), axis=0, keepdims=True)
        m2 = jnp.max(jnp.where(sub8 == i1, neg, blk), axis=0, keepdims=True)
        gscore.append(m1 + m2)
    kept = []
    for g in range(N_EXPERT_GROUPS):
        beat = jnp.zeros((1, tm), I32)
        for o in range(N_EXPERT_GROUPS):
            if o < g:
                beat = beat + (gscore[o] >= gscore[g]).astype(I32)
            elif o > g:
                beat = beat + (gscore[o] > gscore[g]).astype(I32)
        kept.append(jnp.where(beat < TOPK_GROUPS, biased[g * GROUP_SIZE:(g + 1) * GROUP_SIZE, :], neg))
    cur = jnp.concatenate(kept, axis=0)
    sub = lax.broadcasted_iota(I32, (ne, tm), 0)
    sel_any = jnp.zeros((ne, tm), F32)
    picks = []
    wsum = jnp.zeros((1, tm), F32)
    for k in range(TOP_K):
        m = jnp.max(cur, axis=0, keepdims=True)
        ei = jnp.min(jnp.where(cur == m, sub, ne), axis=0, keepdims=True)
        sel = sub == ei
        w = jnp.sum(jnp.where(sel, scores, 0.0), axis=0, keepdims=True)
        cur = jnp.where(sel, neg, cur)
        sel_any = sel_any + sel.astype(F32)
        wsum = wsum + w
        picks.append((ei, sel, w))
    r = lax.broadcasted_iota(I32, (tm, tm), 0)
    c = lax.broadcasted_iota(I32, (tm, tm), 1)
    upper = (r <= c).astype(BF16)
    cum = jnp.dot(sel_any.astype(BF16), upper, preferred_element_type=F32)
    carry = carry_ref[:, 0:1]
    excl = cum - sel_any + carry
    total = carry + jnp.sum(sel_any, axis=1, keepdims=True)
    carry_ref[...] = jnp.broadcast_to(total, carry_ref.shape)
    cnt_ref[...] = jnp.broadcast_to(total, cnt_ref.shape)
    denom = wsum + 1e-20
    eidx_ref[...] = jnp.zeros(eidx_ref.shape, I32)
    rank_ref[...] = jnp.zeros(rank_ref.shape, I32)
    wts_ref[...] = jnp.zeros(wts_ref.shape, F32)
    for k, (ei, sel, w) in enumerate(picks):
        eidx_ref[k:k + 1, :] = ei
        rank_ref[k:k + 1, :] = jnp.sum(jnp.where(sel, excl, 0.0), axis=0, keepdims=True).astype(I32)
        wts_ref[k:k + 1, :] = w / denom * ROUTED_SCALE


def _route(logits_t, bias):
    ne, t = logits_t.shape
    tm = TM_ROUTE
    return pl.pallas_call(
        _route_kernel,
        grid=(t // tm,),
        in_specs=[pl.BlockSpec((ne, tm), lambda i: (0, i)), pl.BlockSpec((ne, 1), lambda i: (0, 0))],
        out_specs=[pl.BlockSpec((TOP_K_PAD, tm), lambda i: (0, i)),
                   pl.BlockSpec((TOP_K_PAD, tm), lambda i: (0, i)),
                   pl.BlockSpec((TOP_K_PAD, tm), lambda i: (0, i)),
                   pl.BlockSpec((ne, LANES), lambda i: (0, 0))],
        out_shape=[jax.ShapeDtypeStruct((TOP_K_PAD, t), I32),
                   jax.ShapeDtypeStruct((TOP_K_PAD, t), I32),
                   jax.ShapeDtypeStruct((TOP_K_PAD, t), F32),
                   jax.ShapeDtypeStruct((ne, LANES), F32)],
        scratch_shapes=[pltpu.VMEM((ne, LANES), F32)],
        compiler_params=_cparams(("arbitrary",)),
        name="route",
    )(logits_t, bias)


def _wait_rows(src_rows, dst_rows, sem, n_rows):
    def body(_, c):
        pltpu.make_async_copy(src_rows, dst_rows, sem).wait()
        return c
    lax.fori_loop(0, n_rows // WAIT_ROWS, body, 0)


def _dispatch_kernel(zfill_ref, slots_ref, hp_ref, xs_hbm, zero_vmem, sem, zsem):
    i = pl.program_id(0)
    te = zero_vmem.shape[0]
    tc = hp_ref.shape[0]
    n_tiles = xs_hbm.shape[0] // te

    def zero_copy(tile):
        return pltpu.make_async_copy(zero_vmem, xs_hbm.at[pl.ds(pl.multiple_of(tile * te, te), te)], zsem)

    @pl.when(i == 0)
    def _():
        zero_vmem[...] = jnp.zeros(zero_vmem.shape, U32)

        def zstart(tile, c):
            @pl.when(zfill_ref[tile] != 0)
            def _():
                zero_copy(tile).start()
            return c

        def zwait(tile, c):
            @pl.when(zfill_ref[tile] != 0)
            def _():
                zero_copy(tile).wait()
            return c

        lax.fori_loop(0, n_tiles, zstart, 0)
        lax.fori_loop(0, n_tiles, zwait, 0)

    base = i * (tc * TOP_K_PAD)

    def issue(t, c):
        for k in range(TOP_K):
            slot = slots_ref[base + t * TOP_K_PAD + k]
            pltpu.make_async_copy(hp_ref.at[t], xs_hbm.at[slot], sem).start(priority=k % 2)
        return c

    lax.fori_loop(0, tc, issue, 0, unroll=ISSUE_UNROLL)
    _wait_rows(hp_ref.at[pl.ds(0, WAIT_ROWS)], xs_hbm.at[pl.ds(0, WAIT_ROWS)], sem, tc * TOP_K)


def _dispatch(zfill, slots_flat, hp, n_rows):
    t, planes, lanes = hp.shape
    tc = TC_DISP
    return pl.pallas_call(
        _dispatch_kernel,
        grid_spec=pltpu.PrefetchScalarGridSpec(
            num_scalar_prefetch=2,
            grid=(t // tc,),
            in_specs=[pl.BlockSpec((tc, planes, lanes), lambda i, z, s: (i, 0, 0))],
            out_specs=pl.BlockSpec(memory_space=pl.ANY),
            scratch_shapes=[pltpu.VMEM((TE_ROWS, planes, lanes), U32),
                            pltpu.SemaphoreType.DMA, pltpu.SemaphoreType.DMA],
        ),
        out_shape=jax.ShapeDtypeStruct((n_rows, planes, lanes), U32),
        compiler_params=_cparams(("arbitrary",)),
        name="dispatch",
    )(zfill, slots_flat, hp)


def _experts_kernel(te_ref, nv_ref, nx_ref, xs_ref, wg_hbm, wu_hbm, wd_hbm, y_ref,
                    sg, su, sd, wgb, wub, wdb, sem):
    i = pl.program_id(0)
    e = te_ref[i]
    prev = te_ref[jnp.maximum(i - 1, 0)]
    valid = i < nv_ref[0]

    def fetch(ex):
        return (pltpu.make_async_copy(wg_hbm.at[ex], sg, sem.at[0]),
                pltpu.make_async_copy(wu_hbm.at[ex], su, sem.at[1]),
                pltpu.make_async_copy(wd_hbm.at[ex], sd, sem.at[2]))

    @pl.when(i == 0)
    def _():
        for cp in fetch(e):
            cp.start()

    @pl.when(valid & ((i == 0) | (e != prev)))
    def _():
        for cp in fetch(e):
            cp.wait()
        wgb[...] = sg[...].astype(BF16)
        wub[...] = su[...].astype(BF16)
        wdb[...] = sd[...].astype(BF16)

        @pl.when(nx_ref[i] >= 0)
        def _():
            for cp in fetch(nx_ref[i]):
                cp.start()

    @pl.when(valid)
    def _():
        xs = _unpack_bf16_pairs(_load_row_planes(xs_ref, xs_ref.shape[0] // TE_ROWS))
        g = jnp.dot(xs, wgb[...], preferred_element_type=F32)
        u = jnp.dot(xs, wub[...], preferred_element_type=F32)
        hid = (g * _sigmoid(g) * u).astype(BF16)
        y = jnp.dot(hid, wdb[...], preferred_element_type=F32)
        _store_row_planes(y_ref, _pack_bf16_pairs(y))

    @pl.when(jnp.logical_not(valid))
    def _():
        y_ref[...] = jnp.zeros(y_ref.shape, U32)


def _experts(tile_expert, n_valid, next_expert, xs, w_gate, w_up, w_down):
    n_rows, xp, _ = xs.shape
    ne, d, ff = w_gate.shape
    te = TE_ROWS
    in_map = lambda i, te_ref, nv_ref, nx_ref: (jnp.minimum(i, nv_ref[0] - 1), 0)
    out_map = lambda i, te_ref, nv_ref, nx_ref: (i, 0)
    hbm = pl.BlockSpec(memory_space=pl.ANY)
    y = pl.pallas_call(
        _experts_kernel,
        grid_spec=pltpu.PrefetchScalarGridSpec(
            num_scalar_prefetch=3,
            grid=(n_rows // te,),
            in_specs=[pl.BlockSpec((te * xp, LANES), in_map), hbm, hbm, hbm],
            out_specs=pl.BlockSpec((te * xp, LANES), out_map),
            scratch_shapes=[pltpu.VMEM((d, ff), F32), pltpu.VMEM((d, ff), F32), pltpu.VMEM((ff, d), F32),
                            pltpu.VMEM((d, ff), BF16), pltpu.VMEM((d, ff), BF16), pltpu.VMEM((ff, d), BF16),
                            pltpu.SemaphoreType.DMA((3,))],
        ),
        out_shape=jax.ShapeDtypeStruct((n_rows * xp, LANES), U32),
        compiler_params=_cparams(("arbitrary",)),
        name="experts",
    )(tile_expert, n_valid, next_expert, xs.reshape(n_rows * xp, LANES), w_gate, w_up, w_down)
    return y.reshape(n_rows, xp, LANES)


def _combine_kernel(slots_ref, y_hbm, h_ref, wts_ref, wsg_ref, wsu_ref, wsd_ref, g_ref, b_ref,
                    o_ref, gbuf, rbuf, shbuf, sem, *, alpha):
    tm, d = o_ref.shape
    planes = gbuf.shape[3]
    i = pl.program_id(0)
    cur = i % 2
    grp = SUBLANES

    def issue_rows(step, t0, n):
        buf = step % 2
        base = step * (tm * TOP_K_PAD)
        for u in range(n):
            for k in range(TOP_K):
                slot = slots_ref[base + (t0 + u) * TOP_K_PAD + k]
                pltpu.make_async_copy(y_hbm.at[slot], gbuf.at[buf, k, t0 + u], sem.at[buf]).start(priority=k % 2)

    @pl.when(i == 0)
    def _():
        def first(t, c):
            issue_rows(0, t, 1)
            return c
        lax.fori_loop(0, tm, first, 0, unroll=ISSUE_UNROLL)

    hb = h_ref[...].astype(BF16)
    sg = jnp.dot(hb, wsg_ref[...], preferred_element_type=F32)
    su = jnp.dot(hb, wsu_ref[...], preferred_element_type=F32)
    shbuf[...] = jnp.dot((sg * _sigmoid(sg) * su).astype(BF16), wsd_ref[...], preferred_element_type=F32)
    _wait_rows(y_hbm.at[pl.ds(0, WAIT_ROWS)], gbuf.at[cur, 0, pl.ds(0, WAIT_ROWS)], sem.at[cur], tm * TOP_K)

    def group(g, prefetch):
        t0 = pl.multiple_of(g * grp, grp)
        packed = [gbuf[cur, k, pl.ds(t0, grp)] for k in range(TOP_K)]
        w = wts_ref[pl.ds(t0, grp)]
        resid = alpha * h_ref[pl.ds(t0, grp), :] + shbuf[pl.ds(t0, grp), :]
        if prefetch:
            issue_rows(i + 1, t0, grp)
        r_lo = r_hi = None
        for k in range(TOP_K):
            lo = w[:, k:k + 1, :] * pltpu.bitcast(packed[k] << 16, F32)
            hi = w[:, k:k + 1, :] * pltpu.bitcast(packed[k] & jnp.uint32(0xFFFF0000), F32)
            r_lo = lo if r_lo is None else r_lo + lo
            r_hi = hi if r_hi is None else r_hi + hi
        p0 = pl.multiple_of(t0 * planes, grp * planes)
        rbuf[0, pl.ds(p0, grp * planes), :] = r_lo.reshape(grp * planes, LANES)
        rbuf[1, pl.ds(p0, grp * planes), :] = r_hi.reshape(grp * planes, LANES)
        routed = jnp.concatenate([rbuf[half, pl.ds(p0 + j, grp, stride=planes), :]
                                  for half in range(2) for j in range(planes)], axis=1)
        o_ref[pl.ds(t0, grp), :] = _layer_norm(resid + routed, g_ref[...], b_ref[...])

    has_next = i + 1 < pl.num_programs(0)

    @pl.when(has_next)
    def _():
        lax.fori_loop(0, tm // grp, lambda g, c: (group(g, True), c)[1], 0, unroll=GROUP_UNROLL)

    @pl.when(jnp.logical_not(has_next))
    def _():
        lax.fori_loop(0, tm // grp, lambda g, c: (group(g, False), c)[1], 0, unroll=GROUP_UNROLL)


def _combine(slots_flat, y, h, wts_planes, wsg, wsu, wsd, g, b, alpha):
    t, d = h.shape
    tm = TM_COMB
    planes = y.shape[1]
    full = lambda a: pl.BlockSpec(a.shape, lambda i, s: (0,) * a.ndim)
    once = lambda a: pl.BlockSpec(a.shape, lambda i, s: (0,) * a.ndim, pipeline_mode=pl.Buffered(1))
    return pl.pallas_call(
        functools.partial(_combine_kernel, alpha=alpha),
        grid_spec=pltpu.PrefetchScalarGridSpec(
            num_scalar_prefetch=1,
            grid=(t // tm,),
            in_specs=[pl.BlockSpec(memory_space=pl.ANY),
                      pl.BlockSpec((tm, d), lambda i, s: (i, 0)),
                      pl.BlockSpec((tm, TOP_K_PAD, LANES), lambda i, s: (i, 0, 0)),
                      once(wsg), once(wsu), once(wsd), full(g), full(b)],
            out_specs=pl.BlockSpec((tm, d), lambda i, s: (i, 0)),
            scratch_shapes=[pltpu.VMEM((2, TOP_K, tm, planes, LANES), U32),
                            pltpu.VMEM((2, tm * planes, LANES), F32), pltpu.VMEM((tm, d), F32),
                            pltpu.SemaphoreType.DMA((2,))],
        ),
        out_shape=jax.ShapeDtypeStruct((t, d), F32),
        compiler_params=_cparams(("arbitrary",)),
        name="combine",
    )(slots_flat, y, h, wts_planes, wsg, wsu, wsd, g, b)


def _prep_attention_weights(w_in, w_uq, w_ukv):
    d = w_in.shape[0]
    ql = w_uq.shape[0]
    kvl = w_ukv.shape[0]
    half = QK_ROPE_DIM // 2
    pad = LANES - QK_ROPE_DIM
    w_kr = w_in[:, ql + kvl:ql + kvl + QK_ROPE_DIM]
    w_kr_rot = jnp.concatenate([-w_kr[:, half:], w_kr[:, :half]], axis=1)
    zpad = jnp.zeros((d, pad), w_in.dtype)
    wa = jnp.concatenate([w_in[:, :ql + kvl], w_kr, zpad, w_kr_rot, zpad], axis=1).astype(BF16)
    wc = w_in[:, ql + kvl + QK_ROPE_DIM:].astype(BF16)
    uq = w_uq.reshape(ql, N_HEADS, QK_NOPE_DIM + QK_ROPE_DIM)
    uq_nope, uq_rope = uq[..., :QK_NOPE_DIM], uq[..., QK_NOPE_DIM:]
    zq = jnp.zeros((ql, N_HEADS, pad), w_uq.dtype)
    wuq = jnp.concatenate([uq_nope, uq_rope, zq], axis=-1).reshape(ql, N_HEADS * HEAD_QK_PAD).astype(BF16)
    uq_rot = jnp.concatenate([-uq_rope[..., half:], uq_rope[..., :half], zq], axis=-1)
    wuqr = uq_rot.reshape(ql, N_HEADS * LANES).astype(BF16)
    ukv = w_ukv.reshape(kvl, N_HEADS, QK_NOPE_DIM + V_HEAD_DIM)
    wukv = jnp.concatenate([ukv[..., :QK_NOPE_DIM].reshape(kvl, -1), ukv[..., QK_NOPE_DIM:].reshape(kvl, -1)],
                           axis=1).astype(BF16)
    return wa, wc, wuq, wuqr, wukv


def _layer(h, pos2, invf4, batch, seq, alpha, w_in, q_norm_g, w_uq, kv_norm_g, w_ukv, conv_w, conv_b,
           conv_ln_g, conv_ln_b, w_o, ln1_g, ln1_b, w_router, router_bias, w_gate, w_up, w_down,
           ws_gate, ws_up, ws_down, ln2_g, ln2_b):
    t, d = h.shape
    row = lambda a: a.reshape(1, -1)
    wa, wc, wuq, wuqr, wukv = _prep_attention_weights(w_in, w_uq, w_ukv)
    q, k, v = _qkv(h, pos2, invf4, wa, row(q_norm_g), row(kv_norm_g), wuq, wuqr, wukv)
    conv = _conv(h, wc, conv_w, row(conv_b), row(conv_ln_g), row(conv_ln_b), seq)
    attn = _attn(q, k, v, batch, seq)
    aw = attn.shape[1]
    wr32 = jnp.pad(w_router.astype(F32), ((0, 0), (0, LANES - N_EXPERTS)))
    wr_hi = wr32.astype(BF16)
    wr_lo = (wr32 - wr_hi.astype(F32)).astype(BF16)
    wr = jnp.concatenate([wr_hi, wr_hi, wr_lo], axis=0)
    h1, h1p, logits_t = _oproj(attn, conv, h, w_o[:aw].astype(BF16), w_o[aw:].astype(BF16),
                               row(ln1_g), row(ln1_b), wr, alpha)
    return _moe_ffn(h1, h1p, logits_t, alpha, router_bias, w_gate, w_up, w_down,
                    ws_gate, ws_up, ws_down, ln2_g, ln2_b)


def _moe_ffn(h1, h1p, logits_t, alpha, router_bias, w_gate, w_up, w_down, ws_gate, ws_up, ws_down, ln2_g, ln2_b):
    t, d = h1.shape
    row = lambda a: a.reshape(1, -1)
    eidx, rank, wts, cnt = _route(logits_t, router_bias.astype(F32).reshape(N_EXPERTS, 1))
    counts = cnt[:, 0].astype(I32)
    padded = (counts + TE_ROWS - 1) // TE_ROWS * TE_ROWS
    pend = jnp.cumsum(padded)
    poff = pend - padded
    n_rows = t * TOP_K + N_EXPERTS * TE_ROWS
    n_tiles = n_rows // TE_ROWS
    n_valid = (pend[-1] // TE_ROWS).astype(I32).reshape(1)
    tile_row = jnp.minimum(jnp.arange(n_tiles, dtype=I32), n_valid[0] - 1) * TE_ROWS
    tile_expert = jnp.sum((pend[None, :] <= tile_row[:, None]).astype(I32), axis=1)
    tile_expert = jnp.minimum(tile_expert, N_EXPERTS - 1)
    tiles = jnp.arange(n_tiles, dtype=I32)
    last_tile = jnp.where(padded > 0, pend // TE_ROWS - 1, -1)
    zfill = ((tiles >= n_valid[0]) | jnp.any(tiles[:, None] == last_tile[None, :], axis=1)).astype(I32)
    ids = jnp.arange(N_EXPERTS, dtype=I32)
    later = (ids[None, :] > ids[:, None]) & (padded[None, :] > 0)
    next_of = jnp.min(jnp.where(later, ids[None, :], N_EXPERTS), axis=1)
    next_of = jnp.where(next_of == N_EXPERTS, -1, next_of)
    next_expert = jnp.sum(jnp.where(tile_expert[:, None] == ids[None, :], next_of[None, :], 0), axis=1).astype(I32)
    slot_base = jnp.sum(jnp.where(eidx[None] == ids[:, None, None], poff[:, None, None], 0), axis=0)
    slots_flat = (slot_base + rank).T.reshape(-1)
    xs = _dispatch(zfill, slots_flat, h1p.reshape(t, -1, LANES), n_rows)
    y = _experts(tile_expert, n_valid, next_expert, xs, w_gate, w_up, w_down)
    wts_planes = jnp.broadcast_to(wts.T[:, :, None], (t, TOP_K_PAD, LANES))
    return _combine(slots_flat, y, h1, wts_planes, ws_gate.astype(BF16), ws_up.astype(BF16),
                    ws_down.astype(BF16), row(ln2_g), row(ln2_b), alpha)


def kernel(x, positions, w_in, q_norm_g, w_uq, kv_norm_g, w_ukv, conv_w, conv_b, conv_ln_g, conv_ln_b, w_o, ln1_g, ln1_b, w_router, router_bias, w_gate, w_up, w_down, ws_gate, ws_up, ws_down, ln2_g, ln2_b):
    batch, seq, d = x.shape
    depth = w_in.shape[0]
    alpha = (2.0 * depth) ** 0.25
    inv_freq = ROPE_BASE ** (-jnp.arange(0, QK_ROPE_DIM, 2, dtype=F32) / QK_ROPE_DIM)
    invf4 = jnp.tile(inv_freq, LANES // inv_freq.shape[0]).reshape(1, LANES)
    pos2 = positions.reshape(batch * seq, 1)
    h = x.reshape(batch * seq, d)
    for l in range(depth):
        h = _layer(h, pos2, invf4, batch, seq, alpha, w_in[l], q_norm_g[l], w_uq[l], kv_norm_g[l], w_ukv[l],
                   conv_w[l], conv_b[l], conv_ln_g[l], conv_ln_b[l], w_o[l], ln1_g[l], ln1_b[l],
                   w_router[l], router_bias[l], w_gate[l], w_up[l], w_down[l],
                   ws_gate[l], ws_up[l], ws_down[l], ln2_g[l], ln2_b[l])
    return h.reshape(batch, seq, d)
```

```python
import functools

import jax
import jax.numpy as jnp
from jax import lax
from jax.experimental import pallas as pl
from jax.experimental.pallas import tpu as pltpu

F32 = jnp.float32
BF16 = jnp.bfloat16
I32 = jnp.int32
U32 = jnp.uint32

N_HEADS = 8
QK_NOPE_DIM = 128
QK_ROPE_DIM = 64
V_HEAD_DIM = 128
HEAD_QK_PAD = 256
CONV_KERNEL = 31
N_EXPERTS = 64
TOP_K = 6
TOP_K_PAD = 8
N_EXPERT_GROUPS = 8
GROUP_SIZE = N_EXPERTS // N_EXPERT_GROUPS
TOPK_GROUPS = 4
ROUTED_SCALE = 2.5
ROPE_BASE = 10000.0
LN_EPS = 1e-5
RMS_EPS = 1e-6

LANES = 128
SUBLANES = 8
CONV_HALO = 32

TM_PROJ = 256
TM_CONV = 512
TQ_ATTN = 512
TK_ATTN = 512
HEADS_PER_ATTN_STEP = 4
TM_ROUTE = 512
TE_ROWS = 256
TM_COMB = 256
TC_DISP = 1024
WAIT_ROWS = 128
ISSUE_UNROLL = 4
GROUP_UNROLL = 8
CONV_ROWS = 64
CONV_COLS = 256
CONV_SHIFT_BUFS = 4
VMEM_LIMIT = 56 * 1024 * 1024


def _cparams(sem):
    return pltpu.CompilerParams(dimension_semantics=sem, vmem_limit_bytes=VMEM_LIMIT)


def _sigmoid(v):
    return 1.0 / (1.0 + jnp.exp(-v))


def _layer_norm(v, g, b):
    mu = jnp.mean(v, axis=-1, keepdims=True)
    d = v - mu
    var = jnp.mean(d * d, axis=-1, keepdims=True)
    return d * lax.rsqrt(var + LN_EPS) * g + b


def _rms_norm(v, g):
    ms = jnp.mean(v * v, axis=-1, keepdims=True)
    return v * lax.rsqrt(ms + RMS_EPS) * g


def _qkv_kernel(x_ref, pos_ref, invf_ref, wa_ref, qg_ref, kvg_ref, wuq_ref, wuqr_ref, wukv_ref,
                qt_ref, k_ref, vt_ref):
    ql = qg_ref.shape[1]
    kvl = kvg_ref.shape[1]
    xb = x_ref[...].astype(BF16)
    lat = jnp.dot(xb, wa_ref[...], preferred_element_type=F32)
    ang = pos_ref[...].astype(F32) * invf_ref[...]
    cos = jnp.cos(ang)
    sin = jnp.sin(ang)
    cq = _rms_norm(lat[:, :ql], qg_ref[...]).astype(BF16)
    ckv = _rms_norm(lat[:, ql:ql + kvl], kvg_ref[...]).astype(BF16)
    kr = lat[:, ql + kvl:ql + kvl + LANES] * cos + lat[:, ql + kvl + LANES:ql + kvl + 2 * LANES] * sin
    kr = kr.astype(BF16)
    q = jnp.dot(cq, wuq_ref[...], preferred_element_type=F32)
    qrot = jnp.dot(cq, wuqr_ref[...], preferred_element_type=F32)
    kv = jnp.dot(ckv, wukv_ref[...], preferred_element_type=F32)
    for h in range(N_HEADS):
        c0 = h * HEAD_QK_PAD
        qt_ref[c0:c0 + LANES, :] = q[:, c0:c0 + LANES].T.astype(BF16)
        qt_ref[c0 + LANES:c0 + 2 * LANES, :] = (
            q[:, c0 + LANES:c0 + 2 * LANES] * cos + qrot[:, h * LANES:(h + 1) * LANES] * sin).T.astype(BF16)
        k_ref[:, c0:c0 + LANES] = kv[:, h * LANES:(h + 1) * LANES].astype(BF16)
        k_ref[:, c0 + LANES:c0 + 2 * LANES] = kr
    vt_ref[...] = kv[:, N_HEADS * QK_NOPE_DIM:].T.astype(BF16)


def _qkv(x2, pos2, invf4, wa, qg, kvg, wuq, wuqr, wukv):
    t, d = x2.shape
    tm = TM_PROJ
    full = lambda a: pl.BlockSpec(a.shape, lambda i: (0,) * a.ndim)
    return pl.pallas_call(
        _qkv_kernel,
        grid=(t // tm,),
        in_specs=[pl.BlockSpec((tm, d), lambda i: (i, 0)),
                  pl.BlockSpec((tm, 1), lambda i: (i, 0)),
                  full(invf4), full(wa), full(qg), full(kvg), full(wuq), full(wuqr), full(wukv)],
        out_specs=[pl.BlockSpec((N_HEADS * HEAD_QK_PAD, tm), lambda i: (0, i)),
                   pl.BlockSpec((tm, N_HEADS * HEAD_QK_PAD), lambda i: (i, 0)),
                   pl.BlockSpec((N_HEADS * V_HEAD_DIM, tm), lambda i: (0, i))],
        out_shape=[jax.ShapeDtypeStruct((N_HEADS * HEAD_QK_PAD, t), BF16),
                   jax.ShapeDtypeStruct((t, N_HEADS * HEAD_QK_PAD), BF16),
                   jax.ShapeDtypeStruct((N_HEADS * V_HEAD_DIM, t), BF16)],
        compiler_params=_cparams(("arbitrary",)),
        name="qkv",
    )(x2, pos2, invf4, wa, qg, kvg, wuq, wuqr, wukv)


def _conv_kernel(x_ref, wc_ref, cw_ref, cb_ref, g_ref, b_ref, o_ref, ubuf, ybuf, shbuf, *, tiles_per_seq):
    tm, cw = o_ref.shape
    i = pl.program_id(0)
    xb = x_ref[...].astype(BF16)

    @pl.when(i % tiles_per_seq == 0)
    def _():
        ubuf[0:CONV_HALO, :] = jnp.zeros((CONV_HALO, cw), F32)

    shift0 = CONV_HALO - (CONV_KERNEL - 1)
    n_sh = shbuf.shape[0]
    chunk_no = 0
    for c0 in range(0, cw, CONV_COLS):
        a = jnp.dot(xb, wc_ref[:, c0:c0 + CONV_COLS], preferred_element_type=F32)
        gate = jnp.dot(xb, wc_ref[:, cw + c0:cw + c0 + CONV_COLS], preferred_element_type=F32)
        ubuf[CONV_HALO:CONV_HALO + tm, c0:c0 + CONV_COLS] = a * _sigmoid(gate)
        for r0 in range(0, tm, CONV_ROWS):
            sh = shbuf.at[chunk_no % n_sh]
            chunk_no += 1
            win = ubuf[r0:r0 + CONV_ROWS + CONV_HALO, c0:c0 + CONV_COLS]
            for r in range(1, SUBLANES):
                sh[r] = win[r:r + CONV_ROWS + CONV_HALO - SUBLANES, :]
            acc = jnp.zeros((CONV_ROWS, CONV_COLS), F32)
            for k in range(CONV_KERNEL):
                wk = cw_ref[k:k + 1, c0:c0 + CONV_COLS]
                r, j = (shift0 + k) % SUBLANES, (shift0 + k) // SUBLANES
                if r == 0:
                    tap = win[j * SUBLANES:j * SUBLANES + CONV_ROWS, :]
                else:
                    tap = sh[r, j * SUBLANES:j * SUBLANES + CONV_ROWS, :]
                acc = acc + wk * tap
            ybuf[r0:r0 + CONV_ROWS, c0:c0 + CONV_COLS] = acc
    ubuf[0:CONV_HALO, :] = ubuf[tm:tm + CONV_HALO, :]
    y = _layer_norm(ybuf[...] + cb_ref[...], g_ref[...], b_ref[...])
    o_ref[...] = (y * _sigmoid(y)).astype(BF16)


def _conv(x2, wc, cw, cb, g, b, seq):
    t, d = x2.shape
    tm = TM_CONV
    c = cw.shape[1]
    full = lambda a: pl.BlockSpec(a.shape, lambda i: (0,) * a.ndim)
    return pl.pallas_call(
        functools.partial(_conv_kernel, tiles_per_seq=seq // tm),
        grid=(t // tm,),
        in_specs=[pl.BlockSpec((tm, d), lambda i: (i, 0)), full(wc), full(cw), full(cb), full(g), full(b)],
        out_specs=pl.BlockSpec((tm, c), lambda i: (i, 0)),
        out_shape=jax.ShapeDtypeStruct((t, c), BF16),
        scratch_shapes=[pltpu.VMEM((tm + CONV_HALO, c), F32), pltpu.VMEM((tm, c), F32),
                        pltpu.VMEM((CONV_SHIFT_BUFS, SUBLANES, CONV_ROWS + CONV_HALO - SUBLANES, CONV_COLS), F32)],
        compiler_params=_cparams(("arbitrary",)),
        name="conv",
    )(x2, wc, cw, cb, g, b)


def _attn_kernel(qt_ref, k_ref, vt_ref, o_ref, *, scale, tk):
    tq = qt_ref.shape[1]
    heads = qt_ref.shape[0] // HEAD_QK_PAD
    i = pl.program_id(2)
    c = scale * 1.4426950408889634
    n_full = (i * tq) // tk
    q_off = i * tq - n_full * tk

    def block(j, carry, masked):
        r0 = pl.multiple_of(j * tk, tk)
        out = []
        scores = []
        for h in range(heads):
            qt = qt_ref[h * HEAD_QK_PAD:(h + 1) * HEAD_QK_PAD, :]
            kb = k_ref[pl.ds(r0, tk), h * HEAD_QK_PAD:(h + 1) * HEAD_QK_PAD]
            scores.append(jnp.dot(kb, qt, preferred_element_type=F32))
        for h, (m, l, acc) in enumerate(carry):
            vtb = vt_ref[h * V_HEAD_DIM:(h + 1) * V_HEAD_DIM, pl.ds(r0, tk)]
            st = scores[h]
            if masked:
                key = lax.broadcasted_iota(I32, (tk, tq), 0)
                qry = lax.broadcasted_iota(I32, (tk, tq), 1) + q_off
                st = jnp.where(key <= qry, st, -jnp.inf)
            m_blk = jnp.max(jnp.max(st.reshape(SUBLANES, tk // SUBLANES, tq), axis=0), axis=0, keepdims=True)
            m_new = jnp.maximum(m, m_blk)
            alpha = jnp.exp2((m - m_new) * c)
            pt = jnp.exp2((st - m_new) * c)
            l_blk = jnp.sum(jnp.sum(pt.reshape(SUBLANES, tk // SUBLANES, tq), axis=0), axis=0, keepdims=True)
            l = alpha * l + l_blk
            acc = alpha * acc + jnp.dot(vtb, pt.astype(BF16), preferred_element_type=F32)
            out.append((m_new, l, acc))
        return tuple(out)

    init = tuple((jnp.full((1, tq), -jnp.inf, F32), jnp.zeros((1, tq), F32),
                  jnp.zeros((V_HEAD_DIM, tq), F32)) for _ in range(heads))
    carry = lax.fori_loop(0, n_full, lambda j, cr: block(j, cr, False), init)
    final = block(n_full, carry, True)
    for h, (_, l, acc) in enumerate(final):
        o_ref[:, h * V_HEAD_DIM:(h + 1) * V_HEAD_DIM] = (acc / l).T.astype(BF16)


def _attn(q, k, v, batch, seq):
    tq = TQ_ATTN
    nq = seq // tq
    hb = HEADS_PER_ATTN_STEP
    scale = (QK_NOPE_DIM + QK_ROPE_DIM) ** -0.5
    return pl.pallas_call(
        functools.partial(_attn_kernel, scale=scale, tk=TK_ATTN),
        grid=(batch, N_HEADS // hb, nq),
        in_specs=[pl.BlockSpec((hb * HEAD_QK_PAD, tq), lambda b, h, i: (h, b * nq + i)),
                  pl.BlockSpec((seq, hb * HEAD_QK_PAD), lambda b, h, i: (b, h)),
                  pl.BlockSpec((hb * V_HEAD_DIM, seq), lambda b, h, i: (h, b))],
        out_specs=pl.BlockSpec((tq, hb * V_HEAD_DIM), lambda b, h, i: (b * nq + i, h)),
        out_shape=jax.ShapeDtypeStruct((batch * seq, N_HEADS * V_HEAD_DIM), BF16),
        compiler_params=_cparams(("arbitrary", "arbitrary", "arbitrary")),
        name="attn",
    )(q, k, v)


def _pack_bf16_pairs(v):
    c = v.shape[1] // 2
    lo = pltpu.bitcast(v[:, :c].astype(BF16).astype(F32), U32)
    hi = pltpu.bitcast(v[:, c:].astype(BF16).astype(F32), U32)
    return (hi & jnp.uint32(0xFFFF0000)) | (lo >> 16)


def _unpack_bf16_pairs(p):
    lo = pltpu.bitcast(p << 16, F32).astype(BF16)
    hi = pltpu.bitcast(p & jnp.uint32(0xFFFF0000), F32).astype(BF16)
    return jnp.concatenate([lo, hi], axis=1)


def _store_row_planes(ref, v):
    n = v.shape[0]
    p = ref.shape[0] // n
    for j in range(p):
        ref[pl.ds(j, n, stride=p), :] = v[:, j * LANES:(j + 1) * LANES]


def _load_row_planes(ref, p):
    n = ref.shape[0] // p
    return jnp.concatenate([ref[pl.ds(j, n, stride=p), :] for j in range(p)], axis=1)


def _oproj_kernel(a_ref, c_ref, x_ref, woa_ref, woc_ref, g_ref, b_ref, wr_ref,
                  h_ref, hp_ref, lg_ref, *, alpha):
    mix = jnp.dot(a_ref[...], woa_ref[...], preferred_element_type=F32)
    mix = mix + jnp.dot(c_ref[...], woc_ref[...], preferred_element_type=F32)
    h = _layer_norm(alpha * x_ref[...] + mix, g_ref[...], b_ref[...])
    h_ref[...] = h
    _store_row_planes(hp_ref, _pack_bf16_pairs(h))
    h_hi = h.astype(BF16)
    h_lo = (h - h_hi.astype(F32)).astype(BF16)
    logits = jnp.dot(jnp.concatenate([h_hi, h_lo, h_hi], axis=1), wr_ref[...], preferred_element_type=F32)
    lg_ref[...] = logits.T[:N_EXPERTS, :]


def _oproj(attn, conv, x2, woa, woc, g, b, wr, alpha):
    t, d = x2.shape
    tm = TM_PROJ
    full = lambda a: pl.BlockSpec(a.shape, lambda i: (0,) * a.ndim)
    return pl.pallas_call(
        functools.partial(_oproj_kernel, alpha=alpha),
        grid=(t // tm,),
        in_specs=[pl.BlockSpec((tm, attn.shape[1]), lambda i: (i, 0)),
                  pl.BlockSpec((tm, conv.shape[1]), lambda i: (i, 0)),
                  pl.BlockSpec((tm, d), lambda i: (i, 0)),
                  full(woa), full(woc), full(g), full(b), full(wr)],
        out_specs=[pl.BlockSpec((tm, d), lambda i: (i, 0)),
                   pl.BlockSpec((tm * (d // 2 // LANES), LANES), lambda i: (i, 0)),
                   pl.BlockSpec((N_EXPERTS, tm), lambda i: (0, i))],
        out_shape=[jax.ShapeDtypeStruct((t, d), F32),
                   jax.ShapeDtypeStruct((t * (d // 2 // LANES), LANES), U32),
                   jax.ShapeDtypeStruct((N_EXPERTS, t), F32)],
        compiler_params=_cparams(("arbitrary",)),
        name="oproj",
    )(attn, conv, x2, woa, woc, g, b, wr)


def _route_kernel(lg_ref, bias_ref, eidx_ref, rank_ref, wts_ref, cnt_ref, carry_ref):
    ne, tm = lg_ref.shape
    i = pl.program_id(0)

    @pl.when(i == 0)
    def _():
        carry_ref[...] = jnp.zeros(carry_ref.shape, F32)

    scores = _sigmoid(lg_ref[...])
    biased = scores + bias_ref[...]
    neg = -jnp.inf
    sub8 = lax.broadcasted_iota(I32, (GROUP_SIZE, tm), 0)
    gscore = []
    for g in range(N_EXPERT_GROUPS):
        blk = biased[g * GROUP_SIZE:(g + 1) * GROUP_SIZE, :]
        m1 = jnp.max(blk, axis=0, keepdims=True)
        i1 = jnp.min(jnp.where(blk == m1, sub8, GROUP_SIZE), axis=0, keepdims=True)
        m2 = jnp.max(jnp.where(sub8 == i1, neg, blk), axis=0, keepdims=True)
        gscore.append(m1 + m2)
    kept = []
    for g in range(N_EXPERT_GROUPS):
        beat = jnp.zeros((1, tm), I32)
        for o in range(N_EXPERT_GROUPS):
            if o < g:
                beat = beat + (gscore[o] >= gscore[g]).astype(I32)
            elif o > g:
                beat = beat + (gscore[o] > gscore[g]).astype(I32)
        kept.append(jnp.where(beat < TOPK_GROUPS, biased[g * GROUP_SIZE:(g + 1) * GROUP_SIZE, :], neg))
    cur = jnp.concatenate(kept, axis=0)
    sub = lax.broadcasted_iota(I32, (ne, tm), 0)
    sel_any = jnp.zeros((ne, tm), F32)
    picks = []
    wsum = jnp.zeros((1, tm), F32)
    for k in range(TOP_K):
        m = jnp.max(cur, axis=0, keepdims=True)
        ei = jnp.min(jnp.where(cur == m, sub, ne), axis=0, keepdims=True)
        sel = sub == ei
        w = jnp.sum(jnp.where(sel, scores, 0.0), axis=0, keepdims=True)
        cur = jnp.where(sel, neg, cur)
        sel_any = sel_any + sel.astype(F32)
        wsum = wsum + w
        picks.append((ei, sel, w))
    r = lax.broadcasted_iota(I32, (tm, tm), 0)
    c = lax.broadcasted_iota(I32, (tm, tm), 1)
    upper = (r <= c).astype(BF16)
    cum = jnp.dot(sel_any.astype(BF16), upper, preferred_element_type=F32)
    carry = carry_ref[:, 0:1]
    excl = cum - sel_any + carry
    total = carry + jnp.sum(sel_any, axis=1, keepdims=True)
    carry_ref[...] = jnp.broadcast_to(total, carry_ref.shape)
    cnt_ref[...] = jnp.broadcast_to(total, cnt_ref.shape)
    denom = wsum + 1e-20
    eidx_ref[...] = jnp.zeros(eidx_ref.shape, I32)
    rank_ref[...] = jnp.zeros(rank_ref.shape, I32)
    wts_ref[...] = jnp.zeros(wts_ref.shape, F32)
    for k, (ei, sel, w) in enumerate(picks):
        eidx_ref[k:k + 1, :] = ei
        rank_ref[k:k + 1, :] = jnp.sum(jnp.where(sel, excl, 0.0), axis=0, keepdims=True).astype(I32)
        wts_ref[k:k + 1, :] = w / denom * ROUTED_SCALE


def _route(logits_t, bias):
    ne, t = logits_t.shape
    tm = TM_ROUTE
    return pl.pallas_call(
        _route_kernel,
        grid=(t // tm,),
        in_specs=[pl.BlockSpec((ne, tm), lambda i: (0, i)), pl.BlockSpec((ne, 1), lambda i: (0, 0))],
        out_specs=[pl.BlockSpec((TOP_K_PAD, tm), lambda i: (0, i)),
                   pl.BlockSpec((TOP_K_PAD, tm), lambda i: (0, i)),
                   pl.BlockSpec((TOP_K_PAD, tm), lambda i: (0, i)),
                   pl.BlockSpec((ne, LANES), lambda i: (0, 0))],
        out_shape=[jax.ShapeDtypeStruct((TOP_K_PAD, t), I32),
                   jax.ShapeDtypeStruct((TOP_K_PAD, t), I32),
                   jax.ShapeDtypeStruct((TOP_K_PAD, t), F32),
                   jax.ShapeDtypeStruct((ne, LANES), F32)],
        scratch_shapes=[pltpu.VMEM((ne, LANES), F32)],
        compiler_params=_cparams(("arbitrary",)),
        name="route",
    )(logits_t, bias)


def _wait_rows(src_rows, dst_rows, sem, n_rows):
    def body(_, c):
        pltpu.make_async_copy(src_rows, dst_rows, sem).wait()
        return c
    lax.fori_loop(0, n_rows // WAIT_ROWS, body, 0)


def _dispatch_kernel(zfill_ref, slots_ref, hp_ref, xs_hbm, zero_vmem, sem, zsem):
    i = pl.program_id(0)
    te = zero_vmem.shape[0]
    tc = hp_ref.shape[0]
    n_tiles = xs_hbm.shape[0] // te

    def zero_copy(tile):
        return pltpu.make_async_copy(zero_vmem, xs_hbm.at[pl.ds(pl.multiple_of(tile * te, te), te)], zsem)

    @pl.when(i == 0)
    def _():
        zero_vmem[...] = jnp.zeros(zero_vmem.shape, U32)

        def zstart(tile, c):
            @pl.when(zfill_ref[tile] != 0)
            def _():
                zero_copy(tile).start()
            return c

        def zwait(tile, c):
            @pl.when(zfill_ref[tile] != 0)
            def _():
                zero_copy(tile).wait()
            return c

        lax.fori_loop(0, n_tiles, zstart, 0)
        lax.fori_loop(0, n_tiles, zwait, 0)

    base = i * (tc * TOP_K_PAD)

    def issue(t, c):
        for k in range(TOP_K):
            slot = slots_ref[base + t * TOP_K_PAD + k]
            pltpu.make_async_copy(hp_ref.at[t], xs_hbm.at[slot], sem).start(priority=k % 2)
        return c

    lax.fori_loop(0, tc, issue, 0, unroll=ISSUE_UNROLL)
    _wait_rows(hp_ref.at[pl.ds(0, WAIT_ROWS)], xs_hbm.at[pl.ds(0, WAIT_ROWS)], sem, tc * TOP_K)


def _dispatch(zfill, slots_flat, hp, n_rows):
    t, planes, lanes = hp.shape
    tc = TC_DISP
    return pl.pallas_call(
        _dispatch_kernel,
        grid_spec=pltpu.PrefetchScalarGridSpec(
            num_scalar_prefetch=2,
            grid=(t // tc,),
            in_specs=[pl.BlockSpec((tc, planes, lanes), lambda i, z, s: (i, 0, 0))],
            out_specs=pl.BlockSpec(memory_space=pl.ANY),
            scratch_shapes=[pltpu.VMEM((TE_ROWS, planes, lanes), U32),
                            pltpu.SemaphoreType.DMA, pltpu.SemaphoreType.DMA],
        ),
        out_shape=jax.ShapeDtypeStruct((n_rows, planes, lanes), U32),
        compiler_params=_cparams(("arbitrary",)),
        name="dispatch",
    )(zfill, slots_flat, hp)


def _experts_kernel(te_ref, nv_ref, nx_ref, xs_ref, wg_hbm, wu_hbm, wd_hbm, y_ref,
                    sg, su, sd, wgb, wub, wdb, sem):
    i = pl.program_id(0)
    e = te_ref[i]
    prev = te_ref[jnp.maximum(i - 1, 0)]
    valid = i < nv_ref[0]

    def fetch(ex):
        return (pltpu.make_async_copy(wg_hbm.at[ex], sg, sem.at[0]),
                pltpu.make_async_copy(wu_hbm.at[ex], su, sem.at[1]),
                pltpu.make_async_copy(wd_hbm.at[ex], sd, sem.at[2]))

    @pl.when(i == 0)
    def _():
        for cp in fetch(e):
            cp.start()

    @pl.when(valid & ((i == 0) | (e != prev)))
    def _():
        for cp in fetch(e):
            cp.wait()
        wgb[...] = sg[...].astype(BF16)
        wub[...] = su[...].astype(BF16)
        wdb[...] = sd[...].astype(BF16)

        @pl.when(nx_ref[i] >= 0)
        def _():
            for cp in fetch(nx_ref[i]):
                cp.start()

    @pl.when(valid)
    def _():
        xs = _unpack_bf16_pairs(_load_row_planes(xs_ref, xs_ref.shape[0] // TE_ROWS))
        g = jnp.dot(xs, wgb[...], preferred_element_type=F32)
        u = jnp.dot(xs, wub[...], preferred_element_type=F32)
        hid = (g * _sigmoid(g) * u).astype(BF16)
        y = jnp.dot(hid, wdb[...], preferred_element_type=F32)
        _store_row_planes(y_ref, _pack_bf16_pairs(y))

    @pl.when(jnp.logical_not(valid))
    def _():
        y_ref[...] = jnp.zeros(y_ref.shape, U32)


def _experts(tile_expert, n_valid, next_expert, xs, w_gate, w_up, w_down):
    n_rows, xp, _ = xs.shape
    ne, d, ff = w_gate.shape
    te = TE_ROWS
    in_map = lambda i, te_ref, nv_ref, nx_ref: (jnp.minimum(i, nv_ref[0] - 1), 0)
    out_map = lambda i, te_ref, nv_ref, nx_ref: (i, 0)
    hbm = pl.BlockSpec(memory_space=pl.ANY)
    y = pl.pallas_call(
        _experts_kernel,
        grid_spec=pltpu.PrefetchScalarGridSpec(
            num_scalar_prefetch=3,
            grid=(n_rows // te,),
            in_specs=[pl.BlockSpec((te * xp, LANES), in_map), hbm, hbm, hbm],
            out_specs=pl.BlockSpec((te * xp, LANES), out_map),
            scratch_shapes=[pltpu.VMEM((d, ff), F32), pltpu.VMEM((d, ff), F32), pltpu.VMEM((ff, d), F32),
                            pltpu.VMEM((d, ff), BF16), pltpu.VMEM((d, ff), BF16), pltpu.VMEM((ff, d), BF16),
                            pltpu.SemaphoreType.DMA((3,))],
        ),
        out_shape=jax.ShapeDtypeStruct((n_rows * xp, LANES), U32),
        compiler_params=_cparams(("arbitrary",)),
        name="experts",
    )(tile_expert, n_valid, next_expert, xs.reshape(n_rows * xp, LANES), w_gate, w_up, w_down)
    return y.reshape(n_rows, xp, LANES)


def _combine_kernel(slots_ref, y_hbm, h_ref, wts_ref, wsg_ref, wsu_ref, wsd_ref, g_ref, b_ref,
                    o_ref, gbuf, rbuf, shbuf, sem, *, alpha):
    tm, d = o_ref.shape
    planes = gbuf.shape[3]
    i = pl.program_id(0)
    cur = i % 2
    grp = SUBLANES

    def issue_rows(step, t0, n):
        buf = step % 2
        base = step * (tm * TOP_K_PAD)
        for u in range(n):
            for k in range(TOP_K):
                slot = slots_ref[base + (t0 + u) * TOP_K_PAD + k]
                pltpu.make_async_copy(y_hbm.at[slot], gbuf.at[buf, k, t0 + u], sem.at[buf]).start(priority=k % 2)

    @pl.when(i == 0)
    def _():
        def first(t, c):
            issue_rows(0, t, 1)
            return c
        lax.fori_loop(0, tm, first, 0, unroll=ISSUE_UNROLL)

    hb = h_ref[...].astype(BF16)
    sg = jnp.dot(hb, wsg_ref[...], preferred_element_type=F32)
    su = jnp.dot(hb, wsu_ref[...], preferred_element_type=F32)
    shbuf[...] = jnp.dot((sg * _sigmoid(sg) * su).astype(BF16), wsd_ref[...], preferred_element_type=F32)
    _wait_rows(y_hbm.at[pl.ds(0, WAIT_ROWS)], gbuf.at[cur, 0, pl.ds(0, WAIT_ROWS)], sem.at[cur], tm * TOP_K)

    def group(g, prefetch):
        t0 = pl.multiple_of(g * grp, grp)
        packed = [gbuf[cur, k, pl.ds(t0, grp)] for k in range(TOP_K)]
        w = wts_ref[pl.ds(t0, grp)]
        resid = alpha * h_ref[pl.ds(t0, grp), :] + shbuf[pl.ds(t0, grp), :]
        if prefetch:
            issue_rows(i + 1, t0, grp)
        r_lo = r_hi = None
        for k in range(TOP_K):
            lo = w[:, k:k + 1, :] * pltpu.bitcast(packed[k] << 16, F32)
            hi = w[:, k:k + 1, :] * pltpu.bitcast(packed[k] & jnp.uint32(0xFFFF0000), F32)
            r_lo = lo if r_lo is None else r_lo + lo
            r_hi = hi if r_hi is None else r_hi + hi
        p0 = pl.multiple_of(t0 * planes, grp * planes)
        rbuf[0, pl.ds(p0, grp * planes), :] = r_lo.reshape(grp * planes, LANES)
        rbuf[1, pl.ds(p0, grp * planes), :] = r_hi.reshape(grp * planes, LANES)
        routed = jnp.concatenate([rbuf[half, pl.ds(p0 + j, grp, stride=planes), :]
                                  for half in range(2) for j in range(planes)], axis=1)
        o_ref[pl.ds(t0, grp), :] = _layer_norm(resid + routed, g_ref[...], b_ref[...])

    has_next = i + 1 < pl.num_programs(0)

    @pl.when(has_next)
    def _():
        lax.fori_loop(0, tm // grp, lambda g, c: (group(g, True), c)[1], 0, unroll=GROUP_UNROLL)

    @pl.when(jnp.logical_not(has_next))
    def _():
        lax.fori_loop(0, tm // grp, lambda g, c: (group(g, False), c)[1], 0, unroll=GROUP_UNROLL)


def _combine(slots_flat, y, h, wts_planes, wsg, wsu, wsd, g, b, alpha):
    t, d = h.shape
    tm = TM_COMB
    planes = y.shape[1]
    full = lambda a: pl.BlockSpec(a.shape, lambda i, s: (0,) * a.ndim)
    once = lambda a: pl.BlockSpec(a.shape, lambda i, s: (0,) * a.ndim, pipeline_mode=pl.Buffered(1))
    return pl.pallas_call(
        functools.partial(_combine_kernel, alpha=alpha),
        grid_spec=pltpu.PrefetchScalarGridSpec(
            num_scalar_prefetch=1,
            grid=(t // tm,),
            in_specs=[pl.BlockSpec(memory_space=pl.ANY),
                      pl.BlockSpec((tm, d), lambda i, s: (i, 0)),
                      pl.BlockSpec((tm, TOP_K_PAD, LANES), lambda i, s: (i, 0, 0)),
                      once(wsg), once(wsu), once(wsd), full(g), full(b)],
            out_specs=pl.BlockSpec((tm, d), lambda i, s: (i, 0)),
            scratch_shapes=[pltpu.VMEM((2, TOP_K, tm, planes, LANES), U32),
                            pltpu.VMEM((2, tm * planes, LANES), F32), pltpu.VMEM((tm, d), F32),
                            pltpu.SemaphoreType.DMA((2,))],
        ),
        out_shape=jax.ShapeDtypeStruct((t, d), F32),
        compiler_params=_cparams(("arbitrary",)),
        name="combine",
    )(slots_flat, y, h, wts_planes, wsg, wsu, wsd, g, b)


def _prep_attention_weights(w_in, w_uq, w_ukv):
    d = w_in.shape[0]
    ql = w_uq.shape[0]
    kvl = w_ukv.shape[0]
    half = QK_ROPE_DIM // 2
    pad = LANES - QK_ROPE_DIM
    w_kr = w_in[:, ql + kvl:ql + kvl + QK_ROPE_DIM]
    w_kr_rot = jnp.concatenate([-w_kr[:, half:], w_kr[:, :half]], axis=1)
    zpad = jnp.zeros((d, pad), w_in.dtype)
    wa = jnp.concatenate([w_in[:, :ql + kvl], w_kr, zpad, w_kr_rot, zpad], axis=1).astype(BF16)
    wc = w_in[:, ql + kvl + QK_ROPE_DIM:].astype(BF16)
    uq = w_uq.reshape(ql, N_HEADS, QK_NOPE_DIM + QK_ROPE_DIM)
    uq_nope, uq_rope = uq[..., :QK_NOPE_DIM], uq[..., QK_NOPE_DIM:]
    zq = jnp.zeros((ql, N_HEADS, pad), w_uq.dtype)
    wuq = jnp.concatenate([uq_nope, uq_rope, zq], axis=-1).reshape(ql, N_HEADS * HEAD_QK_PAD).astype(BF16)
    uq_rot = jnp.concatenate([-uq_rope[..., half:], uq_rope[..., :half], zq], axis=-1)
    wuqr = uq_rot.reshape(ql, N_HEADS * LANES).astype(BF16)
    ukv = w_ukv.reshape(kvl, N_HEADS, QK_NOPE_DIM + V_HEAD_DIM)
    wukv = jnp.concatenate([ukv[..., :QK_NOPE_DIM].reshape(kvl, -1), ukv[..., QK_NOPE_DIM:].reshape(kvl, -1)],
                           axis=1).astype(BF16)
    return wa, wc, wuq, wuqr, wukv


def _layer(h, pos2, invf4, batch, seq, alpha, w_in, q_norm_g, w_uq, kv_norm_g, w_ukv, conv_w, conv_b,
           conv_ln_g, conv_ln_b, w_o, ln1_g, ln1_b, w_router, router_bias, w_gate, w_up, w_down,
           ws_gate, ws_up, ws_down, ln2_g, ln2_b):
    t, d = h.shape
    row = lambda a: a.reshape(1, -1)
    wa, wc, wuq, wuqr, wukv = _prep_attention_weights(w_in, w_uq, w_ukv)
    q, k, v = _qkv(h, pos2, invf4, wa, row(q_norm_g), row(kv_norm_g), wuq, wuqr, wukv)
    conv = _conv(h, wc, conv_w, row(conv_b), row(conv_ln_g), row(conv_ln_b), seq)
    attn = _attn(q, k, v, batch, seq)
    aw = attn.shape[1]
    wr32 = jnp.pad(w_router.astype(F32), ((0, 0), (0, LANES - N_EXPERTS)))
    wr_hi = wr32.astype(BF16)
    wr_lo = (wr32 - wr_hi.astype(F32)).astype(BF16)
    wr = jnp.concatenate([wr_hi, wr_hi, wr_lo], axis=0)
    h1, h1p, logits_t = _oproj(attn, conv, h, w_o[:aw].astype(BF16), w_o[aw:].astype(BF16),
                               row(ln1_g), row(ln1_b), wr, alpha)
    return _moe_ffn(h1, h1p, logits_t, alpha, router_bias, w_gate, w_up, w_down,
                    ws_gate, ws_up, ws_down, ln2_g, ln2_b)


def _moe_ffn(h1, h1p, logits_t, alpha, router_bias, w_gate, w_up, w_down, ws_gate, ws_up, ws_down, ln2_g, ln2_b):
    t, d = h1.shape
    row = lambda a: a.reshape(1, -1)
    eidx, rank, wts, cnt = _route(logits_t, router_bias.astype(F32).reshape(N_EXPERTS, 1))
    counts = cnt[:, 0].astype(I32)
    padded = (counts + TE_ROWS - 1) // TE_ROWS * TE_ROWS
    pend = jnp.cumsum(padded)
    poff = pend - padded
    n_rows = t * TOP_K + N_EXPERTS * TE_ROWS
    n_tiles = n_rows // TE_ROWS
    n_valid = (pend[-1] // TE_ROWS).astype(I32).reshape(1)
    tile_row = jnp.minimum(jnp.arange(n_tiles, dtype=I32), n_valid[0] - 1) * TE_ROWS
    tile_expert = jnp.sum((pend[None, :] <= tile_row[:, None]).astype(I32), axis=1)
    tile_expert = jnp.minimum(tile_expert, N_EXPERTS - 1)
    tiles = jnp.arange(n_tiles, dtype=I32)
    last_tile = jnp.where(padded > 0, pend // TE_ROWS - 1, -1)
    zfill = ((tiles >= n_valid[0]) | jnp.any(tiles[:, None] == last_tile[None, :], axis=1)).astype(I32)
    ids = jnp.arange(N_EXPERTS, dtype=I32)
    later = (ids[None, :] > ids[:, None]) & (padded[None, :] > 0)
    next_of = jnp.min(jnp.where(later, ids[None, :], N_EXPERTS), axis=1)
    next_of = jnp.where(next_of == N_EXPERTS, -1, next_of)
    next_expert = jnp.sum(jnp.where(tile_expert[:, None] == ids[None, :], next_of[None, :], 0), axis=1).astype(I32)
    slot_base = jnp.sum(jnp.where(eidx[None] == ids[:, None, None], poff[:, None, None], 0), axis=0)
    slots_flat = (slot_base + rank).T.reshape(-1)
    xs = _dispatch(zfill, slots_flat, h1p.reshape(t, -1, LANES), n_rows)
    y = _experts(tile_expert, n_valid, next_expert, xs, w_gate, w_up, w_down)
    wts_planes = jnp.broadcast_to(wts.T[:, :, None], (t, TOP_K_PAD, LANES))
    return _combine(slots_flat, y, h1, wts_planes, ws_gate.astype(BF16), ws_up.astype(BF16),
                    ws_down.astype(BF16), row(ln2_g), row(ln2_b), alpha)


def kernel(x, positions, w_in, q_norm_g, w_uq, kv_norm_g, w_ukv, conv_w, conv_b, conv_ln_g, conv_ln_b, w_o, ln1_g, ln1_b, w_router, router_bias, w_gate, w_up, w_down, ws_gate, ws_up, ws_down, ln2_g, ln2_b):
    batch, seq, d = x.shape
    depth = w_in.shape[0]
    alpha = (2.0 * depth) ** 0.25
    inv_freq = ROPE_BASE ** (-jnp.arange(0, QK_ROPE_DIM, 2, dtype=F32) / QK_ROPE_DIM)
    invf4 = jnp.tile(inv_freq, LANES // inv_freq.shape[0]).reshape(1, LANES)
    pos2 = positions.reshape(batch * seq, 1)
    h = x.reshape(batch * seq, d)
    for l in range(depth):
        h = _layer(h, pos2, invf4, batch, seq, alpha, w_in[l], q_norm_g[l], w_uq[l], kv_norm_g[l], w_ukv[l],
                   conv_w[l], conv_b[l], conv_ln_g[l], conv_ln_b[l], w_o[l], ln1_g[l], ln1_b[l],
                   w_router[l], router_bias[l], w_gate[l], w_up[l], w_down[l],
                   ws_gate[l], ws_up[l], ws_down[l], ln2_g[l], ln2_b[l])
    return h.reshape(batch, seq, d)
```

```python
import functools

import jax
import jax.numpy as jnp
from jax import lax
from jax.experimental import pallas as pl
from jax.experimental.pallas import tpu as pltpu

F32 = jnp.float32
BF16 = jnp.bfloat16
I32 = jnp.int32
U32 = jnp.uint32

N_HEADS = 8
QK_NOPE_DIM = 128
QK_ROPE_DIM = 64
V_HEAD_DIM = 128
HEAD_QK_PAD = 256
CONV_KERNEL = 31
N_EXPERTS = 64
TOP_K = 6
TOP_K_PAD = 8
N_EXPERT_GROUPS = 8
GROUP_SIZE = N_EXPERTS // N_EXPERT_GROUPS
TOPK_GROUPS = 4
ROUTED_SCALE = 2.5
ROPE_BASE = 10000.0
LN_EPS = 1e-5
RMS_EPS = 1e-6

LANES = 128
SUBLANES = 8
CONV_HALO = 32

TM_PROJ = 256
TM_CONV = 512
TQ_ATTN = 512
TK_ATTN = 512
HEADS_PER_ATTN_STEP = 4
TM_ROUTE = 512
TE_ROWS = 256
TM_COMB = 256
TC_DISP = 1024
WAIT_ROWS = 128
ISSUE_UNROLL = 4
GROUP_UNROLL = 8
CONV_ROWS = 64
CONV_COLS = 256
CONV_SHIFT_BUFS = 4
VMEM_LIMIT = 56 * 1024 * 1024


def _cparams(sem):
    return pltpu.CompilerParams(dimension_semantics=sem, vmem_limit_bytes=VMEM_LIMIT)


def _sigmoid(v):
    return 1.0 / (1.0 + jnp.exp(-v))


def _layer_norm(v, g, b):
    mu = jnp.mean(v, axis=-1, keepdims=True)
    d = v - mu
    var = jnp.mean(d * d, axis=-1, keepdims=True)
    return d * lax.rsqrt(var + LN_EPS) * g + b


def _rms_norm(v, g):
    ms = jnp.mean(v * v, axis=-1, keepdims=True)
    return v * lax.rsqrt(ms + RMS_EPS) * g


def _qkv_kernel(x_ref, pos_ref, invf_ref, wa_ref, qg_ref, kvg_ref, wuq_ref, wuqr_ref, wukv_ref,
                qt_ref, k_ref, vt_ref):
    ql = qg_ref.shape[1]
    kvl = kvg_ref.shape[1]
    xb = x_ref[...].astype(BF16)
    lat = jnp.dot(xb, wa_ref[...], preferred_element_type=F32)
    ang = pos_ref[...].astype(F32) * invf_ref[...]
    cos = jnp.cos(ang)
    sin = jnp.sin(ang)
    cq = _rms_norm(lat[:, :ql], qg_ref[...]).astype(BF16)
    ckv = _rms_norm(lat[:, ql:ql + kvl], kvg_ref[...]).astype(BF16)
    kr = lat[:, ql + kvl:ql + kvl + LANES] * cos + lat[:, ql + kvl + LANES:ql + kvl + 2 * LANES] * sin
    kr = kr.astype(BF16)
    q = jnp.dot(cq, wuq_ref[...], preferred_element_type=F32)
    qrot = jnp.dot(cq, wuqr_ref[...], preferred_element_type=F32)
    kv = jnp.dot(ckv, wukv_ref[...], preferred_element_type=F32)
    for h in range(N_HEADS):
        c0 = h * HEAD_QK_PAD
        qt_ref[c0:c0 + LANES, :] = q[:, c0:c0 + LANES].T.astype(BF16)
        qt_ref[c0 + LANES:c0 + 2 * LANES, :] = (
            q[:, c0 + LANES:c0 + 2 * LANES] * cos + qrot[:, h * LANES:(h + 1) * LANES] * sin).T.astype(BF16)
        k_ref[:, c0:c0 + LANES] = kv[:, h * LANES:(h + 1) * LANES].astype(BF16)
        k_ref[:, c0 + LANES:c0 + 2 * LANES] = kr
    vt_ref[...] = kv[:, N_HEADS * QK_NOPE_DIM:].T.astype(BF16)


def _qkv(x2, pos2, invf4, wa, qg, kvg, wuq, wuqr, wukv):
    t, d = x2.shape
    tm = TM_PROJ
    full = lambda a: pl.BlockSpec(a.shape, lambda i: (0,) * a.ndim)
    return pl.pallas_call(
        _qkv_kernel,
        grid=(t // tm,),
        in_specs=[pl.BlockSpec((tm, d), lambda i: (i, 0)),
                  pl.BlockSpec((tm, 1), lambda i: (i, 0)),
                  full(invf4), full(wa), full(qg), full(kvg), full(wuq), full(wuqr), full(wukv)],
        out_specs=[pl.BlockSpec((N_HEADS * HEAD_QK_PAD, tm), lambda i: (0, i)),
                   pl.BlockSpec((tm, N_HEADS * HEAD_QK_PAD), lambda i: (i, 0)),
                   pl.BlockSpec((N_HEADS * V_HEAD_DIM, tm), lambda i: (0, i))],
        out_shape=[jax.ShapeDtypeStruct((N_HEADS * HEAD_QK_PAD, t), BF16),
                   jax.ShapeDtypeStruct((t, N_HEADS * HEAD_QK_PAD), BF16),
                   jax.ShapeDtypeStruct((N_HEADS * V_HEAD_DIM, t), BF16)],
        compiler_params=_cparams(("arbitrary",)),
        name="qkv",
    )(x2, pos2, invf4, wa, qg, kvg, wuq, wuqr, wukv)


def _conv_kernel(x_ref, wc_ref, cw_ref, cb_ref, g_ref, b_ref, o_ref, ubuf, ybuf, shbuf, *, tiles_per_seq):
    tm, cw = o_ref.shape
    i = pl.program_id(0)
    xb = x_ref[...].astype(BF16)

    @pl.when(i % tiles_per_seq == 0)
    def _():
        ubuf[0:CONV_HALO, :] = jnp.zeros((CONV_HALO, cw), F32)

    shift0 = CONV_HALO - (CONV_KERNEL - 1)
    n_sh = shbuf.shape[0]
    chunk_no = 0
    for c0 in range(0, cw, CONV_COLS):
        a = jnp.dot(xb, wc_ref[:, c0:c0 + CONV_COLS], preferred_element_type=F32)
        gate = jnp.dot(xb, wc_ref[:, cw + c0:cw + c0 + CONV_COLS], preferred_element_type=F32)
        ubuf[CONV_HALO:CONV_HALO + tm, c0:c0 + CONV_COLS] = a * _sigmoid(gate)
        for r0 in range(0, tm, CONV_ROWS):
            sh = shbuf.at[chunk_no % n_sh]
            chunk_no += 1
            win = ubuf[r0:r0 + CONV_ROWS + CONV_HALO, c0:c0 + CONV_COLS]
            for r in range(1, SUBLANES):
                sh[r] = win[r:r + CONV_ROWS + CONV_HALO - SUBLANES, :]
            acc = jnp.zeros((CONV_ROWS, CONV_COLS), F32)
            for k in range(CONV_KERNEL):
                wk = cw_ref[k:k + 1, c0:c0 + CONV_COLS]
                r, j = (shift0 + k) % SUBLANES, (shift0 + k) // SUBLANES
                if r == 0:
                    tap = win[j * SUBLANES:j * SUBLANES + CONV_ROWS, :]
                else:
                    tap = sh[r, j * SUBLANES:j * SUBLANES + CONV_ROWS, :]
                acc = acc + wk * tap
            ybuf[r0:r0 + CONV_ROWS, c0:c0 + CONV_COLS] = acc
    ubuf[0:CONV_HALO, :] = ubuf[tm:tm + CONV_HALO, :]
    y = _layer_norm(ybuf[...] + cb_ref[...], g_ref[...], b_ref[...])
    o_ref[...] = (y * _sigmoid(y)).astype(BF16)


def _conv(x2, wc, cw, cb, g, b, seq):
    t, d = x2.shape
    tm = TM_CONV
    c = cw.shape[1]
    full = lambda a: pl.BlockSpec(a.shape, lambda i: (0,) * a.ndim)
    return pl.pallas_call(
        functools.partial(_conv_kernel, tiles_per_seq=seq // tm),
        grid=(t // tm,),
        in_specs=[pl.BlockSpec((tm, d), lambda i: (i, 0)), full(wc), full(cw), full(cb), full(g), full(b)],
        out_specs=pl.BlockSpec((tm, c), lambda i: (i, 0)),
        out_shape=jax.ShapeDtypeStruct((t, c), BF16),
        scratch_shapes=[pltpu.VMEM((tm + CONV_HALO, c), F32), pltpu.VMEM((tm, c), F32),
                        pltpu.VMEM((CONV_SHIFT_BUFS, SUBLANES, CONV_ROWS + CONV_HALO - SUBLANES, CONV_COLS), F32)],
        compiler_params=_cparams(("arbitrary",)),
        name="conv",
    )(x2, wc, cw, cb, g, b)


def _attn_kernel(qt_ref, k_ref, vt_ref, o_ref, *, scale, tk):
    tq = qt_ref.shape[1]
    heads = qt_ref.shape[0] // HEAD_QK_PAD
    i = pl.program_id(2)
    c = scale * 1.4426950408889634
    n_full = (i * tq) // tk
    q_off = i * tq - n_full * tk

    def block(j, carry, masked):
        r0 = pl.multiple_of(j * tk, tk)
        out = []
        scores = []
        for h in range(heads):
            qt = qt_ref[h * HEAD_QK_PAD:(h + 1) * HEAD_QK_PAD, :]
            kb = k_ref[pl.ds(r0, tk), h * HEAD_QK_PAD:(h + 1) * HEAD_QK_PAD]
            scores.append(jnp.dot(kb, qt, preferred_element_type=F32))
        for h, (m, l, acc) in enumerate(carry):
            vtb = vt_ref[h * V_HEAD_DIM:(h + 1) * V_HEAD_DIM, pl.ds(r0, tk)]
            st = scores[h]
            if masked:
                key = lax.broadcasted_iota(I32, (tk, tq), 0)
                qry = lax.broadcasted_iota(I32, (tk, tq), 1) + q_off
                st = jnp.where(key <= qry, st, -jnp.inf)
            m_blk = jnp.max(jnp.max(st.reshape(SUBLANES, tk // SUBLANES, tq), axis=0), axis=0, keepdims=True)
            m_new = jnp.maximum(m, m_blk)
            alpha = jnp.exp2((m - m_new) * c)
            pt = jnp.exp2((st - m_new) * c)
            l_blk = jnp.sum(jnp.sum(pt.reshape(SUBLANES, tk // SUBLANES, tq), axis=0), axis=0, keepdims=True)
            l = alpha * l + l_blk
            acc = alpha * acc + jnp.dot(vtb, pt.astype(BF16), preferred_element_type=F32)
            out.append((m_new, l, acc))
        return tuple(out)

    init = tuple((jnp.full((1, tq), -jnp.inf, F32), jnp.zeros((1, tq), F32),
                  jnp.zeros((V_HEAD_DIM, tq), F32)) for _ in range(heads))
    carry = lax.fori_loop(0, n_full, lambda j, cr: block(j, cr, False), init)
    final = block(n_full, carry, True)
    for h, (_, l, acc) in enumerate(final):
        o_ref[:, h * V_HEAD_DIM:(h + 1) * V_HEAD_DIM] = (acc / l).T.astype(BF16)


def _attn(q, k, v, batch, seq):
    tq = TQ_ATTN
    nq = seq // tq
    hb = HEADS_PER_ATTN_STEP
    scale = (QK_NOPE_DIM + QK_ROPE_DIM) ** -0.5
    return pl.pallas_call(
        functools.partial(_attn_kernel, scale=scale, tk=TK_ATTN),
        grid=(batch, N_HEADS // hb, nq),
        in_specs=[pl.BlockSpec((hb * HEAD_QK_PAD, tq), lambda b, h, i: (h, b * nq + i)),
                  pl.BlockSpec((seq, hb * HEAD_QK_PAD), lambda b, h, i: (b, h)),
                  pl.BlockSpec((hb * V_HEAD_DIM, seq), lambda b, h, i: (h, b))],
        out_specs=pl.BlockSpec((tq, hb * V_HEAD_DIM), lambda b, h, i: (b * nq + i, h)),
        out_shape=jax.ShapeDtypeStruct((batch * seq, N_HEADS * V_HEAD_DIM), BF16),
        compiler_params=_cparams(("arbitrary", "arbitrary", "arbitrary")),
        name="attn",
    )(q, k, v)


def _pack_bf16_pairs(v):
    c = v.shape[1] // 2
    lo = pltpu.bitcast(v[:, :c].astype(BF16).astype(F32), U32)
    hi = pltpu.bitcast(v[:, c:].astype(BF16).astype(F32), U32)
    return (hi & jnp.uint32(0xFFFF0000)) | (lo >> 16)


def _unpack_bf16_pairs(p):
    lo = pltpu.bitcast(p << 16, F32).astype(BF16)
    hi = pltpu.bitcast(p & jnp.uint32(0xFFFF0000), F32).astype(BF16)
    return jnp.concatenate([lo, hi], axis=1)


def _store_row_planes(ref, v):
    n = v.shape[0]
    p = ref.shape[0] // n
    for j in range(p):
        ref[pl.ds(j, n, stride=p), :] = v[:, j * LANES:(j + 1) * LANES]


def _load_row_planes(ref, p):
    n = ref.shape[0] // p
    return jnp.concatenate([ref[pl.ds(j, n, stride=p), :] for j in range(p)], axis=1)


def _oproj_kernel(a_ref, c_ref, x_ref, woa_ref, woc_ref, g_ref, b_ref, wr_ref,
                  h_ref, hp_ref, lg_ref, *, alpha):
    mix = jnp.dot(a_ref[...], woa_ref[...], preferred_element_type=F32)
    mix = mix + jnp.dot(c_ref[...], woc_ref[...], preferred_element_type=F32)
    h = _layer_norm(alpha * x_ref[...] + mix, g_ref[...], b_ref[...])
    h_ref[...] = h
    _store_row_planes(hp_ref, _pack_bf16_pairs(h))
    h_hi = h.astype(BF16)
    h_lo = (h - h_hi.astype(F32)).astype(BF16)
    logits = jnp.dot(jnp.concatenate([h_hi, h_lo, h_hi], axis=1), wr_ref[...], preferred_element_type=F32)
    lg_ref[...] = logits.T[:N_EXPERTS, :]


def _oproj(attn, conv, x2, woa, woc, g, b, wr, alpha):
    t, d = x2.shape
    tm = TM_PROJ
    full = lambda a: pl.BlockSpec(a.shape, lambda i: (0,) * a.ndim)
    return pl.pallas_call(
        functools.partial(_oproj_kernel, alpha=alpha),
        grid=(t // tm,),
        in_specs=[pl.BlockSpec((tm, attn.shape[1]), lambda i: (i, 0)),
                  pl.BlockSpec((tm, conv.shape[1]), lambda i: (i, 0)),
                  pl.BlockSpec((tm, d), lambda i: (i, 0)),
                  full(woa), full(woc), full(g), full(b), full(wr)],
        out_specs=[pl.BlockSpec((tm, d), lambda i: (i, 0)),
                   pl.BlockSpec((tm * (d // 2 // LANES), LANES), lambda i: (i, 0)),
                   pl.BlockSpec((N_EXPERTS, tm), lambda i: (0, i))],
        out_shape=[jax.ShapeDtypeStruct((t, d), F32),
                   jax.ShapeDtypeStruct((t * (d // 2 // LANES), LANES), U32),
                   jax.ShapeDtypeStruct((N_EXPERTS, t), F32)],
        compiler_params=_cparams(("arbitrary",)),
        name="oproj",
    )(attn, conv, x2, woa, woc, g, b, wr)


def _route_kernel(lg_ref, bias_ref, eidx_ref, rank_ref, wts_ref, cnt_ref, carry_ref):
    ne, tm = lg_ref.shape
    i = pl.program_id(0)

    @pl.when(i == 0)
    def _():
        carry_ref[...] = jnp.zeros(carry_ref.shape, F32)

    scores = _sigmoid(lg_ref[...])
    biased = scores + bias_ref[...]
    neg = -jnp.inf
    sub8 = lax.broadcasted_iota(I32, (GROUP_SIZE, tm), 0)
    gscore = []
    for g in range(N_EXPERT_GROUPS):
        blk = biased[g * GROUP_SIZE:(g + 1) * GROUP_SIZE, :]
        m1 = jnp.max(blk, axis=0, keepdims=True)
        i1 = jnp.min(jnp.where(blk == m1, sub8, GROUP_SIZE), axis=0, keepdims=True)
        m2 = jnp.max(jnp.where(sub8 == i1, neg, blk), axis=0, keepdims=True)
        gscore.append(m1 + m2)
    kept = []
    for g in range(N_EXPERT_GROUPS):
        beat = jnp.zeros((1, tm), I32)
        for o in range(N_EXPERT_GROUPS):
            if o < g:
                beat = beat + (gscore[o] >= gscore[g]).astype(I32)
            elif o > g:
                beat = beat + (gscore[o] > gscore[g]).astype(I32)
        kept.append(jnp.where(beat < TOPK_GROUPS, biased[g * GROUP_SIZE:(g + 1) * GROUP_SIZE, :], neg))
    cur = jnp.concatenate(kept, axis=0)
    sub = lax.broadcasted_iota(I32, (ne, tm), 0)
    sel_any = jnp.zeros((ne, tm), F32)
    picks = []
    wsum = jnp.zeros((1, tm), F32)
    for k in range(TOP_K):
        m = jnp.max(cur, axis=0, keepdims=True)
        ei = jnp.min(jnp.where(cur == m, sub, ne), axis=0, keepdims=True)
        sel = sub == ei
        w = jnp.sum(jnp.where(sel, scores, 0.0), axis=0, keepdims=True)
        cur = jnp.where(sel, neg, cur)
        sel_any = sel_any + sel.astype(F32)
        wsum = wsum + w
        picks.append((ei, sel, w))
    r = lax.broadcasted_iota(I32, (tm, tm), 0)
    c = lax.broadcasted_iota(I32, (tm, tm), 1)
    upper = (r <= c).astype(BF16)
    cum = jnp.dot(sel_any.astype(BF16), upper, preferred_element_type=F32)
    carry = carry_ref[:, 0:1]
    excl = cum - sel_any + carry
    total = carry + jnp.sum(sel_any, axis=1, keepdims=True)
    carry_ref[...] = jnp.broadcast_to(total, carry_ref.shape)
    cnt_ref[...] = jnp.broadcast_to(total, cnt_ref.shape)
    denom = wsum + 1e-20
    eidx_ref[...] = jnp.zeros(eidx_ref.shape, I32)
    rank_ref[...] = jnp.zeros(rank_ref.shape, I32)
    wts_ref[...] = jnp.zeros(wts_ref.shape, F32)
    for k, (ei, sel, w) in enumerate(picks):
        eidx_ref[k:k + 1, :] = ei
        rank_ref[k:k + 1, :] = jnp.sum(jnp.where(sel, excl, 0.0), axis=0, keepdims=True).astype(I32)
        wts_ref[k:k + 1, :] = w / denom * ROUTED_SCALE


def _route(logits_t, bias):
    ne, t = logits_t.shape
    tm = TM_ROUTE
    return pl.pallas_call(
        _route_kernel,
        grid=(t // tm,),
        in_specs=[pl.BlockSpec((ne, tm), lambda i: (0, i)), pl.BlockSpec((ne, 1), lambda i: (0, 0))],
        out_specs=[pl.BlockSpec((TOP_K_PAD, tm), lambda i: (0, i)),
                   pl.BlockSpec((TOP_K_PAD, tm), lambda i: (0, i)),
                   pl.BlockSpec((TOP_K_PAD, tm), lambda i: (0, i)),
                   pl.BlockSpec((ne, LANES), lambda i: (0, 0))],
        out_shape=[jax.ShapeDtypeStruct((TOP_K_PAD, t), I32),
                   jax.ShapeDtypeStruct((TOP_K_PAD, t), I32),
                   jax.ShapeDtypeStruct((TOP_K_PAD, t), F32),
                   jax.ShapeDtypeStruct((ne, LANES), F32)],
        scratch_shapes=[pltpu.VMEM((ne, LANES), F32)],
        compiler_params=_cparams(("arbitrary",)),
        name="route",
    )(logits_t, bias)


def _wait_rows(src_rows, dst_rows, sem, n_rows):
    def body(_, c):
        pltpu.make_async_copy(src_rows, dst_rows, sem).wait()
        return c
    lax.fori_loop(0, n_rows // WAIT_ROWS, body, 0)


def _dispatch_kernel(zfill_ref, slots_ref, hp_ref, xs_hbm, zero_vmem, sem, zsem):
    i = pl.program_id(0)
    te = zero_vmem.shape[0]
    tc = hp_ref.shape[0]
    n_tiles = xs_hbm.shape[0] // te

    def zero_copy(tile):
        return pltpu.make_async_copy(zero_vmem, xs_hbm.at[pl.ds(pl.multiple_of(tile * te, te), te)], zsem)

    @pl.when(i == 0)
    def _():
        zero_vmem[...] = jnp.zeros(zero_vmem.shape, U32)

        def zstart(tile, c):
            @pl.when(zfill_ref[tile] != 0)
            def _():
                zero_copy(tile).start()
            return c

        def zwait(tile, c):
            @pl.when(zfill_ref[tile] != 0)
            def _():
                zero_copy(tile).wait()
            return c

        lax.fori_loop(0, n_tiles, zstart, 0)
        lax.fori_loop(0, n_tiles, zwait, 0)

    base = i * (tc * TOP_K_PAD)

    def issue(t, c):
        for k in range(TOP_K):
            slot = slots_ref[base + t * TOP_K_PAD + k]
            pltpu.make_async_copy(hp_ref.at[t], xs_hbm.at[slot], sem).start(priority=k % 2)
        return c

    lax.fori_loop(0, tc, issue, 0, unroll=ISSUE_UNROLL)
    _wait_rows(hp_ref.at[pl.ds(0, WAIT_ROWS)], xs_hbm.at[pl.ds(0, WAIT_ROWS)], sem, tc * TOP_K)


def _dispatch(zfill, slots_flat, hp, n_rows):
    t, planes, lanes = hp.shape
    tc = TC_DISP
    return pl.pallas_call(
        _dispatch_kernel,
        grid_spec=pltpu.PrefetchScalarGridSpec(
            num_scalar_prefetch=2,
            grid=(t // tc,),
            in_specs=[pl.BlockSpec((tc, planes, lanes), lambda i, z, s: (i, 0, 0))],
            out_specs=pl.BlockSpec(memory_space=pl.ANY),
            scratch_shapes=[pltpu.VMEM((TE_ROWS, planes, lanes), U32),
                            pltpu.SemaphoreType.DMA, pltpu.SemaphoreType.DMA],
        ),
        out_shape=jax.ShapeDtypeStruct((n_rows, planes, lanes), U32),
        compiler_params=_cparams(("arbitrary",)),
        name="dispatch",
    )(zfill, slots_flat, hp)


def _experts_kernel(te_ref, nv_ref, nx_ref, xs_ref, wg_hbm, wu_hbm, wd_hbm, y_ref,
                    sg, su, sd, wgb, wub, wdb, sem):
    i = pl.program_id(0)
    e = te_ref[i]
    prev = te_ref[jnp.maximum(i - 1, 0)]
    valid = i < nv_ref[0]

    def fetch(ex):
        return (pltpu.make_async_copy(wg_hbm.at[ex], sg, sem.at[0]),
                pltpu.make_async_copy(wu_hbm.at[ex], su, sem.at[1]),
                pltpu.make_async_copy(wd_hbm.at[ex], sd, sem.at[2]))

    @pl.when(i == 0)
    def _():
        for cp in fetch(e):
            cp.start()

    first = valid & ((i == 0) | (e != prev))

    def ffn():
        xs = _unpack_bf16_pairs(_load_row_planes(xs_ref, xs_ref.shape[0] // TE_ROWS))
        g = jnp.dot(xs, wgb[...], preferred_element_type=F32)
        u = jnp.dot(xs, wub[...], preferred_element_type=F32)
        hid = (g * _sigmoid(g) * u).astype(BF16)
        y = jnp.dot(hid, wdb[...], preferred_element_type=F32)
        _store_row_planes(y_ref, _pack_bf16_pairs(y))

    @pl.when(first)
    def _():
        for cp in fetch(e):
            cp.wait()
        wgb[...] = sg[...].astype(BF16)
        wub[...] = su[...].astype(BF16)
        wdb[...] = sd[...].astype(BF16)
        ffn()

    @pl.when(valid & jnp.logical_not(first))
    def _():
        ffn()

    @pl.when(first & (nx_ref[i] >= 0))
    def _():
        for cp in fetch(nx_ref[i]):
            cp.start()

    @pl.when(jnp.logical_not(valid))
    def _():
        y_ref[...] = jnp.zeros(y_ref.shape, U32)


def _experts(tile_expert, n_valid, next_expert, xs, w_gate, w_up, w_down):
    n_rows, xp, _ = xs.shape
    ne, d, ff = w_gate.shape
    te = TE_ROWS
    in_map = lambda i, te_ref, nv_ref, nx_ref: (jnp.minimum(i, nv_ref[0] - 1), 0)
    out_map = lambda i, te_ref, nv_ref, nx_ref: (i, 0)
    hbm = pl.BlockSpec(memory_space=pl.ANY)
    y = pl.pallas_call(
        _experts_kernel,
        grid_spec=pltpu.PrefetchScalarGridSpec(
            num_scalar_prefetch=3,
            grid=(n_rows // te,),
            in_specs=[pl.BlockSpec((te * xp, LANES), in_map), hbm, hbm, hbm],
            out_specs=pl.BlockSpec((te * xp, LANES), out_map),
            scratch_shapes=[pltpu.VMEM((d, ff), F32), pltpu.VMEM((d, ff), F32), pltpu.VMEM((ff, d), F32),
                            pltpu.VMEM((d, ff), BF16), pltpu.VMEM((d, ff), BF16), pltpu.VMEM((ff, d), BF16),
                            pltpu.SemaphoreType.DMA((3,))],
        ),
        out_shape=jax.ShapeDtypeStruct((n_rows * xp, LANES), U32),
        compiler_params=_cparams(("arbitrary",)),
        name="experts",
    )(tile_expert, n_valid, next_expert, xs.reshape(n_rows * xp, LANES), w_gate, w_up, w_down)
    return y.reshape(n_rows, xp, LANES)


def _combine_kernel(slots_ref, y_hbm, h_ref, wts_ref, wsg_ref, wsu_ref, wsd_ref, g_ref, b_ref,
                    o_ref, gbuf, rbuf, shbuf, sem, *, alpha):
    tm, d = o_ref.shape
    planes = gbuf.shape[3]
    i = pl.program_id(0)
    cur = i % 2
    grp = SUBLANES

    def issue_rows(step, t0, n):
        buf = step % 2
        base = step * (tm * TOP_K_PAD)
        for u in range(n):
            for k in range(TOP_K):
                slot = slots_ref[base + (t0 + u) * TOP_K_PAD + k]
                pltpu.make_async_copy(y_hbm.at[slot], gbuf.at[buf, k, t0 + u], sem.at[buf]).start(priority=k % 2)

    @pl.when(i == 0)
    def _():
        def first(t, c):
            issue_rows(0, t, 1)
            return c
        lax.fori_loop(0, tm, first, 0, unroll=ISSUE_UNROLL)

    hb = h_ref[...].astype(BF16)
    sg = jnp.dot(hb, wsg_ref[...], preferred_element_type=F32)
    su = jnp.dot(hb, wsu_ref[...], preferred_element_type=F32)
    shbuf[...] = jnp.dot((sg * _sigmoid(sg) * su).astype(BF16), wsd_ref[...], preferred_element_type=F32)
    _wait_rows(y_hbm.at[pl.ds(0, WAIT_ROWS)], gbuf.at[cur, 0, pl.ds(0, WAIT_ROWS)], sem.at[cur], tm * TOP_K)

    def group(g, prefetch):
        t0 = pl.multiple_of(g * grp, grp)
        packed = [gbuf[cur, k, pl.ds(t0, grp)] for k in range(TOP_K)]
        w = wts_ref[pl.ds(t0, grp)]
        resid = alpha * h_ref[pl.ds(t0, grp), :] + shbuf[pl.ds(t0, grp), :]
        if prefetch:
            issue_rows(i + 1, t0, grp)
        r_lo = r_hi = None
        for k in range(TOP_K):
            lo = w[:, k:k + 1, :] * pltpu.bitcast(packed[k] << 16, F32)
            hi = w[:, k:k + 1, :] * pltpu.bitcast(packed[k] & jnp.uint32(0xFFFF0000), F32)
            r_lo = lo if r_lo is None else r_lo + lo
            r_hi = hi if r_hi is None else r_hi + hi
        p0 = pl.multiple_of(t0 * planes, grp * planes)
        rbuf[0, pl.ds(p0, grp * planes), :] = r_lo.reshape(grp * planes, LANES)
        rbuf[1, pl.ds(p0, grp * planes), :] = r_hi.reshape(grp * planes, LANES)
        routed = jnp.concatenate([rbuf[half, pl.ds(p0 + j, grp, stride=planes), :]
                                  for half in range(2) for j in range(planes)], axis=1)
        o_ref[pl.ds(t0, grp), :] = _layer_norm(resid + routed, g_ref[...], b_ref[...])

    has_next = i + 1 < pl.num_programs(0)

    @pl.when(has_next)
    def _():
        lax.fori_loop(0, tm // grp, lambda g, c: (group(g, True), c)[1], 0, unroll=GROUP_UNROLL)

    @pl.when(jnp.logical_not(has_next))
    def _():
        lax.fori_loop(0, tm // grp, lambda g, c: (group(g, False), c)[1], 0, unroll=GROUP_UNROLL)


def _combine(slots_flat, y, h, wts_planes, wsg, wsu, wsd, g, b, alpha):
    t, d = h.shape
    tm = TM_COMB
    planes = y.shape[1]
    full = lambda a: pl.BlockSpec(a.shape, lambda i, s: (0,) * a.ndim)
    once = lambda a: pl.BlockSpec(a.shape, lambda i, s: (0,) * a.ndim, pipeline_mode=pl.Buffered(1))
    return pl.pallas_call(
        functools.partial(_combine_kernel, alpha=alpha),
        grid_spec=pltpu.PrefetchScalarGridSpec(
            num_scalar_prefetch=1,
            grid=(t // tm,),
            in_specs=[pl.BlockSpec(memory_space=pl.ANY),
                      pl.BlockSpec((tm, d), lambda i, s: (i, 0)),
                      pl.BlockSpec((tm, TOP_K_PAD, LANES), lambda i, s: (i, 0, 0)),
                      once(wsg), once(wsu), once(wsd), full(g), full(b)],
            out_specs=pl.BlockSpec((tm, d), lambda i, s: (i, 0)),
            scratch_shapes=[pltpu.VMEM((2, TOP_K, tm, planes, LANES), U32),
                            pltpu.VMEM((2, tm * planes, LANES), F32), pltpu.VMEM((tm, d), F32),
                            pltpu.SemaphoreType.DMA((2,))],
        ),
        out_shape=jax.ShapeDtypeStruct((t, d), F32),
        compiler_params=_cparams(("arbitrary",)),
        name="combine",
    )(slots_flat, y, h, wts_planes, wsg, wsu, wsd, g, b)


def _prep_attention_weights(w_in, w_uq, w_ukv):
    d = w_in.shape[0]
    ql = w_uq.shape[0]
    kvl = w_ukv.shape[0]
    half = QK_ROPE_DIM // 2
    pad = LANES - QK_ROPE_DIM
    w_kr = w_in[:, ql + kvl:ql + kvl + QK_ROPE_DIM]
    w_kr_rot = jnp.concatenate([-w_kr[:, half:], w_kr[:, :half]], axis=1)
    zpad = jnp.zeros((d, pad), w_in.dtype)
    wa = jnp.concatenate([w_in[:, :ql + kvl], w_kr, zpad, w_kr_rot, zpad], axis=1).astype(BF16)
    wc = w_in[:, ql + kvl + QK_ROPE_DIM:].astype(BF16)
    uq = w_uq.reshape(ql, N_HEADS, QK_NOPE_DIM + QK_ROPE_DIM)
    uq_nope, uq_rope = uq[..., :QK_NOPE_DIM], uq[..., QK_NOPE_DIM:]
    zq = jnp.zeros((ql, N_HEADS, pad), w_uq.dtype)
    wuq = jnp.concatenate([uq_nope, uq_rope, zq], axis=-1).reshape(ql, N_HEADS * HEAD_QK_PAD).astype(BF16)
    uq_rot = jnp.concatenate([-uq_rope[..., half:], uq_rope[..., :half], zq], axis=-1)
    wuqr = uq_rot.reshape(ql, N_HEADS * LANES).astype(BF16)
    ukv = w_ukv.reshape(kvl, N_HEADS, QK_NOPE_DIM + V_HEAD_DIM)
    wukv = jnp.concatenate([ukv[..., :QK_NOPE_DIM].reshape(kvl, -1), ukv[..., QK_NOPE_DIM:].reshape(kvl, -1)],
                           axis=1).astype(BF16)
    return wa, wc, wuq, wuqr, wukv


def _layer(h, pos2, invf4, batch, seq, alpha, w_in, q_norm_g, w_uq, kv_norm_g, w_ukv, conv_w, conv_b,
           conv_ln_g, conv_ln_b, w_o, ln1_g, ln1_b, w_router, router_bias, w_gate, w_up, w_down,
           ws_gate, ws_up, ws_down, ln2_g, ln2_b):
    t, d = h.shape
    row = lambda a: a.reshape(1, -1)
    wa, wc, wuq, wuqr, wukv = _prep_attention_weights(w_in, w_uq, w_ukv)
    q, k, v = _qkv(h, pos2, invf4, wa, row(q_norm_g), row(kv_norm_g), wuq, wuqr, wukv)
    conv = _conv(h, wc, conv_w, row(conv_b), row(conv_ln_g), row(conv_ln_b), seq)
    attn = _attn(q, k, v, batch, seq)
    aw = attn.shape[1]
    wr32 = jnp.pad(w_router.astype(F32), ((0, 0), (0, LANES - N_EXPERTS)))
    wr_hi = wr32.astype(BF16)
    wr_lo = (wr32 - wr_hi.astype(F32)).astype(BF16)
    wr = jnp.concatenate([wr_hi, wr_hi, wr_lo], axis=0)
    h1, h1p, logits_t = _oproj(attn, conv, h, w_o[:aw].astype(BF16), w_o[aw:].astype(BF16),
                               row(ln1_g), row(ln1_b), wr, alpha)
    return _moe_ffn(h1, h1p, logits_t, alpha, router_bias, w_gate, w_up, w_down,
                    ws_gate, ws_up, ws_down, ln2_g, ln2_b)


def _moe_ffn(h1, h1p, logits_t, alpha, router_bias, w_gate, w_up, w_down, ws_gate, ws_up, ws_down, ln2_g, ln2_b):
    t, d = h1.shape
    row = lambda a: a.reshape(1, -1)
    eidx, rank, wts, cnt = _route(logits_t, router_bias.astype(F32).reshape(N_EXPERTS, 1))
    counts = cnt[:, 0].astype(I32)
    padded = (counts + TE_ROWS - 1) // TE_ROWS * TE_ROWS
    pend = jnp.cumsum(padded)
    poff = pend - padded
    n_rows = t * TOP_K + N_EXPERTS * TE_ROWS
    n_tiles = n_rows // TE_ROWS
    n_valid = (pend[-1] // TE_ROWS).astype(I32).reshape(1)
    tile_row = jnp.minimum(jnp.arange(n_tiles, dtype=I32), n_valid[0] - 1) * TE_ROWS
    tile_expert = jnp.sum((pend[None, :] <= tile_row[:, None]).astype(I32), axis=1)
    tile_expert = jnp.minimum(tile_expert, N_EXPERTS - 1)
    tiles = jnp.arange(n_tiles, dtype=I32)
    last_tile = jnp.where(padded > 0, pend // TE_ROWS - 1, -1)
    zfill = ((tiles >= n_valid[0]) | jnp.any(tiles[:, None] == last_tile[None, :], axis=1)).astype(I32)
    ids = jnp.arange(N_EXPERTS, dtype=I32)
    later = (ids[None, :] > ids[:, None]) & (padded[None, :] > 0)
    next_of = jnp.min(jnp.where(later, ids[None, :], N_EXPERTS), axis=1)
    next_of = jnp.where(next_of == N_EXPERTS, -1, next_of)
    next_expert = jnp.sum(jnp.where(tile_expert[:, None] == ids[None, :], next_of[None, :], 0), axis=1).astype(I32)
    slot_base = jnp.sum(jnp.where(eidx[None] == ids[:, None, None], poff[:, None, None], 0), axis=0)
    slots_flat = (slot_base + rank).T.reshape(-1)
    xs = _dispatch(zfill, slots_flat, h1p.reshape(t, -1, LANES), n_rows)
    y = _experts(tile_expert, n_valid, next_expert, xs, w_gate, w_up, w_down)
    wts_planes = jnp.broadcast_to(wts.T[:, :, None], (t, TOP_K_PAD, LANES))
    return _combine(slots_flat, y, h1, wts_planes, ws_gate.astype(BF16), ws_up.astype(BF16),
                    ws_down.astype(BF16), row(ln2_g), row(ln2_b), alpha)


def kernel(x, positions, w_in, q_norm_g, w_uq, kv_norm_g, w_ukv, conv_w, conv_b, conv_ln_g, conv_ln_b, w_o, ln1_g, ln1_b, w_router, router_bias, w_gate, w_up, w_down, ws_gate, ws_up, ws_down, ln2_g, ln2_b):
    batch, seq, d = x.shape
    depth = w_in.shape[0]
    alpha = (2.0 * depth) ** 0.25
    inv_freq = ROPE_BASE ** (-jnp.arange(0, QK_ROPE_DIM, 2, dtype=F32) / QK_ROPE_DIM)
    invf4 = jnp.tile(inv_freq, LANES // inv_freq.shape[0]).reshape(1, LANES)
    pos2 = positions.reshape(batch * seq, 1)
    h = x.reshape(batch * seq, d)
    for l in range(depth):
        h = _layer(h, pos2, invf4, batch, seq, alpha, w_in[l], q_norm_g[l], w_uq[l], kv_norm_g[l], w_ukv[l],
                   conv_w[l], conv_b[l], conv_ln_g[l], conv_ln_b[l], w_o[l], ln1_g[l], ln1_b[l],
                   w_router[l], router_bias[l], w_gate[l], w_up[l], w_down[l],
                   ws_gate[l], ws_up[l], ws_down[l], ln2_g[l], ln2_b[l])
    return h.reshape(batch, seq, d)
```

```python
import functools

import jax
import jax.numpy as jnp
from jax import lax
from jax.experimental import pallas as pl
from jax.experimental.pallas import tpu as pltpu

F32 = jnp.float32
BF16 = jnp.bfloat16
I32 = jnp.int32
U32 = jnp.uint32

N_HEADS = 8
QK_NOPE_DIM = 128
QK_ROPE_DIM = 64
V_HEAD_DIM = 128
HEAD_QK_PAD = 256
CONV_KERNEL = 31
N_EXPERTS = 64
TOP_K = 6
TOP_K_PAD = 8
N_EXPERT_GROUPS = 8
GROUP_SIZE = N_EXPERTS // N_EXPERT_GROUPS
TOPK_GROUPS = 4
ROUTED_SCALE = 2.5
ROPE_BASE = 10000.0
LN_EPS = 1e-5
RMS_EPS = 1e-6

LANES = 128
SUBLANES = 8
CONV_HALO = 32

TM_PROJ = 256
TM_CONV = 512
TQ_ATTN = 512
TK_ATTN = 512
HEADS_PER_ATTN_STEP = 4
TM_ROUTE = 512
TE_ROWS = 256
TM_COMB = 256
TC_DISP = 1024
WAIT_ROWS = 128
ISSUE_UNROLL = 4
GROUP_UNROLL = 8
CONV_ROWS = 64
CONV_COLS = 256
CONV_SHIFT_BUFS = 4
VMEM_LIMIT = 56 * 1024 * 1024


def _cparams(sem):
    return pltpu.CompilerParams(dimension_semantics=sem, vmem_limit_bytes=VMEM_LIMIT)


def _sigmoid(v):
    return 1.0 / (1.0 + jnp.exp(-v))


def _layer_norm(v, g, b):
    mu = jnp.mean(v, axis=-1, keepdims=True)
    d = v - mu
    var = jnp.mean(d * d, axis=-1, keepdims=True)
    return d * lax.rsqrt(var + LN_EPS) * g + b


def _rms_norm(v, g):
    ms = jnp.mean(v * v, axis=-1, keepdims=True)
    return v * lax.rsqrt(ms + RMS_EPS) * g


def _qkv_kernel(x_ref, pos_ref, invf_ref, wa_ref, qg_ref, kvg_ref, wuq_ref, wuqr_ref, wukv_ref,
                qt_ref, k_ref, vt_ref):
    ql = qg_ref.shape[1]
    kvl = kvg_ref.shape[1]
    xb = x_ref[...].astype(BF16)
    lat = jnp.dot(xb, wa_ref[...], preferred_element_type=F32)
    ang = pos_ref[...].astype(F32) * invf_ref[...]
    cos = jnp.cos(ang)
    sin = jnp.sin(ang)
    cq = _rms_norm(lat[:, :ql], qg_ref[...]).astype(BF16)
    ckv = _rms_norm(lat[:, ql:ql + kvl], kvg_ref[...]).astype(BF16)
    kr = lat[:, ql + kvl:ql + kvl + LANES] * cos + lat[:, ql + kvl + LANES:ql + kvl + 2 * LANES] * sin
    kr = kr.astype(BF16)
    q = jnp.dot(cq, wuq_ref[...], preferred_element_type=F32)
    qrot = jnp.dot(cq, wuqr_ref[...], preferred_element_type=F32)
    kv = jnp.dot(ckv, wukv_ref[...], preferred_element_type=F32)
    for h in range(N_HEADS):
        c0 = h * HEAD_QK_PAD
        qt_ref[c0:c0 + LANES, :] = q[:, c0:c0 + LANES].T.astype(BF16)
        qt_ref[c0 + LANES:c0 + 2 * LANES, :] = (
            q[:, c0 + LANES:c0 + 2 * LANES] * cos + qrot[:, h * LANES:(h + 1) * LANES] * sin).T.astype(BF16)
        k_ref[:, c0:c0 + LANES] = kv[:, h * LANES:(h + 1) * LANES].astype(BF16)
        k_ref[:, c0 + LANES:c0 + 2 * LANES] = kr
    vt_ref[...] = kv[:, N_HEADS * QK_NOPE_DIM:].T.astype(BF16)


def _qkv(x2, pos2, invf4, wa, qg, kvg, wuq, wuqr, wukv):
    t, d = x2.shape
    tm = TM_PROJ
    full = lambda a: pl.BlockSpec(a.shape, lambda i: (0,) * a.ndim)
    return pl.pallas_call(
        _qkv_kernel,
        grid=(t // tm,),
        in_specs=[pl.BlockSpec((tm, d), lambda i: (i, 0)),
                  pl.BlockSpec((tm, 1), lambda i: (i, 0)),
                  full(invf4), full(wa), full(qg), full(kvg), full(wuq), full(wuqr), full(wukv)],
        out_specs=[pl.BlockSpec((N_HEADS * HEAD_QK_PAD, tm), lambda i: (0, i)),
                   pl.BlockSpec((tm, N_HEADS * HEAD_QK_PAD), lambda i: (i, 0)),
                   pl.BlockSpec((N_HEADS * V_HEAD_DIM, tm), lambda i: (0, i))],
        out_shape=[jax.ShapeDtypeStruct((N_HEADS * HEAD_QK_PAD, t), BF16),
                   jax.ShapeDtypeStruct((t, N_HEADS * HEAD_QK_PAD), BF16),
                   jax.ShapeDtypeStruct((N_HEADS * V_HEAD_DIM, t), BF16)],
        compiler_params=_cparams(("arbitrary",)),
        name="qkv",
    )(x2, pos2, invf4, wa, qg, kvg, wuq, wuqr, wukv)


def _conv_kernel(x_ref, wc_ref, cw_ref, cb_ref, g_ref, b_ref, o_ref, ubuf, ybuf, shbuf, *, tiles_per_seq):
    tm, cw = o_ref.shape
    i = pl.program_id(0)
    xb = x_ref[...].astype(BF16)

    @pl.when(i % tiles_per_seq == 0)
    def _():
        ubuf[0:CONV_HALO, :] = jnp.zeros((CONV_HALO, cw), F32)

    shift0 = CONV_HALO - (CONV_KERNEL - 1)
    n_sh = shbuf.shape[0]
    chunk_no = 0
    for c0 in range(0, cw, CONV_COLS):
        a = jnp.dot(xb, wc_ref[:, c0:c0 + CONV_COLS], preferred_element_type=F32)
        gate = jnp.dot(xb, wc_ref[:, cw + c0:cw + c0 + CONV_COLS], preferred_element_type=F32)
        ubuf[CONV_HALO:CONV_HALO + tm, c0:c0 + CONV_COLS] = a * _sigmoid(gate)
        for r0 in range(0, tm, CONV_ROWS):
            sh = shbuf.at[chunk_no % n_sh]
            chunk_no += 1
            win = ubuf[r0:r0 + CONV_ROWS + CONV_HALO, c0:c0 + CONV_COLS]
            for r in range(1, SUBLANES):
                sh[r] = win[r:r + CONV_ROWS + CONV_HALO - SUBLANES, :]
            acc = jnp.zeros((CONV_ROWS, CONV_COLS), F32)
            for k in range(CONV_KERNEL):
                wk = cw_ref[k:k + 1, c0:c0 + CONV_COLS]
                r, j = (shift0 + k) % SUBLANES, (shift0 + k) // SUBLANES
                if r == 0:
                    tap = win[j * SUBLANES:j * SUBLANES + CONV_ROWS, :]
                else:
                    tap = sh[r, j * SUBLANES:j * SUBLANES + CONV_ROWS, :]
                acc = acc + wk * tap
            ybuf[r0:r0 + CONV_ROWS, c0:c0 + CONV_COLS] = acc
    ubuf[0:CONV_HALO, :] = ubuf[tm:tm + CONV_HALO, :]
    y = _layer_norm(ybuf[...] + cb_ref[...], g_ref[...], b_ref[...])
    o_ref[...] = (y * _sigmoid(y)).astype(BF16)


def _conv(x2, wc, cw, cb, g, b, seq):
    t, d = x2.shape
    tm = TM_CONV
    c = cw.shape[1]
    full = lambda a: pl.BlockSpec(a.shape, lambda i: (0,) * a.ndim)
    return pl.pallas_call(
        functools.partial(_conv_kernel, tiles_per_seq=seq // tm),
        grid=(t // tm,),
        in_specs=[pl.BlockSpec((tm, d), lambda i: (i, 0)), full(wc), full(cw), full(cb), full(g), full(b)],
        out_specs=pl.BlockSpec((tm, c), lambda i: (i, 0)),
        out_shape=jax.ShapeDtypeStruct((t, c), BF16),
        scratch_shapes=[pltpu.VMEM((tm + CONV_HALO, c), F32), pltpu.VMEM((tm, c), F32),
                        pltpu.VMEM((CONV_SHIFT_BUFS, SUBLANES, CONV_ROWS + CONV_HALO - SUBLANES, CONV_COLS), F32)],
        compiler_params=_cparams(("arbitrary",)),
        name="conv",
    )(x2, wc, cw, cb, g, b)


def _attn_kernel(qt_ref, k_ref, vt_ref, o_ref, *, scale, tk):
    tq = qt_ref.shape[1]
    heads = qt_ref.shape[0] // HEAD_QK_PAD
    i = pl.program_id(2)
    c = scale * 1.4426950408889634
    n_full = (i * tq) // tk
    q_off = i * tq - n_full * tk

    def block(j, carry, masked):
        r0 = pl.multiple_of(j * tk, tk)
        out = []
        scores = []
        for h in range(heads):
            qt = qt_ref[h * HEAD_QK_PAD:(h + 1) * HEAD_QK_PAD, :]
            kb = k_ref[pl.ds(r0, tk), h * HEAD_QK_PAD:(h + 1) * HEAD_QK_PAD]
            scores.append(jnp.dot(kb, qt, preferred_element_type=F32))
        for h, (m, l, acc) in enumerate(carry):
            vtb = vt_ref[h * V_HEAD_DIM:(h + 1) * V_HEAD_DIM, pl.ds(r0, tk)]
            st = scores[h]
            if masked:
                key = lax.broadcasted_iota(I32, (tk, tq), 0)
                qry = lax.broadcasted_iota(I32, (tk, tq), 1) + q_off
                st = jnp.where(key <= qry, st, -jnp.inf)
            m_blk = jnp.max(jnp.max(st.reshape(SUBLANES, tk // SUBLANES, tq), axis=0), axis=0, keepdims=True)
            m_new = jnp.maximum(m, m_blk)
            alpha = jnp.exp2((m - m_new) * c)
            pt = jnp.exp2((st - m_new) * c)
            l_blk = jnp.sum(jnp.sum(pt.reshape(SUBLANES, tk // SUBLANES, tq), axis=0), axis=0, keepdims=True)
            l = alpha * l + l_blk
            acc = alpha * acc + jnp.dot(vtb, pt.astype(BF16), preferred_element_type=F32)
            out.append((m_new, l, acc))
        return tuple(out)

    init = tuple((jnp.full((1, tq), -jnp.inf, F32), jnp.zeros((1, tq), F32),
                  jnp.zeros((V_HEAD_DIM, tq), F32)) for _ in range(heads))
    carry = lax.fori_loop(0, n_full, lambda j, cr: block(j, cr, False), init)
    final = block(n_full, carry, True)
    for h, (_, l, acc) in enumerate(final):
        o_ref[:, h * V_HEAD_DIM:(h + 1) * V_HEAD_DIM] = (acc / l).T.astype(BF16)


def _attn(q, k, v, batch, seq):
    tq = TQ_ATTN
    nq = seq // tq
    hb = HEADS_PER_ATTN_STEP
    scale = (QK_NOPE_DIM + QK_ROPE_DIM) ** -0.5
    return pl.pallas_call(
        functools.partial(_attn_kernel, scale=scale, tk=TK_ATTN),
        grid=(batch, N_HEADS // hb, nq),
        in_specs=[pl.BlockSpec((hb * HEAD_QK_PAD, tq), lambda b, h, i: (h, b * nq + i)),
                  pl.BlockSpec((seq, hb * HEAD_QK_PAD), lambda b, h, i: (b, h)),
                  pl.BlockSpec((hb * V_HEAD_DIM, seq), lambda b, h, i: (h, b))],
        out_specs=pl.BlockSpec((tq, hb * V_HEAD_DIM), lambda b, h, i: (b * nq + i, h)),
        out_shape=jax.ShapeDtypeStruct((batch * seq, N_HEADS * V_HEAD_DIM), BF16),
        compiler_params=_cparams(("arbitrary", "arbitrary", "arbitrary")),
        name="attn",
    )(q, k, v)


def _pack_bf16_pairs(v):
    c = v.shape[1] // 2
    lo = pltpu.bitcast(v[:, :c].astype(BF16).astype(F32), U32)
    hi = pltpu.bitcast(v[:, c:].astype(BF16).astype(F32), U32)
    return (hi & jnp.uint32(0xFFFF0000)) | (lo >> 16)


def _unpack_bf16_pairs(p):
    lo = pltpu.bitcast(p << 16, F32).astype(BF16)
    hi = pltpu.bitcast(p & jnp.uint32(0xFFFF0000), F32).astype(BF16)
    return jnp.concatenate([lo, hi], axis=1)


def _store_row_planes(ref, v):
    n = v.shape[0]
    p = ref.shape[0] // n
    for j in range(p):
        ref[pl.ds(j, n, stride=p), :] = v[:, j * LANES:(j + 1) * LANES]


def _load_row_planes(ref, p):
    n = ref.shape[0] // p
    return jnp.concatenate([ref[pl.ds(j, n, stride=p), :] for j in range(p)], axis=1)


def _oproj_kernel(a_ref, c_ref, x_ref, woa_ref, woc_ref, g_ref, b_ref, wr_ref,
                  h_ref, hp_ref, lg_ref, *, alpha):
    mix = jnp.dot(a_ref[...], woa_ref[...], preferred_element_type=F32)
    mix = mix + jnp.dot(c_ref[...], woc_ref[...], preferred_element_type=F32)
    h = _layer_norm(alpha * x_ref[...] + mix, g_ref[...], b_ref[...])
    h_ref[...] = h
    _store_row_planes(hp_ref, _pack_bf16_pairs(h))
    h_hi = h.astype(BF16)
    h_lo = (h - h_hi.astype(F32)).astype(BF16)
    logits = jnp.dot(jnp.concatenate([h_hi, h_lo, h_hi], axis=1), wr_ref[...], preferred_element_type=F32)
    lg_ref[...] = logits.T[:N_EXPERTS, :]


def _oproj(attn, conv, x2, woa, woc, g, b, wr, alpha):
    t, d = x2.shape
    tm = TM_PROJ
    full = lambda a: pl.BlockSpec(a.shape, lambda i: (0,) * a.ndim)
    return pl.pallas_call(
        functools.partial(_oproj_kernel, alpha=alpha),
        grid=(t // tm,),
        in_specs=[pl.BlockSpec((tm, attn.shape[1]), lambda i: (i, 0)),
                  pl.BlockSpec((tm, conv.shape[1]), lambda i: (i, 0)),
                  pl.BlockSpec((tm, d), lambda i: (i, 0)),
                  full(woa), full(woc), full(g), full(b), full(wr)],
        out_specs=[pl.BlockSpec((tm, d), lambda i: (i, 0)),
                   pl.BlockSpec((tm * (d // 2 // LANES), LANES), lambda i: (i, 0)),
                   pl.BlockSpec((N_EXPERTS, tm), lambda i: (0, i))],
        out_shape=[jax.ShapeDtypeStruct((t, d), F32),
                   jax.ShapeDtypeStruct((t * (d // 2 // LANES), LANES), U32),
                   jax.ShapeDtypeStruct((N_EXPERTS, t), F32)],
        compiler_params=_cparams(("arbitrary",)),
        name="oproj",
    )(attn, conv, x2, woa, woc, g, b, wr)


def _route_kernel(lg_ref, bias_ref, eidx_ref, rank_ref, wts_ref, cnt_ref, carry_ref):
    ne, tm = lg_ref.shape
    i = pl.program_id(0)

    @pl.when(i == 0)
    def _():
        carry_ref[...] = jnp.zeros(carry_ref.shape, F32)

    scores = _sigmoid(lg_ref[...])
    biased = scores + bias_ref[...]
    neg = -jnp.inf
    sub8 = lax.broadcasted_iota(I32, (GROUP_SIZE, tm), 0)
    gscore = []
    for g in range(N_EXPERT_GROUPS):
        blk = biased[g * GROUP_SIZE:(g + 1) * GROUP_SIZE, :]
        m1 = jnp.max(blk, axis=0, keepdims=True)
        i1 = jnp.min(jnp.where(blk == m1, sub8, GROUP_SIZE), axis=0, keepdims=True)
        m2 = jnp.max(jnp.where(sub8 == i1, neg, blk), axis=0, keepdims=True)
        gscore.append(m1 + m2)
    kept = []
    for g in range(N_EXPERT_GROUPS):
        beat = jnp.zeros((1, tm), I32)
        for o in range(N_EXPERT_GROUPS):
            if o < g:
                beat = beat + (gscore[o] >= gscore[g]).astype(I32)
            elif o > g:
                beat = beat + (gscore[o] > gscore[g]).astype(I32)
        kept.append(jnp.where(beat < TOPK_GROUPS, biased[g * GROUP_SIZE:(g + 1) * GROUP_SIZE, :], neg))
    cur = jnp.concatenate(kept, axis=0)
    sub = lax.broadcasted_iota(I32, (ne, tm), 0)
    sel_any = jnp.zeros((ne, tm), F32)
    picks = []
    wsum = jnp.zeros((1, tm), F32)
    for k in range(TOP_K):
        m = jnp.max(cur, axis=0, keepdims=True)
        ei = jnp.min(jnp.where(cur == m, sub, ne), axis=0, keepdims=True)
        sel = sub == ei
        w = jnp.sum(jnp.where(sel, scores, 0.0), axis=0, keepdims=True)
        cur = jnp.where(sel, neg, cur)
        sel_any = sel_any + sel.astype(F32)
        wsum = wsum + w
        picks.append((ei, sel, w))
    r = lax.broadcasted_iota(I32, (tm, tm), 0)
    c = lax.broadcasted_iota(I32, (tm, tm), 1)
    upper = (r <= c).astype(BF16)
    cum = jnp.dot(sel_any.astype(BF16), upper, preferred_element_type=F32)
    carry = carry_ref[:, 0:1]
    excl = cum - sel_any + carry
    total = carry + jnp.sum(sel_any, axis=1, keepdims=True)
    carry_ref[...] = jnp.broadcast_to(total, carry_ref.shape)
    cnt_ref[...] = jnp.broadcast_to(total, cnt_ref.shape)
    denom = wsum + 1e-20
    eidx_ref[...] = jnp.zeros(eidx_ref.shape, I32)
    rank_ref[...] = jnp.zeros(rank_ref.shape, I32)
    wts_ref[...] = jnp.zeros(wts_ref.shape, F32)
    for k, (ei, sel, w) in enumerate(picks):
        eidx_ref[k:k + 1, :] = ei
        rank_ref[k:k + 1, :] = jnp.sum(jnp.where(sel, excl, 0.0), axis=0, keepdims=True).astype(I32)
        wts_ref[k:k + 1, :] = w / denom * ROUTED_SCALE


def _route(logits_t, bias):
    ne, t = logits_t.shape
    tm = TM_ROUTE
    return pl.pallas_call(
        _route_kernel,
        grid=(t // tm,),
        in_specs=[pl.BlockSpec((ne, tm), lambda i: (0, i)), pl.BlockSpec((ne, 1), lambda i: (0, 0))],
        out_specs=[pl.BlockSpec((TOP_K_PAD, tm), lambda i: (0, i)),
                   pl.BlockSpec((TOP_K_PAD, tm), lambda i: (0, i)),
                   pl.BlockSpec((TOP_K_PAD, tm), lambda i: (0, i)),
                   pl.BlockSpec((ne, LANES), lambda i: (0, 0))],
        out_shape=[jax.ShapeDtypeStruct((TOP_K_PAD, t), I32),
                   jax.ShapeDtypeStruct((TOP_K_PAD, t), I32),
                   jax.ShapeDtypeStruct((TOP_K_PAD, t), F32),
                   jax.ShapeDtypeStruct((ne, LANES), F32)],
        scratch_shapes=[pltpu.VMEM((ne, LANES), F32)],
        compiler_params=_cparams(("arbitrary",)),
        name="route",
    )(logits_t, bias)


def _wait_rows(src_rows, dst_rows, sem, n_rows):
    def body(_, c):
        pltpu.make_async_copy(src_rows, dst_rows, sem).wait()
        return c
    lax.fori_loop(0, n_rows // WAIT_ROWS, body, 0)


def _dispatch_kernel(zfill_ref, slots_ref, hp_ref, xs_hbm, zero_vmem, sem, zsem):
    i = pl.program_id(0)
    te = zero_vmem.shape[0]
    tc = hp_ref.shape[0]
    n_tiles = xs_hbm.shape[0] // te

    def zero_copy(tile):
        return pltpu.make_async_copy(zero_vmem, xs_hbm.at[pl.ds(pl.multiple_of(tile * te, te), te)], zsem)

    @pl.when(i == 0)
    def _():
        zero_vmem[...] = jnp.zeros(zero_vmem.shape, U32)

        def zstart(tile, c):
            @pl.when(zfill_ref[tile] != 0)
            def _():
                zero_copy(tile).start()
            return c

        def zwait(tile, c):
            @pl.when(zfill_ref[tile] != 0)
            def _():
                zero_copy(tile).wait()
            return c

        lax.fori_loop(0, n_tiles, zstart, 0)
        lax.fori_loop(0, n_tiles, zwait, 0)

    base = i * (tc * TOP_K_PAD)

    def issue(t, c):
        for k in range(TOP_K):
            slot = slots_ref[base + t * TOP_K_PAD + k]
            pltpu.make_async_copy(hp_ref.at[t], xs_hbm.at[slot], sem).start(priority=k % 2)
        return c

    lax.fori_loop(0, tc, issue, 0, unroll=ISSUE_UNROLL)
    _wait_rows(hp_ref.at[pl.ds(0, WAIT_ROWS)], xs_hbm.at[pl.ds(0, WAIT_ROWS)], sem, tc * TOP_K)


def _dispatch(zfill, slots_flat, hp, n_rows):
    t, planes, lanes = hp.shape
    tc = TC_DISP
    return pl.pallas_call(
        _dispatch_kernel,
        grid_spec=pltpu.PrefetchScalarGridSpec(
            num_scalar_prefetch=2,
            grid=(t // tc,),
            in_specs=[pl.BlockSpec((tc, planes, lanes), lambda i, z, s: (i, 0, 0))],
            out_specs=pl.BlockSpec(memory_space=pl.ANY),
            scratch_shapes=[pltpu.VMEM((TE_ROWS, planes, lanes), U32),
                            pltpu.SemaphoreType.DMA, pltpu.SemaphoreType.DMA],
        ),
        out_shape=jax.ShapeDtypeStruct((n_rows, planes, lanes), U32),
        compiler_params=_cparams(("arbitrary",)),
        name="dispatch",
    )(zfill, slots_flat, hp)


def _experts_kernel(te_ref, nv_ref, nx_ref, xs_ref, wg_hbm, wu_hbm, wd_hbm, y_ref,
                    sg, su, sd, wgb, wub, wdb, sem):
    i = pl.program_id(0)
    e = te_ref[i]
    prev = te_ref[jnp.maximum(i - 1, 0)]
    valid = i < nv_ref[0]

    def fetch(ex):
        return (pltpu.make_async_copy(wg_hbm.at[ex], sg, sem.at[0]),
                pltpu.make_async_copy(wu_hbm.at[ex], su, sem.at[1]),
                pltpu.make_async_copy(wd_hbm.at[ex], sd, sem.at[2]))

    @pl.when(i == 0)
    def _():
        for cp in fetch(e):
            cp.start()

    first = valid & ((i == 0) | (e != prev))

    def ffn():
        xs = _unpack_bf16_pairs(_load_row_planes(xs_ref, xs_ref.shape[0] // TE_ROWS))
        g = jnp.dot(xs, wgb[...], preferred_element_type=F32)
        u = jnp.dot(xs, wub[...], preferred_element_type=F32)
        hid = (g * _sigmoid(g) * u).astype(BF16)
        y = jnp.dot(hid, wdb[...], preferred_element_type=F32)
        _store_row_planes(y_ref, _pack_bf16_pairs(y))

    @pl.when(first)
    def _():
        for cp in fetch(e):
            cp.wait()
        wgb[...] = sg[...].astype(BF16)
        wub[...] = su[...].astype(BF16)
        wdb[...] = sd[...].astype(BF16)
        for cp in fetch(jnp.where(nx_ref[i] >= 0, nx_ref[i], e)):
            cp.start()
        ffn()

    @pl.when(valid & jnp.logical_not(first))
    def _():
        ffn()

    @pl.when(i == pl.num_programs(0) - 1)
    def _():
        for cp in fetch(e):
            cp.wait()

    @pl.when(jnp.logical_not(valid))
    def _():
        y_ref[...] = jnp.zeros(y_ref.shape, U32)


def _experts(tile_expert, n_valid, next_expert, xs, w_gate, w_up, w_down):
    n_rows, xp, _ = xs.shape
    ne, d, ff = w_gate.shape
    te = TE_ROWS
    in_map = lambda i, te_ref, nv_ref, nx_ref: (jnp.minimum(i, nv_ref[0] - 1), 0)
    out_map = lambda i, te_ref, nv_ref, nx_ref: (i, 0)
    hbm = pl.BlockSpec(memory_space=pl.ANY)
    y = pl.pallas_call(
        _experts_kernel,
        grid_spec=pltpu.PrefetchScalarGridSpec(
            num_scalar_prefetch=3,
            grid=(n_rows // te,),
            in_specs=[pl.BlockSpec((te * xp, LANES), in_map), hbm, hbm, hbm],
            out_specs=pl.BlockSpec((te * xp, LANES), out_map),
            scratch_shapes=[pltpu.VMEM((d, ff), F32), pltpu.VMEM((d, ff), F32), pltpu.VMEM((ff, d), F32),
                            pltpu.VMEM((d, ff), BF16), pltpu.VMEM((d, ff), BF16), pltpu.VMEM((ff, d), BF16),
                            pltpu.SemaphoreType.DMA((3,))],
        ),
        out_shape=jax.ShapeDtypeStruct((n_rows * xp, LANES), U32),
        compiler_params=_cparams(("arbitrary",)),
        name="experts",
    )(tile_expert, n_valid, next_expert, xs.reshape(n_rows * xp, LANES), w_gate, w_up, w_down)
    return y.reshape(n_rows, xp, LANES)


def _combine_kernel(slots_ref, y_hbm, h_ref, wts_ref, wsg_ref, wsu_ref, wsd_ref, g_ref, b_ref,
                    o_ref, gbuf, rbuf, shbuf, sem, *, alpha):
    tm, d = o_ref.shape
    planes = gbuf.shape[3]
    i = pl.program_id(0)
    cur = i % 2
    grp = SUBLANES

    def issue_rows(step, t0, n):
        buf = step % 2
        base = step * (tm * TOP_K_PAD)
        for u in range(n):
            for k in range(TOP_K):
                slot = slots_ref[base + (t0 + u) * TOP_K_PAD + k]
                pltpu.make_async_copy(y_hbm.at[slot], gbuf.at[buf, k, t0 + u], sem.at[buf]).start(priority=k % 2)

    @pl.when(i == 0)
    def _():
        def first(t, c):
            issue_rows(0, t, 1)
            return c
        lax.fori_loop(0, tm, first, 0, unroll=ISSUE_UNROLL)

    hb = h_ref[...].astype(BF16)
    sg = jnp.dot(hb, wsg_ref[...], preferred_element_type=F32)
    su = jnp.dot(hb, wsu_ref[...], preferred_element_type=F32)
    shbuf[...] = jnp.dot((sg * _sigmoid(sg) * su).astype(BF16), wsd_ref[...], preferred_element_type=F32)
    _wait_rows(y_hbm.at[pl.ds(0, WAIT_ROWS)], gbuf.at[cur, 0, pl.ds(0, WAIT_ROWS)], sem.at[cur], tm * TOP_K)

    def group(g, prefetch):
        t0 = pl.multiple_of(g * grp, grp)
        packed = [gbuf[cur, k, pl.ds(t0, grp)] for k in range(TOP_K)]
        w = wts_ref[pl.ds(t0, grp)]
        resid = alpha * h_ref[pl.ds(t0, grp), :] + shbuf[pl.ds(t0, grp), :]
        if prefetch:
            issue_rows(i + 1, t0, grp)
        r_lo = r_hi = None
        for k in range(TOP_K):
            lo = w[:, k:k + 1, :] * pltpu.bitcast(packed[k] << 16, F32)
            hi = w[:, k:k + 1, :] * pltpu.bitcast(packed[k] & jnp.uint32(0xFFFF0000), F32)
            r_lo = lo if r_lo is None else r_lo + lo
            r_hi = hi if r_hi is None else r_hi + hi
        p0 = pl.multiple_of(t0 * planes, grp * planes)
        rbuf[0, pl.ds(p0, grp * planes), :] = r_lo.reshape(grp * planes, LANES)
        rbuf[1, pl.ds(p0, grp * planes), :] = r_hi.reshape(grp * planes, LANES)
        routed = jnp.concatenate([rbuf[half, pl.ds(p0 + j, grp, stride=planes), :]
                                  for half in range(2) for j in range(planes)], axis=1)
        o_ref[pl.ds(t0, grp), :] = _layer_norm(resid + routed, g_ref[...], b_ref[...])

    has_next = i + 1 < pl.num_programs(0)

    @pl.when(has_next)
    def _():
        lax.fori_loop(0, tm // grp, lambda g, c: (group(g, True), c)[1], 0, unroll=GROUP_UNROLL)

    @pl.when(jnp.logical_not(has_next))
    def _():
        lax.fori_loop(0, tm // grp, lambda g, c: (group(g, False), c)[1], 0, unroll=GROUP_UNROLL)


def _combine(slots_flat, y, h, wts_planes, wsg, wsu, wsd, g, b, alpha):
    t, d = h.shape
    tm = TM_COMB
    planes = y.shape[1]
    full = lambda a: pl.BlockSpec(a.shape, lambda i, s: (0,) * a.ndim)
    once = lambda a: pl.BlockSpec(a.shape, lambda i, s: (0,) * a.ndim, pipeline_mode=pl.Buffered(1))
    return pl.pallas_call(
        functools.partial(_combine_kernel, alpha=alpha),
        grid_spec=pltpu.PrefetchScalarGridSpec(
            num_scalar_prefetch=1,
            grid=(t // tm,),
            in_specs=[pl.BlockSpec(memory_space=pl.ANY),
                      pl.BlockSpec((tm, d), lambda i, s: (i, 0)),
                      pl.BlockSpec((tm, TOP_K_PAD, LANES), lambda i, s: (i, 0, 0)),
                      once(wsg), once(wsu), once(wsd), full(g), full(b)],
            out_specs=pl.BlockSpec((tm, d), lambda i, s: (i, 0)),
            scratch_shapes=[pltpu.VMEM((2, TOP_K, tm, planes, LANES), U32),
                            pltpu.VMEM((2, tm * planes, LANES), F32), pltpu.VMEM((tm, d), F32),
                            pltpu.SemaphoreType.DMA((2,))],
        ),
        out_shape=jax.ShapeDtypeStruct((t, d), F32),
        compiler_params=_cparams(("arbitrary",)),
        name="combine",
    )(slots_flat, y, h, wts_planes, wsg, wsu, wsd, g, b)


def _prep_attention_weights(w_in, w_uq, w_ukv):
    d = w_in.shape[0]
    ql = w_uq.shape[0]
    kvl = w_ukv.shape[0]
    half = QK_ROPE_DIM // 2
    pad = LANES - QK_ROPE_DIM
    w_kr = w_in[:, ql + kvl:ql + kvl + QK_ROPE_DIM]
    w_kr_rot = jnp.concatenate([-w_kr[:, half:], w_kr[:, :half]], axis=1)
    zpad = jnp.zeros((d, pad), w_in.dtype)
    wa = jnp.concatenate([w_in[:, :ql + kvl], w_kr, zpad, w_kr_rot, zpad], axis=1).astype(BF16)
    wc = w_in[:, ql + kvl + QK_ROPE_DIM:].astype(BF16)
    uq = w_uq.reshape(ql, N_HEADS, QK_NOPE_DIM + QK_ROPE_DIM)
    uq_nope, uq_rope = uq[..., :QK_NOPE_DIM], uq[..., QK_NOPE_DIM:]
    zq = jnp.zeros((ql, N_HEADS, pad), w_uq.dtype)
    wuq = jnp.concatenate([uq_nope, uq_rope, zq], axis=-1).reshape(ql, N_HEADS * HEAD_QK_PAD).astype(BF16)
    uq_rot = jnp.concatenate([-uq_rope[..., half:], uq_rope[..., :half], zq], axis=-1)
    wuqr = uq_rot.reshape(ql, N_HEADS * LANES).astype(BF16)
    ukv = w_ukv.reshape(kvl, N_HEADS, QK_NOPE_DIM + V_HEAD_DIM)
    wukv = jnp.concatenate([ukv[..., :QK_NOPE_DIM].reshape(kvl, -1), ukv[..., QK_NOPE_DIM:].reshape(kvl, -1)],
                           axis=1).astype(BF16)
    return wa, wc, wuq, wuqr, wukv


def _layer(h, pos2, invf4, batch, seq, alpha, w_in, q_norm_g, w_uq, kv_norm_g, w_ukv, conv_w, conv_b,
           conv_ln_g, conv_ln_b, w_o, ln1_g, ln1_b, w_router, router_bias, w_gate, w_up, w_down,
           ws_gate, ws_up, ws_down, ln2_g, ln2_b):
    t, d = h.shape
    row = lambda a: a.reshape(1, -1)
    wa, wc, wuq, wuqr, wukv = _prep_attention_weights(w_in, w_uq, w_ukv)
    q, k, v = _qkv(h, pos2, invf4, wa, row(q_norm_g), row(kv_norm_g), wuq, wuqr, wukv)
    conv = _conv(h, wc, conv_w, row(conv_b), row(conv_ln_g), row(conv_ln_b), seq)
    attn = _attn(q, k, v, batch, seq)
    aw = attn.shape[1]
    wr32 = jnp.pad(w_router.astype(F32), ((0, 0), (0, LANES - N_EXPERTS)))
    wr_hi = wr32.astype(BF16)
    wr_lo = (wr32 - wr_hi.astype(F32)).astype(BF16)
    wr = jnp.concatenate([wr_hi, wr_hi, wr_lo], axis=0)
    h1, h1p, logits_t = _oproj(attn, conv, h, w_o[:aw].astype(BF16), w_o[aw:].astype(BF16),
                               row(ln1_g), row(ln1_b), wr, alpha)
    return _moe_ffn(h1, h1p, logits_t, alpha, router_bias, w_gate, w_up, w_down,
                    ws_gate, ws_up, ws_down, ln2_g, ln2_b)


def _moe_ffn(h1, h1p, logits_t, alpha, router_bias, w_gate, w_up, w_down, ws_gate, ws_up, ws_down, ln2_g, ln2_b):
    t, d = h1.shape
    row = lambda a: a.reshape(1, -1)
    eidx, rank, wts, cnt = _route(logits_t, router_bias.astype(F32).reshape(N_EXPERTS, 1))
    counts = cnt[:, 0].astype(I32)
    padded = (counts + TE_ROWS - 1) // TE_ROWS * TE_ROWS
    pend = jnp.cumsum(padded)
    poff = pend - padded
    n_rows = t * TOP_K + N_EXPERTS * TE_ROWS
    n_tiles = n_rows // TE_ROWS
    n_valid = (pend[-1] // TE_ROWS).astype(I32).reshape(1)
    tile_row = jnp.minimum(jnp.arange(n_tiles, dtype=I32), n_valid[0] - 1) * TE_ROWS
    tile_expert = jnp.sum((pend[None, :] <= tile_row[:, None]).astype(I32), axis=1)
    tile_expert = jnp.minimum(tile_expert, N_EXPERTS - 1)
    tiles = jnp.arange(n_tiles, dtype=I32)
    last_tile = jnp.where(padded > 0, pend // TE_ROWS - 1, -1)
    zfill = ((tiles >= n_valid[0]) | jnp.any(tiles[:, None] == last_tile[None, :], axis=1)).astype(I32)
    ids = jnp.arange(N_EXPERTS, dtype=I32)
    later = (ids[None, :] > ids[:, None]) & (padded[None, :] > 0)
    next_of = jnp.min(jnp.where(later, ids[None, :], N_EXPERTS), axis=1)
    next_of = jnp.where(next_of == N_EXPERTS, -1, next_of)
    next_expert = jnp.sum(jnp.where(tile_expert[:, None] == ids[None, :], next_of[None, :], 0), axis=1).astype(I32)
    slot_base = jnp.sum(jnp.where(eidx[None] == ids[:, None, None], poff[:, None, None], 0), axis=0)
    slots_flat = (slot_base + rank).T.reshape(-1)
    xs = _dispatch(zfill, slots_flat, h1p.reshape(t, -1, LANES), n_rows)
    y = _experts(tile_expert, n_valid, next_expert, xs, w_gate, w_up, w_down)
    wts_planes = jnp.broadcast_to(wts.T[:, :, None], (t, TOP_K_PAD, LANES))
    return _combine(slots_flat, y, h1, wts_planes, ws_gate.astype(BF16), ws_up.astype(BF16),
                    ws_down.astype(BF16), row(ln2_g), row(ln2_b), alpha)


def kernel(x, positions, w_in, q_norm_g, w_uq, kv_norm_g, w_ukv, conv_w, conv_b, conv_ln_g, conv_ln_b, w_o, ln1_g, ln1_b, w_router, router_bias, w_gate, w_up, w_down, ws_gate, ws_up, ws_down, ln2_g, ln2_b):
    batch, seq, d = x.shape
    depth = w_in.shape[0]
    alpha = (2.0 * depth) ** 0.25
    inv_freq = ROPE_BASE ** (-jnp.arange(0, QK_ROPE_DIM, 2, dtype=F32) / QK_ROPE_DIM)
    invf4 = jnp.tile(inv_freq, LANES // inv_freq.shape[0]).reshape(1, LANES)
    pos2 = positions.reshape(batch * seq, 1)
    h = x.reshape(batch * seq, d)
    for l in range(depth):
        h = _layer(h, pos2, invf4, batch, seq, alpha, w_in[l], q_norm_g[l], w_uq[l], kv_norm_g[l], w_ukv[l],
                   conv_w[l], conv_b[l], conv_ln_g[l], conv_ln_b[l], w_o[l], ln1_g[l], ln1_b[l],
                   w_router[l], router_bias[l], w_gate[l], w_up[l], w_down[l],
                   ws_gate[l], ws_up[l], ws_down[l], ln2_g[l], ln2_b[l])
    return h.reshape(batch, seq, d)
```

```python
import functools

import jax
import jax.numpy as jnp
from jax import lax
from jax.experimental import pallas as pl
from jax.experimental.pallas import tpu as pltpu

F32 = jnp.float32
BF16 = jnp.bfloat16
I32 = jnp.int32
U32 = jnp.uint32

N_HEADS = 8
QK_NOPE_DIM = 128
QK_ROPE_DIM = 64
V_HEAD_DIM = 128
HEAD_QK_PAD = 256
CONV_KERNEL = 31
N_EXPERTS = 64
TOP_K = 6
TOP_K_PAD = 8
N_EXPERT_GROUPS = 8
GROUP_SIZE = N_EXPERTS // N_EXPERT_GROUPS
TOPK_GROUPS = 4
ROUTED_SCALE = 2.5
ROPE_BASE = 10000.0
LN_EPS = 1e-5
RMS_EPS = 1e-6

LANES = 128
SUBLANES = 8
CONV_HALO = 32

TM_PROJ = 256
TM_CONV = 512
TQ_ATTN = 512
TK_ATTN = 512
HEADS_PER_ATTN_STEP = 4
TM_ROUTE = 512
TE_ROWS = 256
TM_COMB = 256
TC_DISP = 1024
WAIT_ROWS = 128
ISSUE_UNROLL = 4
GROUP_UNROLL = 8
CONV_ROWS = 64
CONV_COLS = 256
CONV_SHIFT_BUFS = 4
VMEM_LIMIT = 56 * 1024 * 1024


def _cparams(sem):
    return pltpu.CompilerParams(dimension_semantics=sem, vmem_limit_bytes=VMEM_LIMIT)


def _sigmoid(v):
    return 1.0 / (1.0 + jnp.exp(-v))


def _layer_norm(v, g, b):
    mu = jnp.mean(v, axis=-1, keepdims=True)
    d = v - mu
    var = jnp.mean(d * d, axis=-1, keepdims=True)
    return d * lax.rsqrt(var + LN_EPS) * g + b


def _rms_norm(v, g):
    ms = jnp.mean(v * v, axis=-1, keepdims=True)
    return v * lax.rsqrt(ms + RMS_EPS) * g


def _qkv_kernel(x_ref, pos_ref, invf_ref, wa_ref, qg_ref, kvg_ref, wuq_ref, wuqr_ref, wukv_ref,
                qt_ref, k_ref, vt_ref):
    ql = qg_ref.shape[1]
    kvl = kvg_ref.shape[1]
    xb = x_ref[...].astype(BF16)
    lat = jnp.dot(xb, wa_ref[...], preferred_element_type=F32)
    ang = pos_ref[...].astype(F32) * invf_ref[...]
    cos = jnp.cos(ang)
    sin = jnp.sin(ang)
    cq = _rms_norm(lat[:, :ql], qg_ref[...]).astype(BF16)
    ckv = _rms_norm(lat[:, ql:ql + kvl], kvg_ref[...]).astype(BF16)
    kr = lat[:, ql + kvl:ql + kvl + LANES] * cos + lat[:, ql + kvl + LANES:ql + kvl + 2 * LANES] * sin
    kr = kr.astype(BF16)
    q = jnp.dot(cq, wuq_ref[...], preferred_element_type=F32)
    qrot = jnp.dot(cq, wuqr_ref[...], preferred_element_type=F32)
    kv = jnp.dot(ckv, wukv_ref[...], preferred_element_type=F32)
    for h in range(N_HEADS):
        c0 = h * HEAD_QK_PAD
        qt_ref[c0:c0 + LANES, :] = q[:, c0:c0 + LANES].T.astype(BF16)
        qt_ref[c0 + LANES:c0 + 2 * LANES, :] = (
            q[:, c0 + LANES:c0 + 2 * LANES] * cos + qrot[:, h * LANES:(h + 1) * LANES] * sin).T.astype(BF16)
        k_ref[:, c0:c0 + LANES] = kv[:, h * LANES:(h + 1) * LANES].astype(BF16)
        k_ref[:, c0 + LANES:c0 + 2 * LANES] = kr
    vt_ref[...] = kv[:, N_HEADS * QK_NOPE_DIM:].T.astype(BF16)


def _qkv(x2, pos2, invf4, wa, qg, kvg, wuq, wuqr, wukv):
    t, d = x2.shape
    tm = TM_PROJ
    full = lambda a: pl.BlockSpec(a.shape, lambda i: (0,) * a.ndim)
    return pl.pallas_call(
        _qkv_kernel,
        grid=(t // tm,),
        in_specs=[pl.BlockSpec((tm, d), lambda i: (i, 0)),
                  pl.BlockSpec((tm, 1), lambda i: (i, 0)),
                  full(invf4), full(wa), full(qg), full(kvg), full(wuq), full(wuqr), full(wukv)],
        out_specs=[pl.BlockSpec((N_HEADS * HEAD_QK_PAD, tm), lambda i: (0, i)),
                   pl.BlockSpec((tm, N_HEADS * HEAD_QK_PAD), lambda i: (i, 0)),
                   pl.BlockSpec((N_HEADS * V_HEAD_DIM, tm), lambda i: (0, i))],
        out_shape=[jax.ShapeDtypeStruct((N_HEADS * HEAD_QK_PAD, t), BF16),
                   jax.ShapeDtypeStruct((t, N_HEADS * HEAD_QK_PAD), BF16),
                   jax.ShapeDtypeStruct((N_HEADS * V_HEAD_DIM, t), BF16)],
        compiler_params=_cparams(("arbitrary",)),
        name="qkv",
    )(x2, pos2, invf4, wa, qg, kvg, wuq, wuqr, wukv)


def _conv_kernel(x_ref, wc_ref, cw_ref, cb_ref, g_ref, b_ref, o_ref, ubuf, ybuf, shbuf, *, tiles_per_seq):
    tm, cw = o_ref.shape
    i = pl.program_id(0)
    xb = x_ref[...].astype(BF16)

    @pl.when(i % tiles_per_seq == 0)
    def _():
        ubuf[0:CONV_HALO, :] = jnp.zeros((CONV_HALO, cw), F32)

    shift0 = CONV_HALO - (CONV_KERNEL - 1)
    n_sh = shbuf.shape[0]
    chunk_no = 0
    for c0 in range(0, cw, CONV_COLS):
        a = jnp.dot(xb, wc_ref[:, c0:c0 + CONV_COLS], preferred_element_type=F32)
        gate = jnp.dot(xb, wc_ref[:, cw + c0:cw + c0 + CONV_COLS], preferred_element_type=F32)
        ubuf[CONV_HALO:CONV_HALO + tm, c0:c0 + CONV_COLS] = a * _sigmoid(gate)
        for r0 in range(0, tm, CONV_ROWS):
            sh = shbuf.at[chunk_no % n_sh]
            chunk_no += 1
            win = ubuf[r0:r0 + CONV_ROWS + CONV_HALO, c0:c0 + CONV_COLS]
            for r in range(1, SUBLANES):
                sh[r] = win[r:r + CONV_ROWS + CONV_HALO - SUBLANES, :]
            acc = jnp.zeros((CONV_ROWS, CONV_COLS), F32)
            for k in range(CONV_KERNEL):
                wk = cw_ref[k:k + 1, c0:c0 + CONV_COLS]
                r, j = (shift0 + k) % SUBLANES, (shift0 + k) // SUBLANES
                if r == 0:
                    tap = win[j * SUBLANES:j * SUBLANES + CONV_ROWS, :]
                else:
                    tap = sh[r, j * SUBLANES:j * SUBLANES + CONV_ROWS, :]
                acc = acc + wk * tap
            ybuf[r0:r0 + CONV_ROWS, c0:c0 + CONV_COLS] = acc
    ubuf[0:CONV_HALO, :] = ubuf[tm:tm + CONV_HALO, :]
    y = _layer_norm(ybuf[...] + cb_ref[...], g_ref[...], b_ref[...])
    o_ref[...] = (y * _sigmoid(y)).astype(BF16)


def _conv(x2, wc, cw, cb, g, b, seq):
    t, d = x2.shape
    tm = TM_CONV
    c = cw.shape[1]
    full = lambda a: pl.BlockSpec(a.shape, lambda i: (0,) * a.ndim)
    return pl.pallas_call(
        functools.partial(_conv_kernel, tiles_per_seq=seq // tm),
        grid=(t // tm,),
        in_specs=[pl.BlockSpec((tm, d), lambda i: (i, 0)), full(wc), full(cw), full(cb), full(g), full(b)],
        out_specs=pl.BlockSpec((tm, c), lambda i: (i, 0)),
        out_shape=jax.ShapeDtypeStruct((t, c), BF16),
        scratch_shapes=[pltpu.VMEM((tm + CONV_HALO, c), F32), pltpu.VMEM((tm, c), F32),
                        pltpu.VMEM((CONV_SHIFT_BUFS, SUBLANES, CONV_ROWS + CONV_HALO - SUBLANES, CONV_COLS), F32)],
        compiler_params=_cparams(("arbitrary",)),
        name="conv",
    )(x2, wc, cw, cb, g, b)


def _attn_kernel(qt_ref, k_ref, vt_ref, o_ref, *, scale, tk):
    tq = qt_ref.shape[1]
    heads = qt_ref.shape[0] // HEAD_QK_PAD
    i = pl.program_id(2)
    c = scale * 1.4426950408889634
    n_full = (i * tq) // tk
    q_off = i * tq - n_full * tk

    def block(j, carry, masked):
        r0 = pl.multiple_of(j * tk, tk)
        out = []
        scores = []
        for h in range(heads):
            qt = qt_ref[h * HEAD_QK_PAD:(h + 1) * HEAD_QK_PAD, :]
            kb = k_ref[pl.ds(r0, tk), h * HEAD_QK_PAD:(h + 1) * HEAD_QK_PAD]
            scores.append(jnp.dot(kb, qt, preferred_element_type=F32))
        for h, (m, l, acc) in enumerate(carry):
            vtb = vt_ref[h * V_HEAD_DIM:(h + 1) * V_HEAD_DIM, pl.ds(r0, tk)]
            st = scores[h]
            if masked:
                key = lax.broadcasted_iota(I32, (tk, tq), 0)
                qry = lax.broadcasted_iota(I32, (tk, tq), 1) + q_off
                st = jnp.where(key <= qry, st, -jnp.inf)
            m_blk = jnp.max(jnp.max(st.reshape(SUBLANES, tk // SUBLANES, tq), axis=0), axis=0, keepdims=True)
            m_new = jnp.maximum(m, m_blk)
            alpha = jnp.exp2((m - m_new) * c)
            pt = jnp.exp2((st - m_new) * c)
            l_blk = jnp.sum(jnp.sum(pt.reshape(SUBLANES, tk // SUBLANES, tq), axis=0), axis=0, keepdims=True)
            l = alpha * l + l_blk
            acc = alpha * acc + jnp.dot(vtb, pt.astype(BF16), preferred_element_type=F32)
            out.append((m_new, l, acc))
        return tuple(out)

    init = tuple((jnp.full((1, tq), -jnp.inf, F32), jnp.zeros((1, tq), F32),
                  jnp.zeros((V_HEAD_DIM, tq), F32)) for _ in range(heads))
    carry = lax.fori_loop(0, n_full, lambda j, cr: block(j, cr, False), init)
    final = block(n_full, carry, True)
    for h, (_, l, acc) in enumerate(final):
        o_ref[:, h * V_HEAD_DIM:(h + 1) * V_HEAD_DIM] = (acc / l).T.astype(BF16)


def _attn(q, k, v, batch, seq):
    tq = TQ_ATTN
    nq = seq // tq
    hb = HEADS_PER_ATTN_STEP
    scale = (QK_NOPE_DIM + QK_ROPE_DIM) ** -0.5
    return pl.pallas_call(
        functools.partial(_attn_kernel, scale=scale, tk=TK_ATTN),
        grid=(batch, N_HEADS // hb, nq),
        in_specs=[pl.BlockSpec((hb * HEAD_QK_PAD, tq), lambda b, h, i: (h, b * nq + i)),
                  pl.BlockSpec((seq, hb * HEAD_QK_PAD), lambda b, h, i: (b, h)),
                  pl.BlockSpec((hb * V_HEAD_DIM, seq), lambda b, h, i: (h, b))],
        out_specs=pl.BlockSpec((tq, hb * V_HEAD_DIM), lambda b, h, i: (b * nq + i, h)),
        out_shape=jax.ShapeDtypeStruct((batch * seq, N_HEADS * V_HEAD_DIM), BF16),
        compiler_params=_cparams(("arbitrary", "arbitrary", "arbitrary")),
        name="attn",
    )(q, k, v)


def _pack_bf16_pairs(v):
    c = v.shape[1] // 2
    lo = pltpu.bitcast(v[:, :c].astype(BF16).astype(F32), U32)
    hi = pltpu.bitcast(v[:, c:].astype(BF16).astype(F32), U32)
    return (hi & jnp.uint32(0xFFFF0000)) | (lo >> 16)


def _unpack_bf16_pairs(p):
    lo = pltpu.bitcast(p << 16, F32).astype(BF16)
    hi = pltpu.bitcast(p & jnp.uint32(0xFFFF0000), F32).astype(BF16)
    return jnp.concatenate([lo, hi], axis=1)


def _store_row_planes(ref, v):
    n = v.shape[0]
    p = ref.shape[0] // n
    for j in range(p):
        ref[pl.ds(j, n, stride=p), :] = v[:, j * LANES:(j + 1) * LANES]


def _load_row_planes(ref, p):
    n = ref.shape[0] // p
    return jnp.concatenate([ref[pl.ds(j, n, stride=p), :] for j in range(p)], axis=1)


def _oproj_kernel(a_ref, c_ref, x_ref, woa_ref, woc_ref, g_ref, b_ref, wr_ref,
                  h_ref, hp_ref, lg_ref, *, alpha):
    mix = jnp.dot(a_ref[...], woa_ref[...], preferred_element_type=F32)
    mix = mix + jnp.dot(c_ref[...], woc_ref[...], preferred_element_type=F32)
    h = _layer_norm(alpha * x_ref[...] + mix, g_ref[...], b_ref[...])
    h_ref[...] = h
    _store_row_planes(hp_ref, _pack_bf16_pairs(h))
    h_hi = h.astype(BF16)
    h_lo = (h - h_hi.astype(F32)).astype(BF16)
    logits = jnp.dot(jnp.concatenate([h_hi, h_lo, h_hi], axis=1), wr_ref[...], preferred_element_type=F32)
    lg_ref[...] = logits.T[:N_EXPERTS, :]


def _oproj(attn, conv, x2, woa, woc, g, b, wr, alpha):
    t, d = x2.shape
    tm = TM_PROJ
    full = lambda a: pl.BlockSpec(a.shape, lambda i: (0,) * a.ndim)
    return pl.pallas_call(
        functools.partial(_oproj_kernel, alpha=alpha),
        grid=(t // tm,),
        in_specs=[pl.BlockSpec((tm, attn.shape[1]), lambda i: (i, 0)),
                  pl.BlockSpec((tm, conv.shape[1]), lambda i: (i, 0)),
                  pl.BlockSpec((tm, d), lambda i: (i, 0)),
                  full(woa), full(woc), full(g), full(b), full(wr)],
        out_specs=[pl.BlockSpec((tm, d), lambda i: (i, 0)),
                   pl.BlockSpec((tm * (d // 2 // LANES), LANES), lambda i: (i, 0)),
                   pl.BlockSpec((N_EXPERTS, tm), lambda i: (0, i))],
        out_shape=[jax.ShapeDtypeStruct((t, d), F32),
                   jax.ShapeDtypeStruct((t * (d // 2 // LANES), LANES), U32),
                   jax.ShapeDtypeStruct((N_EXPERTS, t), F32)],
        compiler_params=_cparams(("arbitrary",)),
        name="oproj",
    )(attn, conv, x2, woa, woc, g, b, wr)


def _route_kernel(lg_ref, bias_ref, eidx_ref, rank_ref, wts_ref, cnt_ref, carry_ref):
    ne, tm = lg_ref.shape
    i = pl.program_id(0)

    @pl.when(i == 0)
    def _():
        carry_ref[...] = jnp.zeros(carry_ref.shape, F32)

    scores = _sigmoid(lg_ref[...])
    biased = scores + bias_ref[...]
    neg = -jnp.inf
    sub8 = lax.broadcasted_iota(I32, (GROUP_SIZE, tm), 0)
    gscore = []
    for g in range(N_EXPERT_GROUPS):
        blk = biased[g * GROUP_SIZE:(g + 1) * GROUP_SIZE, :]
        m1 = jnp.max(blk, axis=0, keepdims=True)
        i1 = jnp.min(jnp.where(blk == m1, sub8, GROUP_SIZE), axis=0, keepdims=True)
        m2 = jnp.max(jnp.where(sub8 == i1, neg, blk), axis=0, keepdims=True)
        gscore.append(m1 + m2)
    kept = []
    for g in range(N_EXPERT_GROUPS):
        beat = jnp.zeros((1, tm), I32)
        for o in range(N_EXPERT_GROUPS):
            if o < g:
                beat = beat + (gscore[o] >= gscore[g]).astype(I32)
            elif o > g:
                beat = beat + (gscore[o] > gscore[g]).astype(I32)
        kept.append(jnp.where(beat < TOPK_GROUPS, biased[g * GROUP_SIZE:(g + 1) * GROUP_SIZE, :], neg))
    cur = jnp.concatenate(kept, axis=0)
    sub = lax.broadcasted_iota(I32, (ne, tm), 0)
    sel_any = jnp.zeros((ne, tm), F32)
    picks = []
    wsum = jnp.zeros((1, tm), F32)
    for k in range(TOP_K):
        m = jnp.max(cur, axis=0, keepdims=True)
        ei = jnp.min(jnp.where(cur == m, sub, ne), axis=0, keepdims=True)
        sel = sub == ei
        w = jnp.sum(jnp.where(sel, scores, 0.0), axis=0, keepdims=True)
        cur = jnp.where(sel, neg, cur)
        sel_any = sel_any + sel.astype(F32)
        wsum = wsum + w
        picks.append((ei, sel, w))
    r = lax.broadcasted_iota(I32, (tm, tm), 0)
    c = lax.broadcasted_iota(I32, (tm, tm), 1)
    upper = (r <= c).astype(BF16)
    cum = jnp.dot(sel_any.astype(BF16), upper, preferred_element_type=F32)
    carry = carry_ref[:, 0:1]
    excl = cum - sel_any + carry
    total = carry + jnp.sum(sel_any, axis=1, keepdims=True)
    carry_ref[...] = jnp.broadcast_to(total, carry_ref.shape)
    cnt_ref[...] = jnp.broadcast_to(total, cnt_ref.shape)
    denom = wsum + 1e-20
    eidx_ref[...] = jnp.zeros(eidx_ref.shape, I32)
    rank_ref[...] = jnp.zeros(rank_ref.shape, I32)
    wts_ref[...] = jnp.zeros(wts_ref.shape, F32)
    for k, (ei, sel, w) in enumerate(picks):
        eidx_ref[k:k + 1, :] = ei
        rank_ref[k:k + 1, :] = jnp.sum(jnp.where(sel, excl, 0.0), axis=0, keepdims=True).astype(I32)
        wts_ref[k:k + 1, :] = w / denom * ROUTED_SCALE


def _route(logits_t, bias):
    ne, t = logits_t.shape
    tm = TM_ROUTE
    return pl.pallas_call(
        _route_kernel,
        grid=(t // tm,),
        in_specs=[pl.BlockSpec((ne, tm), lambda i: (0, i)), pl.BlockSpec((ne, 1), lambda i: (0, 0))],
        out_specs=[pl.BlockSpec((TOP_K_PAD, tm), lambda i: (0, i)),
                   pl.BlockSpec((TOP_K_PAD, tm), lambda i: (0, i)),
                   pl.BlockSpec((TOP_K_PAD, tm), lambda i: (0, i)),
                   pl.BlockSpec((ne, LANES), lambda i: (0, 0))],
        out_shape=[jax.ShapeDtypeStruct((TOP_K_PAD, t), I32),
                   jax.ShapeDtypeStruct((TOP_K_PAD, t), I32),
                   jax.ShapeDtypeStruct((TOP_K_PAD, t), F32),
                   jax.ShapeDtypeStruct((ne, LANES), F32)],
        scratch_shapes=[pltpu.VMEM((ne, LANES), F32)],
        compiler_params=_cparams(("arbitrary",)),
        name="route",
    )(logits_t, bias)


def _wait_rows(src_rows, dst_rows, sem, n_rows):
    def body(_, c):
        pltpu.make_async_copy(src_rows, dst_rows, sem).wait()
        return c
    lax.fori_loop(0, n_rows // WAIT_ROWS, body, 0)


def _dispatch_kernel(zfill_ref, slots_ref, hp_ref, xs_hbm, zero_vmem, sem, zsem):
    i = pl.program_id(0)
    te = zero_vmem.shape[0]
    tc = hp_ref.shape[0]
    n_tiles = xs_hbm.shape[0] // te

    def zero_copy(tile):
        return pltpu.make_async_copy(zero_vmem, xs_hbm.at[pl.ds(pl.multiple_of(tile * te, te), te)], zsem)

    @pl.when(i == 0)
    def _():
        zero_vmem[...] = jnp.zeros(zero_vmem.shape, U32)

        def zstart(tile, c):
            @pl.when(zfill_ref[tile] != 0)
            def _():
                zero_copy(tile).start()
            return c

        def zwait(tile, c):
            @pl.when(zfill_ref[tile] != 0)
            def _():
                zero_copy(tile).wait()
            return c

        lax.fori_loop(0, n_tiles, zstart, 0)
        lax.fori_loop(0, n_tiles, zwait, 0)

    base = i * (tc * TOP_K_PAD)

    def issue(t, c):
        for k in range(TOP_K):
            slot = slots_ref[base + t * TOP_K_PAD + k]
            pltpu.make_async_copy(hp_ref.at[t], xs_hbm.at[slot], sem).start(priority=k % 2)
        return c

    lax.fori_loop(0, tc, issue, 0, unroll=ISSUE_UNROLL)
    _wait_rows(hp_ref.at[pl.ds(0, WAIT_ROWS)], xs_hbm.at[pl.ds(0, WAIT_ROWS)], sem, tc * TOP_K)


def _dispatch(zfill, slots_flat, hp, n_rows):
    t, planes, lanes = hp.shape
    tc = TC_DISP
    return pl.pallas_call(
        _dispatch_kernel,
        grid_spec=pltpu.PrefetchScalarGridSpec(
            num_scalar_prefetch=2,
            grid=(t // tc,),
            in_specs=[pl.BlockSpec((tc, planes, lanes), lambda i, z, s: (i, 0, 0))],
            out_specs=pl.BlockSpec(memory_space=pl.ANY),
            scratch_shapes=[pltpu.VMEM((TE_ROWS, planes, lanes), U32),
                            pltpu.SemaphoreType.DMA, pltpu.SemaphoreType.DMA],
        ),
        out_shape=jax.ShapeDtypeStruct((n_rows, planes, lanes), U32),
        compiler_params=_cparams(("arbitrary",)),
        name="dispatch",
    )(zfill, slots_flat, hp)


def _experts_kernel(te_ref, nv_ref, nx_ref, xs_ref, wg_hbm, wu_hbm, wd_hbm, y_ref,
                    sg, su, sd, wgb, wub, wdb, sem):
    i = pl.program_id(0)
    e = te_ref[i]
    prev = te_ref[jnp.maximum(i - 1, 0)]
    valid = i < nv_ref[0]

    def fetch(ex):
        return (pltpu.make_async_copy(wg_hbm.at[ex], sg, sem.at[0]),
                pltpu.make_async_copy(wu_hbm.at[ex], su, sem.at[1]),
                pltpu.make_async_copy(wd_hbm.at[ex], sd, sem.at[2]))

    @pl.when(i == 0)
    def _():
        for cp in fetch(e):
            cp.start()

    @pl.when(valid & ((i == 0) | (e != prev)))
    def _():
        for cp in fetch(e):
            cp.wait()
        wgb[...] = sg[...].astype(BF16)
        wub[...] = su[...].astype(BF16)
        wdb[...] = sd[...].astype(BF16)

        @pl.when(nx_ref[i] >= 0)
        def _():
            for cp in fetch(nx_ref[i]):
                cp.start(priority=1)

    @pl.when(valid)
    def _():
        xs = _unpack_bf16_pairs(_load_row_planes(xs_ref, xs_ref.shape[0] // TE_ROWS))
        g = jnp.dot(xs, wgb[...], preferred_element_type=F32)
        u = jnp.dot(xs, wub[...], preferred_element_type=F32)
        hid = (g * _sigmoid(g) * u).astype(BF16)
        y = jnp.dot(hid, wdb[...], preferred_element_type=F32)
        _store_row_planes(y_ref, _pack_bf16_pairs(y))

    @pl.when(jnp.logical_not(valid))
    def _():
        y_ref[...] = jnp.zeros(y_ref.shape, U32)


def _experts(tile_expert, n_valid, next_expert, xs, w_gate, w_up, w_down):
    n_rows, xp, _ = xs.shape
    ne, d, ff = w_gate.shape
    te = TE_ROWS
    in_map = lambda i, te_ref, nv_ref, nx_ref: (jnp.minimum(i, nv_ref[0] - 1), 0)
    out_map = lambda i, te_ref, nv_ref, nx_ref: (i, 0)
    hbm = pl.BlockSpec(memory_space=pl.ANY)
    y = pl.pallas_call(
        _experts_kernel,
        grid_spec=pltpu.PrefetchScalarGridSpec(
            num_scalar_prefetch=3,
            grid=(n_rows // te,),
            in_specs=[pl.BlockSpec((te * xp, LANES), in_map), hbm, hbm, hbm],
            out_specs=pl.BlockSpec((te * xp, LANES), out_map),
            scratch_shapes=[pltpu.VMEM((d, ff), F32), pltpu.VMEM((d, ff), F32), pltpu.VMEM((ff, d), F32),
                            pltpu.VMEM((d, ff), BF16), pltpu.VMEM((d, ff), BF16), pltpu.VMEM((ff, d), BF16),
                            pltpu.SemaphoreType.DMA((3,))],
        ),
        out_shape=jax.ShapeDtypeStruct((n_rows * xp, LANES), U32),
        compiler_params=_cparams(("arbitrary",)),
        name="experts",
    )(tile_expert, n_valid, next_expert, xs.reshape(n_rows * xp, LANES), w_gate, w_up, w_down)
    return y.reshape(n_rows, xp, LANES)


def _combine_kernel(slots_ref, y_hbm, h_ref, wts_ref, wsg_ref, wsu_ref, wsd_ref, g_ref, b_ref,
                    o_ref, gbuf, rbuf, shbuf, sem, *, alpha):
    tm, d = o_ref.shape
    planes = gbuf.shape[3]
    i = pl.program_id(0)
    cur = i % 2
    grp = SUBLANES

    def issue_rows(step, t0, n):
        buf = step % 2
        base = step * (tm * TOP_K_PAD)
        for u in range(n):
            for k in range(TOP_K):
                slot = slots_ref[base + (t0 + u) * TOP_K_PAD + k]
                pltpu.make_async_copy(y_hbm.at[slot], gbuf.at[buf, k, t0 + u], sem.at[buf]).start(priority=k % 2)

    @pl.when(i == 0)
    def _():
        def first(t, c):
            issue_rows(0, t, 1)
            return c
        lax.fori_loop(0, tm, first, 0, unroll=ISSUE_UNROLL)

    hb = h_ref[...].astype(BF16)
    sg = jnp.dot(hb, wsg_ref[...], preferred_element_type=F32)
    su = jnp.dot(hb, wsu_ref[...], preferred_element_type=F32)
    shbuf[...] = jnp.dot((sg * _sigmoid(sg) * su).astype(BF16), wsd_ref[...], preferred_element_type=F32)
    _wait_rows(y_hbm.at[pl.ds(0, WAIT_ROWS)], gbuf.at[cur, 0, pl.ds(0, WAIT_ROWS)], sem.at[cur], tm * TOP_K)

    def group(g, prefetch):
        t0 = pl.multiple_of(g * grp, grp)
        packed = [gbuf[cur, k, pl.ds(t0, grp)] for k in range(TOP_K)]
        w = wts_ref[pl.ds(t0, grp)]
        resid = alpha * h_ref[pl.ds(t0, grp), :] + shbuf[pl.ds(t0, grp), :]
        if prefetch:
            issue_rows(i + 1, t0, grp)
        r_lo = r_hi = None
        for k in range(TOP_K):
            lo = w[:, k:k + 1, :] * pltpu.bitcast(packed[k] << 16, F32)
            hi = w[:, k:k + 1, :] * pltpu.bitcast(packed[k] & jnp.uint32(0xFFFF0000), F32)
            r_lo = lo if r_lo is None else r_lo + lo
            r_hi = hi if r_hi is None else r_hi + hi
        p0 = pl.multiple_of(t0 * planes, grp * planes)
        rbuf[0, pl.ds(p0, grp * planes), :] = r_lo.reshape(grp * planes, LANES)
        rbuf[1, pl.ds(p0, grp * planes), :] = r_hi.reshape(grp * planes, LANES)
        routed = jnp.concatenate([rbuf[half, pl.ds(p0 + j, grp, stride=planes), :]
                                  for half in range(2) for j in range(planes)], axis=1)
        o_ref[pl.ds(t0, grp), :] = _layer_norm(resid + routed, g_ref[...], b_ref[...])

    has_next = i + 1 < pl.num_programs(0)

    @pl.when(has_next)
    def _():
        lax.fori_loop(0, tm // grp, lambda g, c: (group(g, True), c)[1], 0, unroll=GROUP_UNROLL)

    @pl.when(jnp.logical_not(has_next))
    def _():
        lax.fori_loop(0, tm // grp, lambda g, c: (group(g, False), c)[1], 0, unroll=GROUP_UNROLL)


def _combine(slots_flat, y, h, wts_planes, wsg, wsu, wsd, g, b, alpha):
    t, d = h.shape
    tm = TM_COMB
    planes = y.shape[1]
    full = lambda a: pl.BlockSpec(a.shape, lambda i, s: (0,) * a.ndim)
    once = lambda a: pl.BlockSpec(a.shape, lambda i, s: (0,) * a.ndim, pipeline_mode=pl.Buffered(1))
    return pl.pallas_call(
        functools.partial(_combine_kernel, alpha=alpha),
        grid_spec=pltpu.PrefetchScalarGridSpec(
            num_scalar_prefetch=1,
            grid=(t // tm,),
            in_specs=[pl.BlockSpec(memory_space=pl.ANY),
                      pl.BlockSpec((tm, d), lambda i, s: (i, 0)),
                      pl.BlockSpec((tm, TOP_K_PAD, LANES), lambda i, s: (i, 0, 0)),
                      once(wsg), once(wsu), once(wsd), full(g), full(b)],
            out_specs=pl.BlockSpec((tm, d), lambda i, s: (i, 0)),
            scratch_shapes=[pltpu.VMEM((2, TOP_K, tm, planes, LANES), U32),
                            pltpu.VMEM((2, tm * planes, LANES), F32), pltpu.VMEM((tm, d), F32),
                            pltpu.SemaphoreType.DMA((2,))],
        ),
        out_shape=jax.ShapeDtypeStruct((t, d), F32),
        compiler_params=_cparams(("arbitrary",)),
        name="combine",
    )(slots_flat, y, h, wts_planes, wsg, wsu, wsd, g, b)


def _prep_attention_weights(w_in, w_uq, w_ukv):
    d = w_in.shape[0]
    ql = w_uq.shape[0]
    kvl = w_ukv.shape[0]
    half = QK_ROPE_DIM // 2
    pad = LANES - QK_ROPE_DIM
    w_kr = w_in[:, ql + kvl:ql + kvl + QK_ROPE_DIM]
    w_kr_rot = jnp.concatenate([-w_kr[:, half:], w_kr[:, :half]], axis=1)
    zpad = jnp.zeros((d, pad), w_in.dtype)
    wa = jnp.concatenate([w_in[:, :ql + kvl], w_kr, zpad, w_kr_rot, zpad], axis=1).astype(BF16)
    wc = w_in[:, ql + kvl + QK_ROPE_DIM:].astype(BF16)
    uq = w_uq.reshape(ql, N_HEADS, QK_NOPE_DIM + QK_ROPE_DIM)
    uq_nope, uq_rope = uq[..., :QK_NOPE_DIM], uq[..., QK_NOPE_DIM:]
    zq = jnp.zeros((ql, N_HEADS, pad), w_uq.dtype)
    wuq = jnp.concatenate([uq_nope, uq_rope, zq], axis=-1).reshape(ql, N_HEADS * HEAD_QK_PAD).astype(BF16)
    uq_rot = jnp.concatenate([-uq_rope[..., half:], uq_rope[..., :half], zq], axis=-1)
    wuqr = uq_rot.reshape(ql, N_HEADS * LANES).astype(BF16)
    ukv = w_ukv.reshape(kvl, N_HEADS, QK_NOPE_DIM + V_HEAD_DIM)
    wukv = jnp.concatenate([ukv[..., :QK_NOPE_DIM].reshape(kvl, -1), ukv[..., QK_NOPE_DIM:].reshape(kvl, -1)],
                           axis=1).astype(BF16)
    return wa, wc, wuq, wuqr, wukv


def _layer(h, pos2, invf4, batch, seq, alpha, w_in, q_norm_g, w_uq, kv_norm_g, w_ukv, conv_w, conv_b,
           conv_ln_g, conv_ln_b, w_o, ln1_g, ln1_b, w_router, router_bias, w_gate, w_up, w_down,
           ws_gate, ws_up, ws_down, ln2_g, ln2_b):
    t, d = h.shape
    row = lambda a: a.reshape(1, -1)
    wa, wc, wuq, wuqr, wukv = _prep_attention_weights(w_in, w_uq, w_ukv)
    q, k, v = _qkv(h, pos2, invf4, wa, row(q_norm_g), row(kv_norm_g), wuq, wuqr, wukv)
    conv = _conv(h, wc, conv_w, row(conv_b), row(conv_ln_g), row(conv_ln_b), seq)
    attn = _attn(q, k, v, batch, seq)
    aw = attn.shape[1]
    wr32 = jnp.pad(w_router.astype(F32), ((0, 0), (0, LANES - N_EXPERTS)))
    wr_hi = wr32.astype(BF16)
    wr_lo = (wr32 - wr_hi.astype(F32)).astype(BF16)
    wr = jnp.concatenate([wr_hi, wr_hi, wr_lo], axis=0)
    h1, h1p, logits_t = _oproj(attn, conv, h, w_o[:aw].astype(BF16), w_o[aw:].astype(BF16),
                               row(ln1_g), row(ln1_b), wr, alpha)
    return _moe_ffn(h1, h1p, logits_t, alpha, router_bias, w_gate, w_up, w_down,
                    ws_gate, ws_up, ws_down, ln2_g, ln2_b)


def _moe_ffn(h1, h1p, logits_t, alpha, router_bias, w_gate, w_up, w_down, ws_gate, ws_up, ws_down, ln2_g, ln2_b):
    t, d = h1.shape
    row = lambda a: a.reshape(1, -1)
    eidx, rank, wts, cnt = _route(logits_t, router_bias.astype(F32).reshape(N_EXPERTS, 1))
    counts = cnt[:, 0].astype(I32)
    padded = (counts + TE_ROWS - 1) // TE_ROWS * TE_ROWS
    pend = jnp.cumsum(padded)
    poff = pend - padded
    n_rows = t * TOP_K + N_EXPERTS * TE_ROWS
    n_tiles = n_rows // TE_ROWS
    n_valid = (pend[-1] // TE_ROWS).astype(I32).reshape(1)
    tile_row = jnp.minimum(jnp.arange(n_tiles, dtype=I32), n_valid[0] - 1) * TE_ROWS
    tile_expert = jnp.sum((pend[None, :] <= tile_row[:, None]).astype(I32), axis=1)
    tile_expert = jnp.minimum(tile_expert, N_EXPERTS - 1)
    tiles = jnp.arange(n_tiles, dtype=I32)
    last_tile = jnp.where(padded > 0, pend // TE_ROWS - 1, -1)
    zfill = ((tiles >= n_valid[0]) | jnp.any(tiles[:, None] == last_tile[None, :], axis=1)).astype(I32)
    ids = jnp.arange(N_EXPERTS, dtype=I32)
    later = (ids[None, :] > ids[:, None]) & (padded[None, :] > 0)
    next_of = jnp.min(jnp.where(later, ids[None, :], N_EXPERTS), axis=1)
    next_of = jnp.where(next_of == N_EXPERTS, -1, next_of)
    next_expert = jnp.sum(jnp.where(tile_expert[:, None] == ids[None, :], next_of[None, :], 0), axis=1).astype(I32)
    slot_base = jnp.sum(jnp.where(eidx[None] == ids[:, None, None], poff[:, None, None], 0), axis=0)
    slots_flat = (slot_base + rank).T.reshape(-1)
    xs = _dispatch(zfill, slots_flat, h1p.reshape(t, -1, LANES), n_rows)
    y = _experts(tile_expert, n_valid, next_expert, xs, w_gate, w_up, w_down)
    wts_planes = jnp.broadcast_to(wts.T[:, :, None], (t, TOP_K_PAD, LANES))
    return _combine(slots_flat, y, h1, wts_planes, ws_gate.astype(BF16), ws_up.astype(BF16),
                    ws_down.astype(BF16), row(ln2_g), row(ln2_b), alpha)


def kernel(x, positions, w_in, q_norm_g, w_uq, kv_norm_g, w_ukv, conv_w, conv_b, conv_ln_g, conv_ln_b, w_o, ln1_g, ln1_b, w_router, router_bias, w_gate, w_up, w_down, ws_gate, ws_up, ws_down, ln2_g, ln2_b):
    batch, seq, d = x.shape
    depth = w_in.shape[0]
    alpha = (2.0 * depth) ** 0.25
    inv_freq = ROPE_BASE ** (-jnp.arange(0, QK_ROPE_DIM, 2, dtype=F32) / QK_ROPE_DIM)
    invf4 = jnp.tile(inv_freq, LANES // inv_freq.shape[0]).reshape(1, LANES)
    pos2 = positions.reshape(batch * seq, 1)
    h = x.reshape(batch * seq, d)
    for l in range(depth):
        h = _layer(h, pos2, invf4, batch, seq, alpha, w_in[l], q_norm_g[l], w_uq[l], kv_norm_g[l], w_ukv[l],
                   conv_w[l], conv_b[l], conv_ln_g[l], conv_ln_b[l], w_o[l], ln1_g[l], ln1_b[l],
                   w_router[l], router_bias[l], w_gate[l], w_up[l], w_down[l],
                   ws_gate[l], ws_up[l], ws_down[l], ln2_g[l], ln2_b[l])
    return h.reshape(batch, seq, d)
```

```python
import functools

import jax
import jax.numpy as jnp
from jax import lax
from jax.experimental import pallas as pl
from jax.experimental.pallas import tpu as pltpu

F32 = jnp.float32
BF16 = jnp.bfloat16
I32 = jnp.int32
U32 = jnp.uint32

N_HEADS = 8
QK_NOPE_DIM = 128
QK_ROPE_DIM = 64
V_HEAD_DIM = 128
HEAD_QK_PAD = 256
CONV_KERNEL = 31
N_EXPERTS = 64
TOP_K = 6
TOP_K_PAD = 8
N_EXPERT_GROUPS = 8
GROUP_SIZE = N_EXPERTS // N_EXPERT_GROUPS
TOPK_GROUPS = 4
ROUTED_SCALE = 2.5
ROPE_BASE = 10000.0
LN_EPS = 1e-5
RMS_EPS = 1e-6

LANES = 128
SUBLANES = 8
CONV_HALO = 32

TM_PROJ = 256
TM_CONV = 512
TQ_ATTN = 512
TK_ATTN = 512
HEADS_PER_ATTN_STEP = 4
TM_ROUTE = 512
TE_ROWS = 256
TM_COMB = 256
TC_DISP = 1024
WAIT_ROWS = 128
ISSUE_UNROLL = 4
GROUP_UNROLL = 8
CONV_ROWS = 64
CONV_COLS = 256
CONV_SHIFT_BUFS = 4
VMEM_LIMIT = 56 * 1024 * 1024


def _cparams(sem):
    return pltpu.CompilerParams(dimension_semantics=sem, vmem_limit_bytes=VMEM_LIMIT)


def _sigmoid(v):
    return 1.0 / (1.0 + jnp.exp(-v))


def _layer_norm(v, g, b):
    mu = jnp.mean(v, axis=-1, keepdims=True)
    d = v - mu
    var = jnp.mean(d * d, axis=-1, keepdims=True)
    return d * lax.rsqrt(var + LN_EPS) * g + b


def _rms_norm(v, g):
    ms = jnp.mean(v * v, axis=-1, keepdims=True)
    return v * lax.rsqrt(ms + RMS_EPS) * g


def _qkv_kernel(x_ref, pos_ref, invf_ref, wa_ref, qg_ref, kvg_ref, wuq_ref, wuqr_ref, wukv_ref,
                qt_ref, k_ref, vt_ref):
    ql = qg_ref.shape[1]
    kvl = kvg_ref.shape[1]
    xb = x_ref[...].astype(BF16)
    lat = jnp.dot(xb, wa_ref[...], preferred_element_type=F32)
    ang = pos_ref[...].astype(F32) * invf_ref[...]
    cos = jnp.cos(ang)
    sin = jnp.sin(ang)
    cq = _rms_norm(lat[:, :ql], qg_ref[...]).astype(BF16)
    ckv = _rms_norm(lat[:, ql:ql + kvl], kvg_ref[...]).astype(BF16)
    kr = lat[:, ql + kvl:ql + kvl + LANES] * cos + lat[:, ql + kvl + LANES:ql + kvl + 2 * LANES] * sin
    kr = kr.astype(BF16)
    q = jnp.dot(cq, wuq_ref[...], preferred_element_type=F32)
    qrot = jnp.dot(cq, wuqr_ref[...], preferred_element_type=F32)
    kv = jnp.dot(ckv, wukv_ref[...], preferred_element_type=F32)
    for h in range(N_HEADS):
        c0 = h * HEAD_QK_PAD
        qt_ref[c0:c0 + LANES, :] = q[:, c0:c0 + LANES].T.astype(BF16)
        qt_ref[c0 + LANES:c0 + 2 * LANES, :] = (
            q[:, c0 + LANES:c0 + 2 * LANES] * cos + qrot[:, h * LANES:(h + 1) * LANES] * sin).T.astype(BF16)
        k_ref[:, c0:c0 + LANES] = kv[:, h * LANES:(h + 1) * LANES].astype(BF16)
        k_ref[:, c0 + LANES:c0 + 2 * LANES] = kr
    vt_ref[...] = kv[:, N_HEADS * QK_NOPE_DIM:].T.astype(BF16)


def _qkv(x2, pos2, invf4, wa, qg, kvg, wuq, wuqr, wukv):
    t, d = x2.shape
    tm = TM_PROJ
    full = lambda a: pl.BlockSpec(a.shape, lambda i: (0,) * a.ndim)
    return pl.pallas_call(
        _qkv_kernel,
        grid=(t // tm,),
        in_specs=[pl.BlockSpec((tm, d), lambda i: (i, 0)),
                  pl.BlockSpec((tm, 1), lambda i: (i, 0)),
                  full(invf4), full(wa), full(qg), full(kvg), full(wuq), full(wuqr), full(wukv)],
        out_specs=[pl.BlockSpec((N_HEADS * HEAD_QK_PAD, tm), lambda i: (0, i)),
                   pl.BlockSpec((tm, N_HEADS * HEAD_QK_PAD), lambda i: (i, 0)),
                   pl.BlockSpec((N_HEADS * V_HEAD_DIM, tm), lambda i: (0, i))],
        out_shape=[jax.ShapeDtypeStruct((N_HEADS * HEAD_QK_PAD, t), BF16),
                   jax.ShapeDtypeStruct((t, N_HEADS * HEAD_QK_PAD), BF16),
                   jax.ShapeDtypeStruct((N_HEADS * V_HEAD_DIM, t), BF16)],
        compiler_params=_cparams(("arbitrary",)),
        name="qkv",
    )(x2, pos2, invf4, wa, qg, kvg, wuq, wuqr, wukv)


def _conv_kernel(x_ref, wc_ref, cw_ref, cb_ref, g_ref, b_ref, o_ref, ubuf, ybuf, shbuf, *, tiles_per_seq):
    tm, cw = o_ref.shape
    i = pl.program_id(0)
    xb = x_ref[...].astype(BF16)

    @pl.when(i % tiles_per_seq == 0)
    def _():
        ubuf[0:CONV_HALO, :] = jnp.zeros((CONV_HALO, cw), F32)

    shift0 = CONV_HALO - (CONV_KERNEL - 1)
    n_sh = shbuf.shape[0]
    chunk_no = 0
    for c0 in range(0, cw, CONV_COLS):
        a = jnp.dot(xb, wc_ref[:, c0:c0 + CONV_COLS], preferred_element_type=F32)
        gate = jnp.dot(xb, wc_ref[:, cw + c0:cw + c0 + CONV_COLS], preferred_element_type=F32)
        ubuf[CONV_HALO:CONV_HALO + tm, c0:c0 + CONV_COLS] = a * _sigmoid(gate)
        for r0 in range(0, tm, CONV_ROWS):
            sh = shbuf.at[chunk_no % n_sh]
            chunk_no += 1
            win = ubuf[r0:r0 + CONV_ROWS + CONV_HALO, c0:c0 + CONV_COLS]
            for r in range(1, SUBLANES):
                sh[r] = win[r:r + CONV_ROWS + CONV_HALO - SUBLANES, :]
            acc = jnp.zeros((CONV_ROWS, CONV_COLS), F32)
            for k in range(CONV_KERNEL):
                wk = cw_ref[k:k + 1, c0:c0 + CONV_COLS]
                r, j = (shift0 + k) % SUBLANES, (shift0 + k) // SUBLANES
                if r == 0:
                    tap = win[j * SUBLANES:j * SUBLANES + CONV_ROWS, :]
                else:
                    tap = sh[r, j * SUBLANES:j * SUBLANES + CONV_ROWS, :]
                acc = acc + wk * tap
            ybuf[r0:r0 + CONV_ROWS, c0:c0 + CONV_COLS] = acc
    ubuf[0:CONV_HALO, :] = ubuf[tm:tm + CONV_HALO, :]
    y = _layer_norm(ybuf[...] + cb_ref[...], g_ref[...], b_ref[...])
    o_ref[...] = (y * _sigmoid(y)).astype(BF16)


def _conv(x2, wc, cw, cb, g, b, seq):
    t, d = x2.shape
    tm = TM_CONV
    c = cw.shape[1]
    full = lambda a: pl.BlockSpec(a.shape, lambda i: (0,) * a.ndim)
    return pl.pallas_call(
        functools.partial(_conv_kernel, tiles_per_seq=seq // tm),
        grid=(t // tm,),
        in_specs=[pl.BlockSpec((tm, d), lambda i: (i, 0)), full(wc), full(cw), full(cb), full(g), full(b)],
        out_specs=pl.BlockSpec((tm, c), lambda i: (i, 0)),
        out_shape=jax.ShapeDtypeStruct((t, c), BF16),
        scratch_shapes=[pltpu.VMEM((tm + CONV_HALO, c), F32), pltpu.VMEM((tm, c), F32),
                        pltpu.VMEM((CONV_SHIFT_BUFS, SUBLANES, CONV_ROWS + CONV_HALO - SUBLANES, CONV_COLS), F32)],
        compiler_params=_cparams(("arbitrary",)),
        name="conv",
    )(x2, wc, cw, cb, g, b)


def _attn_kernel(qt_ref, k_ref, vt_ref, o_ref, *, scale, tk):
    tq = qt_ref.shape[1]
    heads = qt_ref.shape[0] // HEAD_QK_PAD
    i = pl.program_id(2)
    c = scale * 1.4426950408889634
    n_full = (i * tq) // tk
    q_off = i * tq - n_full * tk

    def block(j, carry, masked):
        r0 = pl.multiple_of(j * tk, tk)
        out = []
        scores = []
        for h in range(heads):
            qt = qt_ref[h * HEAD_QK_PAD:(h + 1) * HEAD_QK_PAD, :]
            kb = k_ref[pl.ds(r0, tk), h * HEAD_QK_PAD:(h + 1) * HEAD_QK_PAD]
            scores.append(jnp.dot(kb, qt, preferred_element_type=F32))
        for h, (m, l, acc) in enumerate(carry):
            vtb = vt_ref[h * V_HEAD_DIM:(h + 1) * V_HEAD_DIM, pl.ds(r0, tk)]
            st = scores[h]
            if masked:
                key = lax.broadcasted_iota(I32, (tk, tq), 0)
                qry = lax.broadcasted_iota(I32, (tk, tq), 1) + q_off
                st = jnp.where(key <= qry, st, -jnp.inf)
            m_blk = jnp.max(jnp.max(st.reshape(SUBLANES, tk // SUBLANES, tq), axis=0), axis=0, keepdims=True)
            m_new = jnp.maximum(m, m_blk)
            alpha = jnp.exp2((m - m_new) * c)
            pt = jnp.exp2((st - m_new) * c)
            l_blk = jnp.sum(jnp.sum(pt.reshape(SUBLANES, tk // SUBLANES, tq), axis=0), axis=0, keepdims=True)
            l = alpha * l + l_blk
            acc = alpha * acc + jnp.dot(vtb, pt.astype(BF16), preferred_element_type=F32)
            out.append((m_new, l, acc))
        return tuple(out)

    init = tuple((jnp.full((1, tq), -jnp.inf, F32), jnp.zeros((1, tq), F32),
                  jnp.zeros((V_HEAD_DIM, tq), F32)) for _ in range(heads))
    carry = lax.fori_loop(0, n_full, lambda j, cr: block(j, cr, False), init)
    final = block(n_full, carry, True)
    for h, (_, l, acc) in enumerate(final):
        o_ref[:, h * V_HEAD_DIM:(h + 1) * V_HEAD_DIM] = (acc / l).T.astype(BF16)


def _attn(q, k, v, batch, seq):
    tq = TQ_ATTN
    nq = seq // tq
    hb = HEADS_PER_ATTN_STEP
    scale = (QK_NOPE_DIM + QK_ROPE_DIM) ** -0.5
    return pl.pallas_call(
        functools.partial(_attn_kernel, scale=scale, tk=TK_ATTN),
        grid=(batch, N_HEADS // hb, nq),
        in_specs=[pl.BlockSpec((hb * HEAD_QK_PAD, tq), lambda b, h, i: (h, b * nq + i)),
                  pl.BlockSpec((seq, hb * HEAD_QK_PAD), lambda b, h, i: (b, h)),
                  pl.BlockSpec((hb * V_HEAD_DIM, seq), lambda b, h, i: (h, b))],
        out_specs=pl.BlockSpec((tq, hb * V_HEAD_DIM), lambda b, h, i: (b * nq + i, h)),
        out_shape=jax.ShapeDtypeStruct((batch * seq, N_HEADS * V_HEAD_DIM), BF16),
        compiler_params=_cparams(("arbitrary", "arbitrary", "arbitrary")),
        name="attn",
    )(q, k, v)


def _pack_bf16_pairs(v):
    c = v.shape[1] // 2
    lo = pltpu.bitcast(v[:, :c].astype(BF16).astype(F32), U32)
    hi = pltpu.bitcast(v[:, c:].astype(BF16).astype(F32), U32)
    return (hi & jnp.uint32(0xFFFF0000)) | (lo >> 16)


def _unpack_bf16_pairs(p):
    lo = pltpu.bitcast(p << 16, F32).astype(BF16)
    hi = pltpu.bitcast(p & jnp.uint32(0xFFFF0000), F32).astype(BF16)
    return jnp.concatenate([lo, hi], axis=1)


def _store_row_planes(ref, v):
    n = v.shape[0]
    p = ref.shape[0] // n
    for j in range(p):
        ref[pl.ds(j, n, stride=p), :] = v[:, j * LANES:(j + 1) * LANES]


def _load_row_planes(ref, p):
    n = ref.shape[0] // p
    return jnp.concatenate([ref[pl.ds(j, n, stride=p), :] for j in range(p)], axis=1)


def _oproj_kernel(a_ref, c_ref, x_ref, woa_ref, woc_ref, g_ref, b_ref, wr_ref,
                  h_ref, hp_ref, lg_ref, *, alpha):
    mix = jnp.dot(a_ref[...], woa_ref[...], preferred_element_type=F32)
    mix = mix + jnp.dot(c_ref[...], woc_ref[...], preferred_element_type=F32)
    h = _layer_norm(alpha * x_ref[...] + mix, g_ref[...], b_ref[...])
    h_ref[...] = h
    _store_row_planes(hp_ref, _pack_bf16_pairs(h))
    h_hi = h.astype(BF16)
    h_lo = (h - h_hi.astype(F32)).astype(BF16)
    logits = jnp.dot(jnp.concatenate([h_hi, h_lo, h_hi], axis=1), wr_ref[...], preferred_element_type=F32)
    lg_ref[...] = logits.T[:N_EXPERTS, :]


def _oproj(attn, conv, x2, woa, woc, g, b, wr, alpha):
    t, d = x2.shape
    tm = TM_PROJ
    full = lambda a: pl.BlockSpec(a.shape, lambda i: (0,) * a.ndim)
    return pl.pallas_call(
        functools.partial(_oproj_kernel, alpha=alpha),
        grid=(t // tm,),
        in_specs=[pl.BlockSpec((tm, attn.shape[1]), lambda i: (i, 0)),
                  pl.BlockSpec((tm, conv.shape[1]), lambda i: (i, 0)),
                  pl.BlockSpec((tm, d), lambda i: (i, 0)),
                  full(woa), full(woc), full(g), full(b), full(wr)],
        out_specs=[pl.BlockSpec((tm, d), lambda i: (i, 0)),
                   pl.BlockSpec((tm * (d // 2 // LANES), LANES), lambda i: (i, 0)),
                   pl.BlockSpec((N_EXPERTS, tm), lambda i: (0, i))],
        out_shape=[jax.ShapeDtypeStruct((t, d), F32),
                   jax.ShapeDtypeStruct((t * (d // 2 // LANES), LANES), U32),
                   jax.ShapeDtypeStruct((N_EXPERTS, t), F32)],
        compiler_params=_cparams(("arbitrary",)),
        name="oproj",
    )(attn, conv, x2, woa, woc, g, b, wr)


def _route_kernel(lg_ref, bias_ref, eidx_ref, rank_ref, wts_ref, cnt_ref, carry_ref):
    ne, tm = lg_ref.shape
    i = pl.program_id(0)

    @pl.when(i == 0)
    def _():
        carry_ref[...] = jnp.zeros(carry_ref.shape, F32)

    scores = _sigmoid(lg_ref[...])
    biased = scores + bias_ref[...]
    neg = -jnp.inf
    sub8 = lax.broadcasted_iota(I32, (GROUP_SIZE, tm), 0)
    gscore = []
    for g in range(N_EXPERT_GROUPS):
        blk = biased[g * GROUP_SIZE:(g + 1) * GROUP_SIZE, :]
        m1 = jnp.max(blk, axis=0, keepdims=True)
        i1 = jnp.min(jnp.where(blk == m1, sub8, GROUP_SIZE), axis=0, keepdims=True)
        m2 = jnp.max(jnp.where(sub8 == i1, neg, blk), axis=0, keepdims=True)
        gscore.append(m1 + m2)
    kept = []
    for g in range(N_EXPERT_GROUPS):
        beat = jnp.zeros((1, tm), I32)
        for o in range(N_EXPERT_GROUPS):
            if o < g:
                beat = beat + (gscore[o] >= gscore[g]).astype(I32)
            elif o > g:
                beat = beat + (gscore[o] > gscore[g]).astype(I32)
        kept.append(jnp.where(beat < TOPK_GROUPS, biased[g * GROUP_SIZE:(g + 1) * GROUP_SIZE, :], neg))
    cur = jnp.concatenate(kept, axis=0)
    sub = lax.broadcasted_iota(I32, (ne, tm), 0)
    sel_any = jnp.zeros((ne, tm), F32)
    picks = []
    wsum = jnp.zeros((1, tm), F32)
    for k in range(TOP_K):
        m = jnp.max(cur, axis=0, keepdims=True)
        ei = jnp.min(jnp.where(cur == m, sub, ne), axis=0, keepdims=True)
        sel = sub == ei
        w = jnp.sum(jnp.where(sel, scores, 0.0), axis=0, keepdims=True)
        cur = jnp.where(sel, neg, cur)
        sel_any = sel_any + sel.astype(F32)
        wsum = wsum + w
        picks.append((ei, sel, w))
    r = lax.broadcasted_iota(I32, (tm, tm), 0)
    c = lax.broadcasted_iota(I32, (tm, tm), 1)
    upper = (r <= c).astype(BF16)
    cum = jnp.dot(sel_any.astype(BF16), upper, preferred_element_type=F32)
    carry = carry_ref[:, 0:1]
    excl = cum - sel_any + carry
    total = carry + jnp.sum(sel_any, axis=1, keepdims=True)
    carry_ref[...] = jnp.broadcast_to(total, carry_ref.shape)
    cnt_ref[...] = jnp.broadcast_to(total, cnt_ref.shape)
    denom = wsum + 1e-20
    eidx_ref[...] = jnp.zeros(eidx_ref.shape, I32)
    rank_ref[...] = jnp.zeros(rank_ref.shape, I32)
    wts_ref[...] = jnp.zeros(wts_ref.shape, F32)
    for k, (ei, sel, w) in enumerate(picks):
        eidx_ref[k:k + 1, :] = ei
        rank_ref[k:k + 1, :] = jnp.sum(jnp.where(sel, excl, 0.0), axis=0, keepdims=True).astype(I32)
        wts_ref[k:k + 1, :] = w / denom * ROUTED_SCALE


def _route(logits_t, bias):
    ne, t = logits_t.shape
    tm = TM_ROUTE
    return pl.pallas_call(
        _route_kernel,
        grid=(t // tm,),
        in_specs=[pl.BlockSpec((ne, tm), lambda i: (0, i)), pl.BlockSpec((ne, 1), lambda i: (0, 0))],
        out_specs=[pl.BlockSpec((TOP_K_PAD, tm), lambda i: (0, i)),
                   pl.BlockSpec((TOP_K_PAD, tm), lambda i: (0, i)),
                   pl.BlockSpec((TOP_K_PAD, tm), lambda i: (0, i)),
                   pl.BlockSpec((ne, LANES), lambda i: (0, 0))],
        out_shape=[jax.ShapeDtypeStruct((TOP_K_PAD, t), I32),
                   jax.ShapeDtypeStruct((TOP_K_PAD, t), I32),
                   jax.ShapeDtypeStruct((TOP_K_PAD, t), F32),
                   jax.ShapeDtypeStruct((ne, LANES), F32)],
        scratch_shapes=[pltpu.VMEM((ne, LANES), F32)],
        compiler_params=_cparams(("arbitrary",)),
        name="route",
    )(logits_t, bias)


def _wait_rows(src_rows, dst_rows, sem, n_rows):
    def body(_, c):
        pltpu.make_async_copy(src_rows, dst_rows, sem).wait()
        return c
    lax.fori_loop(0, n_rows // WAIT_ROWS, body, 0)


def _dispatch_kernel(zfill_ref, slots_ref, hp_ref, xs_hbm, zero_vmem, sem, zsem):
    i = pl.program_id(0)
    te = zero_vmem.shape[0]
    tc = hp_ref.shape[0]
    n_tiles = xs_hbm.shape[0] // te

    def zero_copy(tile):
        return pltpu.make_async_copy(zero_vmem, xs_hbm.at[pl.ds(pl.multiple_of(tile * te, te), te)], zsem)

    @pl.when(i == 0)
    def _():
        zero_vmem[...] = jnp.zeros(zero_vmem.shape, U32)

        def zstart(tile, c):
            @pl.when(zfill_ref[tile] != 0)
            def _():
                zero_copy(tile).start()
            return c

        def zwait(tile, c):
            @pl.when(zfill_ref[tile] != 0)
            def _():
                zero_copy(tile).wait()
            return c

        lax.fori_loop(0, n_tiles, zstart, 0)
        lax.fori_loop(0, n_tiles, zwait, 0)

    base = i * (tc * TOP_K_PAD)

    def issue(t, c):
        for k in range(TOP_K):
            slot = slots_ref[base + t * TOP_K_PAD + k]
            pltpu.make_async_copy(hp_ref.at[t], xs_hbm.at[slot], sem).start(priority=k % 2)
        return c

    lax.fori_loop(0, tc, issue, 0, unroll=ISSUE_UNROLL)
    _wait_rows(hp_ref.at[pl.ds(0, WAIT_ROWS)], xs_hbm.at[pl.ds(0, WAIT_ROWS)], sem, tc * TOP_K)


def _dispatch(zfill, slots_flat, hp, n_rows):
    t, planes, lanes = hp.shape
    tc = TC_DISP
    return pl.pallas_call(
        _dispatch_kernel,
        grid_spec=pltpu.PrefetchScalarGridSpec(
            num_scalar_prefetch=2,
            grid=(t // tc,),
            in_specs=[pl.BlockSpec((tc, planes, lanes), lambda i, z, s: (i, 0, 0))],
            out_specs=pl.BlockSpec(memory_space=pl.ANY),
            scratch_shapes=[pltpu.VMEM((TE_ROWS, planes, lanes), U32),
                            pltpu.SemaphoreType.DMA, pltpu.SemaphoreType.DMA],
        ),
        out_shape=jax.ShapeDtypeStruct((n_rows, planes, lanes), U32),
        compiler_params=_cparams(("arbitrary",)),
        name="dispatch",
    )(zfill, slots_flat, hp)


def _experts_kernel(te_ref, nv_ref, nx_ref, xs_ref, wg_hbm, wu_hbm, wd_hbm, y_ref,
                    sg, su, sd, wgb, wub, wdb, sem):
    i = pl.program_id(0)
    e = te_ref[i]
    prev = te_ref[jnp.maximum(i - 1, 0)]
    valid = i < nv_ref[0]

    def fetch(ex):
        return (pltpu.make_async_copy(wg_hbm.at[ex], sg, sem.at[0]),
                pltpu.make_async_copy(wu_hbm.at[ex], su, sem.at[1]),
                pltpu.make_async_copy(wd_hbm.at[ex], sd, sem.at[2]))

    @pl.when(i == 0)
    def _():
        for cp in fetch(e):
            cp.start()

    first = valid & ((i == 0) | (e != prev))

    def ffn():
        xs = _unpack_bf16_pairs(_load_row_planes(xs_ref, xs_ref.shape[0] // TE_ROWS))
        g = jnp.dot(xs, wgb[...], preferred_element_type=F32)
        u = jnp.dot(xs, wub[...], preferred_element_type=F32)
        hid = (g * _sigmoid(g) * u).astype(BF16)
        y = jnp.dot(hid, wdb[...], preferred_element_type=F32)
        _store_row_planes(y_ref, _pack_bf16_pairs(y))

    @pl.when(first)
    def _():
        for cp in fetch(e):
            cp.wait()
        wgb[...] = sg[...].astype(BF16)
        wub[...] = su[...].astype(BF16)
        wdb[...] = sd[...].astype(BF16)
        for cp in fetch(jnp.where(nx_ref[i] >= 0, nx_ref[i], e)):
            cp.start(priority=1)
        ffn()

    @pl.when(valid & jnp.logical_not(first))
    def _():
        ffn()

    @pl.when(i == pl.num_programs(0) - 1)
    def _():
        for cp in fetch(e):
            cp.wait()

    @pl.when(jnp.logical_not(valid))
    def _():
        y_ref[...] = jnp.zeros(y_ref.shape, U32)


def _experts(tile_expert, n_valid, next_expert, xs, w_gate, w_up, w_down):
    n_rows, xp, _ = xs.shape
    ne, d, ff = w_gate.shape
    te = TE_ROWS
    in_map = lambda i, te_ref, nv_ref, nx_ref: (jnp.minimum(i, nv_ref[0] - 1), 0)
    out_map = lambda i, te_ref, nv_ref, nx_ref: (i, 0)
    hbm = pl.BlockSpec(memory_space=pl.ANY)
    y = pl.pallas_call(
        _experts_kernel,
        grid_spec=pltpu.PrefetchScalarGridSpec(
            num_scalar_prefetch=3,
            grid=(n_rows // te,),
            in_specs=[pl.BlockSpec((te * xp, LANES), in_map), hbm, hbm, hbm],
            out_specs=pl.BlockSpec((te * xp, LANES), out_map),
            scratch_shapes=[pltpu.VMEM((d, ff), F32), pltpu.VMEM((d, ff), F32), pltpu.VMEM((ff, d), F32),
                            pltpu.VMEM((d, ff), BF16), pltpu.VMEM((d, ff), BF16), pltpu.VMEM((ff, d), BF16),
                            pltpu.SemaphoreType.DMA((3,))],
        ),
        out_shape=jax.ShapeDtypeStruct((n_rows * xp, LANES), U32),
        compiler_params=_cparams(("arbitrary",)),
        name="experts",
    )(tile_expert, n_valid, next_expert, xs.reshape(n_rows * xp, LANES), w_gate, w_up, w_down)
    return y.reshape(n_rows, xp, LANES)


def _combine_kernel(slots_ref, y_hbm, h_ref, wts_ref, wsg_ref, wsu_ref, wsd_ref, g_ref, b_ref,
                    o_ref, gbuf, rbuf, shbuf, sem, *, alpha):
    tm, d = o_ref.shape
    planes = gbuf.shape[3]
    i = pl.program_id(0)
    cur = i % 2
    grp = SUBLANES

    def issue_rows(step, t0, n):
        buf = step % 2
        base = step * (tm * TOP_K_PAD)
        for u in range(n):
            for k in range(TOP_K):
                slot = slots_ref[base + (t0 + u) * TOP_K_PAD + k]
                pltpu.make_async_copy(y_hbm.at[slot], gbuf.at[buf, k, t0 + u], sem.at[buf]).start(priority=k % 2)

    @pl.when(i == 0)
    def _():
        def first(t, c):
            issue_rows(0, t, 1)
            return c
        lax.fori_loop(0, tm, first, 0, unroll=ISSUE_UNROLL)

    hb = h_ref[...].astype(BF16)
    sg = jnp.dot(hb, wsg_ref[...], preferred_element_type=F32)
    su = jnp.dot(hb, wsu_ref[...], preferred_element_type=F32)
    shbuf[...] = jnp.dot((sg * _sigmoid(sg) * su).astype(BF16), wsd_ref[...], preferred_element_type=F32)
    _wait_rows(y_hbm.at[pl.ds(0, WAIT_ROWS)], gbuf.at[cur, 0, pl.ds(0, WAIT_ROWS)], sem.at[cur], tm * TOP_K)

    def group(g, prefetch):
        t0 = pl.multiple_of(g * grp, grp)
        packed = [gbuf[cur, k, pl.ds(t0, grp)] for k in range(TOP_K)]
        w = wts_ref[pl.ds(t0, grp)]
        resid = alpha * h_ref[pl.ds(t0, grp), :] + shbuf[pl.ds(t0, grp), :]
        if prefetch:
            issue_rows(i + 1, t0, grp)
        r_lo = r_hi = None
        for k in range(TOP_K):
            lo = w[:, k:k + 1, :] * pltpu.bitcast(packed[k] << 16, F32)
            hi = w[:, k:k + 1, :] * pltpu.bitcast(packed[k] & jnp.uint32(0xFFFF0000), F32)
            r_lo = lo if r_lo is None else r_lo + lo
            r_hi = hi if r_hi is None else r_hi + hi
        p0 = pl.multiple_of(t0 * planes, grp * planes)
        rbuf[0, pl.ds(p0, grp * planes), :] = r_lo.reshape(grp * planes, LANES)
        rbuf[1, pl.ds(p0, grp * planes), :] = r_hi.reshape(grp * planes, LANES)
        routed = jnp.concatenate([rbuf[half, pl.ds(p0 + j, grp, stride=planes), :]
                                  for half in range(2) for j in range(planes)], axis=1)
        o_ref[pl.ds(t0, grp), :] = _layer_norm(resid + routed, g_ref[...], b_ref[...])

    has_next = i + 1 < pl.num_programs(0)

    @pl.when(has_next)
    def _():
        lax.fori_loop(0, tm // grp, lambda g, c: (group(g, True), c)[1], 0, unroll=GROUP_UNROLL)

    @pl.when(jnp.logical_not(has_next))
    def _():
        lax.fori_loop(0, tm // grp, lambda g, c: (group(g, False), c)[1], 0, unroll=GROUP_UNROLL)


def _combine(slots_flat, y, h, wts_planes, wsg, wsu, wsd, g, b, alpha):
    t, d = h.shape
    tm = TM_COMB
    planes = y.shape[1]
    full = lambda a: pl.BlockSpec(a.shape, lambda i, s: (0,) * a.ndim)
    once = lambda a: pl.BlockSpec(a.shape, lambda i, s: (0,) * a.ndim, pipeline_mode=pl.Buffered(1))
    return pl.pallas_call(
        functools.partial(_combine_kernel, alpha=alpha),
        grid_spec=pltpu.PrefetchScalarGridSpec(
            num_scalar_prefetch=1,
            grid=(t // tm,),
            in_specs=[pl.BlockSpec(memory_space=pl.ANY),
                      pl.BlockSpec((tm, d), lambda i, s: (i, 0)),
                      pl.BlockSpec((tm, TOP_K_PAD, LANES), lambda i, s: (i, 0, 0)),
                      once(wsg), once(wsu), once(wsd), full(g), full(b)],
            out_specs=pl.BlockSpec((tm, d), lambda i, s: (i, 0)),
            scratch_shapes=[pltpu.VMEM((2, TOP_K, tm, planes, LANES), U32),
                            pltpu.VMEM((2, tm * planes, LANES), F32), pltpu.VMEM((tm, d), F32),
                            pltpu.SemaphoreType.DMA((2,))],
        ),
        out_shape=jax.ShapeDtypeStruct((t, d), F32),
        compiler_params=_cparams(("arbitrary",)),
        name="combine",
    )(slots_flat, y, h, wts_planes, wsg, wsu, wsd, g, b)


def _prep_attention_weights(w_in, w_uq, w_ukv):
    d = w_in.shape[0]
    ql = w_uq.shape[0]
    kvl = w_ukv.shape[0]
    half = QK_ROPE_DIM // 2
    pad = LANES - QK_ROPE_DIM
    w_kr = w_in[:, ql + kvl:ql + kvl + QK_ROPE_DIM]
    w_kr_rot = jnp.concatenate([-w_kr[:, half:], w_kr[:, :half]], axis=1)
    zpad = jnp.zeros((d, pad), w_in.dtype)
    wa = jnp.concatenate([w_in[:, :ql + kvl], w_kr, zpad, w_kr_rot, zpad], axis=1).astype(BF16)
    wc = w_in[:, ql + kvl + QK_ROPE_DIM:].astype(BF16)
    uq = w_uq.reshape(ql, N_HEADS, QK_NOPE_DIM + QK_ROPE_DIM)
    uq_nope, uq_rope = uq[..., :QK_NOPE_DIM], uq[..., QK_NOPE_DIM:]
    zq = jnp.zeros((ql, N_HEADS, pad), w_uq.dtype)
    wuq = jnp.concatenate([uq_nope, uq_rope, zq], axis=-1).reshape(ql, N_HEADS * HEAD_QK_PAD).astype(BF16)
    uq_rot = jnp.concatenate([-uq_rope[..., half:], uq_rope[..., :half], zq], axis=-1)
    wuqr = uq_rot.reshape(ql, N_HEADS * LANES).astype(BF16)
    ukv = w_ukv.reshape(kvl, N_HEADS, QK_NOPE_DIM + V_HEAD_DIM)
    wukv = jnp.concatenate([ukv[..., :QK_NOPE_DIM].reshape(kvl, -1), ukv[..., QK_NOPE_DIM:].reshape(kvl, -1)],
                           axis=1).astype(BF16)
    return wa, wc, wuq, wuqr, wukv


def _layer(h, pos2, invf4, batch, seq, alpha, w_in, q_norm_g, w_uq, kv_norm_g, w_ukv, conv_w, conv_b,
           conv_ln_g, conv_ln_b, w_o, ln1_g, ln1_b, w_router, router_bias, w_gate, w_up, w_down,
           ws_gate, ws_up, ws_down, ln2_g, ln2_b):
    t, d = h.shape
    row = lambda a: a.reshape(1, -1)
    wa, wc, wuq, wuqr, wukv = _prep_attention_weights(w_in, w_uq, w_ukv)
    q, k, v = _qkv(h, pos2, invf4, wa, row(q_norm_g), row(kv_norm_g), wuq, wuqr, wukv)
    conv = _conv(h, wc, conv_w, row(conv_b), row(conv_ln_g), row(conv_ln_b), seq)
    attn = _attn(q, k, v, batch, seq)
    aw = attn.shape[1]
    wr32 = jnp.pad(w_router.astype(F32), ((0, 0), (0, LANES - N_EXPERTS)))
    wr_hi = wr32.astype(BF16)
    wr_lo = (wr32 - wr_hi.astype(F32)).astype(BF16)
    wr = jnp.concatenate([wr_hi, wr_hi, wr_lo], axis=0)
    h1, h1p, logits_t = _oproj(attn, conv, h, w_o[:aw].astype(BF16), w_o[aw:].astype(BF16),
                               row(ln1_g), row(ln1_b), wr, alpha)
    return _moe_ffn(h1, h1p, logits_t, alpha, router_bias, w_gate, w_up, w_down,
                    ws_gate, ws_up, ws_down, ln2_g, ln2_b)


def _moe_ffn(h1, h1p, logits_t, alpha, router_bias, w_gate, w_up, w_down, ws_gate, ws_up, ws_down, ln2_g, ln2_b):
    t, d = h1.shape
    row = lambda a: a.reshape(1, -1)
    eidx, rank, wts, cnt = _route(logits_t, router_bias.astype(F32).reshape(N_EXPERTS, 1))
    counts = cnt[:, 0].astype(I32)
    padded = (counts + TE_ROWS - 1) // TE_ROWS * TE_ROWS
    pend = jnp.cumsum(padded)
    poff = pend - padded
    n_rows = t * TOP_K + N_EXPERTS * TE_ROWS
    n_tiles = n_rows // TE_ROWS
    n_valid = (pend[-1] // TE_ROWS).astype(I32).reshape(1)
    tile_row = jnp.minimum(jnp.arange(n_tiles, dtype=I32), n_valid[0] - 1) * TE_ROWS
    tile_expert = jnp.sum((pend[None, :] <= tile_row[:, None]).astype(I32), axis=1)
    tile_expert = jnp.minimum(tile_expert, N_EXPERTS - 1)
    tiles = jnp.arange(n_tiles, dtype=I32)
    last_tile = jnp.where(padded > 0, pend // TE_ROWS - 1, -1)
    zfill = ((tiles >= n_valid[0]) | jnp.any(tiles[:, None] == last_tile[None, :], axis=1)).astype(I32)
    ids = jnp.arange(N_EXPERTS, dtype=I32)
    later = (ids[None, :] > ids[:, None]) & (padded[None, :] > 0)
    next_of = jnp.min(jnp.where(later, ids[None, :], N_EXPERTS), axis=1)
    next_of = jnp.where(next_of == N_EXPERTS, -1, next_of)
    next_expert = jnp.sum(jnp.where(tile_expert[:, None] == ids[None, :], next_of[None, :], 0), axis=1).astype(I32)
    slot_base = jnp.sum(jnp.where(eidx[None] == ids[:, None, None], poff[:, None, None], 0), axis=0)
    slots_flat = (slot_base + rank).T.reshape(-1)
    xs = _dispatch(zfill, slots_flat, h1p.reshape(t, -1, LANES), n_rows)
    y = _experts(tile_expert, n_valid, next_expert, xs, w_gate, w_up, w_down)
    wts_planes = jnp.broadcast_to(wts.T[:, :, None], (t, TOP_K_PAD, LANES))
    return _combine(slots_flat, y, h1, wts_planes, ws_gate.astype(BF16), ws_up.astype(BF16),
                    ws_down.astype(BF16), row(ln2_g), row(ln2_b), alpha)


def kernel(x, positions, w_in, q_norm_g, w_uq, kv_norm_g, w_ukv, conv_w, conv_b, conv_ln_g, conv_ln_b, w_o, ln1_g, ln1_b, w_router, router_bias, w_gate, w_up, w_down, ws_gate, ws_up, ws_down, ln2_g, ln2_b):
    batch, seq, d = x.shape
    depth = w_in.shape[0]
    alpha = (2.0 * depth) ** 0.25
    inv_freq = ROPE_BASE ** (-jnp.arange(0, QK_ROPE_DIM, 2, dtype=F32) / QK_ROPE_DIM)
    invf4 = jnp.tile(inv_freq, LANES // inv_freq.shape[0]).reshape(1, LANES)
    pos2 = positions.reshape(batch * seq, 1)
    h = x.reshape(batch * seq, d)
    for l in range(depth):
        h = _layer(h, pos2, invf4, batch, seq, alpha, w_in[l], q_norm_g[l], w_uq[l], kv_norm_g[l], w_ukv[l],
                   conv_w[l], conv_b[l], conv_ln_g[l], conv_ln_b[l], w_o[l], ln1_g[l], ln1_b[l],
                   w_router[l], router_bias[l], w_gate[l], w_up[l], w_down[l],
                   ws_gate[l], ws_up[l], ws_down[l], ln2_g[l], ln2_b[l])
    return h.reshape(batch, seq, d)
```

```python
import functools

import jax
import jax.numpy as jnp
import numpy as np
from jax import lax
from jax.experimental import pallas as pl
from jax.experimental.pallas import tpu as pltpu

F32 = jnp.float32
BF16 = jnp.bfloat16
I32 = jnp.int32
U32 = jnp.uint32

N_HEADS = 8
QK_NOPE_DIM = 128
QK_ROPE_DIM = 64
V_HEAD_DIM = 128
HEAD_QK_PAD = 256
CONV_KERNEL = 31
N_EXPERTS = 64
TOP_K = 6
TOP_K_PAD = 8
N_EXPERT_GROUPS = 8
GROUP_SIZE = N_EXPERTS // N_EXPERT_GROUPS
TOPK_GROUPS = 4
ROUTED_SCALE = 2.5
ROPE_BASE = 10000.0
LN_EPS = 1e-5
RMS_EPS = 1e-6
LOG2_E = 1.4426950408889634
BF16_HIGH_HALF = np.uint32(0xFFFF0000)

LANES = 128
SUBLANES = 8
CONV_HALO = 32

TM_PROJ = 256
TM_CONV = 512
TQ_ATTN = 512
TK_ATTN = 512
HEADS_PER_ATTN_STEP = 4
TM_ROUTE = 512
TE_ROWS = 256
TM_COMB = 256
TC_DISP = 1024
WAIT_ROWS = 128
ISSUE_UNROLL = 4
GROUP_UNROLL = 8
CONV_ROWS = 128
CONV_COLS = 256
CONV_SHIFT_BUFS = 4
VMEM_LIMIT = 56 * 1024 * 1024


def _cparams(sem):
    return pltpu.CompilerParams(dimension_semantics=sem, vmem_limit_bytes=VMEM_LIMIT)


def _sigmoid(v):
    return 1.0 / (1.0 + jnp.exp(-v))


def _layer_norm(v, g, b):
    mu = jnp.mean(v, axis=-1, keepdims=True)
    d = v - mu
    var = jnp.mean(d * d, axis=-1, keepdims=True)
    return d * lax.rsqrt(var + LN_EPS) * g + b


def _rms_norm(v, g):
    ms = jnp.mean(v * v, axis=-1, keepdims=True)
    return v * lax.rsqrt(ms + RMS_EPS) * g


def _qkv_kernel(x_ref, pos_ref, invf_ref, wa_ref, qg_ref, kvg_ref, wuq_ref, wuqr_ref, wukv_ref,
                qt_ref, k_ref, vt_ref):
    ql = qg_ref.shape[1]
    kvl = kvg_ref.shape[1]
    xb = x_ref[...].astype(BF16)
    lat = jnp.dot(xb, wa_ref[...], preferred_element_type=F32)
    ang = pos_ref[...].astype(F32) * invf_ref[...]
    cos = jnp.cos(ang)
    sin = jnp.sin(ang)
    cq = _rms_norm(lat[:, :ql], qg_ref[...]).astype(BF16)
    ckv = _rms_norm(lat[:, ql:ql + kvl], kvg_ref[...]).astype(BF16)
    kr = lat[:, ql + kvl:ql + kvl + LANES] * cos + lat[:, ql + kvl + LANES:ql + kvl + 2 * LANES] * sin
    kr = kr.astype(BF16)
    q = jnp.dot(cq, wuq_ref[...], preferred_element_type=F32)
    qrot = jnp.dot(cq, wuqr_ref[...], preferred_element_type=F32)
    kv = jnp.dot(ckv, wukv_ref[...], preferred_element_type=F32)
    for h in range(N_HEADS):
        c0 = h * HEAD_QK_PAD
        qt_ref[c0:c0 + LANES, :] = q[:, c0:c0 + LANES].T.astype(BF16)
        qt_ref[c0 + LANES:c0 + 2 * LANES, :] = (
            q[:, c0 + LANES:c0 + 2 * LANES] * cos + qrot[:, h * LANES:(h + 1) * LANES] * sin).T.astype(BF16)
        k_ref[:, c0:c0 + LANES] = kv[:, h * LANES:(h + 1) * LANES].astype(BF16)
        k_ref[:, c0 + LANES:c0 + 2 * LANES] = kr
    vt_ref[...] = kv[:, N_HEADS * QK_NOPE_DIM:].T.astype(BF16)


def _qkv(x2, pos2, invf4, wa, qg, kvg, wuq, wuqr, wukv):
    t, d = x2.shape
    tm = TM_PROJ
    full = lambda a: pl.BlockSpec(a.shape, lambda i: (0,) * a.ndim)
    return pl.pallas_call(
        _qkv_kernel,
        grid=(t // tm,),
        in_specs=[pl.BlockSpec((tm, d), lambda i: (i, 0)),
                  pl.BlockSpec((tm, 1), lambda i: (i, 0)),
                  full(invf4), full(wa), full(qg), full(kvg), full(wuq), full(wuqr), full(wukv)],
        out_specs=[pl.BlockSpec((N_HEADS * HEAD_QK_PAD, tm), lambda i: (0, i)),
                   pl.BlockSpec((tm, N_HEADS * HEAD_QK_PAD), lambda i: (i, 0)),
                   pl.BlockSpec((N_HEADS * V_HEAD_DIM, tm), lambda i: (0, i))],
        out_shape=[jax.ShapeDtypeStruct((N_HEADS * HEAD_QK_PAD, t), BF16),
                   jax.ShapeDtypeStruct((t, N_HEADS * HEAD_QK_PAD), BF16),
                   jax.ShapeDtypeStruct((N_HEADS * V_HEAD_DIM, t), BF16)],
        compiler_params=_cparams(("arbitrary",)),
        name="qkv",
    )(x2, pos2, invf4, wa, qg, kvg, wuq, wuqr, wukv)


def _conv_kernel(x_ref, wc_ref, cw_ref, cb_ref, g_ref, b_ref, o_ref, ubuf, ybuf, shbuf, *, tiles_per_seq):
    tm, cw = o_ref.shape
    i = pl.program_id(0)
    xb = x_ref[...].astype(BF16)

    @pl.when(i % tiles_per_seq == 0)
    def _():
        ubuf[0:CONV_HALO, :] = jnp.zeros((CONV_HALO, cw), F32)

    shift0 = CONV_HALO - (CONV_KERNEL - 1)
    n_sh = shbuf.shape[0]
    chunk_no = 0
    for c0 in range(0, cw, CONV_COLS):
        a = jnp.dot(xb, wc_ref[:, c0:c0 + CONV_COLS], preferred_element_type=F32)
        gate = jnp.dot(xb, wc_ref[:, cw + c0:cw + c0 + CONV_COLS], preferred_element_type=F32)
        ubuf[CONV_HALO:CONV_HALO + tm, c0:c0 + CONV_COLS] = a * _sigmoid(gate)
        for r0 in range(0, tm, CONV_ROWS):
            sh = shbuf.at[chunk_no % n_sh]
            chunk_no += 1
            win = ubuf[r0:r0 + CONV_ROWS + CONV_HALO, c0:c0 + CONV_COLS]
            for r in range(1, SUBLANES):
                sh[r] = win[r:r + CONV_ROWS + CONV_HALO - SUBLANES, :]
            acc = jnp.zeros((CONV_ROWS, CONV_COLS), F32)
            for k in range(CONV_KERNEL):
                wk = cw_ref[k:k + 1, c0:c0 + CONV_COLS]
                r, j = (shift0 + k) % SUBLANES, (shift0 + k) // SUBLANES
                if r == 0:
                    tap = win[j * SUBLANES:j * SUBLANES + CONV_ROWS, :]
                else:
                    tap = sh[r, j * SUBLANES:j * SUBLANES + CONV_ROWS, :]
                acc = acc + wk * tap
            ybuf[r0:r0 + CONV_ROWS, c0:c0 + CONV_COLS] = acc
    ubuf[0:CONV_HALO, :] = ubuf[tm:tm + CONV_HALO, :]
    y = _layer_norm(ybuf[...] + cb_ref[...], g_ref[...], b_ref[...])
    o_ref[...] = (y * _sigmoid(y)).astype(BF16)


def _conv(x2, wc, cw, cb, g, b, seq):
    t, d = x2.shape
    tm = TM_CONV
    c = cw.shape[1]
    full = lambda a: pl.BlockSpec(a.shape, lambda i: (0,) * a.ndim)
    return pl.pallas_call(
        functools.partial(_conv_kernel, tiles_per_seq=seq // tm),
        grid=(t // tm,),
        in_specs=[pl.BlockSpec((tm, d), lambda i: (i, 0)), full(wc), full(cw), full(cb), full(g), full(b)],
        out_specs=pl.BlockSpec((tm, c), lambda i: (i, 0)),
        out_shape=jax.ShapeDtypeStruct((t, c), BF16),
        scratch_shapes=[pltpu.VMEM((tm + CONV_HALO, c), F32), pltpu.VMEM((tm, c), F32),
                        pltpu.VMEM((CONV_SHIFT_BUFS, SUBLANES, CONV_ROWS + CONV_HALO - SUBLANES, CONV_COLS), F32)],
        compiler_params=_cparams(("arbitrary",)),
        name="conv",
    )(x2, wc, cw, cb, g, b)


def _attn_kernel(qt_ref, k_ref, vt_ref, o_ref, *, scale, tk):
    tq = qt_ref.shape[1]
    heads = qt_ref.shape[0] // HEAD_QK_PAD
    i = pl.program_id(2)
    c = scale * LOG2_E
    n_full = (i * tq) // tk
    q_off = i * tq - n_full * tk

    def block(j, carry, masked):
        r0 = pl.multiple_of(j * tk, tk)
        out = []
        scores = []
        for h in range(heads):
            qt = qt_ref[h * HEAD_QK_PAD:(h + 1) * HEAD_QK_PAD, :]
            kb = k_ref[pl.ds(r0, tk), h * HEAD_QK_PAD:(h + 1) * HEAD_QK_PAD]
            scores.append(jnp.dot(kb, qt, preferred_element_type=F32))
        for h, (m, l, acc) in enumerate(carry):
            vtb = vt_ref[h * V_HEAD_DIM:(h + 1) * V_HEAD_DIM, pl.ds(r0, tk)]
            st = scores[h]
            if masked:
                key = lax.broadcasted_iota(I32, (tk, tq), 0)
                qry = lax.broadcasted_iota(I32, (tk, tq), 1) + q_off
                st = jnp.where(key <= qry, st, -jnp.inf)
            m_blk = jnp.max(jnp.max(st.reshape(SUBLANES, tk // SUBLANES, tq), axis=0), axis=0, keepdims=True)
            m_new = jnp.maximum(m, m_blk)
            alpha = jnp.exp2((m - m_new) * c)
            pt = jnp.exp2((st - m_new) * c)
            l_blk = jnp.sum(jnp.sum(pt.reshape(SUBLANES, tk // SUBLANES, tq), axis=0), axis=0, keepdims=True)
            l = alpha * l + l_blk
            acc = alpha * acc + jnp.dot(vtb, pt.astype(BF16), preferred_element_type=F32)
            out.append((m_new, l, acc))
        return tuple(out)

    init = tuple((jnp.full((1, tq), -jnp.inf, F32), jnp.zeros((1, tq), F32),
                  jnp.zeros((V_HEAD_DIM, tq), F32)) for _ in range(heads))
    carry = lax.fori_loop(0, n_full, lambda j, cr: block(j, cr, False), init)
    final = block(n_full, carry, True)
    for h, (_, l, acc) in enumerate(final):
        o_ref[:, h * V_HEAD_DIM:(h + 1) * V_HEAD_DIM] = (acc / l).T.astype(BF16)


def _attn(q, k, v, batch, seq):
    tq = TQ_ATTN
    nq = seq // tq
    hb = HEADS_PER_ATTN_STEP
    scale = (QK_NOPE_DIM + QK_ROPE_DIM) ** -0.5
    return pl.pallas_call(
        functools.partial(_attn_kernel, scale=scale, tk=TK_ATTN),
        grid=(batch, N_HEADS // hb, nq),
        in_specs=[pl.BlockSpec((hb * HEAD_QK_PAD, tq), lambda b, h, i: (h, b * nq + i)),
                  pl.BlockSpec((seq, hb * HEAD_QK_PAD), lambda b, h, i: (b, h)),
                  pl.BlockSpec((hb * V_HEAD_DIM, seq), lambda b, h, i: (h, b))],
        out_specs=pl.BlockSpec((tq, hb * V_HEAD_DIM), lambda b, h, i: (b * nq + i, h)),
        out_shape=jax.ShapeDtypeStruct((batch * seq, N_HEADS * V_HEAD_DIM), BF16),
        compiler_params=_cparams(("arbitrary", "arbitrary", "arbitrary")),
        name="attn",
    )(q, k, v)


def _pack_bf16_pairs(v):
    c = v.shape[1] // 2
    lo = pltpu.bitcast(v[:, :c].astype(BF16).astype(F32), U32)
    hi = pltpu.bitcast(v[:, c:].astype(BF16).astype(F32), U32)
    return (hi & BF16_HIGH_HALF) | (lo >> 16)


def _unpack_bf16_pairs(p):
    lo = pltpu.bitcast(p << 16, F32).astype(BF16)
    hi = pltpu.bitcast(p & BF16_HIGH_HALF, F32).astype(BF16)
    return jnp.concatenate([lo, hi], axis=1)


def _store_row_planes(ref, v):
    n = v.shape[0]
    p = ref.shape[0] // n
    for j in range(p):
        ref[pl.ds(j, n, stride=p), :] = v[:, j * LANES:(j + 1) * LANES]


def _load_row_planes(ref, p):
    n = ref.shape[0] // p
    return jnp.concatenate([ref[pl.ds(j, n, stride=p), :] for j in range(p)], axis=1)


def _oproj_kernel(a_ref, c_ref, x_ref, woa_ref, woc_ref, g_ref, b_ref, wr_ref,
                  h_ref, hp_ref, lg_ref, *, alpha):
    mix = jnp.dot(a_ref[...], woa_ref[...], preferred_element_type=F32)
    mix = mix + jnp.dot(c_ref[...], woc_ref[...], preferred_element_type=F32)
    h = _layer_norm(alpha * x_ref[...] + mix, g_ref[...], b_ref[...])
    h_ref[...] = h
    _store_row_planes(hp_ref, _pack_bf16_pairs(h))
    h_hi = h.astype(BF16)
    h_lo = (h - h_hi.astype(F32)).astype(BF16)
    logits = jnp.dot(jnp.concatenate([h_hi, h_lo, h_hi], axis=1), wr_ref[...], preferred_element_type=F32)
    lg_ref[...] = logits.T[:N_EXPERTS, :]


def _oproj(attn, conv, x2, woa, woc, g, b, wr, alpha):
    t, d = x2.shape
    tm = TM_PROJ
    full = lambda a: pl.BlockSpec(a.shape, lambda i: (0,) * a.ndim)
    return pl.pallas_call(
        functools.partial(_oproj_kernel, alpha=alpha),
        grid=(t // tm,),
        in_specs=[pl.BlockSpec((tm, attn.shape[1]), lambda i: (i, 0)),
                  pl.BlockSpec((tm, conv.shape[1]), lambda i: (i, 0)),
                  pl.BlockSpec((tm, d), lambda i: (i, 0)),
                  full(woa), full(woc), full(g), full(b), full(wr)],
        out_specs=[pl.BlockSpec((tm, d), lambda i: (i, 0)),
                   pl.BlockSpec((tm * (d // 2 // LANES), LANES), lambda i: (i, 0)),
                   pl.BlockSpec((N_EXPERTS, tm), lambda i: (0, i))],
        out_shape=[jax.ShapeDtypeStruct((t, d), F32),
                   jax.ShapeDtypeStruct((t * (d // 2 // LANES), LANES), U32),
                   jax.ShapeDtypeStruct((N_EXPERTS, t), F32)],
        compiler_params=_cparams(("arbitrary",)),
        name="oproj",
    )(attn, conv, x2, woa, woc, g, b, wr)


def _route_kernel(lg_ref, bias_ref, eidx_ref, rank_ref, wts_ref, cnt_ref, carry_ref):
    ne, tm = lg_ref.shape
    i = pl.program_id(0)

    @pl.when(i == 0)
    def _():
        carry_ref[...] = jnp.zeros(carry_ref.shape, F32)

    scores = _sigmoid(lg_ref[...])
    biased = scores + bias_ref[...]
    neg = -jnp.inf
    sub8 = lax.broadcasted_iota(I32, (GROUP_SIZE, tm), 0)
    gscore = []
    for g in range(N_EXPERT_GROUPS):
        blk = biased[g * GROUP_SIZE:(g + 1) * GROUP_SIZE, :]
        m1 = jnp.max(blk, axis=0, keepdims=True)
        i1 = jnp.min(jnp.where(blk == m1, sub8, GROUP_SIZE), axis=0, keepdims=True)
        m2 = jnp.max(jnp.where(sub8 == i1, neg, blk), axis=0, keepdims=True)
        gscore.append(m1 + m2)
    kept = []
    for g in range(N_EXPERT_GROUPS):
        beat = jnp.zeros((1, tm), I32)
        for o in range(N_EXPERT_GROUPS):
            if o < g:
                beat = beat + (gscore[o] >= gscore[g]).astype(I32)
            elif o > g:
                beat = beat + (gscore[o] > gscore[g]).astype(I32)
        kept.append(jnp.where(beat < TOPK_GROUPS, biased[g * GROUP_SIZE:(g + 1) * GROUP_SIZE, :], neg))
    cur = jnp.concatenate(kept, axis=0)
    sub = lax.broadcasted_iota(I32, (ne, tm), 0)
    sel_any = jnp.zeros((ne, tm), F32)
    picks = []
    wsum = jnp.zeros((1, tm), F32)
    for k in range(TOP_K):
        m = jnp.max(cur, axis=0, keepdims=True)
        ei = jnp.min(jnp.where(cur == m, sub, ne), axis=0, keepdims=True)
        sel = sub == ei
        w = jnp.sum(jnp.where(sel, scores, 0.0), axis=0, keepdims=True)
        cur = jnp.where(sel, neg, cur)
        sel_any = sel_any + sel.astype(F32)
        wsum = wsum + w
        picks.append((ei, sel, w))
    r = lax.broadcasted_iota(I32, (tm, tm), 0)
    c = lax.broadcasted_iota(I32, (tm, tm), 1)
    upper = (r <= c).astype(BF16)
    cum = jnp.dot(sel_any.astype(BF16), upper, preferred_element_type=F32)
    carry = carry_ref[:, 0:1]
    excl = cum - sel_any + carry
    total = carry + jnp.sum(sel_any, axis=1, keepdims=True)
    carry_ref[...] = jnp.broadcast_to(total, carry_ref.shape)
    cnt_ref[...] = jnp.broadcast_to(total, cnt_ref.shape)
    denom = wsum + 1e-20
    eidx_ref[...] = jnp.zeros(eidx_ref.shape, I32)
    rank_ref[...] = jnp.zeros(rank_ref.shape, I32)
    wts_ref[...] = jnp.zeros(wts_ref.shape, F32)
    for k, (ei, sel, w) in enumerate(picks):
        eidx_ref[k:k + 1, :] = ei
        rank_ref[k:k + 1, :] = jnp.sum(jnp.where(sel, excl, 0.0), axis=0, keepdims=True).astype(I32)
        wts_ref[k:k + 1, :] = w / denom * ROUTED_SCALE


def _route(logits_t, bias):
    ne, t = logits_t.shape
    tm = TM_ROUTE
    return pl.pallas_call(
        _route_kernel,
        grid=(t // tm,),
        in_specs=[pl.BlockSpec((ne, tm), lambda i: (0, i)), pl.BlockSpec((ne, 1), lambda i: (0, 0))],
        out_specs=[pl.BlockSpec((TOP_K_PAD, tm), lambda i: (0, i)),
                   pl.BlockSpec((TOP_K_PAD, tm), lambda i: (0, i)),
                   pl.BlockSpec((TOP_K_PAD, tm), lambda i: (0, i)),
                   pl.BlockSpec((ne, LANES), lambda i: (0, 0))],
        out_shape=[jax.ShapeDtypeStruct((TOP_K_PAD, t), I32),
                   jax.ShapeDtypeStruct((TOP_K_PAD, t), I32),
                   jax.ShapeDtypeStruct((TOP_K_PAD, t), F32),
                   jax.ShapeDtypeStruct((ne, LANES), F32)],
        scratch_shapes=[pltpu.VMEM((ne, LANES), F32)],
        compiler_params=_cparams(("arbitrary",)),
        name="route",
    )(logits_t, bias)


def _wait_rows(src_rows, dst_rows, sem, n_rows):
    def body(_, c):
        pltpu.make_async_copy(src_rows, dst_rows, sem).wait()
        return c
    lax.fori_loop(0, n_rows // WAIT_ROWS, body, 0)


def _dispatch_kernel(zfill_ref, slots_ref, hp_ref, xs_hbm, zero_vmem, sem, zsem):
    i = pl.program_id(0)
    te = zero_vmem.shape[0]
    tc = hp_ref.shape[0]
    n_tiles = xs_hbm.shape[0] // te

    def zero_copy(tile):
        return pltpu.make_async_copy(zero_vmem, xs_hbm.at[pl.ds(pl.multiple_of(tile * te, te), te)], zsem)

    @pl.when(i == 0)
    def _():
        zero_vmem[...] = jnp.zeros(zero_vmem.shape, U32)

        def zstart(tile, c):
            @pl.when(zfill_ref[tile] != 0)
            def _():
                zero_copy(tile).start()
            return c

        def zwait(tile, c):
            @pl.when(zfill_ref[tile] != 0)
            def _():
                zero_copy(tile).wait()
            return c

        lax.fori_loop(0, n_tiles, zstart, 0)
        lax.fori_loop(0, n_tiles, zwait, 0)

    base = i * (tc * TOP_K_PAD)

    def issue(t, c):
        for k in range(TOP_K):
            slot = slots_ref[base + t * TOP_K_PAD + k]
            pltpu.make_async_copy(hp_ref.at[t], xs_hbm.at[slot], sem).start(priority=k % 2)
        return c

    lax.fori_loop(0, tc, issue, 0, unroll=ISSUE_UNROLL)
    _wait_rows(hp_ref.at[pl.ds(0, WAIT_ROWS)], xs_hbm.at[pl.ds(0, WAIT_ROWS)], sem, tc * TOP_K)


def _dispatch(zfill, slots_flat, hp, n_rows):
    t, planes, lanes = hp.shape
    tc = TC_DISP
    return pl.pallas_call(
        _dispatch_kernel,
        grid_spec=pltpu.PrefetchScalarGridSpec(
            num_scalar_prefetch=2,
            grid=(t // tc,),
            in_specs=[pl.BlockSpec((tc, planes, lanes), lambda i, z, s: (i, 0, 0))],
            out_specs=pl.BlockSpec(memory_space=pl.ANY),
            scratch_shapes=[pltpu.VMEM((TE_ROWS, planes, lanes), U32),
                            pltpu.SemaphoreType.DMA, pltpu.SemaphoreType.DMA],
        ),
        out_shape=jax.ShapeDtypeStruct((n_rows, planes, lanes), U32),
        compiler_params=_cparams(("arbitrary",)),
        name="dispatch",
    )(zfill, slots_flat, hp)


def _experts_kernel(te_ref, nv_ref, nx_ref, xs_ref, wg_hbm, wu_hbm, wd_hbm, y_ref,
                    sg, su, sd, wgb, wub, wdb, sem):
    i = pl.program_id(0)
    e = te_ref[i]
    prev = te_ref[jnp.maximum(i - 1, 0)]
    valid = i < nv_ref[0]

    def fetch(ex):
        return (pltpu.make_async_copy(wg_hbm.at[ex], sg, sem.at[0]),
                pltpu.make_async_copy(wu_hbm.at[ex], su, sem.at[1]),
                pltpu.make_async_copy(wd_hbm.at[ex], sd, sem.at[2]))

    @pl.when(i == 0)
    def _():
        for cp in fetch(e):
            cp.start()

    first = valid & ((i == 0) | (e != prev))

    def ffn():
        xs = _unpack_bf16_pairs(_load_row_planes(xs_ref, xs_ref.shape[0] // TE_ROWS))
        g = jnp.dot(xs, wgb[...], preferred_element_type=F32)
        u = jnp.dot(xs, wub[...], preferred_element_type=F32)
        hid = (g * _sigmoid(g) * u).astype(BF16)
        y = jnp.dot(hid, wdb[...], preferred_element_type=F32)
        _store_row_planes(y_ref, _pack_bf16_pairs(y))

    @pl.when(first)
    def _():
        for cp in fetch(e):
            cp.wait()
        wgb[...] = sg[...].astype(BF16)
        wub[...] = su[...].astype(BF16)
        wdb[...] = sd[...].astype(BF16)
        for cp in fetch(jnp.where(nx_ref[i] >= 0, nx_ref[i], e)):
            cp.start(priority=1)
        ffn()

    @pl.when(valid & jnp.logical_not(first))
    def _():
        ffn()

    @pl.when(i == pl.num_programs(0) - 1)
    def _():
        for cp in fetch(e):
            cp.wait()

    @pl.when(jnp.logical_not(valid))
    def _():
        y_ref[...] = jnp.zeros(y_ref.shape, U32)


def _experts(tile_expert, n_valid, next_expert, xs, w_gate, w_up, w_down):
    n_rows, xp, _ = xs.shape
    ne, d, ff = w_gate.shape
    te = TE_ROWS
    in_map = lambda i, te_ref, nv_ref, nx_ref: (jnp.minimum(i, nv_ref[0] - 1), 0)
    out_map = lambda i, te_ref, nv_ref, nx_ref: (i, 0)
    hbm = pl.BlockSpec(memory_space=pl.ANY)
    y = pl.pallas_call(
        _experts_kernel,
        grid_spec=pltpu.PrefetchScalarGridSpec(
            num_scalar_prefetch=3,
            grid=(n_rows // te,),
            in_specs=[pl.BlockSpec((te * xp, LANES), in_map), hbm, hbm, hbm],
            out_specs=pl.BlockSpec((te * xp, LANES), out_map),
            scratch_shapes=[pltpu.VMEM((d, ff), F32), pltpu.VMEM((d, ff), F32), pltpu.VMEM((ff, d), F32),
                            pltpu.VMEM((d, ff), BF16), pltpu.VMEM((d, ff), BF16), pltpu.VMEM((ff, d), BF16),
                            pltpu.SemaphoreType.DMA((3,))],
        ),
        out_shape=jax.ShapeDtypeStruct((n_rows * xp, LANES), U32),
        compiler_params=_cparams(("arbitrary",)),
        name="experts",
    )(tile_expert, n_valid, next_expert, xs.reshape(n_rows * xp, LANES), w_gate, w_up, w_down)
    return y.reshape(n_rows, xp, LANES)


def _combine_kernel(slots_ref, y_hbm, h_ref, wts_ref, wsg_ref, wsu_ref, wsd_ref, g_ref, b_ref,
                    o_ref, gbuf, rbuf, shbuf, sem, *, alpha):
    tm, d = o_ref.shape
    planes = gbuf.shape[3]
    i = pl.program_id(0)
    cur = i % 2
    grp = SUBLANES

    def issue_rows(step, t0, n):
        buf = step % 2
        base = step * (tm * TOP_K_PAD)
        for u in range(n):
            for k in range(TOP_K):
                slot = slots_ref[base + (t0 + u) * TOP_K_PAD + k]
                pltpu.make_async_copy(y_hbm.at[slot], gbuf.at[buf, k, t0 + u], sem.at[buf]).start(priority=k % 2)

    @pl.when(i == 0)
    def _():
        def first(t, c):
            issue_rows(0, t, 1)
            return c
        lax.fori_loop(0, tm, first, 0, unroll=ISSUE_UNROLL)

    hb = h_ref[...].astype(BF16)
    sg = jnp.dot(hb, wsg_ref[...], preferred_element_type=F32)
    su = jnp.dot(hb, wsu_ref[...], preferred_element_type=F32)
    shbuf[...] = jnp.dot((sg * _sigmoid(sg) * su).astype(BF16), wsd_ref[...], preferred_element_type=F32)
    _wait_rows(y_hbm.at[pl.ds(0, WAIT_ROWS)], gbuf.at[cur, 0, pl.ds(0, WAIT_ROWS)], sem.at[cur], tm * TOP_K)

    def group(g, prefetch):
        t0 = pl.multiple_of(g * grp, grp)
        packed = [gbuf[cur, k, pl.ds(t0, grp)] for k in range(TOP_K)]
        w = wts_ref[pl.ds(t0, grp)]
        resid = alpha * h_ref[pl.ds(t0, grp), :] + shbuf[pl.ds(t0, grp), :]
        if prefetch:
            issue_rows(i + 1, t0, grp)
        r_lo = r_hi = None
        for k in range(TOP_K):
            lo = w[:, k:k + 1, :] * pltpu.bitcast(packed[k] << 16, F32)
            hi = w[:, k:k + 1, :] * pltpu.bitcast(packed[k] & BF16_HIGH_HALF, F32)
            r_lo = lo if r_lo is None else r_lo + lo
            r_hi = hi if r_hi is None else r_hi + hi
        p0 = pl.multiple_of(t0 * planes, grp * planes)
        rbuf[0, pl.ds(p0, grp * planes), :] = r_lo.reshape(grp * planes, LANES)
        rbuf[1, pl.ds(p0, grp * planes), :] = r_hi.reshape(grp * planes, LANES)
        routed = jnp.concatenate([rbuf[half, pl.ds(p0 + j, grp, stride=planes), :]
                                  for half in range(2) for j in range(planes)], axis=1)
        o_ref[pl.ds(t0, grp), :] = _layer_norm(resid + routed, g_ref[...], b_ref[...])

    has_next = i + 1 < pl.num_programs(0)

    @pl.when(has_next)
    def _():
        lax.fori_loop(0, tm // grp, lambda g, c: (group(g, True), c)[1], 0, unroll=GROUP_UNROLL)

    @pl.when(jnp.logical_not(has_next))
    def _():
        lax.fori_loop(0, tm // grp, lambda g, c: (group(g, False), c)[1], 0, unroll=GROUP_UNROLL)


def _combine(slots_flat, y, h, wts_planes, wsg, wsu, wsd, g, b, alpha):
    t, d = h.shape
    tm = TM_COMB
    planes = y.shape[1]
    full = lambda a: pl.BlockSpec(a.shape, lambda i, s: (0,) * a.ndim)
    once = lambda a: pl.BlockSpec(a.shape, lambda i, s: (0,) * a.ndim, pipeline_mode=pl.Buffered(1))
    return pl.pallas_call(
        functools.partial(_combine_kernel, alpha=alpha),
        grid_spec=pltpu.PrefetchScalarGridSpec(
            num_scalar_prefetch=1,
            grid=(t // tm,),
            in_specs=[pl.BlockSpec(memory_space=pl.ANY),
                      pl.BlockSpec((tm, d), lambda i, s: (i, 0)),
                      pl.BlockSpec((tm, TOP_K_PAD, LANES), lambda i, s: (i, 0, 0)),
                      once(wsg), once(wsu), once(wsd), full(g), full(b)],
            out_specs=pl.BlockSpec((tm, d), lambda i, s: (i, 0)),
            scratch_shapes=[pltpu.VMEM((2, TOP_K, tm, planes, LANES), U32),
                            pltpu.VMEM((2, tm * planes, LANES), F32), pltpu.VMEM((tm, d), F32),
                            pltpu.SemaphoreType.DMA((2,))],
        ),
        out_shape=jax.ShapeDtypeStruct((t, d), F32),
        compiler_params=_cparams(("arbitrary",)),
        name="combine",
    )(slots_flat, y, h, wts_planes, wsg, wsu, wsd, g, b)


def _prep_attention_weights(w_in, w_uq, w_ukv):
    d = w_in.shape[0]
    ql = w_uq.shape[0]
    kvl = w_ukv.shape[0]
    half = QK_ROPE_DIM // 2
    pad = LANES - QK_ROPE_DIM
    w_kr = w_in[:, ql + kvl:ql + kvl + QK_ROPE_DIM]
    w_kr_rot = jnp.concatenate([-w_kr[:, half:], w_kr[:, :half]], axis=1)
    zpad = jnp.zeros((d, pad), w_in.dtype)
    wa = jnp.concatenate([w_in[:, :ql + kvl], w_kr, zpad, w_kr_rot, zpad], axis=1).astype(BF16)
    wc = w_in[:, ql + kvl + QK_ROPE_DIM:].astype(BF16)
    uq = w_uq.reshape(ql, N_HEADS, QK_NOPE_DIM + QK_ROPE_DIM)
    uq_nope, uq_rope = uq[..., :QK_NOPE_DIM], uq[..., QK_NOPE_DIM:]
    zq = jnp.zeros((ql, N_HEADS, pad), w_uq.dtype)
    wuq = jnp.concatenate([uq_nope, uq_rope, zq], axis=-1).reshape(ql, N_HEADS * HEAD_QK_PAD).astype(BF16)
    uq_rot = jnp.concatenate([-uq_rope[..., half:], uq_rope[..., :half], zq], axis=-1)
    wuqr = uq_rot.reshape(ql, N_HEADS * LANES).astype(BF16)
    ukv = w_ukv.reshape(kvl, N_HEADS, QK_NOPE_DIM + V_HEAD_DIM)
    wukv = jnp.concatenate([ukv[..., :QK_NOPE_DIM].reshape(kvl, -1), ukv[..., QK_NOPE_DIM:].reshape(kvl, -1)],
                           axis=1).astype(BF16)
    return wa, wc, wuq, wuqr, wukv


def _layer(h, pos2, invf4, batch, seq, alpha, w_in, q_norm_g, w_uq, kv_norm_g, w_ukv, conv_w, conv_b,
           conv_ln_g, conv_ln_b, w_o, ln1_g, ln1_b, w_router, router_bias, w_gate, w_up, w_down,
           ws_gate, ws_up, ws_down, ln2_g, ln2_b):
    t, d = h.shape
    row = lambda a: a.reshape(1, -1)
    wa, wc, wuq, wuqr, wukv = _prep_attention_weights(w_in, w_uq, w_ukv)
    q, k, v = _qkv(h, pos2, invf4, wa, row(q_norm_g), row(kv_norm_g), wuq, wuqr, wukv)
    conv = _conv(h, wc, conv_w, row(conv_b), row(conv_ln_g), row(conv_ln_b), seq)
    attn = _attn(q, k, v, batch, seq)
    aw = attn.shape[1]
    wr32 = jnp.pad(w_router.astype(F32), ((0, 0), (0, LANES - N_EXPERTS)))
    wr_hi = wr32.astype(BF16)
    wr_lo = (wr32 - wr_hi.astype(F32)).astype(BF16)
    wr = jnp.concatenate([wr_hi, wr_hi, wr_lo], axis=0)
    h1, h1p, logits_t = _oproj(attn, conv, h, w_o[:aw].astype(BF16), w_o[aw:].astype(BF16),
                               row(ln1_g), row(ln1_b), wr, alpha)
    return _moe_ffn(h1, h1p, logits_t, alpha, router_bias, w_gate, w_up, w_down,
                    ws_gate, ws_up, ws_down, ln2_g, ln2_b)


def _moe_ffn(h1, h1p, logits_t, alpha, router_bias, w_gate, w_up, w_down, ws_gate, ws_up, ws_down, ln2_g, ln2_b):
    t, d = h1.shape
    row = lambda a: a.reshape(1, -1)
    eidx, rank, wts, cnt = _route(logits_t, router_bias.astype(F32).reshape(N_EXPERTS, 1))
    counts = cnt[:, 0].astype(I32)
    padded = (counts + TE_ROWS - 1) // TE_ROWS * TE_ROWS
    pend = jnp.cumsum(padded)
    poff = pend - padded
    n_rows = t * TOP_K + N_EXPERTS * TE_ROWS
    n_tiles = n_rows // TE_ROWS
    n_valid = (pend[-1] // TE_ROWS).astype(I32).reshape(1)
    tile_row = jnp.minimum(jnp.arange(n_tiles, dtype=I32), n_valid[0] - 1) * TE_ROWS
    tile_expert = jnp.sum((pend[None, :] <= tile_row[:, None]).astype(I32), axis=1)
    tile_expert = jnp.minimum(tile_expert, N_EXPERTS - 1)
    tiles = jnp.arange(n_tiles, dtype=I32)
    last_tile = jnp.where(padded > 0, pend // TE_ROWS - 1, -1)
    zfill = ((tiles >= n_valid[0]) | jnp.any(tiles[:, None] == last_tile[None, :], axis=1)).astype(I32)
    ids = jnp.arange(N_EXPERTS, dtype=I32)
    later = (ids[None, :] > ids[:, None]) & (padded[None, :] > 0)
    next_of = jnp.min(jnp.where(later, ids[None, :], N_EXPERTS), axis=1)
    next_of = jnp.where(next_of == N_EXPERTS, -1, next_of)
    next_expert = jnp.sum(jnp.where(tile_expert[:, None] == ids[None, :], next_of[None, :], 0), axis=1).astype(I32)
    slot_base = jnp.sum(jnp.where(eidx[None] == ids[:, None, None], poff[:, None, None], 0), axis=0)
    slots_flat = (slot_base + rank).T.reshape(-1)
    xs = _dispatch(zfill, slots_flat, h1p.reshape(t, -1, LANES), n_rows)
    y = _experts(tile_expert, n_valid, next_expert, xs, w_gate, w_up, w_down)
    wts_planes = jnp.broadcast_to(wts.T[:, :, None], (t, TOP_K_PAD, LANES))
    return _combine(slots_flat, y, h1, wts_planes, ws_gate.astype(BF16), ws_up.astype(BF16),
                    ws_down.astype(BF16), row(ln2_g), row(ln2_b), alpha)


def kernel(x, positions, w_in, q_norm_g, w_uq, kv_norm_g, w_ukv, conv_w, conv_b, conv_ln_g, conv_ln_b, w_o, ln1_g, ln1_b, w_router, router_bias, w_gate, w_up, w_down, ws_gate, ws_up, ws_down, ln2_g, ln2_b):
    batch, seq, d = x.shape
    depth = w_in.shape[0]
    alpha = (2.0 * depth) ** 0.25
    inv_freq = ROPE_BASE ** (-jnp.arange(0, QK_ROPE_DIM, 2, dtype=F32) / QK_ROPE_DIM)
    invf4 = jnp.tile(inv_freq, LANES // inv_freq.shape[0]).reshape(1, LANES)
    pos2 = positions.reshape(batch * seq, 1)
    h = x.reshape(batch * seq, d)
    for l in range(depth):
        h = _layer(h, pos2, invf4, batch, seq, alpha, w_in[l], q_norm_g[l], w_uq[l], kv_norm_g[l], w_ukv[l],
                   conv_w[l], conv_b[l], conv_ln_g[l], conv_ln_b[l], w_o[l], ln1_g[l], ln1_b[l],
                   w_router[l], router_bias[l], w_gate[l], w_up[l], w_down[l],
                   ws_gate[l], ws_up[l], ws_down[l], ln2_g[l], ln2_b[l])
    return h.reshape(batch, seq, d)
```

```python
import functools

import jax
import jax.numpy as jnp
import numpy as np
from jax import lax
from jax.experimental import pallas as pl
from jax.experimental.pallas import tpu as pltpu

F32 = jnp.float32
BF16 = jnp.bfloat16
I32 = jnp.int32
U32 = jnp.uint32

N_HEADS = 8
QK_NOPE_DIM = 128
QK_ROPE_DIM = 64
V_HEAD_DIM = 128
HEAD_QK_PAD = 256
CONV_KERNEL = 31
N_EXPERTS = 64
TOP_K = 6
TOP_K_PAD = 8
N_EXPERT_GROUPS = 8
GROUP_SIZE = N_EXPERTS // N_EXPERT_GROUPS
TOPK_GROUPS = 4
ROUTED_SCALE = 2.5
ROPE_BASE = 10000.0
LN_EPS = 1e-5
RMS_EPS = 1e-6
LOG2_E = 1.4426950408889634
BF16_HIGH_HALF = np.uint32(0xFFFF0000)

LANES = 128
SUBLANES = 8
CONV_HALO = 32

TM_PROJ = 256
TM_CONV = 512
TQ_ATTN = 512
TK_ATTN = 512
HEADS_PER_ATTN_STEP = 4
TM_ROUTE = 512
TE_ROWS = 256
TM_COMB = 256
TC_DISP = 1024
WAIT_ROWS = 128
ISSUE_UNROLL = 4
COMBINE_EARLY_TOKENS = 32
GROUP_UNROLL = 8
CONV_ROWS = 128
CONV_COLS = 256
CONV_SHIFT_BUFS = 4
VMEM_LIMIT = 56 * 1024 * 1024


def _cparams(sem):
    return pltpu.CompilerParams(dimension_semantics=sem, vmem_limit_bytes=VMEM_LIMIT)


def _sigmoid(v):
    return 1.0 / (1.0 + jnp.exp(-v))


def _layer_norm(v, g, b):
    mu = jnp.mean(v, axis=-1, keepdims=True)
    d = v - mu
    var = jnp.mean(d * d, axis=-1, keepdims=True)
    return d * lax.rsqrt(var + LN_EPS) * g + b


def _rms_norm(v, g):
    ms = jnp.mean(v * v, axis=-1, keepdims=True)
    return v * lax.rsqrt(ms + RMS_EPS) * g


def _qkv_kernel(x_ref, pos_ref, invf_ref, wa_ref, qg_ref, kvg_ref, wuq_ref, wuqr_ref, wukv_ref,
                qt_ref, k_ref, vt_ref):
    ql = qg_ref.shape[1]
    kvl = kvg_ref.shape[1]
    xb = x_ref[...].astype(BF16)
    lat = jnp.dot(xb, wa_ref[...], preferred_element_type=F32)
    ang = pos_ref[...].astype(F32) * invf_ref[...]
    cos = jnp.cos(ang)
    sin = jnp.sin(ang)
    cq = _rms_norm(lat[:, :ql], qg_ref[...]).astype(BF16)
    ckv = _rms_norm(lat[:, ql:ql + kvl], kvg_ref[...]).astype(BF16)
    kr = lat[:, ql + kvl:ql + kvl + LANES] * cos + lat[:, ql + kvl + LANES:ql + kvl + 2 * LANES] * sin
    kr = kr.astype(BF16)
    q = jnp.dot(cq, wuq_ref[...], preferred_element_type=F32)
    qrot = jnp.dot(cq, wuqr_ref[...], preferred_element_type=F32)
    kv = jnp.dot(ckv, wukv_ref[...], preferred_element_type=F32)
    for h in range(N_HEADS):
        c0 = h * HEAD_QK_PAD
        qt_ref[c0:c0 + LANES, :] = q[:, c0:c0 + LANES].T.astype(BF16)
        qt_ref[c0 + LANES:c0 + 2 * LANES, :] = (
            q[:, c0 + LANES:c0 + 2 * LANES] * cos + qrot[:, h * LANES:(h + 1) * LANES] * sin).T.astype(BF16)
        k_ref[:, c0:c0 + LANES] = kv[:, h * LANES:(h + 1) * LANES].astype(BF16)
        k_ref[:, c0 + LANES:c0 + 2 * LANES] = kr
    vt_ref[...] = kv[:, N_HEADS * QK_NOPE_DIM:].T.astype(BF16)


def _qkv(x2, pos2, invf4, wa, qg, kvg, wuq, wuqr, wukv):
    t, d = x2.shape
    tm = TM_PROJ
    full = lambda a: pl.BlockSpec(a.shape, lambda i: (0,) * a.ndim)
    return pl.pallas_call(
        _qkv_kernel,
        grid=(t // tm,),
        in_specs=[pl.BlockSpec((tm, d), lambda i: (i, 0)),
                  pl.BlockSpec((tm, 1), lambda i: (i, 0)),
                  full(invf4), full(wa), full(qg), full(kvg), full(wuq), full(wuqr), full(wukv)],
        out_specs=[pl.BlockSpec((N_HEADS * HEAD_QK_PAD, tm), lambda i: (0, i)),
                   pl.BlockSpec((tm, N_HEADS * HEAD_QK_PAD), lambda i: (i, 0)),
                   pl.BlockSpec((N_HEADS * V_HEAD_DIM, tm), lambda i: (0, i))],
        out_shape=[jax.ShapeDtypeStruct((N_HEADS * HEAD_QK_PAD, t), BF16),
                   jax.ShapeDtypeStruct((t, N_HEADS * HEAD_QK_PAD), BF16),
                   jax.ShapeDtypeStruct((N_HEADS * V_HEAD_DIM, t), BF16)],
        compiler_params=_cparams(("arbitrary",)),
        name="qkv",
    )(x2, pos2, invf4, wa, qg, kvg, wuq, wuqr, wukv)


def _conv_kernel(x_ref, wc_ref, cw_ref, cb_ref, g_ref, b_ref, o_ref, ubuf, ybuf, shbuf, *, tiles_per_seq):
    tm, cw = o_ref.shape
    i = pl.program_id(0)
    xb = x_ref[...].astype(BF16)

    @pl.when(i % tiles_per_seq == 0)
    def _():
        ubuf[0:CONV_HALO, :] = jnp.zeros((CONV_HALO, cw), F32)

    shift0 = CONV_HALO - (CONV_KERNEL - 1)
    n_sh = shbuf.shape[0]
    chunk_no = 0
    for c0 in range(0, cw, CONV_COLS):
        a = jnp.dot(xb, wc_ref[:, c0:c0 + CONV_COLS], preferred_element_type=F32)
        gate = jnp.dot(xb, wc_ref[:, cw + c0:cw + c0 + CONV_COLS], preferred_element_type=F32)
        ubuf[CONV_HALO:CONV_HALO + tm, c0:c0 + CONV_COLS] = a * _sigmoid(gate)
        for r0 in range(0, tm, CONV_ROWS):
            sh = shbuf.at[chunk_no % n_sh]
            chunk_no += 1
            win = ubuf[r0:r0 + CONV_ROWS + CONV_HALO, c0:c0 + CONV_COLS]
            for r in range(1, SUBLANES):
                sh[r] = win[r:r + CONV_ROWS + CONV_HALO - SUBLANES, :]
            acc = jnp.zeros((CONV_ROWS, CONV_COLS), F32)
            for k in range(CONV_KERNEL):
                wk = cw_ref[k:k + 1, c0:c0 + CONV_COLS]
                r, j = (shift0 + k) % SUBLANES, (shift0 + k) // SUBLANES
                if r == 0:
                    tap = win[j * SUBLANES:j * SUBLANES + CONV_ROWS, :]
                else:
                    tap = sh[r, j * SUBLANES:j * SUBLANES + CONV_ROWS, :]
                acc = acc + wk * tap
            ybuf[r0:r0 + CONV_ROWS, c0:c0 + CONV_COLS] = acc
    ubuf[0:CONV_HALO, :] = ubuf[tm:tm + CONV_HALO, :]
    y = _layer_norm(ybuf[...] + cb_ref[...], g_ref[...], b_ref[...])
    o_ref[...] = (y * _sigmoid(y)).astype(BF16)


def _conv(x2, wc, cw, cb, g, b, seq):
    t, d = x2.shape
    tm = TM_CONV
    c = cw.shape[1]
    full = lambda a: pl.BlockSpec(a.shape, lambda i: (0,) * a.ndim)
    return pl.pallas_call(
        functools.partial(_conv_kernel, tiles_per_seq=seq // tm),
        grid=(t // tm,),
        in_specs=[pl.BlockSpec((tm, d), lambda i: (i, 0)), full(wc), full(cw), full(cb), full(g), full(b)],
        out_specs=pl.BlockSpec((tm, c), lambda i: (i, 0)),
        out_shape=jax.ShapeDtypeStruct((t, c), BF16),
        scratch_shapes=[pltpu.VMEM((tm + CONV_HALO, c), F32), pltpu.VMEM((tm, c), F32),
                        pltpu.VMEM((CONV_SHIFT_BUFS, SUBLANES, CONV_ROWS + CONV_HALO - SUBLANES, CONV_COLS), F32)],
        compiler_params=_cparams(("arbitrary",)),
        name="conv",
    )(x2, wc, cw, cb, g, b)


def _attn_kernel(qt_ref, k_ref, vt_ref, o_ref, *, scale, tk):
    tq = qt_ref.shape[1]
    heads = qt_ref.shape[0] // HEAD_QK_PAD
    i = pl.program_id(2)
    c = scale * LOG2_E
    n_full = (i * tq) // tk
    q_off = i * tq - n_full * tk

    def block(j, carry, masked):
        r0 = pl.multiple_of(j * tk, tk)
        out = []
        scores = []
        for h in range(heads):
            qt = qt_ref[h * HEAD_QK_PAD:(h + 1) * HEAD_QK_PAD, :]
            kb = k_ref[pl.ds(r0, tk), h * HEAD_QK_PAD:(h + 1) * HEAD_QK_PAD]
            scores.append(jnp.dot(kb, qt, preferred_element_type=F32))
        for h, (m, l, acc) in enumerate(carry):
            vtb = vt_ref[h * V_HEAD_DIM:(h + 1) * V_HEAD_DIM, pl.ds(r0, tk)]
            st = scores[h]
            if masked:
                key = lax.broadcasted_iota(I32, (tk, tq), 0)
                qry = lax.broadcasted_iota(I32, (tk, tq), 1) + q_off
                st = jnp.where(key <= qry, st, -jnp.inf)
            m_blk = jnp.max(jnp.max(st.reshape(SUBLANES, tk // SUBLANES, tq), axis=0), axis=0, keepdims=True)
            m_new = jnp.maximum(m, m_blk)
            alpha = jnp.exp2((m - m_new) * c)
            pt = jnp.exp2((st - m_new) * c)
            l_blk = jnp.sum(jnp.sum(pt.reshape(SUBLANES, tk // SUBLANES, tq), axis=0), axis=0, keepdims=True)
            l = alpha * l + l_blk
            acc = alpha * acc + jnp.dot(vtb, pt.astype(BF16), preferred_element_type=F32)
            out.append((m_new, l, acc))
        return tuple(out)

    init = tuple((jnp.full((1, tq), -jnp.inf, F32), jnp.zeros((1, tq), F32),
                  jnp.zeros((V_HEAD_DIM, tq), F32)) for _ in range(heads))
    carry = lax.fori_loop(0, n_full, lambda j, cr: block(j, cr, False), init)
    final = block(n_full, carry, True)
    for h, (_, l, acc) in enumerate(final):
        o_ref[:, h * V_HEAD_DIM:(h + 1) * V_HEAD_DIM] = (acc / l).T.astype(BF16)


def _attn(q, k, v, batch, seq):
    tq = TQ_ATTN
    nq = seq // tq
    hb = HEADS_PER_ATTN_STEP
    scale = (QK_NOPE_DIM + QK_ROPE_DIM) ** -0.5
    return pl.pallas_call(
        functools.partial(_attn_kernel, scale=scale, tk=TK_ATTN),
        grid=(batch, N_HEADS // hb, nq),
        in_specs=[pl.BlockSpec((hb * HEAD_QK_PAD, tq), lambda b, h, i: (h, b * nq + i)),
                  pl.BlockSpec((seq, hb * HEAD_QK_PAD), lambda b, h, i: (b, h)),
                  pl.BlockSpec((hb * V_HEAD_DIM, seq), lambda b, h, i: (h, b))],
        out_specs=pl.BlockSpec((tq, hb * V_HEAD_DIM), lambda b, h, i: (b * nq + i, h)),
        out_shape=jax.ShapeDtypeStruct((batch * seq, N_HEADS * V_HEAD_DIM), BF16),
        compiler_params=_cparams(("arbitrary", "arbitrary", "arbitrary")),
        name="attn",
    )(q, k, v)


def _pack_bf16_pairs(v):
    c = v.shape[1] // 2
    lo = pltpu.bitcast(v[:, :c].astype(BF16).astype(F32), U32)
    hi = pltpu.bitcast(v[:, c:].astype(BF16).astype(F32), U32)
    return (hi & BF16_HIGH_HALF) | (lo >> 16)


def _unpack_bf16_pairs(p):
    lo = pltpu.bitcast(p << 16, F32).astype(BF16)
    hi = pltpu.bitcast(p & BF16_HIGH_HALF, F32).astype(BF16)
    return jnp.concatenate([lo, hi], axis=1)


def _store_row_planes(ref, v):
    n = v.shape[0]
    p = ref.shape[0] // n
    for j in range(p):
        ref[pl.ds(j, n, stride=p), :] = v[:, j * LANES:(j + 1) * LANES]


def _load_row_planes(ref, p):
    n = ref.shape[0] // p
    return jnp.concatenate([ref[pl.ds(j, n, stride=p), :] for j in range(p)], axis=1)


def _oproj_kernel(a_ref, c_ref, x_ref, woa_ref, woc_ref, g_ref, b_ref, wr_ref,
                  h_ref, hp_ref, lg_ref, *, alpha):
    mix = jnp.dot(a_ref[...], woa_ref[...], preferred_element_type=F32)
    mix = mix + jnp.dot(c_ref[...], woc_ref[...], preferred_element_type=F32)
    h = _layer_norm(alpha * x_ref[...] + mix, g_ref[...], b_ref[...])
    h_ref[...] = h
    _store_row_planes(hp_ref, _pack_bf16_pairs(h))
    h_hi = h.astype(BF16)
    h_lo = (h - h_hi.astype(F32)).astype(BF16)
    logits = jnp.dot(jnp.concatenate([h_hi, h_lo, h_hi], axis=1), wr_ref[...], preferred_element_type=F32)
    lg_ref[...] = logits.T[:N_EXPERTS, :]


def _oproj(attn, conv, x2, woa, woc, g, b, wr, alpha):
    t, d = x2.shape
    tm = TM_PROJ
    full = lambda a: pl.BlockSpec(a.shape, lambda i: (0,) * a.ndim)
    return pl.pallas_call(
        functools.partial(_oproj_kernel, alpha=alpha),
        grid=(t // tm,),
        in_specs=[pl.BlockSpec((tm, attn.shape[1]), lambda i: (i, 0)),
                  pl.BlockSpec((tm, conv.shape[1]), lambda i: (i, 0)),
                  pl.BlockSpec((tm, d), lambda i: (i, 0)),
                  full(woa), full(woc), full(g), full(b), full(wr)],
        out_specs=[pl.BlockSpec((tm, d), lambda i: (i, 0)),
                   pl.BlockSpec((tm * (d // 2 // LANES), LANES), lambda i: (i, 0)),
                   pl.BlockSpec((N_EXPERTS, tm), lambda i: (0, i))],
        out_shape=[jax.ShapeDtypeStruct((t, d), F32),
                   jax.ShapeDtypeStruct((t * (d // 2 // LANES), LANES), U32),
                   jax.ShapeDtypeStruct((N_EXPERTS, t), F32)],
        compiler_params=_cparams(("arbitrary",)),
        name="oproj",
    )(attn, conv, x2, woa, woc, g, b, wr)


def _route_kernel(lg_ref, bias_ref, eidx_ref, rank_ref, wts_ref, cnt_ref, carry_ref):
    ne, tm = lg_ref.shape
    i = pl.program_id(0)

    @pl.when(i == 0)
    def _():
        carry_ref[...] = jnp.zeros(carry_ref.shape, F32)

    scores = _sigmoid(lg_ref[...])
    biased = scores + bias_ref[...]
    neg = -jnp.inf
    sub8 = lax.broadcasted_iota(I32, (GROUP_SIZE, tm), 0)
    gscore = []
    for g in range(N_EXPERT_GROUPS):
        blk = biased[g * GROUP_SIZE:(g + 1) * GROUP_SIZE, :]
        m1 = jnp.max(blk, axis=0, keepdims=True)
        i1 = jnp.min(jnp.where(blk == m1, sub8, GROUP_SIZE), axis=0, keepdims=True)
        m2 = jnp.max(jnp.where(sub8 == i1, neg, blk), axis=0, keepdims=True)
        gscore.append(m1 + m2)
    kept = []
    for g in range(N_EXPERT_GROUPS):
        beat = jnp.zeros((1, tm), I32)
        for o in range(N_EXPERT_GROUPS):
            if o < g:
                beat = beat + (gscore[o] >= gscore[g]).astype(I32)
            elif o > g:
                beat = beat + (gscore[o] > gscore[g]).astype(I32)
        kept.append(jnp.where(beat < TOPK_GROUPS, biased[g * GROUP_SIZE:(g + 1) * GROUP_SIZE, :], neg))
    cur = jnp.concatenate(kept, axis=0)
    sub = lax.broadcasted_iota(I32, (ne, tm), 0)
    sel_any = jnp.zeros((ne, tm), F32)
    picks = []
    wsum = jnp.zeros((1, tm), F32)
    for k in range(TOP_K):
        m = jnp.max(cur, axis=0, keepdims=True)
        ei = jnp.min(jnp.where(cur == m, sub, ne), axis=0, keepdims=True)
        sel = sub == ei
        w = jnp.sum(jnp.where(sel, scores, 0.0), axis=0, keepdims=True)
        cur = jnp.where(sel, neg, cur)
        sel_any = sel_any + sel.astype(F32)
        wsum = wsum + w
        picks.append((ei, sel, w))
    r = lax.broadcasted_iota(I32, (tm, tm), 0)
    c = lax.broadcasted_iota(I32, (tm, tm), 1)
    upper = (r <= c).astype(BF16)
    cum = jnp.dot(sel_any.astype(BF16), upper, preferred_element_type=F32)
    carry = carry_ref[:, 0:1]
    excl = cum - sel_any + carry
    total = carry + jnp.sum(sel_any, axis=1, keepdims=True)
    carry_ref[...] = jnp.broadcast_to(total, carry_ref.shape)
    cnt_ref[...] = jnp.broadcast_to(total, cnt_ref.shape)
    denom = wsum + 1e-20
    eidx_ref[...] = jnp.zeros(eidx_ref.shape, I32)
    rank_ref[...] = jnp.zeros(rank_ref.shape, I32)
    wts_ref[...] = jnp.zeros(wts_ref.shape, F32)
    for k, (ei, sel, w) in enumerate(picks):
        eidx_ref[k:k + 1, :] = ei
        rank_ref[k:k + 1, :] = jnp.sum(jnp.where(sel, excl, 0.0), axis=0, keepdims=True).astype(I32)
        wts_ref[k:k + 1, :] = w / denom * ROUTED_SCALE


def _route(logits_t, bias):
    ne, t = logits_t.shape
    tm = TM_ROUTE
    return pl.pallas_call(
        _route_kernel,
        grid=(t // tm,),
        in_specs=[pl.BlockSpec((ne, tm), lambda i: (0, i)), pl.BlockSpec((ne, 1), lambda i: (0, 0))],
        out_specs=[pl.BlockSpec((TOP_K_PAD, tm), lambda i: (0, i)),
                   pl.BlockSpec((TOP_K_PAD, tm), lambda i: (0, i)),
                   pl.BlockSpec((TOP_K_PAD, tm), lambda i: (0, i)),
                   pl.BlockSpec((ne, LANES), lambda i: (0, 0))],
        out_shape=[jax.ShapeDtypeStruct((TOP_K_PAD, t), I32),
                   jax.ShapeDtypeStruct((TOP_K_PAD, t), I32),
                   jax.ShapeDtypeStruct((TOP_K_PAD, t), F32),
                   jax.ShapeDtypeStruct((ne, LANES), F32)],
        scratch_shapes=[pltpu.VMEM((ne, LANES), F32)],
        compiler_params=_cparams(("arbitrary",)),
        name="route",
    )(logits_t, bias)


def _wait_rows(src_rows, dst_rows, sem, n_rows):
    def body(_, c):
        pltpu.make_async_copy(src_rows, dst_rows, sem).wait()
        return c
    lax.fori_loop(0, n_rows // WAIT_ROWS, body, 0)


def _dispatch_kernel(zfill_ref, slots_ref, hp_ref, xs_hbm, zero_vmem, sem, zsem):
    i = pl.program_id(0)
    te = zero_vmem.shape[0]
    tc = hp_ref.shape[0]
    n_tiles = xs_hbm.shape[0] // te

    def zero_copy(tile):
        return pltpu.make_async_copy(zero_vmem, xs_hbm.at[pl.ds(pl.multiple_of(tile * te, te), te)], zsem)

    @pl.when(i == 0)
    def _():
        zero_vmem[...] = jnp.zeros(zero_vmem.shape, U32)

        def zstart(tile, c):
            @pl.when(zfill_ref[tile] != 0)
            def _():
                zero_copy(tile).start()
            return c

        def zwait(tile, c):
            @pl.when(zfill_ref[tile] != 0)
            def _():
                zero_copy(tile).wait()
            return c

        lax.fori_loop(0, n_tiles, zstart, 0)
        lax.fori_loop(0, n_tiles, zwait, 0)

    base = i * (tc * TOP_K_PAD)

    def issue(t, c):
        for k in range(TOP_K):
            slot = slots_ref[base + t * TOP_K_PAD + k]
            pltpu.make_async_copy(hp_ref.at[t], xs_hbm.at[slot], sem).start(priority=k % 2)
        return c

    lax.fori_loop(0, tc, issue, 0, unroll=ISSUE_UNROLL)
    _wait_rows(hp_ref.at[pl.ds(0, WAIT_ROWS)], xs_hbm.at[pl.ds(0, WAIT_ROWS)], sem, tc * TOP_K)


def _dispatch(zfill, slots_flat, hp, n_rows):
    t, planes, lanes = hp.shape
    tc = TC_DISP
    return pl.pallas_call(
        _dispatch_kernel,
        grid_spec=pltpu.PrefetchScalarGridSpec(
            num_scalar_prefetch=2,
            grid=(t // tc,),
            in_specs=[pl.BlockSpec((tc, planes, lanes), lambda i, z, s: (i, 0, 0))],
            out_specs=pl.BlockSpec(memory_space=pl.ANY),
            scratch_shapes=[pltpu.VMEM((TE_ROWS, planes, lanes), U32),
                            pltpu.SemaphoreType.DMA, pltpu.SemaphoreType.DMA],
        ),
        out_shape=jax.ShapeDtypeStruct((n_rows, planes, lanes), U32),
        compiler_params=_cparams(("arbitrary",)),
        name="dispatch",
    )(zfill, slots_flat, hp)


def _experts_kernel(te_ref, nv_ref, nx_ref, xs_ref, wg_hbm, wu_hbm, wd_hbm, y_ref,
                    sg, su, sd, wgb, wub, wdb, sem):
    i = pl.program_id(0)
    e = te_ref[i]
    prev = te_ref[jnp.maximum(i - 1, 0)]
    valid = i < nv_ref[0]

    def fetch(ex):
        return (pltpu.make_async_copy(wg_hbm.at[ex], sg, sem.at[0]),
                pltpu.make_async_copy(wu_hbm.at[ex], su, sem.at[1]),
                pltpu.make_async_copy(wd_hbm.at[ex], sd, sem.at[2]))

    @pl.when(i == 0)
    def _():
        for cp in fetch(e):
            cp.start()

    first = valid & ((i == 0) | (e != prev))

    def ffn():
        xs = _unpack_bf16_pairs(_load_row_planes(xs_ref, xs_ref.shape[0] // TE_ROWS))
        g = jnp.dot(xs, wgb[...], preferred_element_type=F32)
        u = jnp.dot(xs, wub[...], preferred_element_type=F32)
        hid = (g * _sigmoid(g) * u).astype(BF16)
        y = jnp.dot(hid, wdb[...], preferred_element_type=F32)
        _store_row_planes(y_ref, _pack_bf16_pairs(y))

    @pl.when(first)
    def _():
        for cp in fetch(e):
            cp.wait()
        wgb[...] = sg[...].astype(BF16)
        wub[...] = su[...].astype(BF16)
        wdb[...] = sd[...].astype(BF16)
        for cp in fetch(jnp.where(nx_ref[i] >= 0, nx_ref[i], e)):
            cp.start(priority=1)
        ffn()

    @pl.when(valid & jnp.logical_not(first))
    def _():
        ffn()

    @pl.when(i == pl.num_programs(0) - 1)
    def _():
        for cp in fetch(e):
            cp.wait()

    @pl.when(jnp.logical_not(valid))
    def _():
        y_ref[...] = jnp.zeros(y_ref.shape, U32)


def _experts(tile_expert, n_valid, next_expert, xs, w_gate, w_up, w_down):
    n_rows, xp, _ = xs.shape
    ne, d, ff = w_gate.shape
    te = TE_ROWS
    in_map = lambda i, te_ref, nv_ref, nx_ref: (jnp.minimum(i, nv_ref[0] - 1), 0)
    out_map = lambda i, te_ref, nv_ref, nx_ref: (i, 0)
    hbm = pl.BlockSpec(memory_space=pl.ANY)
    y = pl.pallas_call(
        _experts_kernel,
        grid_spec=pltpu.PrefetchScalarGridSpec(
            num_scalar_prefetch=3,
            grid=(n_rows // te,),
            in_specs=[pl.BlockSpec((te * xp, LANES), in_map), hbm, hbm, hbm],
            out_specs=pl.BlockSpec((te * xp, LANES), out_map),
            scratch_shapes=[pltpu.VMEM((d, ff), F32), pltpu.VMEM((d, ff), F32), pltpu.VMEM((ff, d), F32),
                            pltpu.VMEM((d, ff), BF16), pltpu.VMEM((d, ff), BF16), pltpu.VMEM((ff, d), BF16),
                            pltpu.SemaphoreType.DMA((3,))],
        ),
        out_shape=jax.ShapeDtypeStruct((n_rows * xp, LANES), U32),
        compiler_params=_cparams(("arbitrary",)),
        name="experts",
    )(tile_expert, n_valid, next_expert, xs.reshape(n_rows * xp, LANES), w_gate, w_up, w_down)
    return y.reshape(n_rows, xp, LANES)


def _combine_kernel(slots_ref, y_hbm, h_ref, wts_ref, wsg_ref, wsu_ref, wsd_ref, g_ref, b_ref,
                    o_ref, gbuf, rbuf, shbuf, sem, *, alpha):
    tm, d = o_ref.shape
    planes = gbuf.shape[3]
    i = pl.program_id(0)
    cur = i % 2
    grp = SUBLANES

    def issue_rows(step, t0, n):
        buf = step % 2
        base = step * (tm * TOP_K_PAD)
        for u in range(n):
            for k in range(TOP_K):
                slot = slots_ref[base + (t0 + u) * TOP_K_PAD + k]
                pltpu.make_async_copy(y_hbm.at[slot], gbuf.at[buf, k, t0 + u], sem.at[buf]).start(priority=k % 2)

    @pl.when(i == 0)
    def _():
        def first(t, c):
            issue_rows(0, t, 1)
            return c
        lax.fori_loop(0, tm, first, 0, unroll=ISSUE_UNROLL)

    has_next = i + 1 < pl.num_programs(0)
    head = COMBINE_EARLY_TOKENS

    @pl.when(has_next)
    def _():
        def early(t, c):
            issue_rows(i + 1, t, 1)
            return c
        lax.fori_loop(0, head, early, 0, unroll=ISSUE_UNROLL)

    hb = h_ref[...].astype(BF16)
    sg = jnp.dot(hb, wsg_ref[...], preferred_element_type=F32)
    su = jnp.dot(hb, wsu_ref[...], preferred_element_type=F32)
    shbuf[...] = jnp.dot((sg * _sigmoid(sg) * su).astype(BF16), wsd_ref[...], preferred_element_type=F32)
    _wait_rows(y_hbm.at[pl.ds(0, WAIT_ROWS)], gbuf.at[cur, 0, pl.ds(0, WAIT_ROWS)], sem.at[cur], tm * TOP_K)

    def group(g, prefetch):
        t0 = pl.multiple_of(g * grp, grp)
        packed = [gbuf[cur, k, pl.ds(t0, grp)] for k in range(TOP_K)]
        w = wts_ref[pl.ds(t0, grp)]
        resid = alpha * h_ref[pl.ds(t0, grp), :] + shbuf[pl.ds(t0, grp), :]
        if prefetch:
            issue_rows(i + 1, t0, grp)
        r_lo = r_hi = None
        for k in range(TOP_K):
            lo = w[:, k:k + 1, :] * pltpu.bitcast(packed[k] << 16, F32)
            hi = w[:, k:k + 1, :] * pltpu.bitcast(packed[k] & BF16_HIGH_HALF, F32)
            r_lo = lo if r_lo is None else r_lo + lo
            r_hi = hi if r_hi is None else r_hi + hi
        p0 = pl.multiple_of(t0 * planes, grp * planes)
        rbuf[0, pl.ds(p0, grp * planes), :] = r_lo.reshape(grp * planes, LANES)
        rbuf[1, pl.ds(p0, grp * planes), :] = r_hi.reshape(grp * planes, LANES)
        routed = jnp.concatenate([rbuf[half, pl.ds(p0 + j, grp, stride=planes), :]
                                  for half in range(2) for j in range(planes)], axis=1)
        o_ref[pl.ds(t0, grp), :] = _layer_norm(resid + routed, g_ref[...], b_ref[...])

    @pl.when(has_next)
    def _():
        lax.fori_loop(0, head // grp, lambda g, c: (group(g, False), c)[1], 0, unroll=head // grp)
        lax.fori_loop(head // grp, tm // grp, lambda g, c: (group(g, True), c)[1], 0,
                      unroll=(tm - head) // grp // 4)

    @pl.when(jnp.logical_not(has_next))
    def _():
        lax.fori_loop(0, tm // grp, lambda g, c: (group(g, False), c)[1], 0, unroll=GROUP_UNROLL)


def _combine(slots_flat, y, h, wts_planes, wsg, wsu, wsd, g, b, alpha):
    t, d = h.shape
    tm = TM_COMB
    planes = y.shape[1]
    full = lambda a: pl.BlockSpec(a.shape, lambda i, s: (0,) * a.ndim)
    once = lambda a: pl.BlockSpec(a.shape, lambda i, s: (0,) * a.ndim, pipeline_mode=pl.Buffered(1))
    return pl.pallas_call(
        functools.partial(_combine_kernel, alpha=alpha),
        grid_spec=pltpu.PrefetchScalarGridSpec(
            num_scalar_prefetch=1,
            grid=(t // tm,),
            in_specs=[pl.BlockSpec(memory_space=pl.ANY),
                      pl.BlockSpec((tm, d), lambda i, s: (i, 0)),
                      pl.BlockSpec((tm, TOP_K_PAD, LANES), lambda i, s: (i, 0, 0)),
                      once(wsg), once(wsu), once(wsd), full(g), full(b)],
            out_specs=pl.BlockSpec((tm, d), lambda i, s: (i, 0)),
            scratch_shapes=[pltpu.VMEM((2, TOP_K, tm, planes, LANES), U32),
                            pltpu.VMEM((2, tm * planes, LANES), F32), pltpu.VMEM((tm, d), F32),
                            pltpu.SemaphoreType.DMA((2,))],
        ),
        out_shape=jax.ShapeDtypeStruct((t, d), F32),
        compiler_params=_cparams(("arbitrary",)),
        name="combine",
    )(slots_flat, y, h, wts_planes, wsg, wsu, wsd, g, b)


def _prep_attention_weights(w_in, w_uq, w_ukv):
    d = w_in.shape[0]
    ql = w_uq.shape[0]
    kvl = w_ukv.shape[0]
    half = QK_ROPE_DIM // 2
    pad = LANES - QK_ROPE_DIM
    w_kr = w_in[:, ql + kvl:ql + kvl + QK_ROPE_DIM]
    w_kr_rot = jnp.concatenate([-w_kr[:, half:], w_kr[:, :half]], axis=1)
    zpad = jnp.zeros((d, pad), w_in.dtype)
    wa = jnp.concatenate([w_in[:, :ql + kvl], w_kr, zpad, w_kr_rot, zpad], axis=1).astype(BF16)
    wc = w_in[:, ql + kvl + QK_ROPE_DIM:].astype(BF16)
    uq = w_uq.reshape(ql, N_HEADS, QK_NOPE_DIM + QK_ROPE_DIM)
    uq_nope, uq_rope = uq[..., :QK_NOPE_DIM], uq[..., QK_NOPE_DIM:]
    zq = jnp.zeros((ql, N_HEADS, pad), w_uq.dtype)
    wuq = jnp.concatenate([uq_nope, uq_rope, zq], axis=-1).reshape(ql, N_HEADS * HEAD_QK_PAD).astype(BF16)
    uq_rot = jnp.concatenate([-uq_rope[..., half:], uq_rope[..., :half], zq], axis=-1)
    wuqr = uq_rot.reshape(ql, N_HEADS * LANES).astype(BF16)
    ukv = w_ukv.reshape(kvl, N_HEADS, QK_NOPE_DIM + V_HEAD_DIM)
    wukv = jnp.concatenate([ukv[..., :QK_NOPE_DIM].reshape(kvl, -1), ukv[..., QK_NOPE_DIM:].reshape(kvl, -1)],
                           axis=1).astype(BF16)
    return wa, wc, wuq, wuqr, wukv


def _layer(h, pos2, invf4, batch, seq, alpha, w_in, q_norm_g, w_uq, kv_norm_g, w_ukv, conv_w, conv_b,
           conv_ln_g, conv_ln_b, w_o, ln1_g, ln1_b, w_router, router_bias, w_gate, w_up, w_down,
           ws_gate, ws_up, ws_down, ln2_g, ln2_b):
    t, d = h.shape
    row = lambda a: a.reshape(1, -1)
    wa, wc, wuq, wuqr, wukv = _prep_attention_weights(w_in, w_uq, w_ukv)
    q, k, v = _qkv(h, pos2, invf4, wa, row(q_norm_g), row(kv_norm_g), wuq, wuqr, wukv)
    conv = _conv(h, wc, conv_w, row(conv_b), row(conv_ln_g), row(conv_ln_b), seq)
    attn = _attn(q, k, v, batch, seq)
    aw = attn.shape[1]
    wr32 = jnp.pad(w_router.astype(F32), ((0, 0), (0, LANES - N_EXPERTS)))
    wr_hi = wr32.astype(BF16)
    wr_lo = (wr32 - wr_hi.astype(F32)).astype(BF16)
    wr = jnp.concatenate([wr_hi, wr_hi, wr_lo], axis=0)
    h1, h1p, logits_t = _oproj(attn, conv, h, w_o[:aw].astype(BF16), w_o[aw:].astype(BF16),
                               row(ln1_g), row(ln1_b), wr, alpha)
    return _moe_ffn(h1, h1p, logits_t, alpha, router_bias, w_gate, w_up, w_down,
                    ws_gate, ws_up, ws_down, ln2_g, ln2_b)


def _moe_ffn(h1, h1p, logits_t, alpha, router_bias, w_gate, w_up, w_down, ws_gate, ws_up, ws_down, ln2_g, ln2_b):
    t, d = h1.shape
    row = lambda a: a.reshape(1, -1)
    eidx, rank, wts, cnt = _route(logits_t, router_bias.astype(F32).reshape(N_EXPERTS, 1))
    counts = cnt[:, 0].astype(I32)
    padded = (counts + TE_ROWS - 1) // TE_ROWS * TE_ROWS
    pend = jnp.cumsum(padded)
    poff = pend - padded
    n_rows = t * TOP_K + N_EXPERTS * TE_ROWS
    n_tiles = n_rows // TE_ROWS
    n_valid = (pend[-1] // TE_ROWS).astype(I32).reshape(1)
    tile_row = jnp.minimum(jnp.arange(n_tiles, dtype=I32), n_valid[0] - 1) * TE_ROWS
    tile_expert = jnp.sum((pend[None, :] <= tile_row[:, None]).astype(I32), axis=1)
    tile_expert = jnp.minimum(tile_expert, N_EXPERTS - 1)
    tiles = jnp.arange(n_tiles, dtype=I32)
    last_tile = jnp.where(padded > 0, pend // TE_ROWS - 1, -1)
    zfill = ((tiles >= n_valid[0]) | jnp.any(tiles[:, None] == last_tile[None, :], axis=1)).astype(I32)
    ids = jnp.arange(N_EXPERTS, dtype=I32)
    later = (ids[None, :] > ids[:, None]) & (padded[None, :] > 0)
    next_of = jnp.min(jnp.where(later, ids[None, :], N_EXPERTS), axis=1)
    next_of = jnp.where(next_of == N_EXPERTS, -1, next_of)
    next_expert = jnp.sum(jnp.where(tile_expert[:, None] == ids[None, :], next_of[None, :], 0), axis=1).astype(I32)
    slot_base = jnp.sum(jnp.where(eidx[None] == ids[:, None, None], poff[:, None, None], 0), axis=0)
    slots_flat = (slot_base + rank).T.reshape(-1)
    xs = _dispatch(zfill, slots_flat, h1p.reshape(t, -1, LANES), n_rows)
    y = _experts(tile_expert, n_valid, next_expert, xs, w_gate, w_up, w_down)
    wts_planes = jnp.broadcast_to(wts.T[:, :, None], (t, TOP_K_PAD, LANES))
    return _combine(slots_flat, y, h1, wts_planes, ws_gate.astype(BF16), ws_up.astype(BF16),
                    ws_down.astype(BF16), row(ln2_g), row(ln2_b), alpha)


def kernel(x, positions, w_in, q_norm_g, w_uq, kv_norm_g, w_ukv, conv_w, conv_b, conv_ln_g, conv_ln_b, w_o, ln1_g, ln1_b, w_router, router_bias, w_gate, w_up, w_down, ws_gate, ws_up, ws_down, ln2_g, ln2_b):
    batch, seq, d = x.shape
    depth = w_in.shape[0]
    alpha = (2.0 * depth) ** 0.25
    inv_freq = ROPE_BASE ** (-jnp.arange(0, QK_ROPE_DIM, 2, dtype=F32) / QK_ROPE_DIM)
    invf4 = jnp.tile(inv_freq, LANES // inv_freq.shape[0]).reshape(1, LANES)
    pos2 = positions.reshape(batch * seq, 1)
    h = x.reshape(batch * seq, d)
    for l in range(depth):
        h = _layer(h, pos2, invf4, batch, seq, alpha, w_in[l], q_norm_g[l], w_uq[l], kv_norm_g[l], w_ukv[l],
                   conv_w[l], conv_b[l], conv_ln_g[l], conv_ln_b[l], w_o[l], ln1_g[l], ln1_b[l],
                   w_router[l], router_bias[l], w_gate[l], w_up[l], w_down[l],
                   ws_gate[l], ws_up[l], ws_down[l], ln2_g[l], ln2_b[l])
    return h.reshape(batch, seq, d)
```

```python
import functools

import jax
import jax.numpy as jnp
import numpy as np
from jax import lax
from jax.experimental import pallas as pl
from jax.experimental.pallas import tpu as pltpu

F32 = jnp.float32
BF16 = jnp.bfloat16
I32 = jnp.int32
U32 = jnp.uint32

N_HEADS = 8
QK_NOPE_DIM = 128
QK_ROPE_DIM = 64
V_HEAD_DIM = 128
HEAD_QK_PAD = 256
CONV_KERNEL = 31
N_EXPERTS = 64
TOP_K = 6
TOP_K_PAD = 8
N_EXPERT_GROUPS = 8
GROUP_SIZE = N_EXPERTS // N_EXPERT_GROUPS
TOPK_GROUPS = 4
ROUTED_SCALE = 2.5
ROPE_BASE = 10000.0
LN_EPS = 1e-5
RMS_EPS = 1e-6
LOG2_E = 1.4426950408889634
BF16_HIGH_HALF = np.uint32(0xFFFF0000)

LANES = 128
SUBLANES = 8
CONV_HALO = 32

TM_PROJ = 256
TM_CONV = 512
TQ_ATTN = 512
TK_ATTN = 512
HEADS_PER_ATTN_STEP = 4
ATTN_QUERY_SPLIT = 2
TM_ROUTE = 512
TE_ROWS = 256
TM_COMB = 256
TC_DISP = 1024
WAIT_ROWS = 128
ISSUE_UNROLL = 4
GROUP_UNROLL = 8
CONV_ROWS = 128
CONV_COLS = 256
CONV_SHIFT_BUFS = 4
VMEM_LIMIT = 56 * 1024 * 1024


def _cparams(sem):
    return pltpu.CompilerParams(dimension_semantics=sem, vmem_limit_bytes=VMEM_LIMIT)


def _sigmoid(v):
    return 1.0 / (1.0 + jnp.exp(-v))


def _layer_norm(v, g, b):
    mu = jnp.mean(v, axis=-1, keepdims=True)
    d = v - mu
    var = jnp.mean(d * d, axis=-1, keepdims=True)
    return d * lax.rsqrt(var + LN_EPS) * g + b


def _rms_norm(v, g):
    ms = jnp.mean(v * v, axis=-1, keepdims=True)
    return v * lax.rsqrt(ms + RMS_EPS) * g


def _qkv_kernel(x_ref, pos_ref, invf_ref, wa_ref, qg_ref, kvg_ref, wuq_ref, wuqr_ref, wukv_ref,
                qt_ref, k_ref, vt_ref):
    ql = qg_ref.shape[1]
    kvl = kvg_ref.shape[1]
    xb = x_ref[...].astype(BF16)
    lat = jnp.dot(xb, wa_ref[...], preferred_element_type=F32)
    ang = pos_ref[...].astype(F32) * invf_ref[...]
    cos = jnp.cos(ang)
    sin = jnp.sin(ang)
    cq = _rms_norm(lat[:, :ql], qg_ref[...]).astype(BF16)
    ckv = _rms_norm(lat[:, ql:ql + kvl], kvg_ref[...]).astype(BF16)
    kr = lat[:, ql + kvl:ql + kvl + LANES] * cos + lat[:, ql + kvl + LANES:ql + kvl + 2 * LANES] * sin
    kr = kr.astype(BF16)
    q = jnp.dot(cq, wuq_ref[...], preferred_element_type=F32)
    qrot = jnp.dot(cq, wuqr_ref[...], preferred_element_type=F32)
    kv = jnp.dot(ckv, wukv_ref[...], preferred_element_type=F32)
    for h in range(N_HEADS):
        c0 = h * HEAD_QK_PAD
        qt_ref[c0:c0 + LANES, :] = q[:, c0:c0 + LANES].T.astype(BF16)
        qt_ref[c0 + LANES:c0 + 2 * LANES, :] = (
            q[:, c0 + LANES:c0 + 2 * LANES] * cos + qrot[:, h * LANES:(h + 1) * LANES] * sin).T.astype(BF16)
        k_ref[:, c0:c0 + LANES] = kv[:, h * LANES:(h + 1) * LANES].astype(BF16)
        k_ref[:, c0 + LANES:c0 + 2 * LANES] = kr
    vt_ref[...] = kv[:, N_HEADS * QK_NOPE_DIM:].T.astype(BF16)


def _qkv(x2, pos2, invf4, wa, qg, kvg, wuq, wuqr, wukv):
    t, d = x2.shape
    tm = TM_PROJ
    full = lambda a: pl.BlockSpec(a.shape, lambda i: (0,) * a.ndim)
    return pl.pallas_call(
        _qkv_kernel,
        grid=(t // tm,),
        in_specs=[pl.BlockSpec((tm, d), lambda i: (i, 0)),
                  pl.BlockSpec((tm, 1), lambda i: (i, 0)),
                  full(invf4), full(wa), full(qg), full(kvg), full(wuq), full(wuqr), full(wukv)],
        out_specs=[pl.BlockSpec((N_HEADS * HEAD_QK_PAD, tm), lambda i: (0, i)),
                   pl.BlockSpec((tm, N_HEADS * HEAD_QK_PAD), lambda i: (i, 0)),
                   pl.BlockSpec((N_HEADS * V_HEAD_DIM, tm), lambda i: (0, i))],
        out_shape=[jax.ShapeDtypeStruct((N_HEADS * HEAD_QK_PAD, t), BF16),
                   jax.ShapeDtypeStruct((t, N_HEADS * HEAD_QK_PAD), BF16),
                   jax.ShapeDtypeStruct((N_HEADS * V_HEAD_DIM, t), BF16)],
        compiler_params=_cparams(("arbitrary",)),
        name="qkv",
    )(x2, pos2, invf4, wa, qg, kvg, wuq, wuqr, wukv)


def _conv_kernel(x_ref, wc_ref, cw_ref, cb_ref, g_ref, b_ref, o_ref, ubuf, ybuf, shbuf, *, tiles_per_seq):
    tm, cw = o_ref.shape
    i = pl.program_id(0)
    xb = x_ref[...].astype(BF16)

    @pl.when(i % tiles_per_seq == 0)
    def _():
        ubuf[0:CONV_HALO, :] = jnp.zeros((CONV_HALO, cw), F32)

    shift0 = CONV_HALO - (CONV_KERNEL - 1)
    n_sh = shbuf.shape[0]
    chunk_no = 0
    for c0 in range(0, cw, CONV_COLS):
        a = jnp.dot(xb, wc_ref[:, c0:c0 + CONV_COLS], preferred_element_type=F32)
        gate = jnp.dot(xb, wc_ref[:, cw + c0:cw + c0 + CONV_COLS], preferred_element_type=F32)
        ubuf[CONV_HALO:CONV_HALO + tm, c0:c0 + CONV_COLS] = a * _sigmoid(gate)
        for r0 in range(0, tm, CONV_ROWS):
            sh = shbuf.at[chunk_no % n_sh]
            chunk_no += 1
            win = ubuf[r0:r0 + CONV_ROWS + CONV_HALO, c0:c0 + CONV_COLS]
            for r in range(1, SUBLANES):
                sh[r] = win[r:r + CONV_ROWS + CONV_HALO - SUBLANES, :]
            acc = jnp.zeros((CONV_ROWS, CONV_COLS), F32)
            for k in range(CONV_KERNEL):
                wk = cw_ref[k:k + 1, c0:c0 + CONV_COLS]
                r, j = (shift0 + k) % SUBLANES, (shift0 + k) // SUBLANES
                if r == 0:
                    tap = win[j * SUBLANES:j * SUBLANES + CONV_ROWS, :]
                else:
                    tap = sh[r, j * SUBLANES:j * SUBLANES + CONV_ROWS, :]
                acc = acc + wk * tap
            ybuf[r0:r0 + CONV_ROWS, c0:c0 + CONV_COLS] = acc
    ubuf[0:CONV_HALO, :] = ubuf[tm:tm + CONV_HALO, :]
    y = _layer_norm(ybuf[...] + cb_ref[...], g_ref[...], b_ref[...])
    o_ref[...] = (y * _sigmoid(y)).astype(BF16)


def _conv(x2, wc, cw, cb, g, b, seq):
    t, d = x2.shape
    tm = TM_CONV
    c = cw.shape[1]
    full = lambda a: pl.BlockSpec(a.shape, lambda i: (0,) * a.ndim)
    return pl.pallas_call(
        functools.partial(_conv_kernel, tiles_per_seq=seq // tm),
        grid=(t // tm,),
        in_specs=[pl.BlockSpec((tm, d), lambda i: (i, 0)), full(wc), full(cw), full(cb), full(g), full(b)],
        out_specs=pl.BlockSpec((tm, c), lambda i: (i, 0)),
        out_shape=jax.ShapeDtypeStruct((t, c), BF16),
        scratch_shapes=[pltpu.VMEM((tm + CONV_HALO, c), F32), pltpu.VMEM((tm, c), F32),
                        pltpu.VMEM((CONV_SHIFT_BUFS, SUBLANES, CONV_ROWS + CONV_HALO - SUBLANES, CONV_COLS), F32)],
        compiler_params=_cparams(("arbitrary",)),
        name="conv",
    )(x2, wc, cw, cb, g, b)


def _attn_kernel(qt_ref, k_ref, vt_ref, o_ref, *, scale, tk):
    tq = qt_ref.shape[1]
    heads = qt_ref.shape[0] // HEAD_QK_PAD
    i = pl.program_id(2)
    c = scale * LOG2_E
    n_full = (i * tq) // tk
    q_off = i * tq - n_full * tk
    tqh = tq // ATTN_QUERY_SPLIT
    chains = [(h, s) for h in range(heads) for s in range(ATTN_QUERY_SPLIT)]

    def block(j, carry, masked):
        r0 = pl.multiple_of(j * tk, tk)
        nkeys = [min(tk, (s + 1) * tqh) if (masked and tq == tk) else tk for _, s in chains]
        scores = []
        for (h, s), nk in zip(chains, nkeys):
            qt = qt_ref[h * HEAD_QK_PAD:(h + 1) * HEAD_QK_PAD, s * tqh:(s + 1) * tqh]
            kb = k_ref[pl.ds(r0, nk), h * HEAD_QK_PAD:(h + 1) * HEAD_QK_PAD]
            scores.append(jnp.dot(kb, qt, preferred_element_type=F32))
        out = []
        for (h, s), nk, st, (m, l, acc) in zip(chains, nkeys, scores, carry):
            vtb = vt_ref[h * V_HEAD_DIM:(h + 1) * V_HEAD_DIM, pl.ds(r0, nk)]
            if masked:
                key = lax.broadcasted_iota(I32, (nk, tqh), 0)
                qry = lax.broadcasted_iota(I32, (nk, tqh), 1) + (q_off + s * tqh)
                st = jnp.where(key <= qry, st, -jnp.inf)
            m_blk = jnp.max(jnp.max(st.reshape(SUBLANES, nk // SUBLANES, tqh), axis=0), axis=0, keepdims=True)
            m_new = jnp.maximum(m, m_blk)
            alpha = jnp.exp2((m - m_new) * c)
            pt = jnp.exp2((st - m_new) * c)
            l_blk = jnp.sum(jnp.sum(pt.reshape(SUBLANES, nk // SUBLANES, tqh), axis=0), axis=0, keepdims=True)
            l = alpha * l + l_blk
            acc = alpha * acc + jnp.dot(vtb, pt.astype(BF16), preferred_element_type=F32)
            out.append((m_new, l, acc))
        return tuple(out)

    init = tuple((jnp.full((1, tqh), -jnp.inf, F32), jnp.zeros((1, tqh), F32),
                  jnp.zeros((V_HEAD_DIM, tqh), F32)) for _ in chains)
    carry = lax.fori_loop(0, n_full, lambda j, cr: block(j, cr, False), init)
    final = block(n_full, carry, True)
    for (h, s), (_, l, acc) in zip(chains, final):
        o_ref[s * tqh:(s + 1) * tqh, h * V_HEAD_DIM:(h + 1) * V_HEAD_DIM] = (acc / l).T.astype(BF16)


def _attn(q, k, v, batch, seq):
    tq = TQ_ATTN
    nq = seq // tq
    hb = HEADS_PER_ATTN_STEP
    scale = (QK_NOPE_DIM + QK_ROPE_DIM) ** -0.5
    return pl.pallas_call(
        functools.partial(_attn_kernel, scale=scale, tk=TK_ATTN),
        grid=(batch, N_HEADS // hb, nq),
        in_specs=[pl.BlockSpec((hb * HEAD_QK_PAD, tq), lambda b, h, i: (h, b * nq + i)),
                  pl.BlockSpec((seq, hb * HEAD_QK_PAD), lambda b, h, i: (b, h)),
                  pl.BlockSpec((hb * V_HEAD_DIM, seq), lambda b, h, i: (h, b))],
        out_specs=pl.BlockSpec((tq, hb * V_HEAD_DIM), lambda b, h, i: (b * nq + i, h)),
        out_shape=jax.ShapeDtypeStruct((batch * seq, N_HEADS * V_HEAD_DIM), BF16),
        compiler_params=_cparams(("arbitrary", "arbitrary", "arbitrary")),
        name="attn",
    )(q, k, v)


def _pack_bf16_pairs(v):
    c = v.shape[1] // 2
    lo = pltpu.bitcast(v[:, :c].astype(BF16).astype(F32), U32)
    hi = pltpu.bitcast(v[:, c:].astype(BF16).astype(F32), U32)
    return (hi & BF16_HIGH_HALF) | (lo >> 16)


def _unpack_bf16_pairs(p):
    lo = pltpu.bitcast(p << 16, F32).astype(BF16)
    hi = pltpu.bitcast(p & BF16_HIGH_HALF, F32).astype(BF16)
    return jnp.concatenate([lo, hi], axis=1)


def _store_row_planes(ref, v):
    n = v.shape[0]
    p = ref.shape[0] // n
    for j in range(p):
        ref[pl.ds(j, n, stride=p), :] = v[:, j * LANES:(j + 1) * LANES]


def _load_row_planes(ref, p):
    n = ref.shape[0] // p
    return jnp.concatenate([ref[pl.ds(j, n, stride=p), :] for j in range(p)], axis=1)


def _oproj_kernel(a_ref, c_ref, x_ref, woa_ref, woc_ref, g_ref, b_ref, wr_ref,
                  h_ref, hp_ref, lg_ref, *, alpha):
    mix = jnp.dot(a_ref[...], woa_ref[...], preferred_element_type=F32)
    mix = mix + jnp.dot(c_ref[...], woc_ref[...], preferred_element_type=F32)
    h = _layer_norm(alpha * x_ref[...] + mix, g_ref[...], b_ref[...])
    h_ref[...] = h
    _store_row_planes(hp_ref, _pack_bf16_pairs(h))
    h_hi = h.astype(BF16)
    h_lo = (h - h_hi.astype(F32)).astype(BF16)
    logits = jnp.dot(jnp.concatenate([h_hi, h_lo, h_hi], axis=1), wr_ref[...], preferred_element_type=F32)
    lg_ref[...] = logits.T[:N_EXPERTS, :]


def _oproj(attn, conv, x2, woa, woc, g, b, wr, alpha):
    t, d = x2.shape
    tm = TM_PROJ
    full = lambda a: pl.BlockSpec(a.shape, lambda i: (0,) * a.ndim)
    return pl.pallas_call(
        functools.partial(_oproj_kernel, alpha=alpha),
        grid=(t // tm,),
        in_specs=[pl.BlockSpec((tm, attn.shape[1]), lambda i: (i, 0)),
                  pl.BlockSpec((tm, conv.shape[1]), lambda i: (i, 0)),
                  pl.BlockSpec((tm, d), lambda i: (i, 0)),
                  full(woa), full(woc), full(g), full(b), full(wr)],
        out_specs=[pl.BlockSpec((tm, d), lambda i: (i, 0)),
                   pl.BlockSpec((tm * (d // 2 // LANES), LANES), lambda i: (i, 0)),
                   pl.BlockSpec((N_EXPERTS, tm), lambda i: (0, i))],
        out_shape=[jax.ShapeDtypeStruct((t, d), F32),
                   jax.ShapeDtypeStruct((t * (d // 2 // LANES), LANES), U32),
                   jax.ShapeDtypeStruct((N_EXPERTS, t), F32)],
        compiler_params=_cparams(("arbitrary",)),
        name="oproj",
    )(attn, conv, x2, woa, woc, g, b, wr)


def _route_kernel(lg_ref, bias_ref, eidx_ref, rank_ref, wts_ref, cnt_ref, carry_ref):
    ne, tm = lg_ref.shape
    i = pl.program_id(0)

    @pl.when(i == 0)
    def _():
        carry_ref[...] = jnp.zeros(carry_ref.shape, F32)

    scores = _sigmoid(lg_ref[...])
    biased = scores + bias_ref[...]
    neg = -jnp.inf
    sub8 = lax.broadcasted_iota(I32, (GROUP_SIZE, tm), 0)
    gscore = []
    for g in range(N_EXPERT_GROUPS):
        blk = biased[g * GROUP_SIZE:(g + 1) * GROUP_SIZE, :]
        m1 = jnp.max(blk, axis=0, keepdims=True)
        i1 = jnp.min(jnp.where(blk == m1, sub8, GROUP_SIZE), axis=0, keepdims=True)
        m2 = jnp.max(jnp.where(sub8 == i1, neg, blk), axis=0, keepdims=True)
        gscore.append(m1 + m2)
    kept = []
    for g in range(N_EXPERT_GROUPS):
        beat = jnp.zeros((1, tm), I32)
        for o in range(N_EXPERT_GROUPS):
            if o < g:
                beat = beat + (gscore[o] >= gscore[g]).astype(I32)
            elif o > g:
                beat = beat + (gscore[o] > gscore[g]).astype(I32)
        kept.append(jnp.where(beat < TOPK_GROUPS, biased[g * GROUP_SIZE:(g + 1) * GROUP_SIZE, :], neg))
    cur = jnp.concatenate(kept, axis=0)
    sub = lax.broadcasted_iota(I32, (ne, tm), 0)
    sel_any = jnp.zeros((ne, tm), F32)
    picks = []
    wsum = jnp.zeros((1, tm), F32)
    for k in range(TOP_K):
        m = jnp.max(cur, axis=0, keepdims=True)
        ei = jnp.min(jnp.where(cur == m, sub, ne), axis=0, keepdims=True)
        sel = sub == ei
        w = jnp.sum(jnp.where(sel, scores, 0.0), axis=0, keepdims=True)
        cur = jnp.where(sel, neg, cur)
        sel_any = sel_any + sel.astype(F32)
        wsum = wsum + w
        picks.append((ei, sel, w))
    r = lax.broadcasted_iota(I32, (tm, tm), 0)
    c = lax.broadcasted_iota(I32, (tm, tm), 1)
    upper = (r <= c).astype(BF16)
    cum = jnp.dot(sel_any.astype(BF16), upper, preferred_element_type=F32)
    carry = carry_ref[:, 0:1]
    excl = cum - sel_any + carry
    total = carry + jnp.sum(sel_any, axis=1, keepdims=True)
    carry_ref[...] = jnp.broadcast_to(total, carry_ref.shape)
    cnt_ref[...] = jnp.broadcast_to(total, cnt_ref.shape)
    denom = wsum + 1e-20
    eidx_ref[...] = jnp.zeros(eidx_ref.shape, I32)
    rank_ref[...] = jnp.zeros(rank_ref.shape, I32)
    wts_ref[...] = jnp.zeros(wts_ref.shape, F32)
    for k, (ei, sel, w) in enumerate(picks):
        eidx_ref[k:k + 1, :] = ei
        rank_ref[k:k + 1, :] = jnp.sum(jnp.where(sel, excl, 0.0), axis=0, keepdims=True).astype(I32)
        wts_ref[k:k + 1, :] = w / denom * ROUTED_SCALE


def _route(logits_t, bias):
    ne, t = logits_t.shape
    tm = TM_ROUTE
    return pl.pallas_call(
        _route_kernel,
        grid=(t // tm,),
        in_specs=[pl.BlockSpec((ne, tm), lambda i: (0, i)), pl.BlockSpec((ne, 1), lambda i: (0, 0))],
        out_specs=[pl.BlockSpec((TOP_K_PAD, tm), lambda i: (0, i)),
                   pl.BlockSpec((TOP_K_PAD, tm), lambda i: (0, i)),
                   pl.BlockSpec((TOP_K_PAD, tm), lambda i: (0, i)),
                   pl.BlockSpec((ne, LANES), lambda i: (0, 0))],
        out_shape=[jax.ShapeDtypeStruct((TOP_K_PAD, t), I32),
                   jax.ShapeDtypeStruct((TOP_K_PAD, t), I32),
                   jax.ShapeDtypeStruct((TOP_K_PAD, t), F32),
                   jax.ShapeDtypeStruct((ne, LANES), F32)],
        scratch_shapes=[pltpu.VMEM((ne, LANES), F32)],
        compiler_params=_cparams(("arbitrary",)),
        name="route",
    )(logits_t, bias)


def _wait_rows(src_rows, dst_rows, sem, n_rows):
    def body(_, c):
        pltpu.make_async_copy(src_rows, dst_rows, sem).wait()
        return c
    lax.fori_loop(0, n_rows // WAIT_ROWS, body, 0)


def _dispatch_kernel(zfill_ref, slots_ref, hp_ref, xs_hbm, zero_vmem, sem, zsem):
    i = pl.program_id(0)
    te = zero_vmem.shape[0]
    tc = hp_ref.shape[0]
    n_tiles = xs_hbm.shape[0] // te

    def zero_copy(tile):
        return pltpu.make_async_copy(zero_vmem, xs_hbm.at[pl.ds(pl.multiple_of(tile * te, te), te)], zsem)

    @pl.when(i == 0)
    def _():
        zero_vmem[...] = jnp.zeros(zero_vmem.shape, U32)

        def zstart(tile, c):
            @pl.when(zfill_ref[tile] != 0)
            def _():
                zero_copy(tile).start()
            return c

        def zwait(tile, c):
            @pl.when(zfill_ref[tile] != 0)
            def _():
                zero_copy(tile).wait()
            return c

        lax.fori_loop(0, n_tiles, zstart, 0)
        lax.fori_loop(0, n_tiles, zwait, 0)

    base = i * (tc * TOP_K_PAD)

    def issue(t, c):
        for k in range(TOP_K):
            slot = slots_ref[base + t * TOP_K_PAD + k]
            pltpu.make_async_copy(hp_ref.at[t], xs_hbm.at[slot], sem).start(priority=k % 2)
        return c

    lax.fori_loop(0, tc, issue, 0, unroll=ISSUE_UNROLL)
    _wait_rows(hp_ref.at[pl.ds(0, WAIT_ROWS)], xs_hbm.at[pl.ds(0, WAIT_ROWS)], sem, tc * TOP_K)


def _dispatch(zfill, slots_flat, hp, n_rows):
    t, planes, lanes = hp.shape
    tc = TC_DISP
    return pl.pallas_call(
        _dispatch_kernel,
        grid_spec=pltpu.PrefetchScalarGridSpec(
            num_scalar_prefetch=2,
            grid=(t // tc,),
            in_specs=[pl.BlockSpec((tc, planes, lanes), lambda i, z, s: (i, 0, 0))],
            out_specs=pl.BlockSpec(memory_space=pl.ANY),
            scratch_shapes=[pltpu.VMEM((TE_ROWS, planes, lanes), U32),
                            pltpu.SemaphoreType.DMA, pltpu.SemaphoreType.DMA],
        ),
        out_shape=jax.ShapeDtypeStruct((n_rows, planes, lanes), U32),
        compiler_params=_cparams(("arbitrary",)),
        name="dispatch",
    )(zfill, slots_flat, hp)


def _experts_kernel(te_ref, nv_ref, nx_ref, xs_ref, wg_hbm, wu_hbm, wd_hbm, y_ref,
                    sg, su, sd, wgb, wub, wdb, sem):
    i = pl.program_id(0)
    e = te_ref[i]
    prev = te_ref[jnp.maximum(i - 1, 0)]
    valid = i < nv_ref[0]

    def fetch(ex):
        return (pltpu.make_async_copy(wg_hbm.at[ex], sg, sem.at[0]),
                pltpu.make_async_copy(wu_hbm.at[ex], su, sem.at[1]),
                pltpu.make_async_copy(wd_hbm.at[ex], sd, sem.at[2]))

    @pl.when(i == 0)
    def _():
        for cp in fetch(e):
            cp.start()

    first = valid & ((i == 0) | (e != prev))

    def ffn():
        xs = _unpack_bf16_pairs(_load_row_planes(xs_ref, xs_ref.shape[0] // TE_ROWS))
        g = jnp.dot(xs, wgb[...], preferred_element_type=F32)
        u = jnp.dot(xs, wub[...], preferred_element_type=F32)
        hid = (g * _sigmoid(g) * u).astype(BF16)
        y = jnp.dot(hid, wdb[...], preferred_element_type=F32)
        _store_row_planes(y_ref, _pack_bf16_pairs(y))

    @pl.when(first)
    def _():
        for cp in fetch(e):
            cp.wait()
        wgb[...] = sg[...].astype(BF16)
        wub[...] = su[...].astype(BF16)
        wdb[...] = sd[...].astype(BF16)
        for cp in fetch(jnp.where(nx_ref[i] >= 0, nx_ref[i], e)):
            cp.start(priority=1)
        ffn()

    @pl.when(valid & jnp.logical_not(first))
    def _():
        ffn()

    @pl.when(i == pl.num_programs(0) - 1)
    def _():
        for cp in fetch(e):
            cp.wait()

    @pl.when(jnp.logical_not(valid))
    def _():
        y_ref[...] = jnp.zeros(y_ref.shape, U32)


def _experts(tile_expert, n_valid, next_expert, xs, w_gate, w_up, w_down):
    n_rows, xp, _ = xs.shape
    ne, d, ff = w_gate.shape
    te = TE_ROWS
    in_map = lambda i, te_ref, nv_ref, nx_ref: (jnp.minimum(i, nv_ref[0] - 1), 0)
    out_map = lambda i, te_ref, nv_ref, nx_ref: (i, 0)
    hbm = pl.BlockSpec(memory_space=pl.ANY)
    y = pl.pallas_call(
        _experts_kernel,
        grid_spec=pltpu.PrefetchScalarGridSpec(
            num_scalar_prefetch=3,
            grid=(n_rows // te,),
            in_specs=[pl.BlockSpec((te * xp, LANES), in_map), hbm, hbm, hbm],
            out_specs=pl.BlockSpec((te * xp, LANES), out_map),
            scratch_shapes=[pltpu.VMEM((d, ff), F32), pltpu.VMEM((d, ff), F32), pltpu.VMEM((ff, d), F32),
                            pltpu.VMEM((d, ff), BF16), pltpu.VMEM((d, ff), BF16), pltpu.VMEM((ff, d), BF16),
                            pltpu.SemaphoreType.DMA((3,))],
        ),
        out_shape=jax.ShapeDtypeStruct((n_rows * xp, LANES), U32),
        compiler_params=_cparams(("arbitrary",)),
        name="experts",
    )(tile_expert, n_valid, next_expert, xs.reshape(n_rows * xp, LANES), w_gate, w_up, w_down)
    return y.reshape(n_rows, xp, LANES)


def _combine_kernel(slots_ref, y_hbm, h_ref, wts_ref, wsg_ref, wsu_ref, wsd_ref, g_ref, b_ref,
                    o_ref, gbuf, rbuf, shbuf, sem, *, alpha):
    tm, d = o_ref.shape
    planes = gbuf.shape[3]
    i = pl.program_id(0)
    cur = i % 2
    grp = SUBLANES

    def issue_rows(step, t0, n):
        buf = step % 2
        base = step * (tm * TOP_K_PAD)
        for u in range(n):
            for k in range(TOP_K):
                slot = slots_ref[base + (t0 + u) * TOP_K_PAD + k]
                pltpu.make_async_copy(y_hbm.at[slot], gbuf.at[buf, k, t0 + u], sem.at[buf]).start(priority=k % 2)

    @pl.when(i == 0)
    def _():
        def first(t, c):
            issue_rows(0, t, 1)
            return c
        lax.fori_loop(0, tm, first, 0, unroll=ISSUE_UNROLL)

    hb = h_ref[...].astype(BF16)
    sg = jnp.dot(hb, wsg_ref[...], preferred_element_type=F32)
    su = jnp.dot(hb, wsu_ref[...], preferred_element_type=F32)
    shbuf[...] = jnp.dot((sg * _sigmoid(sg) * su).astype(BF16), wsd_ref[...], preferred_element_type=F32)
    _wait_rows(y_hbm.at[pl.ds(0, WAIT_ROWS)], gbuf.at[cur, 0, pl.ds(0, WAIT_ROWS)], sem.at[cur], tm * TOP_K)

    def group(g, prefetch):
        t0 = pl.multiple_of(g * grp, grp)
        packed = [gbuf[cur, k, pl.ds(t0, grp)] for k in range(TOP_K)]
        w = wts_ref[pl.ds(t0, grp)]
        resid = alpha * h_ref[pl.ds(t0, grp), :] + shbuf[pl.ds(t0, grp), :]
        if prefetch:
            issue_rows(i + 1, t0, grp)
        r_lo = r_hi = None
        for k in range(TOP_K):
            lo = w[:, k:k + 1, :] * pltpu.bitcast(packed[k] << 16, F32)
            hi = w[:, k:k + 1, :] * pltpu.bitcast(packed[k] & BF16_HIGH_HALF, F32)
            r_lo = lo if r_lo is None else r_lo + lo
            r_hi = hi if r_hi is None else r_hi + hi
        p0 = pl.multiple_of(t0 * planes, grp * planes)
        rbuf[0, pl.ds(p0, grp * planes), :] = r_lo.reshape(grp * planes, LANES)
        rbuf[1, pl.ds(p0, grp * planes), :] = r_hi.reshape(grp * planes, LANES)
        routed = jnp.concatenate([rbuf[half, pl.ds(p0 + j, grp, stride=planes), :]
                                  for half in range(2) for j in range(planes)], axis=1)
        o_ref[pl.ds(t0, grp), :] = _layer_norm(resid + routed, g_ref[...], b_ref[...])

    has_next = i + 1 < pl.num_programs(0)

    @pl.when(has_next)
    def _():
        lax.fori_loop(0, tm // grp, lambda g, c: (group(g, True), c)[1], 0, unroll=GROUP_UNROLL)

    @pl.when(jnp.logical_not(has_next))
    def _():
        lax.fori_loop(0, tm // grp, lambda g, c: (group(g, False), c)[1], 0, unroll=GROUP_UNROLL)


def _combine(slots_flat, y, h, wts_planes, wsg, wsu, wsd, g, b, alpha):
    t, d = h.shape
    tm = TM_COMB
    planes = y.shape[1]
    full = lambda a: pl.BlockSpec(a.shape, lambda i, s: (0,) * a.ndim)
    once = lambda a: pl.BlockSpec(a.shape, lambda i, s: (0,) * a.ndim, pipeline_mode=pl.Buffered(1))
    return pl.pallas_call(
        functools.partial(_combine_kernel, alpha=alpha),
        grid_spec=pltpu.PrefetchScalarGridSpec(
            num_scalar_prefetch=1,
            grid=(t // tm,),
            in_specs=[pl.BlockSpec(memory_space=pl.ANY),
                      pl.BlockSpec((tm, d), lambda i, s: (i, 0)),
                      pl.BlockSpec((tm, TOP_K_PAD, LANES), lambda i, s: (i, 0, 0)),
                      once(wsg), once(wsu), once(wsd), full(g), full(b)],
            out_specs=pl.BlockSpec((tm, d), lambda i, s: (i, 0)),
            scratch_shapes=[pltpu.VMEM((2, TOP_K, tm, planes, LANES), U32),
                            pltpu.VMEM((2, tm * planes, LANES), F32), pltpu.VMEM((tm, d), F32),
                            pltpu.SemaphoreType.DMA((2,))],
        ),
        out_shape=jax.ShapeDtypeStruct((t, d), F32),
        compiler_params=_cparams(("arbitrary",)),
        name="combine",
    )(slots_flat, y, h, wts_planes, wsg, wsu, wsd, g, b)


def _prep_attention_weights(w_in, w_uq, w_ukv):
    d = w_in.shape[0]
    ql = w_uq.shape[0]
    kvl = w_ukv.shape[0]
    half = QK_ROPE_DIM // 2
    pad = LANES - QK_ROPE_DIM
    w_kr = w_in[:, ql + kvl:ql + kvl + QK_ROPE_DIM]
    w_kr_rot = jnp.concatenate([-w_kr[:, half:], w_kr[:, :half]], axis=1)
    zpad = jnp.zeros((d, pad), w_in.dtype)
    wa = jnp.concatenate([w_in[:, :ql + kvl], w_kr, zpad, w_kr_rot, zpad], axis=1).astype(BF16)
    wc = w_in[:, ql + kvl + QK_ROPE_DIM:].astype(BF16)
    uq = w_uq.reshape(ql, N_HEADS, QK_NOPE_DIM + QK_ROPE_DIM)
    uq_nope, uq_rope = uq[..., :QK_NOPE_DIM], uq[..., QK_NOPE_DIM:]
    zq = jnp.zeros((ql, N_HEADS, pad), w_uq.dtype)
    wuq = jnp.concatenate([uq_nope, uq_rope, zq], axis=-1).reshape(ql, N_HEADS * HEAD_QK_PAD).astype(BF16)
    uq_rot = jnp.concatenate([-uq_rope[..., half:], uq_rope[..., :half], zq], axis=-1)
    wuqr = uq_rot.reshape(ql, N_HEADS * LANES).astype(BF16)
    ukv = w_ukv.reshape(kvl, N_HEADS, QK_NOPE_DIM + V_HEAD_DIM)
    wukv = jnp.concatenate([ukv[..., :QK_NOPE_DIM].reshape(kvl, -1), ukv[..., QK_NOPE_DIM:].reshape(kvl, -1)],
                           axis=1).astype(BF16)
    return wa, wc, wuq, wuqr, wukv


def _layer(h, pos2, invf4, batch, seq, alpha, w_in, q_norm_g, w_uq, kv_norm_g, w_ukv, conv_w, conv_b,
           conv_ln_g, conv_ln_b, w_o, ln1_g, ln1_b, w_router, router_bias, w_gate, w_up, w_down,
           ws_gate, ws_up, ws_down, ln2_g, ln2_b):
    t, d = h.shape
    row = lambda a: a.reshape(1, -1)
    wa, wc, wuq, wuqr, wukv = _prep_attention_weights(w_in, w_uq, w_ukv)
    q, k, v = _qkv(h, pos2, invf4, wa, row(q_norm_g), row(kv_norm_g), wuq, wuqr, wukv)
    conv = _conv(h, wc, conv_w, row(conv_b), row(conv_ln_g), row(conv_ln_b), seq)
    attn = _attn(q, k, v, batch, seq)
    aw = attn.shape[1]
    wr32 = jnp.pad(w_router.astype(F32), ((0, 0), (0, LANES - N_EXPERTS)))
    wr_hi = wr32.astype(BF16)
    wr_lo = (wr32 - wr_hi.astype(F32)).astype(BF16)
    wr = jnp.concatenate([wr_hi, wr_hi, wr_lo], axis=0)
    h1, h1p, logits_t = _oproj(attn, conv, h, w_o[:aw].astype(BF16), w_o[aw:].astype(BF16),
                               row(ln1_g), row(ln1_b), wr, alpha)
    return _moe_ffn(h1, h1p, logits_t, alpha, router_bias, w_gate, w_up, w_down,
                    ws_gate, ws_up, ws_down, ln2_g, ln2_b)


def _moe_ffn(h1, h1p, logits_t, alpha, router_bias, w_gate, w_up, w_down, ws_gate, ws_up, ws_down, ln2_g, ln2_b):
    t, d = h1.shape
    row = lambda a: a.reshape(1, -1)
    eidx, rank, wts, cnt = _route(logits_t, router_bias.astype(F32).reshape(N_EXPERTS, 1))
    counts = cnt[:, 0].astype(I32)
    padded = (counts + TE_ROWS - 1) // TE_ROWS * TE_ROWS
    pend = jnp.cumsum(padded)
    poff = pend - padded
    n_rows = t * TOP_K + N_EXPERTS * TE_ROWS
    n_tiles = n_rows // TE_ROWS
    n_valid = (pend[-1] // TE_ROWS).astype(I32).reshape(1)
    tile_row = jnp.minimum(jnp.arange(n_tiles, dtype=I32), n_valid[0] - 1) * TE_ROWS
    tile_expert = jnp.sum((pend[None, :] <= tile_row[:, None]).astype(I32), axis=1)
    tile_expert = jnp.minimum(tile_expert, N_EXPERTS - 1)
    tiles = jnp.arange(n_tiles, dtype=I32)
    last_tile = jnp.where(padded > 0, pend // TE_ROWS - 1, -1)
    zfill = ((tiles >= n_valid[0]) | jnp.any(tiles[:, None] == last_tile[None, :], axis=1)).astype(I32)
    ids = jnp.arange(N_EXPERTS, dtype=I32)
    later = (ids[None, :] > ids[:, None]) & (padded[None, :] > 0)
    next_of = jnp.min(jnp.where(later, ids[None, :], N_EXPERTS), axis=1)
    next_of = jnp.where(next_of == N_EXPERTS, -1, next_of)
    next_expert = jnp.sum(jnp.where(tile_expert[:, None] == ids[None, :], next_of[None, :], 0), axis=1).astype(I32)
    slot_base = jnp.sum(jnp.where(eidx[None] == ids[:, None, None], poff[:, None, None], 0), axis=0)
    slots_flat = (slot_base + rank).T.reshape(-1)
    xs = _dispatch(zfill, slots_flat, h1p.reshape(t, -1, LANES), n_rows)
    y = _experts(tile_expert, n_valid, next_expert, xs, w_gate, w_up, w_down)
    wts_planes = jnp.broadcast_to(wts.T[:, :, None], (t, TOP_K_PAD, LANES))
    return _combine(slots_flat, y, h1, wts_planes, ws_gate.astype(BF16), ws_up.astype(BF16),
                    ws_down.astype(BF16), row(ln2_g), row(ln2_b), alpha)


def kernel(x, positions, w_in, q_norm_g, w_uq, kv_norm_g, w_ukv, conv_w, conv_b, conv_ln_g, conv_ln_b, w_o, ln1_g, ln1_b, w_router, router_bias, w_gate, w_up, w_down, ws_gate, ws_up, ws_down, ln2_g, ln2_b):
    batch, seq, d = x.shape
    depth = w_in.shape[0]
    alpha = (2.0 * depth) ** 0.25
    inv_freq = ROPE_BASE ** (-jnp.arange(0, QK_ROPE_DIM, 2, dtype=F32) / QK_ROPE_DIM)
    invf4 = jnp.tile(inv_freq, LANES // inv_freq.shape[0]).reshape(1, LANES)
    pos2 = positions.reshape(batch * seq, 1)
    h = x.reshape(batch * seq, d)
    for l in range(depth):
        h = _layer(h, pos2, invf4, batch, seq, alpha, w_in[l], q_norm_g[l], w_uq[l], kv_norm_g[l], w_ukv[l],
                   conv_w[l], conv_b[l], conv_ln_g[l], conv_ln_b[l], w_o[l], ln1_g[l], ln1_b[l],
                   w_router[l], router_bias[l], w_gate[l], w_up[l], w_down[l],
                   ws_gate[l], ws_up[l], ws_down[l], ln2_g[l], ln2_b[l])
    return h.reshape(batch, seq, d)
```

```python
import functools

import jax
import jax.numpy as jnp
import numpy as np
from jax import lax
from jax.experimental import pallas as pl
from jax.experimental.pallas import tpu as pltpu

F32 = jnp.float32
BF16 = jnp.bfloat16
I32 = jnp.int32
U32 = jnp.uint32

N_HEADS = 8
QK_NOPE_DIM = 128
QK_ROPE_DIM = 64
V_HEAD_DIM = 128
HEAD_QK_PAD = 256
CONV_KERNEL = 31
N_EXPERTS = 64
TOP_K = 6
TOP_K_PAD = 8
N_EXPERT_GROUPS = 8
GROUP_SIZE = N_EXPERTS // N_EXPERT_GROUPS
TOPK_GROUPS = 4
ROUTED_SCALE = 2.5
ROPE_BASE = 10000.0
LN_EPS = 1e-5
RMS_EPS = 1e-6
LOG2_E = 1.4426950408889634
BF16_HIGH_HALF = np.uint32(0xFFFF0000)

LANES = 128
SUBLANES = 8
CONV_HALO = 32

TM_PROJ = 256
TM_CONV = 512
TQ_ATTN = 512
TK_ATTN = 512
HEADS_PER_ATTN_STEP = 4
ATTN_QUERY_SPLIT = 2
ATTN_SCORES_AHEAD = 4
TM_ROUTE = 512
TE_ROWS = 256
TM_COMB = 256
TC_DISP = 1024
WAIT_ROWS = 128
ISSUE_UNROLL = 4
GROUP_UNROLL = 8
CONV_ROWS = 128
CONV_COLS = 256
CONV_SHIFT_BUFS = 4
VMEM_LIMIT = 56 * 1024 * 1024


def _cparams(sem):
    return pltpu.CompilerParams(dimension_semantics=sem, vmem_limit_bytes=VMEM_LIMIT)


def _sigmoid(v):
    return 1.0 / (1.0 + jnp.exp(-v))


def _layer_norm(v, g, b):
    mu = jnp.mean(v, axis=-1, keepdims=True)
    d = v - mu
    var = jnp.mean(d * d, axis=-1, keepdims=True)
    return d * lax.rsqrt(var + LN_EPS) * g + b


def _rms_norm(v, g):
    ms = jnp.mean(v * v, axis=-1, keepdims=True)
    return v * lax.rsqrt(ms + RMS_EPS) * g


def _qkv_kernel(x_ref, pos_ref, invf_ref, wa_ref, qg_ref, kvg_ref, wuq_ref, wuqr_ref, wukv_ref,
                qt_ref, k_ref, vt_ref):
    ql = qg_ref.shape[1]
    kvl = kvg_ref.shape[1]
    xb = x_ref[...].astype(BF16)
    lat = jnp.dot(xb, wa_ref[...], preferred_element_type=F32)
    ang = pos_ref[...].astype(F32) * invf_ref[...]
    cos = jnp.cos(ang)
    sin = jnp.sin(ang)
    cq = _rms_norm(lat[:, :ql], qg_ref[...]).astype(BF16)
    ckv = _rms_norm(lat[:, ql:ql + kvl], kvg_ref[...]).astype(BF16)
    kr = lat[:, ql + kvl:ql + kvl + LANES] * cos + lat[:, ql + kvl + LANES:ql + kvl + 2 * LANES] * sin
    kr = kr.astype(BF16)
    q = jnp.dot(cq, wuq_ref[...], preferred_element_type=F32)
    qrot = jnp.dot(cq, wuqr_ref[...], preferred_element_type=F32)
    kv = jnp.dot(ckv, wukv_ref[...], preferred_element_type=F32)
    for h in range(N_HEADS):
        c0 = h * HEAD_QK_PAD
        qt_ref[c0:c0 + LANES, :] = q[:, c0:c0 + LANES].T.astype(BF16)
        qt_ref[c0 + LANES:c0 + 2 * LANES, :] = (
            q[:, c0 + LANES:c0 + 2 * LANES] * cos + qrot[:, h * LANES:(h + 1) * LANES] * sin).T.astype(BF16)
        k_ref[:, c0:c0 + LANES] = kv[:, h * LANES:(h + 1) * LANES].astype(BF16)
        k_ref[:, c0 + LANES:c0 + 2 * LANES] = kr
    vt_ref[...] = kv[:, N_HEADS * QK_NOPE_DIM:].T.astype(BF16)


def _qkv(x2, pos2, invf4, wa, qg, kvg, wuq, wuqr, wukv):
    t, d = x2.shape
    tm = TM_PROJ
    full = lambda a: pl.BlockSpec(a.shape, lambda i: (0,) * a.ndim)
    return pl.pallas_call(
        _qkv_kernel,
        grid=(t // tm,),
        in_specs=[pl.BlockSpec((tm, d), lambda i: (i, 0)),
                  pl.BlockSpec((tm, 1), lambda i: (i, 0)),
                  full(invf4), full(wa), full(qg), full(kvg), full(wuq), full(wuqr), full(wukv)],
        out_specs=[pl.BlockSpec((N_HEADS * HEAD_QK_PAD, tm), lambda i: (0, i)),
                   pl.BlockSpec((tm, N_HEADS * HEAD_QK_PAD), lambda i: (i, 0)),
                   pl.BlockSpec((N_HEADS * V_HEAD_DIM, tm), lambda i: (0, i))],
        out_shape=[jax.ShapeDtypeStruct((N_HEADS * HEAD_QK_PAD, t), BF16),
                   jax.ShapeDtypeStruct((t, N_HEADS * HEAD_QK_PAD), BF16),
                   jax.ShapeDtypeStruct((N_HEADS * V_HEAD_DIM, t), BF16)],
        compiler_params=_cparams(("arbitrary",)),
        name="qkv",
    )(x2, pos2, invf4, wa, qg, kvg, wuq, wuqr, wukv)


def _conv_kernel(x_ref, wc_ref, cw_ref, cb_ref, g_ref, b_ref, o_ref, ubuf, ybuf, shbuf, *, tiles_per_seq):
    tm, cw = o_ref.shape
    i = pl.program_id(0)
    xb = x_ref[...].astype(BF16)

    @pl.when(i % tiles_per_seq == 0)
    def _():
        ubuf[0:CONV_HALO, :] = jnp.zeros((CONV_HALO, cw), F32)

    shift0 = CONV_HALO - (CONV_KERNEL - 1)
    n_sh = shbuf.shape[0]
    chunk_no = 0
    for c0 in range(0, cw, CONV_COLS):
        a = jnp.dot(xb, wc_ref[:, c0:c0 + CONV_COLS], preferred_element_type=F32)
        gate = jnp.dot(xb, wc_ref[:, cw + c0:cw + c0 + CONV_COLS], preferred_element_type=F32)
        ubuf[CONV_HALO:CONV_HALO + tm, c0:c0 + CONV_COLS] = a * _sigmoid(gate)
        for r0 in range(0, tm, CONV_ROWS):
            sh = shbuf.at[chunk_no % n_sh]
            chunk_no += 1
            win = ubuf[r0:r0 + CONV_ROWS + CONV_HALO, c0:c0 + CONV_COLS]
            for r in range(1, SUBLANES):
                sh[r] = win[r:r + CONV_ROWS + CONV_HALO - SUBLANES, :]
            acc = jnp.zeros((CONV_ROWS, CONV_COLS), F32)
            for k in range(CONV_KERNEL):
                wk = cw_ref[k:k + 1, c0:c0 + CONV_COLS]
                r, j = (shift0 + k) % SUBLANES, (shift0 + k) // SUBLANES
                if r == 0:
                    tap = win[j * SUBLANES:j * SUBLANES + CONV_ROWS, :]
                else:
                    tap = sh[r, j * SUBLANES:j * SUBLANES + CONV_ROWS, :]
                acc = acc + wk * tap
            ybuf[r0:r0 + CONV_ROWS, c0:c0 + CONV_COLS] = acc
    ubuf[0:CONV_HALO, :] = ubuf[tm:tm + CONV_HALO, :]
    y = _layer_norm(ybuf[...] + cb_ref[...], g_ref[...], b_ref[...])
    o_ref[...] = (y * _sigmoid(y)).astype(BF16)


def _conv(x2, wc, cw, cb, g, b, seq):
    t, d = x2.shape
    tm = TM_CONV
    c = cw.shape[1]
    full = lambda a: pl.BlockSpec(a.shape, lambda i: (0,) * a.ndim)
    return pl.pallas_call(
        functools.partial(_conv_kernel, tiles_per_seq=seq // tm),
        grid=(t // tm,),
        in_specs=[pl.BlockSpec((tm, d), lambda i: (i, 0)), full(wc), full(cw), full(cb), full(g), full(b)],
        out_specs=pl.BlockSpec((tm, c), lambda i: (i, 0)),
        out_shape=jax.ShapeDtypeStruct((t, c), BF16),
        scratch_shapes=[pltpu.VMEM((tm + CONV_HALO, c), F32), pltpu.VMEM((tm, c), F32),
                        pltpu.VMEM((CONV_SHIFT_BUFS, SUBLANES, CONV_ROWS + CONV_HALO - SUBLANES, CONV_COLS), F32)],
        compiler_params=_cparams(("arbitrary",)),
        name="conv",
    )(x2, wc, cw, cb, g, b)


def _attn_kernel(qt_ref, k_ref, vt_ref, o_ref, *, scale, tk):
    tq = qt_ref.shape[1]
    heads = qt_ref.shape[0] // HEAD_QK_PAD
    i = pl.program_id(2)
    c = scale * LOG2_E
    n_full = (i * tq) // tk
    q_off = i * tq - n_full * tk
    tqh = tq // ATTN_QUERY_SPLIT
    chains = [(h, s) for h in range(heads) for s in range(ATTN_QUERY_SPLIT)]

    def block(j, carry, masked):
        r0 = pl.multiple_of(j * tk, tk)
        nkeys = [min(tk, (s + 1) * tqh) if (masked and tq == tk) else tk for _, s in chains]
        def score(n):
            (h, s), nk = chains[n], nkeys[n]
            qt = qt_ref[h * HEAD_QK_PAD:(h + 1) * HEAD_QK_PAD, s * tqh:(s + 1) * tqh]
            kb = k_ref[pl.ds(r0, nk), h * HEAD_QK_PAD:(h + 1) * HEAD_QK_PAD]
            return jnp.dot(kb, qt, preferred_element_type=F32)

        pending = [score(n) for n in range(min(ATTN_SCORES_AHEAD, len(chains)))]
        out = []
        for n, ((h, s), nk, (m, l, acc)) in enumerate(zip(chains, nkeys, carry)):
            st = pending.pop(0)
            if n + ATTN_SCORES_AHEAD < len(chains):
                pending.append(score(n + ATTN_SCORES_AHEAD))
            vtb = vt_ref[h * V_HEAD_DIM:(h + 1) * V_HEAD_DIM, pl.ds(r0, nk)]
            if masked:
                key = lax.broadcasted_iota(I32, (nk, tqh), 0)
                qry = lax.broadcasted_iota(I32, (nk, tqh), 1) + (q_off + s * tqh)
                st = jnp.where(key <= qry, st, -jnp.inf)
            m_blk = jnp.max(jnp.max(st.reshape(SUBLANES, nk // SUBLANES, tqh), axis=0), axis=0, keepdims=True)
            m_new = jnp.maximum(m, m_blk)
            alpha = jnp.exp2((m - m_new) * c)
            pt = jnp.exp2((st - m_new) * c)
            l_blk = jnp.sum(jnp.sum(pt.reshape(SUBLANES, nk // SUBLANES, tqh), axis=0), axis=0, keepdims=True)
            l = alpha * l + l_blk
            acc = alpha * acc + jnp.dot(vtb, pt.astype(BF16), preferred_element_type=F32)
            out.append((m_new, l, acc))
        return tuple(out)

    init = tuple((jnp.full((1, tqh), -jnp.inf, F32), jnp.zeros((1, tqh), F32),
                  jnp.zeros((V_HEAD_DIM, tqh), F32)) for _ in chains)
    carry = lax.fori_loop(0, n_full, lambda j, cr: block(j, cr, False), init)
    final = block(n_full, carry, True)
    for (h, s), (_, l, acc) in zip(chains, final):
        o_ref[s * tqh:(s + 1) * tqh, h * V_HEAD_DIM:(h + 1) * V_HEAD_DIM] = (acc / l).T.astype(BF16)


def _attn(q, k, v, batch, seq):
    tq = TQ_ATTN
    nq = seq // tq
    hb = HEADS_PER_ATTN_STEP
    scale = (QK_NOPE_DIM + QK_ROPE_DIM) ** -0.5
    return pl.pallas_call(
        functools.partial(_attn_kernel, scale=scale, tk=TK_ATTN),
        grid=(batch, N_HEADS // hb, nq),
        in_specs=[pl.BlockSpec((hb * HEAD_QK_PAD, tq), lambda b, h, i: (h, b * nq + i)),
                  pl.BlockSpec((seq, hb * HEAD_QK_PAD), lambda b, h, i: (b, h)),
                  pl.BlockSpec((hb * V_HEAD_DIM, seq), lambda b, h, i: (h, b))],
        out_specs=pl.BlockSpec((tq, hb * V_HEAD_DIM), lambda b, h, i: (b * nq + i, h)),
        out_shape=jax.ShapeDtypeStruct((batch * seq, N_HEADS * V_HEAD_DIM), BF16),
        compiler_params=_cparams(("arbitrary", "arbitrary", "arbitrary")),
        name="attn",
    )(q, k, v)


def _pack_bf16_pairs(v):
    c = v.shape[1] // 2
    lo = pltpu.bitcast(v[:, :c].astype(BF16).astype(F32), U32)
    hi = pltpu.bitcast(v[:, c:].astype(BF16).astype(F32), U32)
    return (hi & BF16_HIGH_HALF) | (lo >> 16)


def _unpack_bf16_pairs(p):
    lo = pltpu.bitcast(p << 16, F32).astype(BF16)
    hi = pltpu.bitcast(p & BF16_HIGH_HALF, F32).astype(BF16)
    return jnp.concatenate([lo, hi], axis=1)


def _store_row_planes(ref, v):
    n = v.shape[0]
    p = ref.shape[0] // n
    for j in range(p):
        ref[pl.ds(j, n, stride=p), :] = v[:, j * LANES:(j + 1) * LANES]


def _load_row_planes(ref, p):
    n = ref.shape[0] // p
    return jnp.concatenate([ref[pl.ds(j, n, stride=p), :] for j in range(p)], axis=1)


def _oproj_kernel(a_ref, c_ref, x_ref, woa_ref, woc_ref, g_ref, b_ref, wr_ref,
                  h_ref, hp_ref, lg_ref, *, alpha):
    mix = jnp.dot(a_ref[...], woa_ref[...], preferred_element_type=F32)
    mix = mix + jnp.dot(c_ref[...], woc_ref[...], preferred_element_type=F32)
    h = _layer_norm(alpha * x_ref[...] + mix, g_ref[...], b_ref[...])
    h_ref[...] = h
    _store_row_planes(hp_ref, _pack_bf16_pairs(h))
    h_hi = h.astype(BF16)
    h_lo = (h - h_hi.astype(F32)).astype(BF16)
    logits = jnp.dot(jnp.concatenate([h_hi, h_lo, h_hi], axis=1), wr_ref[...], preferred_element_type=F32)
    lg_ref[...] = logits.T[:N_EXPERTS, :]


def _oproj(attn, conv, x2, woa, woc, g, b, wr, alpha):
    t, d = x2.shape
    tm = TM_PROJ
    full = lambda a: pl.BlockSpec(a.shape, lambda i: (0,) * a.ndim)
    return pl.pallas_call(
        functools.partial(_oproj_kernel, alpha=alpha),
        grid=(t // tm,),
        in_specs=[pl.BlockSpec((tm, attn.shape[1]), lambda i: (i, 0)),
                  pl.BlockSpec((tm, conv.shape[1]), lambda i: (i, 0)),
                  pl.BlockSpec((tm, d), lambda i: (i, 0)),
                  full(woa), full(woc), full(g), full(b), full(wr)],
        out_specs=[pl.BlockSpec((tm, d), lambda i: (i, 0)),
                   pl.BlockSpec((tm * (d // 2 // LANES), LANES), lambda i: (i, 0)),
                   pl.BlockSpec((N_EXPERTS, tm), lambda i: (0, i))],
        out_shape=[jax.ShapeDtypeStruct((t, d), F32),
                   jax.ShapeDtypeStruct((t * (d // 2 // LANES), LANES), U32),
                   jax.ShapeDtypeStruct((N_EXPERTS, t), F32)],
        compiler_params=_cparams(("arbitrary",)),
        name="oproj",
    )(attn, conv, x2, woa, woc, g, b, wr)


def _route_kernel(lg_ref, bias_ref, eidx_ref, rank_ref, wts_ref, cnt_ref, carry_ref):
    ne, tm = lg_ref.shape
    i = pl.program_id(0)

    @pl.when(i == 0)
    def _():
        carry_ref[...] = jnp.zeros(carry_ref.shape, F32)

    scores = _sigmoid(lg_ref[...])
    biased = scores + bias_ref[...]
    neg = -jnp.inf
    sub8 = lax.broadcasted_iota(I32, (GROUP_SIZE, tm), 0)
    gscore = []
    for g in range(N_EXPERT_GROUPS):
        blk = biased[g * GROUP_SIZE:(g + 1) * GROUP_SIZE, :]
        m1 = jnp.max(blk, axis=0, keepdims=True)
        i1 = jnp.min(jnp.where(blk == m1, sub8, GROUP_SIZE), axis=0, keepdims=True)
        m2 = jnp.max(jnp.where(sub8 == i1, neg, blk), axis=0, keepdims=True)
        gscore.append(m1 + m2)
    kept = []
    for g in range(N_EXPERT_GROUPS):
        beat = jnp.zeros((1, tm), I32)
        for o in range(N_EXPERT_GROUPS):
            if o < g:
                beat = beat + (gscore[o] >= gscore[g]).astype(I32)
            elif o > g:
                beat = beat + (gscore[o] > gscore[g]).astype(I32)
        kept.append(jnp.where(beat < TOPK_GROUPS, biased[g * GROUP_SIZE:(g + 1) * GROUP_SIZE, :], neg))
    cur = jnp.concatenate(kept, axis=0)
    sub = lax.broadcasted_iota(I32, (ne, tm), 0)
    sel_any = jnp.zeros((ne, tm), F32)
    picks = []
    wsum = jnp.zeros((1, tm), F32)
    for k in range(TOP_K):
        m = jnp.max(cur, axis=0, keepdims=True)
        ei = jnp.min(jnp.where(cur == m, sub, ne), axis=0, keepdims=True)
        sel = sub == ei
        w = jnp.sum(jnp.where(sel, scores, 0.0), axis=0, keepdims=True)
        cur = jnp.where(sel, neg, cur)
        sel_any = sel_any + sel.astype(F32)
        wsum = wsum + w
        picks.append((ei, sel, w))
    r = lax.broadcasted_iota(I32, (tm, tm), 0)
    c = lax.broadcasted_iota(I32, (tm, tm), 1)
    upper = (r <= c).astype(BF16)
    cum = jnp.dot(sel_any.astype(BF16), upper, preferred_element_type=F32)
    carry = carry_ref[:, 0:1]
    excl = cum - sel_any + carry
    total = carry + jnp.sum(sel_any, axis=1, keepdims=True)
    carry_ref[...] = jnp.broadcast_to(total, carry_ref.shape)
    cnt_ref[...] = jnp.broadcast_to(total, cnt_ref.shape)
    denom = wsum + 1e-20
    eidx_ref[...] = jnp.zeros(eidx_ref.shape, I32)
    rank_ref[...] = jnp.zeros(rank_ref.shape, I32)
    wts_ref[...] = jnp.zeros(wts_ref.shape, F32)
    for k, (ei, sel, w) in enumerate(picks):
        eidx_ref[k:k + 1, :] = ei
        rank_ref[k:k + 1, :] = jnp.sum(jnp.where(sel, excl, 0.0), axis=0, keepdims=True).astype(I32)
        wts_ref[k:k + 1, :] = w / denom * ROUTED_SCALE


def _route(logits_t, bias):
    ne, t = logits_t.shape
    tm = TM_ROUTE
    return pl.pallas_call(
        _route_kernel,
        grid=(t // tm,),
        in_specs=[pl.BlockSpec((ne, tm), lambda i: (0, i)), pl.BlockSpec((ne, 1), lambda i: (0, 0))],
        out_specs=[pl.BlockSpec((TOP_K_PAD, tm), lambda i: (0, i)),
                   pl.BlockSpec((TOP_K_PAD, tm), lambda i: (0, i)),
                   pl.BlockSpec((TOP_K_PAD, tm), lambda i: (0, i)),
                   pl.BlockSpec((ne, LANES), lambda i: (0, 0))],
        out_shape=[jax.ShapeDtypeStruct((TOP_K_PAD, t), I32),
                   jax.ShapeDtypeStruct((TOP_K_PAD, t), I32),
                   jax.ShapeDtypeStruct((TOP_K_PAD, t), F32),
                   jax.ShapeDtypeStruct((ne, LANES), F32)],
        scratch_shapes=[pltpu.VMEM((ne, LANES), F32)],
        compiler_params=_cparams(("arbitrary",)),
        name="route",
    )(logits_t, bias)


def _wait_rows(src_rows, dst_rows, sem, n_rows):
    def body(_, c):
        pltpu.make_async_copy(src_rows, dst_rows, sem).wait()
        return c
    lax.fori_loop(0, n_rows // WAIT_ROWS, body, 0)


def _dispatch_kernel(zfill_ref, slots_ref, hp_ref, xs_hbm, zero_vmem, sem, zsem):
    i = pl.program_id(0)
    te = zero_vmem.shape[0]
    tc = hp_ref.shape[0]
    n_tiles = xs_hbm.shape[0] // te

    def zero_copy(tile):
        return pltpu.make_async_copy(zero_vmem, xs_hbm.at[pl.ds(pl.multiple_of(tile * te, te), te)], zsem)

    @pl.when(i == 0)
    def _():
        zero_vmem[...] = jnp.zeros(zero_vmem.shape, U32)

        def zstart(tile, c):
            @pl.when(zfill_ref[tile] != 0)
            def _():
                zero_copy(tile).start()
            return c

        def zwait(tile, c):
            @pl.when(zfill_ref[tile] != 0)
            def _():
                zero_copy(tile).wait()
            return c

        lax.fori_loop(0, n_tiles, zstart, 0)
        lax.fori_loop(0, n_tiles, zwait, 0)

    base = i * (tc * TOP_K_PAD)

    def issue(t, c):
        for k in range(TOP_K):
            slot = slots_ref[base + t * TOP_K_PAD + k]
            pltpu.make_async_copy(hp_ref.at[t], xs_hbm.at[slot], sem).start(priority=k % 2)
        return c

    lax.fori_loop(0, tc, issue, 0, unroll=ISSUE_UNROLL)
    _wait_rows(hp_ref.at[pl.ds(0, WAIT_ROWS)], xs_hbm.at[pl.ds(0, WAIT_ROWS)], sem, tc * TOP_K)


def _dispatch(zfill, slots_flat, hp, n_rows):
    t, planes, lanes = hp.shape
    tc = TC_DISP
    return pl.pallas_call(
        _dispatch_kernel,
        grid_spec=pltpu.PrefetchScalarGridSpec(
            num_scalar_prefetch=2,
            grid=(t // tc,),
            in_specs=[pl.BlockSpec((tc, planes, lanes), lambda i, z, s: (i, 0, 0))],
            out_specs=pl.BlockSpec(memory_space=pl.ANY),
            scratch_shapes=[pltpu.VMEM((TE_ROWS, planes, lanes), U32),
                            pltpu.SemaphoreType.DMA, pltpu.SemaphoreType.DMA],
        ),
        out_shape=jax.ShapeDtypeStruct((n_rows, planes, lanes), U32),
        compiler_params=_cparams(("arbitrary",)),
        name="dispatch",
    )(zfill, slots_flat, hp)


def _experts_kernel(te_ref, nv_ref, nx_ref, xs_ref, wg_hbm, wu_hbm, wd_hbm, y_ref,
                    sg, su, sd, wgb, wub, wdb, sem):
    i = pl.program_id(0)
    e = te_ref[i]
    prev = te_ref[jnp.maximum(i - 1, 0)]
    valid = i < nv_ref[0]

    def fetch(ex):
        return (pltpu.make_async_copy(wg_hbm.at[ex], sg, sem.at[0]),
                pltpu.make_async_copy(wu_hbm.at[ex], su, sem.at[1]),
                pltpu.make_async_copy(wd_hbm.at[ex], sd, sem.at[2]))

    @pl.when(i == 0)
    def _():
        for cp in fetch(e):
            cp.start()

    first = valid & ((i == 0) | (e != prev))

    def ffn():
        xs = _unpack_bf16_pairs(_load_row_planes(xs_ref, xs_ref.shape[0] // TE_ROWS))
        g = jnp.dot(xs, wgb[...], preferred_element_type=F32)
        u = jnp.dot(xs, wub[...], preferred_element_type=F32)
        hid = (g * _sigmoid(g) * u).astype(BF16)
        y = jnp.dot(hid, wdb[...], preferred_element_type=F32)
        _store_row_planes(y_ref, _pack_bf16_pairs(y))

    @pl.when(first)
    def _():
        for cp in fetch(e):
            cp.wait()
        wgb[...] = sg[...].astype(BF16)
        wub[...] = su[...].astype(BF16)
        wdb[...] = sd[...].astype(BF16)
        for cp in fetch(jnp.where(nx_ref[i] >= 0, nx_ref[i], e)):
            cp.start(priority=1)
        ffn()

    @pl.when(valid & jnp.logical_not(first))
    def _():
        ffn()

    @pl.when(i == pl.num_programs(0) - 1)
    def _():
        for cp in fetch(e):
            cp.wait()

    @pl.when(jnp.logical_not(valid))
    def _():
        y_ref[...] = jnp.zeros(y_ref.shape, U32)


def _experts(tile_expert, n_valid, next_expert, xs, w_gate, w_up, w_down):
    n_rows, xp, _ = xs.shape
    ne, d, ff = w_gate.shape
    te = TE_ROWS
    in_map = lambda i, te_ref, nv_ref, nx_ref: (jnp.minimum(i, nv_ref[0] - 1), 0)
    out_map = lambda i, te_ref, nv_ref, nx_ref: (i, 0)
    hbm = pl.BlockSpec(memory_space=pl.ANY)
    y = pl.pallas_call(
        _experts_kernel,
        grid_spec=pltpu.PrefetchScalarGridSpec(
            num_scalar_prefetch=3,
            grid=(n_rows // te,),
            in_specs=[pl.BlockSpec((te * xp, LANES), in_map), hbm, hbm, hbm],
            out_specs=pl.BlockSpec((te * xp, LANES), out_map),
            scratch_shapes=[pltpu.VMEM((d, ff), F32), pltpu.VMEM((d, ff), F32), pltpu.VMEM((ff, d), F32),
                            pltpu.VMEM((d, ff), BF16), pltpu.VMEM((d, ff), BF16), pltpu.VMEM((ff, d), BF16),
                            pltpu.SemaphoreType.DMA((3,))],
        ),
        out_shape=jax.ShapeDtypeStruct((n_rows * xp, LANES), U32),
        compiler_params=_cparams(("arbitrary",)),
        name="experts",
    )(tile_expert, n_valid, next_expert, xs.reshape(n_rows * xp, LANES), w_gate, w_up, w_down)
    return y.reshape(n_rows, xp, LANES)


def _combine_kernel(slots_ref, y_hbm, h_ref, wts_ref, wsg_ref, wsu_ref, wsd_ref, g_ref, b_ref,
                    o_ref, gbuf, rbuf, shbuf, sem, *, alpha):
    tm, d = o_ref.shape
    planes = gbuf.shape[3]
    i = pl.program_id(0)
    cur = i % 2
    grp = SUBLANES

    def issue_rows(step, t0, n):
        buf = step % 2
        base = step * (tm * TOP_K_PAD)
        for u in range(n):
            for k in range(TOP_K):
                slot = slots_ref[base + (t0 + u) * TOP_K_PAD + k]
                pltpu.make_async_copy(y_hbm.at[slot], gbuf.at[buf, k, t0 + u], sem.at[buf]).start(priority=k % 2)

    @pl.when(i == 0)
    def _():
        def first(t, c):
            issue_rows(0, t, 1)
            return c
        lax.fori_loop(0, tm, first, 0, unroll=ISSUE_UNROLL)

    hb = h_ref[...].astype(BF16)
    sg = jnp.dot(hb, wsg_ref[...], preferred_element_type=F32)
    su = jnp.dot(hb, wsu_ref[...], preferred_element_type=F32)
    shbuf[...] = jnp.dot((sg * _sigmoid(sg) * su).astype(BF16), wsd_ref[...], preferred_element_type=F32)
    _wait_rows(y_hbm.at[pl.ds(0, WAIT_ROWS)], gbuf.at[cur, 0, pl.ds(0, WAIT_ROWS)], sem.at[cur], tm * TOP_K)

    def group(g, prefetch):
        t0 = pl.multiple_of(g * grp, grp)
        packed = [gbuf[cur, k, pl.ds(t0, grp)] for k in range(TOP_K)]
        w = wts_ref[pl.ds(t0, grp)]
        resid = alpha * h_ref[pl.ds(t0, grp), :] + shbuf[pl.ds(t0, grp), :]
        if prefetch:
            issue_rows(i + 1, t0, grp)
        r_lo = r_hi = None
        for k in range(TOP_K):
            lo = w[:, k:k + 1, :] * pltpu.bitcast(packed[k] << 16, F32)
            hi = w[:, k:k + 1, :] * pltpu.bitcast(packed[k] & BF16_HIGH_HALF, F32)
            r_lo = lo if r_lo is None else r_lo + lo
            r_hi = hi if r_hi is None else r_hi + hi
        p0 = pl.multiple_of(t0 * planes, grp * planes)
        rbuf[0, pl.ds(p0, grp * planes), :] = r_lo.reshape(grp * planes, LANES)
        rbuf[1, pl.ds(p0, grp * planes), :] = r_hi.reshape(grp * planes, LANES)
        routed = jnp.concatenate([rbuf[half, pl.ds(p0 + j, grp, stride=planes), :]
                                  for half in range(2) for j in range(planes)], axis=1)
        o_ref[pl.ds(t0, grp), :] = _layer_norm(resid + routed, g_ref[...], b_ref[...])

    has_next = i + 1 < pl.num_programs(0)

    @pl.when(has_next)
    def _():
        lax.fori_loop(0, tm // grp, lambda g, c: (group(g, True), c)[1], 0, unroll=GROUP_UNROLL)

    @pl.when(jnp.logical_not(has_next))
    def _():
        lax.fori_loop(0, tm // grp, lambda g, c: (group(g, False), c)[1], 0, unroll=GROUP_UNROLL)


def _combine(slots_flat, y, h, wts_planes, wsg, wsu, wsd, g, b, alpha):
    t, d = h.shape
    tm = TM_COMB
    planes = y.shape[1]
    full = lambda a: pl.BlockSpec(a.shape, lambda i, s: (0,) * a.ndim)
    once = lambda a: pl.BlockSpec(a.shape, lambda i, s: (0,) * a.ndim, pipeline_mode=pl.Buffered(1))
    return pl.pallas_call(
        functools.partial(_combine_kernel, alpha=alpha),
        grid_spec=pltpu.PrefetchScalarGridSpec(
            num_scalar_prefetch=1,
            grid=(t // tm,),
            in_specs=[pl.BlockSpec(memory_space=pl.ANY),
                      pl.BlockSpec((tm, d), lambda i, s: (i, 0)),
                      pl.BlockSpec((tm, TOP_K_PAD, LANES), lambda i, s: (i, 0, 0)),
                      once(wsg), once(wsu), once(wsd), full(g), full(b)],
            out_specs=pl.BlockSpec((tm, d), lambda i, s: (i, 0)),
            scratch_shapes=[pltpu.VMEM((2, TOP_K, tm, planes, LANES), U32),
                            pltpu.VMEM((2, tm * planes, LANES), F32), pltpu.VMEM((tm, d), F32),
                            pltpu.SemaphoreType.DMA((2,))],
        ),
        out_shape=jax.ShapeDtypeStruct((t, d), F32),
        compiler_params=_cparams(("arbitrary",)),
        name="combine",
    )(slots_flat, y, h, wts_planes, wsg, wsu, wsd, g, b)


def _prep_attention_weights(w_in, w_uq, w_ukv):
    d = w_in.shape[0]
    ql = w_uq.shape[0]
    kvl = w_ukv.shape[0]
    half = QK_ROPE_DIM // 2
    pad = LANES - QK_ROPE_DIM
    w_kr = w_in[:, ql + kvl:ql + kvl + QK_ROPE_DIM]
    w_kr_rot = jnp.concatenate([-w_kr[:, half:], w_kr[:, :half]], axis=1)
    zpad = jnp.zeros((d, pad), w_in.dtype)
    wa = jnp.concatenate([w_in[:, :ql + kvl], w_kr, zpad, w_kr_rot, zpad], axis=1).astype(BF16)
    wc = w_in[:, ql + kvl + QK_ROPE_DIM:].astype(BF16)
    uq = w_uq.reshape(ql, N_HEADS, QK_NOPE_DIM + QK_ROPE_DIM)
    uq_nope, uq_rope = uq[..., :QK_NOPE_DIM], uq[..., QK_NOPE_DIM:]
    zq = jnp.zeros((ql, N_HEADS, pad), w_uq.dtype)
    wuq = jnp.concatenate([uq_nope, uq_rope, zq], axis=-1).reshape(ql, N_HEADS * HEAD_QK_PAD).astype(BF16)
    uq_rot = jnp.concatenate([-uq_rope[..., half:], uq_rope[..., :half], zq], axis=-1)
    wuqr = uq_rot.reshape(ql, N_HEADS * LANES).astype(BF16)
    ukv = w_ukv.reshape(kvl, N_HEADS, QK_NOPE_DIM + V_HEAD_DIM)
    wukv = jnp.concatenate([ukv[..., :QK_NOPE_DIM].reshape(kvl, -1), ukv[..., QK_NOPE_DIM:].reshape(kvl, -1)],
                           axis=1).astype(BF16)
    return wa, wc, wuq, wuqr, wukv


def _layer(h, pos2, invf4, batch, seq, alpha, w_in, q_norm_g, w_uq, kv_norm_g, w_ukv, conv_w, conv_b,
           conv_ln_g, conv_ln_b, w_o, ln1_g, ln1_b, w_router, router_bias, w_gate, w_up, w_down,
           ws_gate, ws_up, ws_down, ln2_g, ln2_b):
    t, d = h.shape
    row = lambda a: a.reshape(1, -1)
    wa, wc, wuq, wuqr, wukv = _prep_attention_weights(w_in, w_uq, w_ukv)
    q, k, v = _qkv(h, pos2, invf4, wa, row(q_norm_g), row(kv_norm_g), wuq, wuqr, wukv)
    conv = _conv(h, wc, conv_w, row(conv_b), row(conv_ln_g), row(conv_ln_b), seq)
    attn = _attn(q, k, v, batch, seq)
    aw = attn.shape[1]
    wr32 = jnp.pad(w_router.astype(F32), ((0, 0), (0, LANES - N_EXPERTS)))
    wr_hi = wr32.astype(BF16)
    wr_lo = (wr32 - wr_hi.astype(F32)).astype(BF16)
    wr = jnp.concatenate([wr_hi, wr_hi, wr_lo], axis=0)
    h1, h1p, logits_t = _oproj(attn, conv, h, w_o[:aw].astype(BF16), w_o[aw:].astype(BF16),
                               row(ln1_g), row(ln1_b), wr, alpha)
    return _moe_ffn(h1, h1p, logits_t, alpha, router_bias, w_gate, w_up, w_down,
                    ws_gate, ws_up, ws_down, ln2_g, ln2_b)


def _moe_ffn(h1, h1p, logits_t, alpha, router_bias, w_gate, w_up, w_down, ws_gate, ws_up, ws_down, ln2_g, ln2_b):
    t, d = h1.shape
    row = lambda a: a.reshape(1, -1)
    eidx, rank, wts, cnt = _route(logits_t, router_bias.astype(F32).reshape(N_EXPERTS, 1))
    counts = cnt[:, 0].astype(I32)
    padded = (counts + TE_ROWS - 1) // TE_ROWS * TE_ROWS
    pend = jnp.cumsum(padded)
    poff = pend - padded
    n_rows = t * TOP_K + N_EXPERTS * TE_ROWS
    n_tiles = n_rows // TE_ROWS
    n_valid = (pend[-1] // TE_ROWS).astype(I32).reshape(1)
    tile_row = jnp.minimum(jnp.arange(n_tiles, dtype=I32), n_valid[0] - 1) * TE_ROWS
    tile_expert = jnp.sum((pend[None, :] <= tile_row[:, None]).astype(I32), axis=1)
    tile_expert = jnp.minimum(tile_expert, N_EXPERTS - 1)
    tiles = jnp.arange(n_tiles, dtype=I32)
    last_tile = jnp.where(padded > 0, pend // TE_ROWS - 1, -1)
    zfill = ((tiles >= n_valid[0]) | jnp.any(tiles[:, None] == last_tile[None, :], axis=1)).astype(I32)
    ids = jnp.arange(N_EXPERTS, dtype=I32)
    later = (ids[None, :] > ids[:, None]) & (padded[None, :] > 0)
    next_of = jnp.min(jnp.where(later, ids[None, :], N_EXPERTS), axis=1)
    next_of = jnp.where(next_of == N_EXPERTS, -1, next_of)
    next_expert = jnp.sum(jnp.where(tile_expert[:, None] == ids[None, :], next_of[None, :], 0), axis=1).astype(I32)
    slot_base = jnp.sum(jnp.where(eidx[None] == ids[:, None, None], poff[:, None, None], 0), axis=0)
    slots_flat = (slot_base + rank).T.reshape(-1)
    xs = _dispatch(zfill, slots_flat, h1p.reshape(t, -1, LANES), n_rows)
    y = _experts(tile_expert, n_valid, next_expert, xs, w_gate, w_up, w_down)
    wts_planes = jnp.broadcast_to(wts.T[:, :, None], (t, TOP_K_PAD, LANES))
    return _combine(slots_flat, y, h1, wts_planes, ws_gate.astype(BF16), ws_up.astype(BF16),
                    ws_down.astype(BF16), row(ln2_g), row(ln2_b), alpha)


def kernel(x, positions, w_in, q_norm_g, w_uq, kv_norm_g, w_ukv, conv_w, conv_b, conv_ln_g, conv_ln_b, w_o, ln1_g, ln1_b, w_router, router_bias, w_gate, w_up, w_down, ws_gate, ws_up, ws_down, ln2_g, ln2_b):
    batch, seq, d = x.shape
    depth = w_in.shape[0]
    alpha = (2.0 * depth) ** 0.25
    inv_freq = ROPE_BASE ** (-jnp.arange(0, QK_ROPE_DIM, 2, dtype=F32) / QK_ROPE_DIM)
    invf4 = jnp.tile(inv_freq, LANES // inv_freq.shape[0]).reshape(1, LANES)
    pos2 = positions.reshape(batch * seq, 1)
    h = x.reshape(batch * seq, d)
    for l in range(depth):
        h = _layer(h, pos2, invf4, batch, seq, alpha, w_in[l], q_norm_g[l], w_uq[l], kv_norm_g[l], w_ukv[l],
                   conv_w[l], conv_b[l], conv_ln_g[l], conv_ln_b[l], w_o[l], ln1_g[l], ln1_b[l],
                   w_router[l], router_bias[l], w_gate[l], w_up[l], w_down[l],
                   ws_gate[l], ws_up[l], ws_down[l], ln2_g[l], ln2_b[l])
    return h.reshape(batch, seq, d)
```

```python
import functools

import jax
import jax.numpy as jnp
import numpy as np
from jax import lax
from jax.experimental import pallas as pl
from jax.experimental.pallas import tpu as pltpu

F32 = jnp.float32
BF16 = jnp.bfloat16
I32 = jnp.int32
U32 = jnp.uint32

N_HEADS = 8
QK_NOPE_DIM = 128
QK_ROPE_DIM = 64
V_HEAD_DIM = 128
HEAD_QK_PAD = 256
CONV_KERNEL = 31
N_EXPERTS = 64
TOP_K = 6
TOP_K_PAD = 8
N_EXPERT_GROUPS = 8
GROUP_SIZE = N_EXPERTS // N_EXPERT_GROUPS
TOPK_GROUPS = 4
ROUTED_SCALE = 2.5
ROPE_BASE = 10000.0
LN_EPS = 1e-5
RMS_EPS = 1e-6
LOG2_E = 1.4426950408889634
BF16_HIGH_HALF = np.uint32(0xFFFF0000)

LANES = 128
SUBLANES = 8
CONV_HALO = 32

TM_PROJ = 256
TM_CONV = 512
TQ_ATTN = 512
TK_ATTN = 512
HEADS_PER_ATTN_STEP = 4
ATTN_QUERY_SPLIT = 2
ATTN_SCORES_AHEAD = 3
TM_ROUTE = 512
TE_ROWS = 256
TM_COMB = 256
TC_DISP = 1024
WAIT_ROWS = 128
ISSUE_UNROLL = 4
GROUP_UNROLL = 8
CONV_ROWS = 128
CONV_COLS = 256
CONV_SHIFT_BUFS = 4
VMEM_LIMIT = 56 * 1024 * 1024


def _cparams(sem):
    return pltpu.CompilerParams(dimension_semantics=sem, vmem_limit_bytes=VMEM_LIMIT)


def _sigmoid(v):
    return 1.0 / (1.0 + jnp.exp(-v))


def _layer_norm(v, g, b):
    mu = jnp.mean(v, axis=-1, keepdims=True)
    d = v - mu
    var = jnp.mean(d * d, axis=-1, keepdims=True)
    return d * lax.rsqrt(var + LN_EPS) * g + b


def _rms_norm(v, g):
    ms = jnp.mean(v * v, axis=-1, keepdims=True)
    return v * lax.rsqrt(ms + RMS_EPS) * g


def _qkv_kernel(x_ref, pos_ref, invf_ref, wa_ref, qg_ref, kvg_ref, wuq_ref, wuqr_ref, wukv_ref,
                qt_ref, k_ref, vt_ref):
    ql = qg_ref.shape[1]
    kvl = kvg_ref.shape[1]
    xb = x_ref[...].astype(BF16)
    lat = jnp.dot(xb, wa_ref[...], preferred_element_type=F32)
    ang = pos_ref[...].astype(F32) * invf_ref[...]
    cos = jnp.cos(ang)
    sin = jnp.sin(ang)
    cq = _rms_norm(lat[:, :ql], qg_ref[...]).astype(BF16)
    ckv = _rms_norm(lat[:, ql:ql + kvl], kvg_ref[...]).astype(BF16)
    kr = lat[:, ql + kvl:ql + kvl + LANES] * cos + lat[:, ql + kvl + LANES:ql + kvl + 2 * LANES] * sin
    kr = kr.astype(BF16)
    q = jnp.dot(cq, wuq_ref[...], preferred_element_type=F32)
    qrot = jnp.dot(cq, wuqr_ref[...], preferred_element_type=F32)
    kv = jnp.dot(ckv, wukv_ref[...], preferred_element_type=F32)
    for h in range(N_HEADS):
        c0 = h * HEAD_QK_PAD
        qt_ref[c0:c0 + LANES, :] = q[:, c0:c0 + LANES].T.astype(BF16)
        qt_ref[c0 + LANES:c0 + 2 * LANES, :] = (
            q[:, c0 + LANES:c0 + 2 * LANES] * cos + qrot[:, h * LANES:(h + 1) * LANES] * sin).T.astype(BF16)
        k_ref[:, c0:c0 + LANES] = kv[:, h * LANES:(h + 1) * LANES].astype(BF16)
        k_ref[:, c0 + LANES:c0 + 2 * LANES] = kr
    vt_ref[...] = kv[:, N_HEADS * QK_NOPE_DIM:].T.astype(BF16)


def _qkv(x2, pos2, invf4, wa, qg, kvg, wuq, wuqr, wukv):
    t, d = x2.shape
    tm = TM_PROJ
    full = lambda a: pl.BlockSpec(a.shape, lambda i: (0,) * a.ndim)
    return pl.pallas_call(
        _qkv_kernel,
        grid=(t // tm,),
        in_specs=[pl.BlockSpec((tm, d), lambda i: (i, 0)),
                  pl.BlockSpec((tm, 1), lambda i: (i, 0)),
                  full(invf4), full(wa), full(qg), full(kvg), full(wuq), full(wuqr), full(wukv)],
        out_specs=[pl.BlockSpec((N_HEADS * HEAD_QK_PAD, tm), lambda i: (0, i)),
                   pl.BlockSpec((tm, N_HEADS * HEAD_QK_PAD), lambda i: (i, 0)),
                   pl.BlockSpec((N_HEADS * V_HEAD_DIM, tm), lambda i: (0, i))],
        out_shape=[jax.ShapeDtypeStruct((N_HEADS * HEAD_QK_PAD, t), BF16),
                   jax.ShapeDtypeStruct((t, N_HEADS * HEAD_QK_PAD), BF16),
                   jax.ShapeDtypeStruct((N_HEADS * V_HEAD_DIM, t), BF16)],
        compiler_params=_cparams(("arbitrary",)),
        name="qkv",
    )(x2, pos2, invf4, wa, qg, kvg, wuq, wuqr, wukv)


def _conv_kernel(x_ref, wc_ref, cw_ref, cb_ref, g_ref, b_ref, o_ref, ubuf, ybuf, shbuf, *, tiles_per_seq):
    tm, cw = o_ref.shape
    i = pl.program_id(0)
    xb = x_ref[...].astype(BF16)

    @pl.when(i % tiles_per_seq == 0)
    def _():
        ubuf[0:CONV_HALO, :] = jnp.zeros((CONV_HALO, cw), F32)

    shift0 = CONV_HALO - (CONV_KERNEL - 1)
    n_sh = shbuf.shape[0]
    chunk_no = 0
    for c0 in range(0, cw, CONV_COLS):
        a = jnp.dot(xb, wc_ref[:, c0:c0 + CONV_COLS], preferred_element_type=F32)
        gate = jnp.dot(xb, wc_ref[:, cw + c0:cw + c0 + CONV_COLS], preferred_element_type=F32)
        ubuf[CONV_HALO:CONV_HALO + tm, c0:c0 + CONV_COLS] = a * _sigmoid(gate)
        for r0 in range(0, tm, CONV_ROWS):
            sh = shbuf.at[chunk_no % n_sh]
            chunk_no += 1
            win = ubuf[r0:r0 + CONV_ROWS + CONV_HALO, c0:c0 + CONV_COLS]
            for r in range(1, SUBLANES):
                sh[r] = win[r:r + CONV_ROWS + CONV_HALO - SUBLANES, :]
            acc = jnp.zeros((CONV_ROWS, CONV_COLS), F32)
            for k in range(CONV_KERNEL):
                wk = cw_ref[k:k + 1, c0:c0 + CONV_COLS]
                r, j = (shift0 + k) % SUBLANES, (shift0 + k) // SUBLANES
                if r == 0:
                    tap = win[j * SUBLANES:j * SUBLANES + CONV_ROWS, :]
                else:
                    tap = sh[r, j * SUBLANES:j * SUBLANES + CONV_ROWS, :]
                acc = acc + wk * tap
            ybuf[r0:r0 + CONV_ROWS, c0:c0 + CONV_COLS] = acc
    ubuf[0:CONV_HALO, :] = ubuf[tm:tm + CONV_HALO, :]
    y = _layer_norm(ybuf[...] + cb_ref[...], g_ref[...], b_ref[...])
    o_ref[...] = (y * _sigmoid(y)).astype(BF16)


def _conv(x2, wc, cw, cb, g, b, seq):
    t, d = x2.shape
    tm = TM_CONV
    c = cw.shape[1]
    full = lambda a: pl.BlockSpec(a.shape, lambda i: (0,) * a.ndim)
    return pl.pallas_call(
        functools.partial(_conv_kernel, tiles_per_seq=seq // tm),
        grid=(t // tm,),
        in_specs=[pl.BlockSpec((tm, d), lambda i: (i, 0)), full(wc), full(cw), full(cb), full(g), full(b)],
        out_specs=pl.BlockSpec((tm, c), lambda i: (i, 0)),
        out_shape=jax.ShapeDtypeStruct((t, c), BF16),
        scratch_shapes=[pltpu.VMEM((tm + CONV_HALO, c), F32), pltpu.VMEM((tm, c), F32),
                        pltpu.VMEM((CONV_SHIFT_BUFS, SUBLANES, CONV_ROWS + CONV_HALO - SUBLANES, CONV_COLS), F32)],
        compiler_params=_cparams(("arbitrary",)),
        name="conv",
    )(x2, wc, cw, cb, g, b)


def _attn_kernel(qt_ref, k_ref, vt_ref, o_ref, *, scale, tk):
    tq = qt_ref.shape[1]
    heads = qt_ref.shape[0] // HEAD_QK_PAD
    i = pl.program_id(2)
    c = scale * LOG2_E
    n_full = (i * tq) // tk
    q_off = i * tq - n_full * tk
    tqh = tq // ATTN_QUERY_SPLIT
    chains = [(h, s) for h in range(heads) for s in range(ATTN_QUERY_SPLIT)]

    def block(j, carry, masked):
        r0 = pl.multiple_of(j * tk, tk)
        nkeys = [min(tk, (s + 1) * tqh) if (masked and tq == tk) else tk for _, s in chains]
        def score(n):
            (h, s), nk = chains[n], nkeys[n]
            qt = qt_ref[h * HEAD_QK_PAD:(h + 1) * HEAD_QK_PAD, s * tqh:(s + 1) * tqh]
            kb = k_ref[pl.ds(r0, nk), h * HEAD_QK_PAD:(h + 1) * HEAD_QK_PAD]
            return jnp.dot(kb, qt, preferred_element_type=F32)

        pending = [score(n) for n in range(min(ATTN_SCORES_AHEAD, len(chains)))]
        out = []
        for n, ((h, s), nk, (m, l, acc)) in enumerate(zip(chains, nkeys, carry)):
            st = pending.pop(0)
            if n + ATTN_SCORES_AHEAD < len(chains):
                pending.append(score(n + ATTN_SCORES_AHEAD))
            vtb = vt_ref[h * V_HEAD_DIM:(h + 1) * V_HEAD_DIM, pl.ds(r0, nk)]
            if masked:
                key = lax.broadcasted_iota(I32, (nk, tqh), 0)
                qry = lax.broadcasted_iota(I32, (nk, tqh), 1) + (q_off + s * tqh)
                st = jnp.where(key <= qry, st, -jnp.inf)
            m_blk = jnp.max(jnp.max(st.reshape(SUBLANES, nk // SUBLANES, tqh), axis=0), axis=0, keepdims=True)
            m_new = jnp.maximum(m, m_blk)
            alpha = jnp.exp2((m - m_new) * c)
            pt = jnp.exp2((st - m_new) * c)
            l_blk = jnp.sum(jnp.sum(pt.reshape(SUBLANES, nk // SUBLANES, tqh), axis=0), axis=0, keepdims=True)
            l = alpha * l + l_blk
            acc = alpha * acc + jnp.dot(vtb, pt.astype(BF16), preferred_element_type=F32)
            out.append((m_new, l, acc))
        return tuple(out)

    init = tuple((jnp.full((1, tqh), -jnp.inf, F32), jnp.zeros((1, tqh), F32),
                  jnp.zeros((V_HEAD_DIM, tqh), F32)) for _ in chains)
    carry = lax.fori_loop(0, n_full, lambda j, cr: block(j, cr, False), init)
    final = block(n_full, carry, True)
    for (h, s), (_, l, acc) in zip(chains, final):
        o_ref[s * tqh:(s + 1) * tqh, h * V_HEAD_DIM:(h + 1) * V_HEAD_DIM] = (acc / l).T.astype(BF16)


def _attn(q, k, v, batch, seq):
    tq = TQ_ATTN
    nq = seq // tq
    hb = HEADS_PER_ATTN_STEP
    scale = (QK_NOPE_DIM + QK_ROPE_DIM) ** -0.5
    return pl.pallas_call(
        functools.partial(_attn_kernel, scale=scale, tk=TK_ATTN),
        grid=(batch, N_HEADS // hb, nq),
        in_specs=[pl.BlockSpec((hb * HEAD_QK_PAD, tq), lambda b, h, i: (h, b * nq + i)),
                  pl.BlockSpec((seq, hb * HEAD_QK_PAD), lambda b, h, i: (b, h)),
                  pl.BlockSpec((hb * V_HEAD_DIM, seq), lambda b, h, i: (h, b))],
        out_specs=pl.BlockSpec((tq, hb * V_HEAD_DIM), lambda b, h, i: (b * nq + i, h)),
        out_shape=jax.ShapeDtypeStruct((batch * seq, N_HEADS * V_HEAD_DIM), BF16),
        compiler_params=_cparams(("arbitrary", "arbitrary", "arbitrary")),
        name="attn",
    )(q, k, v)


def _pack_bf16_pairs(v):
    c = v.shape[1] // 2
    lo = pltpu.bitcast(v[:, :c].astype(BF16).astype(F32), U32)
    hi = pltpu.bitcast(v[:, c:].astype(BF16).astype(F32), U32)
    return (hi & BF16_HIGH_HALF) | (lo >> 16)


def _unpack_bf16_pairs(p):
    lo = pltpu.bitcast(p << 16, F32).astype(BF16)
    hi = pltpu.bitcast(p & BF16_HIGH_HALF, F32).astype(BF16)
    return jnp.concatenate([lo, hi], axis=1)


def _store_row_planes(ref, v):
    n = v.shape[0]
    p = ref.shape[0] // n
    for j in range(p):
        ref[pl.ds(j, n, stride=p), :] = v[:, j * LANES:(j + 1) * LANES]


def _load_row_planes(ref, p):
    n = ref.shape[0] // p
    return jnp.concatenate([ref[pl.ds(j, n, stride=p), :] for j in range(p)], axis=1)


def _oproj_kernel(a_ref, c_ref, x_ref, woa_ref, woc_ref, g_ref, b_ref, wr_ref,
                  h_ref, hp_ref, lg_ref, *, alpha):
    mix = jnp.dot(a_ref[...], woa_ref[...], preferred_element_type=F32)
    mix = mix + jnp.dot(c_ref[...], woc_ref[...], preferred_element_type=F32)
    h = _layer_norm(alpha * x_ref[...] + mix, g_ref[...], b_ref[...])
    h_ref[...] = h
    _store_row_planes(hp_ref, _pack_bf16_pairs(h))
    h_hi = h.astype(BF16)
    h_lo = (h - h_hi.astype(F32)).astype(BF16)
    logits = jnp.dot(jnp.concatenate([h_hi, h_lo, h_hi], axis=1), wr_ref[...], preferred_element_type=F32)
    lg_ref[...] = logits.T[:N_EXPERTS, :]


def _oproj(attn, conv, x2, woa, woc, g, b, wr, alpha):
    t, d = x2.shape
    tm = TM_PROJ
    full = lambda a: pl.BlockSpec(a.shape, lambda i: (0,) * a.ndim)
    return pl.pallas_call(
        functools.partial(_oproj_kernel, alpha=alpha),
        grid=(t // tm,),
        in_specs=[pl.BlockSpec((tm, attn.shape[1]), lambda i: (i, 0)),
                  pl.BlockSpec((tm, conv.shape[1]), lambda i: (i, 0)),
                  pl.BlockSpec((tm, d), lambda i: (i, 0)),
                  full(woa), full(woc), full(g), full(b), full(wr)],
        out_specs=[pl.BlockSpec((tm, d), lambda i: (i, 0)),
                   pl.BlockSpec((tm * (d // 2 // LANES), LANES), lambda i: (i, 0)),
                   pl.BlockSpec((N_EXPERTS, tm), lambda i: (0, i))],
        out_shape=[jax.ShapeDtypeStruct((t, d), F32),
                   jax.ShapeDtypeStruct((t * (d // 2 // LANES), LANES), U32),
                   jax.ShapeDtypeStruct((N_EXPERTS, t), F32)],
        compiler_params=_cparams(("arbitrary",)),
        name="oproj",
    )(attn, conv, x2, woa, woc, g, b, wr)


def _route_kernel(lg_ref, bias_ref, eidx_ref, rank_ref, wts_ref, cnt_ref, carry_ref):
    ne, tm = lg_ref.shape
    i = pl.program_id(0)

    @pl.when(i == 0)
    def _():
        carry_ref[...] = jnp.zeros(carry_ref.shape, F32)

    scores = _sigmoid(lg_ref[...])
    biased = scores + bias_ref[...]
    neg = -jnp.inf
    sub8 = lax.broadcasted_iota(I32, (GROUP_SIZE, tm), 0)
    gscore = []
    for g in range(N_EXPERT_GROUPS):
        blk = biased[g * GROUP_SIZE:(g + 1) * GROUP_SIZE, :]
        m1 = jnp.max(blk, axis=0, keepdims=True)
        i1 = jnp.min(jnp.where(blk == m1, sub8, GROUP_SIZE), axis=0, keepdims=True)
        m2 = jnp.max(jnp.where(sub8 == i1, neg, blk), axis=0, keepdims=True)
        gscore.append(m1 + m2)
    kept = []
    for g in range(N_EXPERT_GROUPS):
        beat = jnp.zeros((1, tm), I32)
        for o in range(N_EXPERT_GROUPS):
            if o < g:
                beat = beat + (gscore[o] >= gscore[g]).astype(I32)
            elif o > g:
                beat = beat + (gscore[o] > gscore[g]).astype(I32)
        kept.append(jnp.where(beat < TOPK_GROUPS, biased[g * GROUP_SIZE:(g + 1) * GROUP_SIZE, :], neg))
    cur = jnp.concatenate(kept, axis=0)
    sub = lax.broadcasted_iota(I32, (ne, tm), 0)
    sel_any = jnp.zeros((ne, tm), F32)
    picks = []
    wsum = jnp.zeros((1, tm), F32)
    for k in range(TOP_K):
        m = jnp.max(cur, axis=0, keepdims=True)
        ei = jnp.min(jnp.where(cur == m, sub, ne), axis=0, keepdims=True)
        sel = sub == ei
        w = jnp.sum(jnp.where(sel, scores, 0.0), axis=0, keepdims=True)
        cur = jnp.where(sel, neg, cur)
        sel_any = sel_any + sel.astype(F32)
        wsum = wsum + w
        picks.append((ei, sel, w))
    r = lax.broadcasted_iota(I32, (tm, tm), 0)
    c = lax.broadcasted_iota(I32, (tm, tm), 1)
    upper = (r <= c).astype(BF16)
    cum = jnp.dot(sel_any.astype(BF16), upper, preferred_element_type=F32)
    carry = carry_ref[:, 0:1]
    excl = cum - sel_any + carry
    total = carry + jnp.sum(sel_any, axis=1, keepdims=True)
    carry_ref[...] = jnp.broadcast_to(total, carry_ref.shape)
    cnt_ref[...] = jnp.broadcast_to(total, cnt_ref.shape)
    denom = wsum + 1e-20
    eidx_ref[...] = jnp.zeros(eidx_ref.shape, I32)
    rank_ref[...] = jnp.zeros(rank_ref.shape, I32)
    wts_ref[...] = jnp.zeros(wts_ref.shape, F32)
    for k, (ei, sel, w) in enumerate(picks):
        eidx_ref[k:k + 1, :] = ei
        rank_ref[k:k + 1, :] = jnp.sum(jnp.where(sel, excl, 0.0), axis=0, keepdims=True).astype(I32)
        wts_ref[k:k + 1, :] = w / denom * ROUTED_SCALE


def _route(logits_t, bias):
    ne, t = logits_t.shape
    tm = TM_ROUTE
    return pl.pallas_call(
        _route_kernel,
        grid=(t // tm,),
        in_specs=[pl.BlockSpec((ne, tm), lambda i: (0, i)), pl.BlockSpec((ne, 1), lambda i: (0, 0))],
        out_specs=[pl.BlockSpec((TOP_K_PAD, tm), lambda i: (0, i)),
                   pl.BlockSpec((TOP_K_PAD, tm), lambda i: (0, i)),
                   pl.BlockSpec((TOP_K_PAD, tm), lambda i: (0, i)),
                   pl.BlockSpec((ne, LANES), lambda i: (0, 0))],
        out_shape=[jax.ShapeDtypeStruct((TOP_K_PAD, t), I32),
                   jax.ShapeDtypeStruct((TOP_K_PAD, t), I32),
                   jax.ShapeDtypeStruct((TOP_K_PAD, t), F32),
                   jax.ShapeDtypeStruct((ne, LANES), F32)],
        scratch_shapes=[pltpu.VMEM((ne, LANES), F32)],
        compiler_params=_cparams(("arbitrary",)),
        name="route",
    )(logits_t, bias)


def _wait_rows(src_rows, dst_rows, sem, n_rows):
    def body(_, c):
        pltpu.make_async_copy(src_rows, dst_rows, sem).wait()
        return c
    lax.fori_loop(0, n_rows // WAIT_ROWS, body, 0)


def _dispatch_kernel(zfill_ref, slots_ref, hp_ref, xs_hbm, zero_vmem, sem, zsem):
    i = pl.program_id(0)
    te = zero_vmem.shape[0]
    tc = hp_ref.shape[0]
    n_tiles = xs_hbm.shape[0] // te

    def zero_copy(tile):
        return pltpu.make_async_copy(zero_vmem, xs_hbm.at[pl.ds(pl.multiple_of(tile * te, te), te)], zsem)

    @pl.when(i == 0)
    def _():
        zero_vmem[...] = jnp.zeros(zero_vmem.shape, U32)

        def zstart(tile, c):
            @pl.when(zfill_ref[tile] != 0)
            def _():
                zero_copy(tile).start()
            return c

        def zwait(tile, c):
            @pl.when(zfill_ref[tile] != 0)
            def _():
                zero_copy(tile).wait()
            return c

        lax.fori_loop(0, n_tiles, zstart, 0)
        lax.fori_loop(0, n_tiles, zwait, 0)

    base = i * (tc * TOP_K_PAD)

    def issue(t, c):
        for k in range(TOP_K):
            slot = slots_ref[base + t * TOP_K_PAD + k]
            pltpu.make_async_copy(hp_ref.at[t], xs_hbm.at[slot], sem).start(priority=k % 2)
        return c

    lax.fori_loop(0, tc, issue, 0, unroll=ISSUE_UNROLL)
    _wait_rows(hp_ref.at[pl.ds(0, WAIT_ROWS)], xs_hbm.at[pl.ds(0, WAIT_ROWS)], sem, tc * TOP_K)


def _dispatch(zfill, slots_flat, hp, n_rows):
    t, planes, lanes = hp.shape
    tc = TC_DISP
    return pl.pallas_call(
        _dispatch_kernel,
        grid_spec=pltpu.PrefetchScalarGridSpec(
            num_scalar_prefetch=2,
            grid=(t // tc,),
            in_specs=[pl.BlockSpec((tc, planes, lanes), lambda i, z, s: (i, 0, 0))],
            out_specs=pl.BlockSpec(memory_space=pl.ANY),
            scratch_shapes=[pltpu.VMEM((TE_ROWS, planes, lanes), U32),
                            pltpu.SemaphoreType.DMA, pltpu.SemaphoreType.DMA],
        ),
        out_shape=jax.ShapeDtypeStruct((n_rows, planes, lanes), U32),
        compiler_params=_cparams(("arbitrary",)),
        name="dispatch",
    )(zfill, slots_flat, hp)


def _experts_kernel(te_ref, nv_ref, nx_ref, xs_ref, wg_hbm, wu_hbm, wd_hbm, y_ref,
                    sg, su, sd, wgb, wub, wdb, sem):
    i = pl.program_id(0)
    e = te_ref[i]
    prev = te_ref[jnp.maximum(i - 1, 0)]
    valid = i < nv_ref[0]

    def fetch(ex):
        return (pltpu.make_async_copy(wg_hbm.at[ex], sg, sem.at[0]),
                pltpu.make_async_copy(wu_hbm.at[ex], su, sem.at[1]),
                pltpu.make_async_copy(wd_hbm.at[ex], sd, sem.at[2]))

    @pl.when(i == 0)
    def _():
        for cp in fetch(e):
            cp.start()

    first = valid & ((i == 0) | (e != prev))

    def ffn():
        xs = _unpack_bf16_pairs(_load_row_planes(xs_ref, xs_ref.shape[0] // TE_ROWS))
        g = jnp.dot(xs, wgb[...], preferred_element_type=F32)
        u = jnp.dot(xs, wub[...], preferred_element_type=F32)
        hid = (g * _sigmoid(g) * u).astype(BF16)
        y = jnp.dot(hid, wdb[...], preferred_element_type=F32)
        _store_row_planes(y_ref, _pack_bf16_pairs(y))

    @pl.when(first)
    def _():
        for cp in fetch(e):
            cp.wait()
        wgb[...] = sg[...].astype(BF16)
        wub[...] = su[...].astype(BF16)
        wdb[...] = sd[...].astype(BF16)
        for cp in fetch(jnp.where(nx_ref[i] >= 0, nx_ref[i], e)):
            cp.start(priority=1)
        ffn()

    @pl.when(valid & jnp.logical_not(first))
    def _():
        ffn()

    @pl.when(i == pl.num_programs(0) - 1)
    def _():
        for cp in fetch(e):
            cp.wait()

    @pl.when(jnp.logical_not(valid))
    def _():
        y_ref[...] = jnp.zeros(y_ref.shape, U32)


def _experts(tile_expert, n_valid, next_expert, xs, w_gate, w_up, w_down):
    n_rows, xp, _ = xs.shape
    ne, d, ff = w_gate.shape
    te = TE_ROWS
    in_map = lambda i, te_ref, nv_ref, nx_ref: (jnp.minimum(i, nv_ref[0] - 1), 0)
    out_map = lambda i, te_ref, nv_ref, nx_ref: (i, 0)
    hbm = pl.BlockSpec(memory_space=pl.ANY)
    y = pl.pallas_call(
        _experts_kernel,
        grid_spec=pltpu.PrefetchScalarGridSpec(
            num_scalar_prefetch=3,
            grid=(n_rows // te,),
            in_specs=[pl.BlockSpec((te * xp, LANES), in_map), hbm, hbm, hbm],
            out_specs=pl.BlockSpec((te * xp, LANES), out_map),
            scratch_shapes=[pltpu.VMEM((d, ff), F32), pltpu.VMEM((d, ff), F32), pltpu.VMEM((ff, d), F32),
                            pltpu.VMEM((d, ff), BF16), pltpu.VMEM((d, ff), BF16), pltpu.VMEM((ff, d), BF16),
                            pltpu.SemaphoreType.DMA((3,))],
        ),
        out_shape=jax.ShapeDtypeStruct((n_rows * xp, LANES), U32),
        compiler_params=_cparams(("arbitrary",)),
        name="experts",
    )(tile_expert, n_valid, next_expert, xs.reshape(n_rows * xp, LANES), w_gate, w_up, w_down)
    return y.reshape(n_rows, xp, LANES)


def _combine_kernel(slots_ref, y_hbm, h_ref, wts_ref, wsg_ref, wsu_ref, wsd_ref, g_ref, b_ref,
                    o_ref, gbuf, rbuf, shbuf, sem, *, alpha):
    tm, d = o_ref.shape
    planes = gbuf.shape[3]
    i = pl.program_id(0)
    cur = i % 2
    grp = SUBLANES

    def issue_rows(step, t0, n):
        buf = step % 2
        base = step * (tm * TOP_K_PAD)
        for u in range(n):
            for k in range(TOP_K):
                slot = slots_ref[base + (t0 + u) * TOP_K_PAD + k]
                pltpu.make_async_copy(y_hbm.at[slot], gbuf.at[buf, k, t0 + u], sem.at[buf]).start(priority=k % 2)

    @pl.when(i == 0)
    def _():
        def first(t, c):
            issue_rows(0, t, 1)
            return c
        lax.fori_loop(0, tm, first, 0, unroll=ISSUE_UNROLL)

    hb = h_ref[...].astype(BF16)
    sg = jnp.dot(hb, wsg_ref[...], preferred_element_type=F32)
    su = jnp.dot(hb, wsu_ref[...], preferred_element_type=F32)
    shbuf[...] = jnp.dot((sg * _sigmoid(sg) * su).astype(BF16), wsd_ref[...], preferred_element_type=F32)
    _wait_rows(y_hbm.at[pl.ds(0, WAIT_ROWS)], gbuf.at[cur, 0, pl.ds(0, WAIT_ROWS)], sem.at[cur], tm * TOP_K)

    def group(g, prefetch):
        t0 = pl.multiple_of(g * grp, grp)
        packed = [gbuf[cur, k, pl.ds(t0, grp)] for k in range(TOP_K)]
        w = wts_ref[pl.ds(t0, grp)]
        resid = alpha * h_ref[pl.ds(t0, grp), :] + shbuf[pl.ds(t0, grp), :]
        if prefetch:
            issue_rows(i + 1, t0, grp)
        r_lo = r_hi = None
        for k in range(TOP_K):
            lo = w[:, k:k + 1, :] * pltpu.bitcast(packed[k] << 16, F32)
            hi = w[:, k:k + 1, :] * pltpu.bitcast(packed[k] & BF16_HIGH_HALF, F32)
            r_lo = lo if r_lo is None else r_lo + lo
            r_hi = hi if r_hi is None else r_hi + hi
        p0 = pl.multiple_of(t0 * planes, grp * planes)
        rbuf[0, pl.ds(p0, grp * planes), :] = r_lo.reshape(grp * planes, LANES)
        rbuf[1, pl.ds(p0, grp * planes), :] = r_hi.reshape(grp * planes, LANES)
        routed = jnp.concatenate([rbuf[half, pl.ds(p0 + j, grp, stride=planes), :]
                                  for half in range(2) for j in range(planes)], axis=1)
        o_ref[pl.ds(t0, grp), :] = _layer_norm(resid + routed, g_ref[...], b_ref[...])

    has_next = i + 1 < pl.num_programs(0)

    @pl.when(has_next)
    def _():
        lax.fori_loop(0, tm // grp, lambda g, c: (group(g, True), c)[1], 0, unroll=GROUP_UNROLL)

    @pl.when(jnp.logical_not(has_next))
    def _():
        lax.fori_loop(0, tm // grp, lambda g, c: (group(g, False), c)[1], 0, unroll=GROUP_UNROLL)


def _combine(slots_flat, y, h, wts_planes, wsg, wsu, wsd, g, b, alpha):
    t, d = h.shape
    tm = TM_COMB
    planes = y.shape[1]
    full = lambda a: pl.BlockSpec(a.shape, lambda i, s: (0,) * a.ndim)
    once = lambda a: pl.BlockSpec(a.shape, lambda i, s: (0,) * a.ndim, pipeline_mode=pl.Buffered(1))
    return pl.pallas_call(
        functools.partial(_combine_kernel, alpha=alpha),
        grid_spec=pltpu.PrefetchScalarGridSpec(
            num_scalar_prefetch=1,
            grid=(t // tm,),
            in_specs=[pl.BlockSpec(memory_space=pl.ANY),
                      pl.BlockSpec((tm, d), lambda i, s: (i, 0)),
                      pl.BlockSpec((tm, TOP_K_PAD, LANES), lambda i, s: (i, 0, 0)),
                      once(wsg), once(wsu), once(wsd), full(g), full(b)],
            out_specs=pl.BlockSpec((tm, d), lambda i, s: (i, 0)),
            scratch_shapes=[pltpu.VMEM((2, TOP_K, tm, planes, LANES), U32),
                            pltpu.VMEM((2, tm * planes, LANES), F32), pltpu.VMEM((tm, d), F32),
                            pltpu.SemaphoreType.DMA((2,))],
        ),
        out_shape=jax.ShapeDtypeStruct((t, d), F32),
        compiler_params=_cparams(("arbitrary",)),
        name="combine",
    )(slots_flat, y, h, wts_planes, wsg, wsu, wsd, g, b)


def _prep_attention_weights(w_in, w_uq, w_ukv):
    d = w_in.shape[0]
    ql = w_uq.shape[0]
    kvl = w_ukv.shape[0]
    half = QK_ROPE_DIM // 2
    pad = LANES - QK_ROPE_DIM
    w_kr = w_in[:, ql + kvl:ql + kvl + QK_ROPE_DIM]
    w_kr_rot = jnp.concatenate([-w_kr[:, half:], w_kr[:, :half]], axis=1)
    zpad = jnp.zeros((d, pad), w_in.dtype)
    wa = jnp.concatenate([w_in[:, :ql + kvl], w_kr, zpad, w_kr_rot, zpad], axis=1).astype(BF16)
    wc = w_in[:, ql + kvl + QK_ROPE_DIM:].astype(BF16)
    uq = w_uq.reshape(ql, N_HEADS, QK_NOPE_DIM + QK_ROPE_DIM)
    uq_nope, uq_rope = uq[..., :QK_NOPE_DIM], uq[..., QK_NOPE_DIM:]
    zq = jnp.zeros((ql, N_HEADS, pad), w_uq.dtype)
    wuq = jnp.concatenate([uq_nope, uq_rope, zq], axis=-1).reshape(ql, N_HEADS * HEAD_QK_PAD).astype(BF16)
    uq_rot = jnp.concatenate([-uq_rope[..., half:], uq_rope[..., :half], zq], axis=-1)
    wuqr = uq_rot.reshape(ql, N_HEADS * LANES).astype(BF16)
    ukv = w_ukv.reshape(kvl, N_HEADS, QK_NOPE_DIM + V_HEAD_DIM)
    wukv = jnp.concatenate([ukv[..., :QK_NOPE_DIM].reshape(kvl, -1), ukv[..., QK_NOPE_DIM:].reshape(kvl, -1)],
                           axis=1).astype(BF16)
    return wa, wc, wuq, wuqr, wukv


def _layer(h, pos2, invf4, batch, seq, alpha, w_in, q_norm_g, w_uq, kv_norm_g, w_ukv, conv_w, conv_b,
           conv_ln_g, conv_ln_b, w_o, ln1_g, ln1_b, w_router, router_bias, w_gate, w_up, w_down,
           ws_gate, ws_up, ws_down, ln2_g, ln2_b):
    t, d = h.shape
    row = lambda a: a.reshape(1, -1)
    wa, wc, wuq, wuqr, wukv = _prep_attention_weights(w_in, w_uq, w_ukv)
    q, k, v = _qkv(h, pos2, invf4, wa, row(q_norm_g), row(kv_norm_g), wuq, wuqr, wukv)
    conv = _conv(h, wc, conv_w, row(conv_b), row(conv_ln_g), row(conv_ln_b), seq)
    attn = _attn(q, k, v, batch, seq)
    aw = attn.shape[1]
    wr32 = jnp.pad(w_router.astype(F32), ((0, 0), (0, LANES - N_EXPERTS)))
    wr_hi = wr32.astype(BF16)
    wr_lo = (wr32 - wr_hi.astype(F32)).astype(BF16)
    wr = jnp.concatenate([wr_hi, wr_hi, wr_lo], axis=0)
    h1, h1p, logits_t = _oproj(attn, conv, h, w_o[:aw].astype(BF16), w_o[aw:].astype(BF16),
                               row(ln1_g), row(ln1_b), wr, alpha)
    return _moe_ffn(h1, h1p, logits_t, alpha, router_bias, w_gate, w_up, w_down,
                    ws_gate, ws_up, ws_down, ln2_g, ln2_b)


def _moe_ffn(h1, h1p, logits_t, alpha, router_bias, w_gate, w_up, w_down, ws_gate, ws_up, ws_down, ln2_g, ln2_b):
    t, d = h1.shape
    row = lambda a: a.reshape(1, -1)
    eidx, rank, wts, cnt = _route(logits_t, router_bias.astype(F32).reshape(N_EXPERTS, 1))
    counts = cnt[:, 0].astype(I32)
    padded = (counts + TE_ROWS - 1) // TE_ROWS * TE_ROWS
    pend = jnp.cumsum(padded)
    poff = pend - padded
    n_rows = t * TOP_K + N_EXPERTS * TE_ROWS
    n_tiles = n_rows // TE_ROWS
    n_valid = (pend[-1] // TE_ROWS).astype(I32).reshape(1)
    tile_row = jnp.minimum(jnp.arange(n_tiles, dtype=I32), n_valid[0] - 1) * TE_ROWS
    tile_expert = jnp.sum((pend[None, :] <= tile_row[:, None]).astype(I32), axis=1)
    tile_expert = jnp.minimum(tile_expert, N_EXPERTS - 1)
    tiles = jnp.arange(n_tiles, dtype=I32)
    last_tile = jnp.where(padded > 0, pend // TE_ROWS - 1, -1)
    zfill = ((tiles >= n_valid[0]) | jnp.any(tiles[:, None] == last_tile[None, :], axis=1)).astype(I32)
    ids = jnp.arange(N_EXPERTS, dtype=I32)
    later = (ids[None, :] > ids[:, None]) & (padded[None, :] > 0)
    next_of = jnp.min(jnp.where(later, ids[None, :], N_EXPERTS), axis=1)
    next_of = jnp.where(next_of == N_EXPERTS, -1, next_of)
    next_expert = jnp.sum(jnp.where(tile_expert[:, None] == ids[None, :], next_of[None, :], 0), axis=1).astype(I32)
    slot_base = jnp.sum(jnp.where(eidx[None] == ids[:, None, None], poff[:, None, None], 0), axis=0)
    slots_flat = (slot_base + rank).T.reshape(-1)
    xs = _dispatch(zfill, slots_flat, h1p.reshape(t, -1, LANES), n_rows)
    y = _experts(tile_expert, n_valid, next_expert, xs, w_gate, w_up, w_down)
    wts_planes = jnp.broadcast_to(wts.T[:, :, None], (t, TOP_K_PAD, LANES))
    return _combine(slots_flat, y, h1, wts_planes, ws_gate.astype(BF16), ws_up.astype(BF16),
                    ws_down.astype(BF16), row(ln2_g), row(ln2_b), alpha)


def kernel(x, positions, w_in, q_norm_g, w_uq, kv_norm_g, w_ukv, conv_w, conv_b, conv_ln_g, conv_ln_b, w_o, ln1_g, ln1_b, w_router, router_bias, w_gate, w_up, w_down, ws_gate, ws_up, ws_down, ln2_g, ln2_b):
    batch, seq, d = x.shape
    depth = w_in.shape[0]
    alpha = (2.0 * depth) ** 0.25
    inv_freq = ROPE_BASE ** (-jnp.arange(0, QK_ROPE_DIM, 2, dtype=F32) / QK_ROPE_DIM)
    invf4 = jnp.tile(inv_freq, LANES // inv_freq.shape[0]).reshape(1, LANES)
    pos2 = positions.reshape(batch * seq, 1)
    h = x.reshape(batch * seq, d)
    for l in range(depth):
        h = _layer(h, pos2, invf4, batch, seq, alpha, w_in[l], q_norm_g[l], w_uq[l], kv_norm_g[l], w_ukv[l],
                   conv_w[l], conv_b[l], conv_ln_g[l], conv_ln_b[l], w_o[l], ln1_g[l], ln1_b[l],
                   w_router[l], router_bias[l], w_gate[l], w_up[l], w_down[l],
                   ws_gate[l], ws_up[l], ws_down[l], ln2_g[l], ln2_b[l])
    return h.reshape(batch, seq, d)
```
